```python
import jax
import jax.numpy as jnp
from jax import lax
import numpy as np

D_MODEL = 1024
BATCH = 1
SEQ = 16384
DEPTH = 1

POOL_WIDTH = D_MODEL
POOL_GROUPS = 4
POOL_GROUP_WIDTH = POOL_WIDTH // POOL_GROUPS
POOL_WINDOWS = (2, 4, 8, 16)
GLA_HEADS = 4
GLA_KEY_WIDTH = D_MODEL // 2
GLA_VALUE_WIDTH = D_MODEL
GLA_HEAD_K = GLA_KEY_WIDTH // GLA_HEADS
GLA_HEAD_V = GLA_VALUE_WIDTH // GLA_HEADS
GLA_GATE_RANK = 16
GLA_GATE_NORMALIZER = 16.0
GLA_CHUNK = 64
N_BRANCHES = 2
RMS_EPS = 1e-6
IN_SPLITS = (POOL_WIDTH, POOL_WIDTH, GLA_KEY_WIDTH, GLA_KEY_WIDTH, GLA_VALUE_WIDTH, GLA_VALUE_WIDTH, GLA_GATE_RANK, GLA_GATE_RANK, D_MODEL, D_MODEL)
IN_WIDTH = sum(IN_SPLITS)

kernel_name = "hybrid_pool_gla_gated_block"


def rms_norm(x, g):
    xf = x.astype(jnp.float32)
    xf = xf * lax.rsqrt(jnp.mean(xf * xf, axis=-1, keepdims=True) + RMS_EPS)
    return (xf * g.astype(jnp.float32)).astype(x.dtype)


def multiscale_pool(u):
    b, s, _ = u.shape
    ug = u.reshape(b, s, POOL_GROUPS, POOL_GROUP_WIDTH).astype(jnp.float32)
    csum = jnp.concatenate([jnp.zeros_like(ug[:, :1]), jnp.cumsum(ug, axis=1)], axis=1)
    pos = jnp.arange(s)
    outs = []
    for gi, w in enumerate(POOL_WINDOWS):
        lo = jnp.clip(pos - w // 2, 0, s)
        hi = jnp.clip(pos + w // 2, 0, s)
        cs = csum[:, :, gi]
        window_sum = jnp.take(cs, hi, axis=1) - jnp.take(cs, lo, axis=1)
        count = (hi - lo).astype(jnp.float32)[None, :, None]
        outs.append(window_sum / count - ug[:, :, gi])
    return jnp.stack(outs, axis=2).astype(u.dtype)


def gla_chunked(q, k, v, log_a, include_diag):
    b, s, h, dk = q.shape
    dv = v.shape[-1]
    n = s // GLA_CHUNK
    f32 = jnp.float32
    q = q.astype(f32).reshape(b, n, GLA_CHUNK, h, dk)
    k = k.astype(f32).reshape(b, n, GLA_CHUNK, h, dk)
    v = v.astype(f32).reshape(b, n, GLA_CHUNK, h, dv)
    la = log_a.astype(f32).reshape(b, n, GLA_CHUNK, h, dk)
    cum = jnp.cumsum(la, axis=2)
    q_dec = q * jnp.exp(cum)
    k_inv = k * jnp.exp(-cum)
    k_end = k * jnp.exp(cum[:, :, -1:] - cum)
    mask = jnp.tril(jnp.ones((GLA_CHUNK, GLA_CHUNK), dtype=bool), k=0 if include_diag else -1)
    scores = jnp.einsum('bnihd,bnjhd->bnhij', q_dec, k_inv)
    scores = jnp.where(mask, scores, 0.0)
    o_intra = jnp.einsum('bnhij,bnjhv->bnihv', scores, v)
    chunk_kv = jnp.einsum('bnjhd,bnjhv->nbhdv', k_end, v)
    chunk_decay = jnp.moveaxis(jnp.exp(cum[:, :, -1]), 1, 0)

    def step(state, inp):
        d, kv = inp
        return d[..., None] * state + kv, state

    _, states = lax.scan(step, jnp.zeros((b, h, dk, dv), f32), (chunk_decay, chunk_kv))
    o_inter = jnp.einsum('bnihd,nbhdv->bnihv', q_dec, states)
    return (o_intra + o_inter).reshape(b, s, h, dv)


def setup_inputs(seed: int = 0) -> dict:
    key = jax.random.key(seed)
    ks = jax.random.split(key, 20)

    def nrm(k, shape, scale):
        return jax.random.normal(k, shape, jnp.float32) * scale

    L = DEPTH
    return {
        "x": nrm(ks[0], (BATCH, SEQ, D_MODEL), 1.0),
        "c": nrm(ks[1], (BATCH, D_MODEL), 1.0),
        "w_ada": nrm(ks[2], (L, D_MODEL, 3 * D_MODEL), 0.5 * D_MODEL ** -0.5),
        "b_ada": nrm(ks[3], (L, 3 * D_MODEL), 0.02),
        "g_pre": 1.0 + nrm(ks[4], (L, D_MODEL), 0.02),
        "g_post": 1.0 + nrm(ks[5], (L, D_MODEL), 0.02),
        "w_in": nrm(ks[6], (L, D_MODEL, IN_WIDTH), D_MODEL ** -0.5),
        "pool_w": nrm(ks[7], (L, POOL_GROUPS, POOL_GROUP_WIDTH, POOL_GROUP_WIDTH), POOL_GROUP_WIDTH ** -0.5),
        "pool_scale": 1.0 + nrm(ks[8], (L, POOL_WIDTH), 0.02),
        "gk_up_fwd": nrm(ks[9], (L, GLA_GATE_RANK, GLA_KEY_WIDTH), GLA_GATE_RANK ** -0.5),
        "gk_bias_fwd": nrm(ks[10], (L, GLA_KEY_WIDTH), 0.02),
        "gk_up_bwd": nrm(ks[11], (L, GLA_GATE_RANK, GLA_KEY_WIDTH), GLA_GATE_RANK ** -0.5),
        "gk_bias_bwd": nrm(ks[12], (L, GLA_KEY_WIDTH), 0.02),
        "gla_norm_g": 1.0 + nrm(ks[13], (L, GLA_HEAD_V), 0.02),
        "w_proj_pool": nrm(ks[14], (L, POOL_WIDTH, D_MODEL), POOL_WIDTH ** -0.5),
        "w_proj_gla": nrm(ks[15], (L, GLA_VALUE_WIDTH, D_MODEL), GLA_VALUE_WIDTH ** -0.5),
        "w_out": nrm(ks[16], (L, D_MODEL, D_MODEL), D_MODEL ** -0.5),
    }


def reference(x, c, w_ada, b_ada, g_pre, g_post, w_in, pool_w, pool_scale, gk_up_fwd, gk_bias_fwd, gk_up_bwd, gk_bias_bwd, gla_norm_g, w_proj_pool, w_proj_gla, w_out):
    b, s, _ = x.shape
    split_idx = [int(v) for v in np.cumsum(IN_SPLITS)[:-1]]
    for l in range(DEPTH):
        ada = jax.nn.silu(c) @ w_ada[l] + b_ada[l]
        shift, scale, gate = jnp.split(ada[:, None, :], 3, axis=-1)
        h = rms_norm(x, g_pre[l]) * (1.0 + scale) + shift
        proj = h @ w_in[l]
        (p_in, p_gate, q, k, v, gla_gate, lr_f, lr_b, bg_pool, bg_gla) = jnp.split(proj, split_idx, axis=-1)

        pooled = multiscale_pool(p_in)
        a = jnp.einsum('bsgc,gcd->bsgd', pooled, pool_w[l]).reshape(b, s, POOL_WIDTH) * pool_scale[l]
        y_pool = (a * jax.nn.silu(p_gate)) @ w_proj_pool[l]

        q = q.reshape(b, s, GLA_HEADS, GLA_HEAD_K) * (GLA_HEAD_K ** -0.5)
        k = k.reshape(b, s, GLA_HEADS, GLA_HEAD_K)
        v = v.reshape(b, s, GLA_HEADS, GLA_HEAD_V)
        la_f = (jax.nn.log_sigmoid((lr_f @ gk_up_fwd[l] + gk_bias_fwd[l]).astype(jnp.float32)) / GLA_GATE_NORMALIZER).reshape(b, s, GLA_HEADS, GLA_HEAD_K)
        la_b = (jax.nn.log_sigmoid((lr_b @ gk_up_bwd[l] + gk_bias_bwd[l]).astype(jnp.float32)) / GLA_GATE_NORMALIZER).reshape(b, s, GLA_HEADS, GLA_HEAD_K)
        o_f = gla_chunked(q, k, v, la_f, True)
        o_b = jnp.flip(gla_chunked(jnp.flip(q, 1), jnp.flip(k, 1), jnp.flip(v, 1), jnp.flip(la_b, 1), False), 1)
        o = rms_norm((o_f + o_b).astype(x.dtype), gla_norm_g[l]).reshape(b, s, GLA_VALUE_WIDTH)
        y_gla = (o * jax.nn.silu(gla_gate)) @ w_proj_gla[l]

        merged = jax.nn.sigmoid(bg_pool) * y_pool + jax.nn.sigmoid(bg_gla) * y_gla
        out = merged @ w_out[l]
        x = x + gate * rms_norm(out, g_post[l])
    return x
```

```python
import functools

import jax
import jax.numpy as jnp
from jax import lax
from jax.experimental import pallas as pl
from jax.experimental.pallas import tpu as pltpu

D_MODEL = 1024
POOL_GROUPS = 4
POOL_GROUP_WIDTH = 256
POOL_WINDOWS = (2, 4, 8, 16)
GLA_HEADS = 4
GLA_KEY_WIDTH = 512
GLA_HEAD_K = 128
GLA_HEAD_V = 256
GLA_GATE_RANK = 16
GLA_GATE_NORMALIZER = 16.0
GLA_CHUNK = 64
RMS_EPS = 1e-6
IN_SPLITS = (1024, 1024, 512, 512, 1024, 1024, 16, 16, 1024, 1024)

SEQ_TILE = 256
HALO = 16
POOL_ROW_BLOCK = 128
VMEM_LIMIT_BYTES = 56 * 1024 * 1024

F32 = jnp.float32
BF16 = jnp.bfloat16


def _dot(a, b):
    return jnp.dot(a, b, preferred_element_type=F32)


def _dot_nt(a, b):
    return lax.dot_general(a, b, (((1,), (1,)), ((), ())), preferred_element_type=F32)


def _dot_tn(a, b):
    return lax.dot_general(a, b, (((0,), (0,)), ((), ())), preferred_element_type=F32)


def _sigmoid(x):
    return 1.0 / (1.0 + jnp.exp(-x))


def _silu(x):
    return x * _sigmoid(x)


def _log_sigmoid(x):
    return jnp.minimum(x, 0.0) - jnp.log1p(jnp.exp(-jnp.abs(x)))


def _norm_mod(xv, g_pre, scale, shift):
    ms = jnp.mean(xv * xv, axis=-1, keepdims=True)
    hn = (xv * lax.rsqrt(ms + RMS_EPS)) * g_pre
    return hn * (1.0 + scale) + shift


def _split_hi_lo(a):
    hi = a.astype(BF16)
    lo = (a - hi.astype(F32)).astype(BF16)
    return hi, lo


def _ada_kernel(c_ref, w_ref, b_ref, o_ref):
    s = _silu(c_ref[...])
    o_ref[...] = jnp.dot(s, w_ref[...], preferred_element_type=F32,
                         precision=lax.Precision.HIGHEST) + b_ref[...]


def _ada_call(c8, w_ada, b_ada):
    n_blk = 3
    return pl.pallas_call(
        _ada_kernel,
        grid=(n_blk,),
        in_specs=[
            pl.BlockSpec((8, D_MODEL), lambda j: (0, 0)),
            pl.BlockSpec((D_MODEL, D_MODEL), lambda j: (0, j)),
            pl.BlockSpec((1, D_MODEL), lambda j: (0, j)),
        ],
        out_specs=pl.BlockSpec((8, D_MODEL), lambda j: (0, j)),
        out_shape=jax.ShapeDtypeStruct((8, 3 * D_MODEL), F32),
        compiler_params=pltpu.CompilerParams(dimension_semantics=("arbitrary",)),
        name="ada_call",
    )(c8, w_ada, b_ada)


def _bwd_state_kernel(x_ref, mod_ref, gpre_ref, wk_ref, wv_ref, wlr_ref, gkb_ref, gbias_ref,
                      sb_ref, st_scr):
    t = SEQ_TILE
    i = pl.program_id(0)

    @pl.when(i == 0)
    def _():
        st_scr[...] = jnp.zeros_like(st_scr)

    sb_ref[0] = st_scr[...]

    shift = mod_ref[0:1, :]
    scale = mod_ref[1:2, :]
    h = _norm_mod(x_ref[...], gpre_ref[...], scale, shift).astype(BF16)
    k = _dot(h, wk_ref[...])
    v = _dot(h, wv_ref[...])
    lr = _dot(h, wlr_ref[...]).astype(BF16)
    pre = _dot(lr, gkb_ref[...]) + gbias_ref[...]
    la = _log_sigmoid(pre) / GLA_GATE_NORMALIZER

    row = lax.broadcasted_iota(jnp.int32, (t, t), 0)
    col = lax.broadcasted_iota(jnp.int32, (t, t), 1)
    upper = (col >= row).astype(BF16)
    la_hi, la_lo = _split_hi_lo(la)
    rc = _dot(upper, la_hi) + _dot(upper, la_lo)
    tot = rc[0:1, :]
    k_end = (k * jnp.exp(tot - rc)).astype(BF16)
    decay = jnp.exp(tot)
    v_bf = v.astype(BF16)
    for hd in range(GLA_HEADS):
        ks = slice(hd * GLA_HEAD_K, (hd + 1) * GLA_HEAD_K)
        vs = slice(hd * GLA_HEAD_V, (hd + 1) * GLA_HEAD_V)
        st_scr[hd] = st_scr[hd] * decay[:, ks] + _dot_tn(v_bf[:, vs], k_end[:, ks])


def _bwd_state_call(x2, mod, g_pre, w_k, w_v, w_lr, gk_b, gbias_b):
    s = x2.shape[0]
    nt = s // SEQ_TILE
    const = lambda i: (0, 0)
    return pl.pallas_call(
        _bwd_state_kernel,
        grid=(nt,),
        in_specs=[
            pl.BlockSpec((SEQ_TILE, D_MODEL), lambda i: (nt - 1 - i, 0)),
            pl.BlockSpec((3, D_MODEL), const),
            pl.BlockSpec((1, D_MODEL), const),
            pl.BlockSpec(w_k.shape, const),
            pl.BlockSpec(w_v.shape, const),
            pl.BlockSpec(w_lr.shape, const),
            pl.BlockSpec(gk_b.shape, const),
            pl.BlockSpec(gbias_b.shape, const),
        ],
        out_specs=pl.BlockSpec((1, GLA_HEADS, GLA_HEAD_V, GLA_HEAD_K), lambda i: (nt - 1 - i, 0, 0, 0)),
        out_shape=jax.ShapeDtypeStruct((nt, GLA_HEADS, GLA_HEAD_V, GLA_HEAD_K), F32),
        scratch_shapes=[pltpu.VMEM((GLA_HEADS, GLA_HEAD_V, GLA_HEAD_K), F32)],
        compiler_params=pltpu.CompilerParams(dimension_semantics=("arbitrary",),
                                             vmem_limit_bytes=VMEM_LIMIT_BYTES),
        name="bwd_state_call",
    )(x2, mod, g_pre, w_k, w_v, w_lr, gk_b, gbias_b)


def _main_kernel(seq_len,
                 x_ref, xp_ref, xn_ref, mod_ref, gpre_ref, gpost_ref,
                 wpin_ref, wpg_ref, wq_ref, wk_ref, wv_ref, wgg_ref, wlr_ref, wbgp_ref, wbgg_ref,
                 poolw_ref, pscale_ref, gkcat_ref, gbias_ref, glag_ref,
                 wpp_ref, wpgla_ref, wout_ref, sb_ref,
                 o_ref,
                 sf_scr, sbw_scr, hext_scr, q_scr, k_scr, v_scr, cf_scr, rb_scr, og_scr):
    t = SEQ_TILE
    i = pl.program_id(0)

    @pl.when(i == 0)
    def _():
        sf_scr[...] = jnp.zeros_like(sf_scr)

    sbw_scr[...] = sb_ref[0]

    shift = mod_ref[0:1, :]
    scale = mod_ref[1:2, :]
    gate = mod_ref[2:3, :]
    g_pre = gpre_ref[...]

    x_tile = x_ref[...]
    hext_scr[0:HALO, :] = _norm_mod(xp_ref[...], g_pre, scale, shift).astype(BF16)
    hext_scr[HALO:HALO + t, :] = _norm_mod(x_tile, g_pre, scale, shift).astype(BF16)
    hext_scr[HALO + t:HALO + t + HALO, :] = _norm_mod(xn_ref[...], g_pre, scale, shift).astype(BF16)
    h_ext = hext_scr[...]
    h = hext_scr[HALO:HALO + t, :]

    te = t + 2 * HALO
    g_row = i * t - HALO + lax.broadcasted_iota(jnp.int32, (te, 1), 0)
    valid = jnp.logical_and(g_row >= 0, g_row < seq_len)
    u_ext = jnp.where(valid, _dot(h_ext, wpin_ref[...]), 0.0)
    u_ext_bf = u_ext.astype(BF16)
    u = u_ext[HALO:HALO + t, :]
    pos = i * t + lax.broadcasted_iota(jnp.int32, (t, 1), 0)
    rb_ = POOL_ROW_BLOCK
    kb_ = POOL_ROW_BLOCK + 2 * HALO
    band_r = lax.broadcasted_iota(jnp.int32, (rb_, kb_), 0)
    band_c = lax.broadcasted_iota(jnp.int32, (rb_, kb_), 1)
    band_d = band_c - HALO - band_r
    a_parts = []
    for gi, w in enumerate(POOL_WINDOWS):
        hw = w // 2
        cs = slice(gi * POOL_GROUP_WIDTH, (gi + 1) * POOL_GROUP_WIDTH)
        band = jnp.logical_and(band_d >= -hw, band_d < hw).astype(BF16)
        wsum = jnp.concatenate(
            [_dot(band, u_ext_bf[b * rb_:b * rb_ + kb_, cs]) for b in range(t // rb_)], axis=0)
        count = (jnp.minimum(pos + hw, seq_len) - jnp.maximum(pos - hw, 0)).astype(F32)
        pooled = wsum / count - u[:, cs]
        a_parts.append(_dot(pooled.astype(BF16), poolw_ref[gi]))
    a = jnp.concatenate(a_parts, axis=1) * pscale_ref[...]
    p_gate = _dot(h, wpg_ref[...])
    y_pool = _dot((a * _silu(p_gate)).astype(BF16), wpp_ref[...])
    merged = _sigmoid(_dot(h, wbgp_ref[...])) * y_pool

    q_scr[...] = _dot(h, wq_ref[...]) * (GLA_HEAD_K ** -0.5)
    k_scr[...] = _dot(h, wk_ref[...])
    v_scr[...] = _dot(h, wv_ref[...]).astype(BF16)
    lr = _dot(h, wlr_ref[...]).astype(BF16)
    la = _log_sigmoid(_dot(lr, gkcat_ref[...]) + gbias_ref[...]) / GLA_GATE_NORMALIZER
    row = lax.broadcasted_iota(jnp.int32, (t, t), 0)
    col = lax.broadcasted_iota(jnp.int32, (t, t), 1)
    same_chunk = (row // GLA_CHUNK) == (col // GLA_CHUNK)
    lower = jnp.logical_and(same_chunk, col <= row).astype(BF16)
    upper = jnp.logical_and(same_chunk, col >= row).astype(BF16)
    laf_hi, laf_lo = _split_hi_lo(la[:, :GLA_KEY_WIDTH])
    lab_hi, lab_lo = _split_hi_lo(la[:, GLA_KEY_WIDTH:])
    cf_scr[...] = _dot(lower, laf_hi) + _dot(lower, laf_lo)
    rb_scr[...] = _dot(upper, lab_hi) + _dot(upper, lab_lo)

    crow = lax.broadcasted_iota(jnp.int32, (GLA_CHUNK, GLA_CHUNK), 0)
    ccol = lax.broadcasted_iota(jnp.int32, (GLA_CHUNK, GLA_CHUNK), 1)
    mask_f = ccol <= crow
    mask_b = ccol > crow
    n_chunks = t // GLA_CHUNK

    for c in range(n_chunks):
        rs = slice(c * GLA_CHUNK, (c + 1) * GLA_CHUNK)
        for hd in range(GLA_HEADS):
            ks = slice(hd * GLA_HEAD_K, (hd + 1) * GLA_HEAD_K)
            vs = slice(hd * GLA_HEAD_V, (hd + 1) * GLA_HEAD_V)
            cum = cf_scr[rs, ks]
            tot = cf_scr[(c + 1) * GLA_CHUNK - 1:(c + 1) * GLA_CHUNK, ks]
            qc = q_scr[rs, ks]
            kc = k_scr[rs, ks]
            vc = v_scr[rs, vs]
            q_dec = (qc * jnp.exp(cum)).astype(BF16)
            k_inv = (kc * jnp.exp(-cum)).astype(BF16)
            k_end = (kc * jnp.exp(tot - cum)).astype(BF16)
            sc = jnp.where(mask_f, _dot_nt(q_dec, k_inv), 0.0).astype(BF16)
            st = sf_scr[hd]
            og_scr[rs, vs] = _dot(sc, vc) + _dot_nt(q_dec, st.astype(BF16))
            sf_scr[hd] = st * jnp.exp(tot) + _dot_tn(vc, k_end)

    for c in reversed(range(n_chunks)):
        rs = slice(c * GLA_CHUNK, (c + 1) * GLA_CHUNK)
        for hd in range(GLA_HEADS):
            ks = slice(hd * GLA_HEAD_K, (hd + 1) * GLA_HEAD_K)
            vs = slice(hd * GLA_HEAD_V, (hd + 1) * GLA_HEAD_V)
            cum = rb_scr[rs, ks]
            tot = rb_scr[c * GLA_CHUNK:c * GLA_CHUNK + 1, ks]
            qc = q_scr[rs, ks]
            kc = k_scr[rs, ks]
            vc = v_scr[rs, vs]
            q_dec = (qc * jnp.exp(cum)).astype(BF16)
            k_inv = (kc * jnp.exp(-cum)).astype(BF16)
            k_end = (kc * jnp.exp(tot - cum)).astype(BF16)
            sc = jnp.where(mask_b, _dot_nt(q_dec, k_inv), 0.0).astype(BF16)
            st = sbw_scr[hd]
            og_scr[rs, vs] += _dot(sc, vc) + _dot_nt(q_dec, st.astype(BF16))
            sbw_scr[hd] = st * jnp.exp(tot) + _dot_tn(vc, k_end)

    gla_gate = _dot(h, wgg_ref[...])
    glag = glag_ref[...]
    z_parts = []
    for hd in range(GLA_HEADS):
        vs = slice(hd * GLA_HEAD_V, (hd + 1) * GLA_HEAD_V)
        oh = og_scr[:, vs]
        ms = jnp.mean(oh * oh, axis=-1, keepdims=True)
        z_parts.append((oh * lax.rsqrt(ms + RMS_EPS)) * glag)
    o_n = jnp.concatenate(z_parts, axis=1)
    y_gla = _dot((o_n * _silu(gla_gate)).astype(BF16), wpgla_ref[...])
    merged = merged + _sigmoid(_dot(h, wbgg_ref[...])) * y_gla

    out = _dot(merged.astype(BF16), wout_ref[...])
    ms = jnp.mean(out * out, axis=-1, keepdims=True)
    out_n = (out * lax.rsqrt(ms + RMS_EPS)) * gpost_ref[...]
    o_ref[...] = x_tile + gate * out_n


def _main_call(x2, mod, g_pre, g_post, weights, sb):
    s = x2.shape[0]
    nt = s // SEQ_TILE
    hb = SEQ_TILE // HALO
    n_hb = s // HALO
    const2 = lambda i: (0, 0)

    def resident(arr):
        nd = arr.ndim
        return pl.BlockSpec(arr.shape, lambda i: (0,) * nd, pipeline_mode=pl.Buffered(1))

    in_specs = [
        pl.BlockSpec((SEQ_TILE, D_MODEL), lambda i: (i, 0)),
        pl.BlockSpec((HALO, D_MODEL), lambda i: (jnp.maximum(i * hb - 1, 0), 0)),
        pl.BlockSpec((HALO, D_MODEL), lambda i: (jnp.minimum((i + 1) * hb, n_hb - 1), 0)),
        pl.BlockSpec((3, D_MODEL), const2),
        pl.BlockSpec((1, D_MODEL), const2),
        pl.BlockSpec((1, D_MODEL), const2),
    ] + [resident(w) for w in weights] + [
        pl.BlockSpec((1, GLA_HEADS, GLA_HEAD_V, GLA_HEAD_K), lambda i: (i, 0, 0, 0)),
    ]
    scratch = [
        pltpu.VMEM((GLA_HEADS, GLA_HEAD_V, GLA_HEAD_K), F32),
        pltpu.VMEM((GLA_HEADS, GLA_HEAD_V, GLA_HEAD_K), F32),
        pltpu.VMEM((SEQ_TILE + 2 * HALO, D_MODEL), BF16),
        pltpu.VMEM((SEQ_TILE, GLA_KEY_WIDTH), F32),
        pltpu.VMEM((SEQ_TILE, GLA_KEY_WIDTH), F32),
        pltpu.VMEM((SEQ_TILE, D_MODEL), BF16),
        pltpu.VMEM((SEQ_TILE, GLA_KEY_WIDTH), F32),
        pltpu.VMEM((SEQ_TILE, GLA_KEY_WIDTH), F32),
        pltpu.VMEM((SEQ_TILE, D_MODEL), F32),
    ]
    return pl.pallas_call(
        functools.partial(_main_kernel, s),
        grid=(nt,),
        in_specs=in_specs,
        out_specs=pl.BlockSpec((SEQ_TILE, D_MODEL), lambda i: (i, 0)),
        out_shape=jax.ShapeDtypeStruct((s, D_MODEL), F32),
        scratch_shapes=scratch,
        compiler_params=pltpu.CompilerParams(dimension_semantics=("arbitrary",),
                                             vmem_limit_bytes=VMEM_LIMIT_BYTES),
        name="main_call",
    )(x2, x2, x2, mod, g_pre, g_post, *weights, sb)


def _layer(x2, c, w_ada, b_ada, g_pre, g_post, w_in, pool_w, pool_scale, gk_up_fwd, gk_bias_fwd,
           gk_up_bwd, gk_bias_bwd, gla_norm_g, w_proj_pool, w_proj_gla, w_out):
    c8 = jnp.broadcast_to(c, (8, D_MODEL))
    ada = _ada_call(c8, w_ada, b_ada.reshape(1, -1))
    mod = ada[0].reshape(3, D_MODEL)

    offs = [0]
    for n in IN_SPLITS:
        offs.append(offs[-1] + n)
    piece = lambda j: w_in[:, offs[j]:offs[j + 1]].astype(BF16)
    w_pin, w_pg, w_q, w_k, w_v, w_gg = (piece(j) for j in range(6))
    w_lr = w_in[:, offs[6]:offs[8]].astype(BF16)
    w_bgp, w_bgg = piece(8), piece(9)

    zeros = jnp.zeros((GLA_GATE_RANK, GLA_KEY_WIDTH), F32)
    gk_cat = jnp.concatenate([jnp.concatenate([gk_up_fwd, zeros], axis=1),
                              jnp.concatenate([zeros, gk_up_bwd], axis=1)], axis=0).astype(BF16)
    gbias_cat = jnp.concatenate([gk_bias_fwd, gk_bias_bwd]).reshape(1, -1)
    gk_b = gk_cat[:, GLA_KEY_WIDTH:]
    gbias_b = gk_bias_bwd.reshape(1, -1)

    g_pre2 = g_pre.reshape(1, -1)
    sb = _bwd_state_call(x2, mod, g_pre2, w_k, w_v, w_lr, gk_b, gbias_b)

    weights = [w_pin, w_pg, w_q, w_k, w_v, w_gg, w_lr, w_bgp, w_bgg,
               pool_w.astype(BF16), pool_scale.reshape(1, -1), gk_cat, gbias_cat,
               gla_norm_g.reshape(1, -1),
               w_proj_pool.astype(BF16), w_proj_gla.astype(BF16), w_out.astype(BF16)]
    return _main_call(x2, mod, g_pre2, g_post.reshape(1, -1), weights, sb)


def kernel(x, c, w_ada, b_ada, g_pre, g_post, w_in, pool_w, pool_scale, gk_up_fwd, gk_bias_fwd, gk_up_bwd, gk_bias_bwd, gla_norm_g, w_proj_pool, w_proj_gla, w_out):
    b, s, d = x.shape
    depth = w_in.shape[0]
    outs = []
    for bi in range(b):
        xb = x[bi]
        for l in range(depth):
            xb = _layer(xb, c[bi:bi + 1], w_ada[l], b_ada[l], g_pre[l], g_post[l], w_in[l], pool_w[l],
                        pool_scale[l], gk_up_fwd[l], gk_bias_fwd[l], gk_up_bwd[l], gk_bias_bwd[l],
                        gla_norm_g[l], w_proj_pool[l], w_proj_gla[l], w_out[l])
        outs.append(xb)
    return jnp.stack(outs, axis=0)
```

```python
import functools

import jax
import jax.numpy as jnp
from jax import lax
from jax.experimental import pallas as pl
from jax.experimental.pallas import tpu as pltpu

D_MODEL = 1024
POOL_GROUPS = 4
POOL_GROUP_WIDTH = 256
POOL_WINDOWS = (2, 4, 8, 16)
GLA_HEADS = 4
GLA_KEY_WIDTH = 512
GLA_HEAD_K = 128
GLA_HEAD_V = 256
GLA_GATE_RANK = 16
GLA_GATE_NORMALIZER = 16.0
GLA_CHUNK = 64
RMS_EPS = 1e-6
IN_SPLITS = (1024, 1024, 512, 512, 1024, 1024, 16, 16, 1024, 1024)

SEQ_TILE = 256
HALO = 16
POOL_ROW_BLOCK = 128
VMEM_LIMIT_BYTES = 56 * 1024 * 1024

F32 = jnp.float32
BF16 = jnp.bfloat16


def _dot(a, b):
    return jnp.dot(a, b, preferred_element_type=F32)


def _dot_nt(a, b):
    return lax.dot_general(a, b, (((1,), (1,)), ((), ())), preferred_element_type=F32)


def _dot_tn(a, b):
    return lax.dot_general(a, b, (((0,), (0,)), ((), ())), preferred_element_type=F32)


def _sigmoid(x):
    return 1.0 / (1.0 + jnp.exp(-x))


def _silu(x):
    return x * _sigmoid(x)


def _log_sigmoid(x):
    return jnp.minimum(x, 0.0) - jnp.log1p(jnp.exp(-jnp.abs(x)))


def _norm_mod(xv, g_pre, scale, shift):
    ms = jnp.mean(xv * xv, axis=-1, keepdims=True)
    hn = (xv * lax.rsqrt(ms + RMS_EPS)) * g_pre
    return hn * (1.0 + scale) + shift


def _split_hi_lo(a):
    hi = a.astype(BF16)
    lo = (a - hi.astype(F32)).astype(BF16)
    return hi, lo


def _ada_kernel(c_ref, w_ref, b_ref, o_ref):
    s = _silu(c_ref[...])
    o_ref[...] = jnp.dot(s, w_ref[...], preferred_element_type=F32,
                         precision=lax.Precision.HIGHEST) + b_ref[...]


def _ada_call(c8, w_ada, b_ada):
    n_blk = 3
    return pl.pallas_call(
        _ada_kernel,
        grid=(n_blk,),
        in_specs=[
            pl.BlockSpec((8, D_MODEL), lambda j: (0, 0)),
            pl.BlockSpec((D_MODEL, D_MODEL), lambda j: (0, j)),
            pl.BlockSpec((1, D_MODEL), lambda j: (0, j)),
        ],
        out_specs=pl.BlockSpec((8, D_MODEL), lambda j: (0, j)),
        out_shape=jax.ShapeDtypeStruct((8, 3 * D_MODEL), F32),
        compiler_params=pltpu.CompilerParams(dimension_semantics=("arbitrary",)),
        name="ada_call",
    )(c8, w_ada, b_ada)


def _bwd_state_kernel(x_ref, mod_ref, gpre_ref, wk_ref, wv_ref, wlr_ref, gkb_ref, gbias_ref,
                      sb_ref, k_ref, v_ref, st_scr):
    t = SEQ_TILE
    i = pl.program_id(0)

    @pl.when(i == 0)
    def _():
        st_scr[...] = jnp.zeros_like(st_scr)

    sb_ref[0] = st_scr[...]

    shift = mod_ref[0:1, :]
    scale = mod_ref[1:2, :]
    h = _norm_mod(x_ref[...], gpre_ref[...], scale, shift).astype(BF16)
    k = _dot(h, wk_ref[...])
    v_bf = _dot(h, wv_ref[...]).astype(BF16)
    k_ref[...] = k
    v_ref[...] = v_bf
    lr = _dot(h, wlr_ref[...]).astype(BF16)
    pre = _dot(lr, gkb_ref[...]) + gbias_ref[...]
    la = _log_sigmoid(pre) / GLA_GATE_NORMALIZER

    row = lax.broadcasted_iota(jnp.int32, (t, t), 0)
    col = lax.broadcasted_iota(jnp.int32, (t, t), 1)
    upper = (col >= row).astype(BF16)
    la_hi, la_lo = _split_hi_lo(la)
    rc = _dot(upper, la_hi) + _dot(upper, la_lo)
    tot = rc[0:1, :]
    k_end = (k * jnp.exp(tot - rc)).astype(BF16)
    decay = jnp.exp(tot)
    for hd in range(GLA_HEADS):
        ks = slice(hd * GLA_HEAD_K, (hd + 1) * GLA_HEAD_K)
        vs = slice(hd * GLA_HEAD_V, (hd + 1) * GLA_HEAD_V)
        st_scr[hd] = st_scr[hd] * decay[:, ks] + _dot_tn(v_bf[:, vs], k_end[:, ks])


def _bwd_state_call(x2, mod, g_pre, w_k, w_v, w_lr, gk_b, gbias_b):
    s = x2.shape[0]
    nt = s // SEQ_TILE
    const = lambda i: (0, 0)
    rev = lambda i: (nt - 1 - i, 0)
    return pl.pallas_call(
        _bwd_state_kernel,
        grid=(nt,),
        in_specs=[
            pl.BlockSpec((SEQ_TILE, D_MODEL), rev),
            pl.BlockSpec((3, D_MODEL), const),
            pl.BlockSpec((1, D_MODEL), const),
            pl.BlockSpec(w_k.shape, const),
            pl.BlockSpec(w_v.shape, const),
            pl.BlockSpec(w_lr.shape, const),
            pl.BlockSpec(gk_b.shape, const),
            pl.BlockSpec(gbias_b.shape, const),
        ],
        out_specs=[
            pl.BlockSpec((1, GLA_HEADS, GLA_HEAD_V, GLA_HEAD_K), lambda i: (nt - 1 - i, 0, 0, 0)),
            pl.BlockSpec((SEQ_TILE, GLA_KEY_WIDTH), rev),
            pl.BlockSpec((SEQ_TILE, D_MODEL), rev),
        ],
        out_shape=[
            jax.ShapeDtypeStruct((nt, GLA_HEADS, GLA_HEAD_V, GLA_HEAD_K), F32),
            jax.ShapeDtypeStruct((s, GLA_KEY_WIDTH), F32),
            jax.ShapeDtypeStruct((s, D_MODEL), BF16),
        ],
        scratch_shapes=[pltpu.VMEM((GLA_HEADS, GLA_HEAD_V, GLA_HEAD_K), F32)],
        compiler_params=pltpu.CompilerParams(dimension_semantics=("arbitrary",),
                                             vmem_limit_bytes=VMEM_LIMIT_BYTES),
        name="bwd_state_call",
    )(x2, mod, g_pre, w_k, w_v, w_lr, gk_b, gbias_b)


def _main_kernel(seq_len,
                 x_ref, xp_ref, xn_ref, mod_ref, gpre_ref, gpost_ref, k_ref, v_ref,
                 wpin_ref, wpg_ref, wq_ref, wgg_ref, wlr_ref, wbgp_ref, wbgg_ref,
                 poolw_ref, pscale_ref, gkcat_ref, gbias_ref, glag_ref,
                 wpp_ref, wpgla_ref, wout_ref, sb_ref,
                 o_ref,
                 sf_scr, hext_scr, q_scr, cf_scr, rb_scr, qd_scr, ke_scr, kif_scr, kib_scr,
                 kv_scr, scat_scr, og_scr):
    t = SEQ_TILE
    i = pl.program_id(0)

    @pl.when(i == 0)
    def _():
        sf_scr[...] = jnp.zeros_like(sf_scr)

    shift = mod_ref[0:1, :]
    scale = mod_ref[1:2, :]
    gate = mod_ref[2:3, :]
    g_pre = gpre_ref[...]

    x_tile = x_ref[...]
    hext_scr[0:HALO, :] = _norm_mod(xp_ref[...], g_pre, scale, shift).astype(BF16)
    hext_scr[HALO:HALO + t, :] = _norm_mod(x_tile, g_pre, scale, shift).astype(BF16)
    hext_scr[HALO + t:HALO + t + HALO, :] = _norm_mod(xn_ref[...], g_pre, scale, shift).astype(BF16)
    h_ext = hext_scr[...]
    h = hext_scr[HALO:HALO + t, :]

    te = t + 2 * HALO
    g_row = i * t - HALO + lax.broadcasted_iota(jnp.int32, (te, 1), 0)
    valid = jnp.logical_and(g_row >= 0, g_row < seq_len)
    u_ext = jnp.where(valid, _dot(h_ext, wpin_ref[...]), 0.0)
    u_ext_bf = u_ext.astype(BF16)
    u = u_ext[HALO:HALO + t, :]
    pos = i * t + lax.broadcasted_iota(jnp.int32, (t, 1), 0)
    rb_ = POOL_ROW_BLOCK
    kb_ = POOL_ROW_BLOCK + 2 * HALO
    band_r = lax.broadcasted_iota(jnp.int32, (rb_, kb_), 0)
    band_c = lax.broadcasted_iota(jnp.int32, (rb_, kb_), 1)
    band_d = band_c - HALO - band_r
    a_parts = []
    for gi, w in enumerate(POOL_WINDOWS):
        hw = w // 2
        cs = slice(gi * POOL_GROUP_WIDTH, (gi + 1) * POOL_GROUP_WIDTH)
        band = jnp.logical_and(band_d >= -hw, band_d < hw).astype(BF16)
        wsum = jnp.concatenate(
            [_dot(band, u_ext_bf[b * rb_:b * rb_ + kb_, cs]) for b in range(t // rb_)], axis=0)
        count = (jnp.minimum(pos + hw, seq_len) - jnp.maximum(pos - hw, 0)).astype(F32)
        pooled = wsum / count - u[:, cs]
        a_parts.append(_dot(pooled.astype(BF16), poolw_ref[gi]))
    a = jnp.concatenate(a_parts, axis=1) * pscale_ref[...]
    p_gate = _dot(h, wpg_ref[...])
    y_pool = _dot((a * _silu(p_gate)).astype(BF16), wpp_ref[...])
    merged = _sigmoid(_dot(h, wbgp_ref[...])) * y_pool

    q_scr[...] = _dot(h, wq_ref[...]) * (GLA_HEAD_K ** -0.5)
    lr = _dot(h, wlr_ref[...]).astype(BF16)
    la = _log_sigmoid(_dot(lr, gkcat_ref[...]) + gbias_ref[...]) / GLA_GATE_NORMALIZER
    row = lax.broadcasted_iota(jnp.int32, (t, t), 0)
    col = lax.broadcasted_iota(jnp.int32, (t, t), 1)
    same_chunk = (row // GLA_CHUNK) == (col // GLA_CHUNK)
    lower = jnp.logical_and(same_chunk, col <= row).astype(BF16)
    upper = jnp.logical_and(same_chunk, col >= row).astype(BF16)
    laf_hi, laf_lo = _split_hi_lo(la[:, :GLA_KEY_WIDTH])
    lab_hi, lab_lo = _split_hi_lo(la[:, GLA_KEY_WIDTH:])
    cf_scr[...] = _dot(lower, laf_hi) + _dot(lower, laf_lo)
    rb_scr[...] = _dot(upper, lab_hi) + _dot(upper, lab_lo)

    n_chunks = t // GLA_CHUNK
    hk, hv = GLA_HEAD_K, GLA_HEAD_V

    dec_f, dec_b = [], []
    for c in range(n_chunks):
        rs = slice(c * GLA_CHUNK, (c + 1) * GLA_CHUNK)
        qc = q_scr[rs, :]
        kc = k_ref[rs, :]
        cum_f = cf_scr[rs, :]
        tot_f = cf_scr[(c + 1) * GLA_CHUNK - 1:(c + 1) * GLA_CHUNK, :]
        cum_b = rb_scr[rs, :]
        tot_b = rb_scr[c * GLA_CHUNK:c * GLA_CHUNK + 1, :]
        qd_f = (qc * jnp.exp(cum_f)).astype(BF16)
        qd_b = (qc * jnp.exp(cum_b)).astype(BF16)
        ke_f = (kc * jnp.exp(tot_f - cum_f)).astype(BF16)
        ke_b = (kc * jnp.exp(tot_b - cum_b)).astype(BF16)
        kif_scr[rs, :] = (kc * jnp.exp(-cum_f)).astype(BF16)
        kib_scr[rs, :] = (kc * jnp.exp(-cum_b)).astype(BF16)
        for hd in range(GLA_HEADS):
            ks = slice(hd * hk, (hd + 1) * hk)
            qd_scr[rs, 2 * hd * hk:(2 * hd + 1) * hk] = qd_f[:, ks]
            qd_scr[rs, (2 * hd + 1) * hk:(2 * hd + 2) * hk] = qd_b[:, ks]
            ke_scr[rs, 2 * hd * hk:(2 * hd + 1) * hk] = ke_f[:, ks]
            ke_scr[rs, (2 * hd + 1) * hk:(2 * hd + 2) * hk] = ke_b[:, ks]
        dec_f.append(jnp.exp(tot_f))
        dec_b.append(jnp.exp(tot_b))

    for hd in range(GLA_HEADS):
        ks = slice(hd * hk, (hd + 1) * hk)
        vs = slice(hd * hv, (hd + 1) * hv)
        cat = slice(2 * hd * hk, (2 * hd + 2) * hk)
        for c in range(n_chunks):
            rs = slice(c * GLA_CHUNK, (c + 1) * GLA_CHUNK)
            kv_scr[c] = _dot_tn(v_ref[rs, vs], ke_scr[rs, cat])
        sf = sf_scr[hd]
        for c in range(n_chunks):
            scat_scr[c, hd, :, 0:hk] = sf.astype(BF16)
            sf = sf * dec_f[c][:, ks] + kv_scr[c, :, 0:hk]
        sf_scr[hd] = sf
        sb = sb_ref[0, hd]
        for c in reversed(range(n_chunks)):
            scat_scr[c, hd, :, hk:2 * hk] = sb.astype(BF16)
            if c > 0:
                sb = sb * dec_b[c][:, ks] + kv_scr[c, :, hk:2 * hk]

    crow = lax.broadcasted_iota(jnp.int32, (GLA_CHUNK, GLA_CHUNK), 0)
    ccol = lax.broadcasted_iota(jnp.int32, (GLA_CHUNK, GLA_CHUNK), 1)
    mask_f = ccol <= crow
    for c in range(n_chunks):
        rs = slice(c * GLA_CHUNK, (c + 1) * GLA_CHUNK)
        for hd in range(GLA_HEADS):
            ks = slice(hd * hk, (hd + 1) * hk)
            vs = slice(hd * hv, (hd + 1) * hv)
            cat = slice(2 * hd * hk, (2 * hd + 2) * hk)
            sc_f = _dot_nt(qd_scr[rs, 2 * hd * hk:(2 * hd + 1) * hk], kif_scr[rs, ks])
            sc_b = _dot_nt(qd_scr[rs, (2 * hd + 1) * hk:(2 * hd + 2) * hk], kib_scr[rs, ks])
            sc = jnp.where(mask_f, sc_f, sc_b).astype(BF16)
            og_scr[rs, vs] = _dot(sc, v_ref[rs, vs]) + _dot_nt(qd_scr[rs, cat], scat_scr[c, hd])

    gla_gate = _dot(h, wgg_ref[...])
    glag = glag_ref[...]
    z_parts = []
    for hd in range(GLA_HEADS):
        vs = slice(hd * hv, (hd + 1) * hv)
        oh = og_scr[:, vs]
        ms = jnp.mean(oh * oh, axis=-1, keepdims=True)
        z_parts.append((oh * lax.rsqrt(ms + RMS_EPS)) * glag)
    o_n = jnp.concatenate(z_parts, axis=1)
    y_gla = _dot((o_n * _silu(gla_gate)).astype(BF16), wpgla_ref[...])
    merged = merged + _sigmoid(_dot(h, wbgg_ref[...])) * y_gla

    out = _dot(merged.astype(BF16), wout_ref[...])
    ms = jnp.mean(out * out, axis=-1, keepdims=True)
    out_n = (out * lax.rsqrt(ms + RMS_EPS)) * gpost_ref[...]
    o_ref[...] = x_tile + gate * out_n


def _main_call(x2, mod, g_pre, g_post, k_all, v_all, weights, sb):
    s = x2.shape[0]
    nt = s // SEQ_TILE
    hb = SEQ_TILE // HALO
    n_hb = s // HALO
    n_chunks = SEQ_TILE // GLA_CHUNK
    const2 = lambda i: (0, 0)
    tile = lambda i: (i, 0)

    def resident(arr):
        nd = arr.ndim
        return pl.BlockSpec(arr.shape, lambda i: (0,) * nd, pipeline_mode=pl.Buffered(1))

    in_specs = [
        pl.BlockSpec((SEQ_TILE, D_MODEL), tile),
        pl.BlockSpec((HALO, D_MODEL), lambda i: (jnp.maximum(i * hb - 1, 0), 0)),
        pl.BlockSpec((HALO, D_MODEL), lambda i: (jnp.minimum((i + 1) * hb, n_hb - 1), 0)),
        pl.BlockSpec((3, D_MODEL), const2),
        pl.BlockSpec((1, D_MODEL), const2),
        pl.BlockSpec((1, D_MODEL), const2),
        pl.BlockSpec((SEQ_TILE, GLA_KEY_WIDTH), tile),
        pl.BlockSpec((SEQ_TILE, D_MODEL), tile),
    ] + [resident(w) for w in weights] + [
        pl.BlockSpec((1, GLA_HEADS, GLA_HEAD_V, GLA_HEAD_K), lambda i: (i, 0, 0, 0)),
    ]
    scratch = [
        pltpu.VMEM((GLA_HEADS, GLA_HEAD_V, GLA_HEAD_K), F32),
        pltpu.VMEM((SEQ_TILE + 2 * HALO, D_MODEL), BF16),
        pltpu.VMEM((SEQ_TILE, GLA_KEY_WIDTH), F32),
        pltpu.VMEM((SEQ_TILE, GLA_KEY_WIDTH), F32),
        pltpu.VMEM((SEQ_TILE, GLA_KEY_WIDTH), F32),
        pltpu.VMEM((SEQ_TILE, 2 * GLA_KEY_WIDTH), BF16),
        pltpu.VMEM((SEQ_TILE, 2 * GLA_KEY_WIDTH), BF16),
        pltpu.VMEM((SEQ_TILE, GLA_KEY_WIDTH), BF16),
        pltpu.VMEM((SEQ_TILE, GLA_KEY_WIDTH), BF16),
        pltpu.VMEM((n_chunks, GLA_HEAD_V, 2 * GLA_HEAD_K), F32),
        pltpu.VMEM((n_chunks, GLA_HEADS, GLA_HEAD_V, 2 * GLA_HEAD_K), BF16),
        pltpu.VMEM((SEQ_TILE, D_MODEL), F32),
    ]
    return pl.pallas_call(
        functools.partial(_main_kernel, s),
        grid=(nt,),
        in_specs=in_specs,
        out_specs=pl.BlockSpec((SEQ_TILE, D_MODEL), tile),
        out_shape=jax.ShapeDtypeStruct((s, D_MODEL), F32),
        scratch_shapes=scratch,
        compiler_params=pltpu.CompilerParams(dimension_semantics=("arbitrary",),
                                             vmem_limit_bytes=VMEM_LIMIT_BYTES),
        name="main_call",
    )(x2, x2, x2, mod, g_pre, g_post, k_all, v_all, *weights, sb)


def _layer(x2, c, w_ada, b_ada, g_pre, g_post, w_in, pool_w, pool_scale, gk_up_fwd, gk_bias_fwd,
           gk_up_bwd, gk_bias_bwd, gla_norm_g, w_proj_pool, w_proj_gla, w_out):
    c8 = jnp.broadcast_to(c, (8, D_MODEL))
    ada = _ada_call(c8, w_ada, b_ada.reshape(1, -1))
    mod = ada[0].reshape(3, D_MODEL)

    offs = [0]
    for n in IN_SPLITS:
        offs.append(offs[-1] + n)
    piece = lambda j: w_in[:, offs[j]:offs[j + 1]].astype(BF16)
    w_pin, w_pg, w_q, w_k, w_v, w_gg = (piece(j) for j in range(6))
    w_lr = w_in[:, offs[6]:offs[8]].astype(BF16)
    w_bgp, w_bgg = piece(8), piece(9)

    zeros = jnp.zeros((GLA_GATE_RANK, GLA_KEY_WIDTH), F32)
    gk_cat = jnp.concatenate([jnp.concatenate([gk_up_fwd, zeros], axis=1),
                              jnp.concatenate([zeros, gk_up_bwd], axis=1)], axis=0).astype(BF16)
    gbias_cat = jnp.concatenate([gk_bias_fwd, gk_bias_bwd]).reshape(1, -1)
    gk_b = gk_cat[:, GLA_KEY_WIDTH:]
    gbias_b = gk_bias_bwd.reshape(1, -1)

    g_pre2 = g_pre.reshape(1, -1)
    sb, k_all, v_all = _bwd_state_call(x2, mod, g_pre2, w_k, w_v, w_lr, gk_b, gbias_b)

    weights = [w_pin, w_pg, w_q, w_gg, w_lr, w_bgp, w_bgg,
               pool_w.astype(BF16), pool_scale.reshape(1, -1), gk_cat, gbias_cat,
               gla_norm_g.reshape(1, -1),
               w_proj_pool.astype(BF16), w_proj_gla.astype(BF16), w_out.astype(BF16)]
    return _main_call(x2, mod, g_pre2, g_post.reshape(1, -1), k_all, v_all, weights, sb)


def kernel(x, c, w_ada, b_ada, g_pre, g_post, w_in, pool_w, pool_scale, gk_up_fwd, gk_bias_fwd, gk_up_bwd, gk_bias_bwd, gla_norm_g, w_proj_pool, w_proj_gla, w_out):
    b, s, d = x.shape
    depth = w_in.shape[0]
    xf = x.reshape(b * s, d)
    outs = []
    for bi in range(b):
        xb = xf if b == 1 else lax.slice_in_dim(xf, bi * s, (bi + 1) * s, axis=0)
        for l in range(depth):
            xb = _layer(xb, c[bi:bi + 1], w_ada[l], b_ada[l], g_pre[l], g_post[l], w_in[l], pool_w[l],
                        pool_scale[l], gk_up_fwd[l], gk_bias_fwd[l], gk_up_bwd[l], gk_bias_bwd[l],
                        gla_norm_g[l], w_proj_pool[l], w_proj_gla[l], w_out[l])
        outs.append(xb)
    out = outs[0] if b == 1 else jnp.concatenate(outs, axis=0)
    return out.reshape(b, s, d)
```

```python
import functools

import jax
import jax.numpy as jnp
from jax import lax
from jax.experimental import pallas as pl
from jax.experimental.pallas import tpu as pltpu

D_MODEL = 1024
POOL_GROUPS = 4
POOL_GROUP_WIDTH = 256
POOL_WINDOWS = (2, 4, 8, 16)
GLA_HEADS = 4
GLA_KEY_WIDTH = 512
GLA_HEAD_K = 128
GLA_HEAD_V = 256
GLA_GATE_RANK = 16
GLA_GATE_NORMALIZER = 16.0
GLA_CHUNK = 64
RMS_EPS = 1e-6
IN_SPLITS = (1024, 1024, 512, 512, 1024, 1024, 16, 16, 1024, 1024)

SUB_TILE = 256
MAIN_SUBTILES = 2
STAGE_SKEW = 2
HALO = 16
POOL_ROW_BLOCK = 128
VMEM_LIMIT_BYTES = 56 * 1024 * 1024

F32 = jnp.float32
BF16 = jnp.bfloat16


def _dot(a, b):
    return jnp.dot(a, b, preferred_element_type=F32)


def _dot_nt(a, b):
    return lax.dot_general(a, b, (((1,), (1,)), ((), ())), preferred_element_type=F32)


def _dot_tn(a, b):
    return lax.dot_general(a, b, (((0,), (0,)), ((), ())), preferred_element_type=F32)


def _sigmoid(x):
    return 1.0 / (1.0 + jnp.exp(-x))


def _silu(x):
    return x * _sigmoid(x)


def _log_sigmoid(x):
    return jnp.minimum(x, 0.0) - jnp.log1p(jnp.exp(-jnp.abs(x)))


def _norm_mod(xv, g_pre, scale, shift):
    ms = jnp.mean(xv * xv, axis=-1, keepdims=True)
    hn = (xv * lax.rsqrt(ms + RMS_EPS)) * g_pre
    return hn * (1.0 + scale) + shift


def _split_hi_lo(a):
    hi = a.astype(BF16)
    lo = (a - hi.astype(F32)).astype(BF16)
    return hi, lo


def _ada_kernel(c_ref, w_ref, b_ref, o_ref):
    s = _silu(c_ref[...])
    o_ref[...] = jnp.dot(s, w_ref[...], preferred_element_type=F32,
                         precision=lax.Precision.HIGHEST) + b_ref[...]


def _ada_call(c8, w_ada, b_ada):
    n_blk = 3
    return pl.pallas_call(
        _ada_kernel,
        grid=(n_blk,),
        in_specs=[
            pl.BlockSpec((8, D_MODEL), lambda j: (0, 0)),
            pl.BlockSpec((D_MODEL, D_MODEL), lambda j: (0, j)),
            pl.BlockSpec((1, D_MODEL), lambda j: (0, j)),
        ],
        out_specs=pl.BlockSpec((8, D_MODEL), lambda j: (0, j)),
        out_shape=jax.ShapeDtypeStruct((8, 3 * D_MODEL), F32),
        compiler_params=pltpu.CompilerParams(dimension_semantics=("arbitrary",)),
        name="ada_call",
    )(c8, w_ada, b_ada)


def _bwd_state_kernel(x_ref, mod_ref, gpre_ref, wk_ref, wv_ref, wlr_ref, gkb_ref, gbias_ref,
                      sb_ref, k_ref, v_ref, st_scr):
    t = SUB_TILE
    i = pl.program_id(0)

    @pl.when(i == 0)
    def _():
        st_scr[...] = jnp.zeros_like(st_scr)

    sb_ref[0] = st_scr[...]

    shift = mod_ref[0:1, :]
    scale = mod_ref[1:2, :]
    h = _norm_mod(x_ref[...], gpre_ref[...], scale, shift).astype(BF16)
    k = _dot(h, wk_ref[...])
    v_bf = _dot(h, wv_ref[...]).astype(BF16)
    k_ref[...] = k
    v_ref[...] = v_bf
    lr = _dot(h, wlr_ref[...]).astype(BF16)
    pre = _dot(lr, gkb_ref[...]) + gbias_ref[...]
    la = _log_sigmoid(pre) / GLA_GATE_NORMALIZER

    row = lax.broadcasted_iota(jnp.int32, (t, t), 0)
    col = lax.broadcasted_iota(jnp.int32, (t, t), 1)
    upper = (col >= row).astype(BF16)
    la_hi, la_lo = _split_hi_lo(la)
    rc = _dot(upper, la_hi) + _dot(upper, la_lo)
    tot = rc[0:1, :]
    k_end = (k * jnp.exp(tot - rc)).astype(BF16)
    decay = jnp.exp(tot)
    for hd in range(GLA_HEADS):
        ks = slice(hd * GLA_HEAD_K, (hd + 1) * GLA_HEAD_K)
        vs = slice(hd * GLA_HEAD_V, (hd + 1) * GLA_HEAD_V)
        st_scr[hd] = st_scr[hd] * decay[:, ks] + _dot_tn(v_bf[:, vs], k_end[:, ks])


def _bwd_state_call(x2, mod, g_pre, w_k, w_v, w_lr, gk_b, gbias_b):
    s = x2.shape[0]
    nt = s // SUB_TILE
    const = lambda i: (0, 0)
    rev = lambda i: (nt - 1 - i, 0)
    return pl.pallas_call(
        _bwd_state_kernel,
        grid=(nt,),
        in_specs=[
            pl.BlockSpec((SUB_TILE, D_MODEL), rev),
            pl.BlockSpec((3, D_MODEL), const),
            pl.BlockSpec((1, D_MODEL), const),
            pl.BlockSpec(w_k.shape, const),
            pl.BlockSpec(w_v.shape, const),
            pl.BlockSpec(w_lr.shape, const),
            pl.BlockSpec(gk_b.shape, const),
            pl.BlockSpec(gbias_b.shape, const),
        ],
        out_specs=[
            pl.BlockSpec((1, GLA_HEADS, GLA_HEAD_V, GLA_HEAD_K), lambda i: (nt - 1 - i, 0, 0, 0)),
            pl.BlockSpec((SUB_TILE, GLA_KEY_WIDTH), rev),
            pl.BlockSpec((SUB_TILE, D_MODEL), rev),
        ],
        out_shape=[
            jax.ShapeDtypeStruct((nt, GLA_HEADS, GLA_HEAD_V, GLA_HEAD_K), F32),
            jax.ShapeDtypeStruct((s, GLA_KEY_WIDTH), F32),
            jax.ShapeDtypeStruct((s, D_MODEL), BF16),
        ],
        scratch_shapes=[pltpu.VMEM((GLA_HEADS, GLA_HEAD_V, GLA_HEAD_K), F32)],
        compiler_params=pltpu.CompilerParams(dimension_semantics=("arbitrary",),
                                             vmem_limit_bytes=VMEM_LIMIT_BYTES),
        name="bwd_state_call",
    )(x2, mod, g_pre, w_k, w_v, w_lr, gk_b, gbias_b)


def _subtile_stages(seq_len, sub, refs):
    (x_ref, mod_ref, gpost_ref, k_ref, v_ref,
     wpin_ref, wpg_ref, wq_ref, wgg_ref, wlr_ref, wbgp_ref, wbgg_ref,
     poolw_ref, pscale_ref, gkcat_ref, gbias_ref, glag_ref,
     wpp_ref, wpgla_ref, wout_ref, sb_ref, o_ref,
     sf_scr, hext_scr, q_scr, cf_scr, rb_scr, qd_scr, ke_scr, kif_scr, kib_scr,
     kv_scr, scat_scr, sc_scr, og_scr) = refs
    t = SUB_TILE
    r0 = sub * t
    rows = slice(r0, r0 + t)
    tile_row0 = pl.program_id(0) * (MAIN_SUBTILES * t) + r0
    n_chunks = t // GLA_CHUNK
    hk, hv = GLA_HEAD_K, GLA_HEAD_V
    gate = mod_ref[2:3, :]

    h_ext = hext_scr[r0:r0 + t + 2 * HALO, :]
    h = hext_scr[r0 + HALO:r0 + HALO + t, :]

    q_scr[rows, :] = _dot(h, wq_ref[...]) * (GLA_HEAD_K ** -0.5)
    lr = _dot(h, wlr_ref[...]).astype(BF16)
    te = t + 2 * HALO
    g_row = tile_row0 - HALO + lax.broadcasted_iota(jnp.int32, (te, 1), 0)
    valid = jnp.logical_and(g_row >= 0, g_row < seq_len)
    u_ext = jnp.where(valid, _dot(h_ext, wpin_ref[...]), 0.0)
    yield

    la = _log_sigmoid(_dot(lr, gkcat_ref[...]) + gbias_ref[...]) / GLA_GATE_NORMALIZER
    p_gate = _dot(h, wpg_ref[...])
    u_ext_bf = u_ext.astype(BF16)
    u = u_ext[HALO:HALO + t, :]
    laf_hi, laf_lo = _split_hi_lo(la[:, :GLA_KEY_WIDTH])
    lab_hi, lab_lo = _split_hi_lo(la[:, GLA_KEY_WIDTH:])
    yield

    row = lax.broadcasted_iota(jnp.int32, (t, t), 0)
    col = lax.broadcasted_iota(jnp.int32, (t, t), 1)
    same_chunk = (row // GLA_CHUNK) == (col // GLA_CHUNK)
    lower = jnp.logical_and(same_chunk, col <= row).astype(BF16)
    upper = jnp.logical_and(same_chunk, col >= row).astype(BF16)
    cf_scr[rows, :] = _dot(lower, laf_hi) + _dot(lower, laf_lo)
    rb_scr[rows, :] = _dot(upper, lab_hi) + _dot(upper, lab_lo)
    bg_pool = _sigmoid(_dot(h, wbgp_ref[...]))
    yield

    pos = tile_row0 + lax.broadcasted_iota(jnp.int32, (t, 1), 0)
    rb_ = POOL_ROW_BLOCK
    kb_ = POOL_ROW_BLOCK + 2 * HALO
    band_r = lax.broadcasted_iota(jnp.int32, (rb_, kb_), 0)
    band_c = lax.broadcasted_iota(jnp.int32, (rb_, kb_), 1)
    band_d = band_c - HALO - band_r
    a_parts = []
    for gi, w in enumerate(POOL_WINDOWS):
        hw = w // 2
        cs = slice(gi * POOL_GROUP_WIDTH, (gi + 1) * POOL_GROUP_WIDTH)
        band = jnp.logical_and(band_d >= -hw, band_d < hw).astype(BF16)
        wsum = jnp.concatenate(
            [_dot(band, u_ext_bf[b * rb_:b * rb_ + kb_, cs]) for b in range(t // rb_)], axis=0)
        count = (jnp.minimum(pos + hw, seq_len) - jnp.maximum(pos - hw, 0)).astype(F32)
        pooled = wsum / count - u[:, cs]
        a_parts.append(_dot(pooled.astype(BF16), poolw_ref[gi]))
    a = jnp.concatenate(a_parts, axis=1) * pscale_ref[...]
    z_pool = (a * _silu(p_gate)).astype(BF16)
    gla_gate = _dot(h, wgg_ref[...])
    bg_gla = _sigmoid(_dot(h, wbgg_ref[...]))

    dec_f, dec_b = [], []
    for c in range(n_chunks):
        rs = slice(r0 + c * GLA_CHUNK, r0 + (c + 1) * GLA_CHUNK)
        last = r0 + (c + 1) * GLA_CHUNK - 1
        first = r0 + c * GLA_CHUNK
        qc = q_scr[rs, :]
        kc = k_ref[rs, :]
        cum_f = cf_scr[rs, :]
        tot_f = cf_scr[last:last + 1, :]
        cum_b = rb_scr[rs, :]
        tot_b = rb_scr[first:first + 1, :]
        qd_f = (qc * jnp.exp(cum_f)).astype(BF16)
        qd_b = (qc * jnp.exp(cum_b)).astype(BF16)
        ke_f = (kc * jnp.exp(tot_f - cum_f)).astype(BF16)
        ke_b = (kc * jnp.exp(tot_b - cum_b)).astype(BF16)
        kif_scr[rs, :] = (kc * jnp.exp(-cum_f)).astype(BF16)
        kib_scr[rs, :] = (kc * jnp.exp(-cum_b)).astype(BF16)
        for hd in range(GLA_HEADS):
            ks = slice(hd * hk, (hd + 1) * hk)
            qd_scr[rs, 2 * hd * hk:(2 * hd + 1) * hk] = qd_f[:, ks]
            qd_scr[rs, (2 * hd + 1) * hk:(2 * hd + 2) * hk] = qd_b[:, ks]
            ke_scr[rs, 2 * hd * hk:(2 * hd + 1) * hk] = ke_f[:, ks]
            ke_scr[rs, (2 * hd + 1) * hk:(2 * hd + 2) * hk] = ke_b[:, ks]
        dec_f.append(jnp.exp(tot_f))
        dec_b.append(jnp.exp(tot_b))
    yield

    merged = bg_pool * _dot(z_pool, wpp_ref[...])
    for hd in range(GLA_HEADS):
        vs = slice(hd * hv, (hd + 1) * hv)
        cat = slice(2 * hd * hk, (2 * hd + 2) * hk)
        for c in range(n_chunks):
            rs = slice(r0 + c * GLA_CHUNK, r0 + (c + 1) * GLA_CHUNK)
            kv_scr[sub, hd, c] = _dot_tn(v_ref[rs, vs], ke_scr[rs, cat])
    yield

    crow = lax.broadcasted_iota(jnp.int32, (GLA_CHUNK, GLA_CHUNK), 0)
    ccol = lax.broadcasted_iota(jnp.int32, (GLA_CHUNK, GLA_CHUNK), 1)
    mask_f = ccol <= crow
    for c in range(n_chunks):
        rs = slice(r0 + c * GLA_CHUNK, r0 + (c + 1) * GLA_CHUNK)
        for hd in range(GLA_HEADS):
            ks = slice(hd * hk, (hd + 1) * hk)
            sc_f = _dot_nt(qd_scr[rs, 2 * hd * hk:(2 * hd + 1) * hk], kif_scr[rs, ks])
            sc_b = _dot_nt(qd_scr[rs, (2 * hd + 1) * hk:(2 * hd + 2) * hk], kib_scr[rs, ks])
            sc_scr[sub, c, hd] = jnp.where(mask_f, sc_f, sc_b).astype(BF16)
    for hd in range(GLA_HEADS):
        ks = slice(hd * hk, (hd + 1) * hk)
        sf = sf_scr[hd]
        for c in range(n_chunks):
            scat_scr[sub, c, hd, :, 0:hk] = sf.astype(BF16)
            sf = sf * dec_f[c][:, ks] + kv_scr[sub, hd, c, :, 0:hk]
        sf_scr[hd] = sf
        sb = sb_ref[sub, hd]
        for c in reversed(range(n_chunks)):
            scat_scr[sub, c, hd, :, hk:2 * hk] = sb.astype(BF16)
            if c > 0:
                sb = sb * dec_b[c][:, ks] + kv_scr[sub, hd, c, :, hk:2 * hk]
    yield

    for c in range(n_chunks):
        rs = slice(r0 + c * GLA_CHUNK, r0 + (c + 1) * GLA_CHUNK)
        for hd in range(GLA_HEADS):
            vs = slice(hd * hv, (hd + 1) * hv)
            cat = slice(2 * hd * hk, (2 * hd + 2) * hk)
            og_scr[rs, vs] = (_dot(sc_scr[sub, c, hd], v_ref[rs, vs])
                              + _dot_nt(qd_scr[rs, cat], scat_scr[sub, c, hd]))
    yield

    glag = glag_ref[...]
    z_parts = []
    for hd in range(GLA_HEADS):
        vs = slice(hd * hv, (hd + 1) * hv)
        oh = og_scr[rows, vs]
        ms = jnp.mean(oh * oh, axis=-1, keepdims=True)
        z_parts.append((oh * lax.rsqrt(ms + RMS_EPS)) * glag)
    o_n = jnp.concatenate(z_parts, axis=1)
    y_gla = _dot((o_n * _silu(gla_gate)).astype(BF16), wpgla_ref[...])
    merged = merged + bg_gla * y_gla
    yield

    out = _dot(merged.astype(BF16), wout_ref[...])
    yield

    ms = jnp.mean(out * out, axis=-1, keepdims=True)
    out_n = (out * lax.rsqrt(ms + RMS_EPS)) * gpost_ref[...]
    o_ref[rows, :] = x_ref[rows, :] + gate * out_n
    yield


def _main_kernel(seq_len,
                 x_ref, xp_ref, xn_ref, mod_ref, gpre_ref, gpost_ref, k_ref, v_ref,
                 wpin_ref, wpg_ref, wq_ref, wgg_ref, wlr_ref, wbgp_ref, wbgg_ref,
                 poolw_ref, pscale_ref, gkcat_ref, gbias_ref, glag_ref,
                 wpp_ref, wpgla_ref, wout_ref, sb_ref,
                 o_ref,
                 sf_scr, hext_scr, q_scr, cf_scr, rb_scr, qd_scr, ke_scr, kif_scr, kib_scr,
                 kv_scr, scat_scr, sc_scr, og_scr):
    tb = MAIN_SUBTILES * SUB_TILE

    @pl.when(pl.program_id(0) == 0)
    def _():
        sf_scr[...] = jnp.zeros_like(sf_scr)

    shift = mod_ref[0:1, :]
    scale = mod_ref[1:2, :]
    g_pre = gpre_ref[...]

    hext_scr[0:HALO, :] = _norm_mod(xp_ref[...], g_pre, scale, shift).astype(BF16)
    for sub in range(MAIN_SUBTILES):
        rows = slice(sub * SUB_TILE, (sub + 1) * SUB_TILE)
        hext_scr[HALO + sub * SUB_TILE:HALO + (sub + 1) * SUB_TILE, :] = _norm_mod(
            x_ref[rows, :], g_pre, scale, shift).astype(BF16)
    hext_scr[HALO + tb:HALO + tb + HALO, :] = _norm_mod(xn_ref[...], g_pre, scale, shift).astype(BF16)

    refs = (x_ref, mod_ref, gpost_ref, k_ref, v_ref,
            wpin_ref, wpg_ref, wq_ref, wgg_ref, wlr_ref, wbgp_ref, wbgg_ref,
            poolw_ref, pscale_ref, gkcat_ref, gbias_ref, glag_ref,
            wpp_ref, wpgla_ref, wout_ref, sb_ref, o_ref,
            sf_scr, hext_scr, q_scr, cf_scr, rb_scr, qd_scr, ke_scr, kif_scr, kib_scr,
            kv_scr, scat_scr, sc_scr, og_scr)
    gens = [_subtile_stages(seq_len, sub, refs) for sub in range(MAIN_SUBTILES)]
    live = [True] * MAIN_SUBTILES
    step = 0
    while any(live):
        for sub, g in enumerate(gens):
            if live[sub] and step >= sub * STAGE_SKEW:
                try:
                    next(g)
                except StopIteration:
                    live[sub] = False
        step += 1


def _main_call(x2, mod, g_pre, g_post, k_all, v_all, weights, sb):
    s = x2.shape[0]
    tb = MAIN_SUBTILES * SUB_TILE
    nt = s // tb
    hb = tb // HALO
    n_hb = s // HALO
    n_chunks = SUB_TILE // GLA_CHUNK
    const2 = lambda i: (0, 0)
    tile = lambda i: (i, 0)

    def resident(arr):
        nd = arr.ndim
        return pl.BlockSpec(arr.shape, lambda i: (0,) * nd, pipeline_mode=pl.Buffered(1))

    in_specs = [
        pl.BlockSpec((tb, D_MODEL), tile),
        pl.BlockSpec((HALO, D_MODEL), lambda i: (jnp.maximum(i * hb - 1, 0), 0)),
        pl.BlockSpec((HALO, D_MODEL), lambda i: (jnp.minimum((i + 1) * hb, n_hb - 1), 0)),
        pl.BlockSpec((3, D_MODEL), const2),
        pl.BlockSpec((1, D_MODEL), const2),
        pl.BlockSpec((1, D_MODEL), const2),
        pl.BlockSpec((tb, GLA_KEY_WIDTH), tile),
        pl.BlockSpec((tb, D_MODEL), tile),
    ] + [resident(w) for w in weights] + [
        pl.BlockSpec((MAIN_SUBTILES, GLA_HEADS, GLA_HEAD_V, GLA_HEAD_K), lambda i: (i, 0, 0, 0)),
    ]
    scratch = [
        pltpu.VMEM((GLA_HEADS, GLA_HEAD_V, GLA_HEAD_K), F32),
        pltpu.VMEM((tb + 2 * HALO, D_MODEL), BF16),
        pltpu.VMEM((tb, GLA_KEY_WIDTH), F32),
        pltpu.VMEM((tb, GLA_KEY_WIDTH), F32),
        pltpu.VMEM((tb, GLA_KEY_WIDTH), F32),
        pltpu.VMEM((tb, 2 * GLA_KEY_WIDTH), BF16),
        pltpu.VMEM((tb, 2 * GLA_KEY_WIDTH), BF16),
        pltpu.VMEM((tb, GLA_KEY_WIDTH), BF16),
        pltpu.VMEM((tb, GLA_KEY_WIDTH), BF16),
        pltpu.VMEM((MAIN_SUBTILES, GLA_HEADS, n_chunks, GLA_HEAD_V, 2 * GLA_HEAD_K), F32),
        pltpu.VMEM((MAIN_SUBTILES, n_chunks, GLA_HEADS, GLA_HEAD_V, 2 * GLA_HEAD_K), BF16),
        pltpu.VMEM((MAIN_SUBTILES, n_chunks, GLA_HEADS, GLA_CHUNK, GLA_CHUNK), BF16),
        pltpu.VMEM((tb, D_MODEL), F32),
    ]
    return pl.pallas_call(
        functools.partial(_main_kernel, s),
        grid=(nt,),
        in_specs=in_specs,
        out_specs=pl.BlockSpec((tb, D_MODEL), tile),
        out_shape=jax.ShapeDtypeStruct((s, D_MODEL), F32),
        scratch_shapes=scratch,
        compiler_params=pltpu.CompilerParams(dimension_semantics=("arbitrary",),
                                             vmem_limit_bytes=VMEM_LIMIT_BYTES),
        name="main_call",
    )(x2, x2, x2, mod, g_pre, g_post, k_all, v_all, *weights, sb)


def _layer(x2, c, w_ada, b_ada, g_pre, g_post, w_in, pool_w, pool_scale, gk_up_fwd, gk_bias_fwd,
           gk_up_bwd, gk_bias_bwd, gla_norm_g, w_proj_pool, w_proj_gla, w_out):
    c8 = jnp.broadcast_to(c, (8, D_MODEL))
    ada = _ada_call(c8, w_ada, b_ada.reshape(1, -1))
    mod = ada[0].reshape(3, D_MODEL)

    offs = [0]
    for n in IN_SPLITS:
        offs.append(offs[-1] + n)
    piece = lambda j: w_in[:, offs[j]:offs[j + 1]].astype(BF16)
    w_pin, w_pg, w_q, w_k, w_v, w_gg = (piece(j) for j in range(6))
    w_lr = w_in[:, offs[6]:offs[8]].astype(BF16)
    w_bgp, w_bgg = piece(8), piece(9)

    zeros = jnp.zeros((GLA_GATE_RANK, GLA_KEY_WIDTH), F32)
    gk_cat = jnp.concatenate([jnp.concatenate([gk_up_fwd, zeros], axis=1),
                              jnp.concatenate([zeros, gk_up_bwd], axis=1)], axis=0).astype(BF16)
    gbias_cat = jnp.concatenate([gk_bias_fwd, gk_bias_bwd]).reshape(1, -1)
    gk_b = gk_cat[:, GLA_KEY_WIDTH:]
    gbias_b = gk_bias_bwd.reshape(1, -1)

    g_pre2 = g_pre.reshape(1, -1)
    sb, k_all, v_all = _bwd_state_call(x2, mod, g_pre2, w_k, w_v, w_lr, gk_b, gbias_b)

    weights = [w_pin, w_pg, w_q, w_gg, w_lr, w_bgp, w_bgg,
               pool_w.astype(BF16), pool_scale.reshape(1, -1), gk_cat, gbias_cat,
               gla_norm_g.reshape(1, -1),
               w_proj_pool.astype(BF16), w_proj_gla.astype(BF16), w_out.astype(BF16)]
    return _main_call(x2, mod, g_pre2, g_post.reshape(1, -1), k_all, v_all, weights, sb)


def kernel(x, c, w_ada, b_ada, g_pre, g_post, w_in, pool_w, pool_scale, gk_up_fwd, gk_bias_fwd, gk_up_bwd, gk_bias_bwd, gla_norm_g, w_proj_pool, w_proj_gla, w_out):
    b, s, d = x.shape
    depth = w_in.shape[0]
    xf = x.reshape(b * s, d)
    outs = []
    for bi in range(b):
        xb = xf if b == 1 else lax.slice_in_dim(xf, bi * s, (bi + 1) * s, axis=0)
        for l in range(depth):
            xb = _layer(xb, c[bi:bi + 1], w_ada[l], b_ada[l], g_pre[l], g_post[l], w_in[l], pool_w[l],
                        pool_scale[l], gk_up_fwd[l], gk_bias_fwd[l], gk_up_bwd[l], gk_bias_bwd[l],
                        gla_norm_g[l], w_proj_pool[l], w_proj_gla[l], w_out[l])
        outs.append(xb)
    out = outs[0] if b == 1 else jnp.concatenate(outs, axis=0)
    return out.reshape(b, s, d)
```

```python
import functools

import jax
import jax.numpy as jnp
from jax import lax
from jax.experimental import pallas as pl
from jax.experimental.pallas import tpu as pltpu

D_MODEL = 1024
POOL_GROUPS = 4
POOL_GROUP_WIDTH = 256
POOL_WINDOWS = (2, 4, 8, 16)
GLA_HEADS = 4
GLA_KEY_WIDTH = 512
GLA_HEAD_K = 128
GLA_HEAD_V = 256
GLA_GATE_RANK = 16
GLA_GATE_NORMALIZER = 16.0
GLA_CHUNK = 64
RMS_EPS = 1e-6
IN_SPLITS = (1024, 1024, 512, 512, 1024, 1024, 16, 16, 1024, 1024)
IN_WIDTH = sum(IN_SPLITS)

LANES = 128
SUB_TILE = 256
PREP_SUBTILES = 2
MAIN_SUBTILES = 2
STAGE_SKEW = 2
HALO = 16
POOL_ROW_BLOCK = 128
VMEM_LIMIT_BYTES = 56 * 1024 * 1024

_LR_SRC = sum(IN_SPLITS[:6])
_BG_SRC = sum(IN_SPLITS[:8])
_BG_DST = _LR_SRC
_LR_DST = _BG_DST + 2 * D_MODEL
W_BF_WIDTH = _LR_DST + LANES
LR_WIDTH = 2 * GLA_GATE_RANK

F32 = jnp.float32
BF16 = jnp.bfloat16


def _dot(a, b):
    return jnp.dot(a, b, preferred_element_type=F32)


def _dot_nt(a, b):
    return lax.dot_general(a, b, (((1,), (1,)), ((), ())), preferred_element_type=F32)


def _dot_tn(a, b):
    return lax.dot_general(a, b, (((0,), (0,)), ((), ())), preferred_element_type=F32)


def _sigmoid(x):
    return 1.0 / (1.0 + jnp.exp(-x))


def _silu(x):
    return x * _sigmoid(x)


def _log_sigmoid(x):
    return jnp.minimum(x, 0.0) - jnp.log1p(jnp.exp(-jnp.abs(x)))


def _norm_mod(xv, g_pre, scale, shift):
    ms = jnp.mean(xv * xv, axis=-1, keepdims=True)
    hn = (xv * lax.rsqrt(ms + RMS_EPS)) * g_pre
    return hn * (1.0 + scale) + shift


def _split_hi_lo(a):
    hi = a.astype(BF16)
    lo = (a - hi.astype(F32)).astype(BF16)
    return hi, lo


def _run_interleaved(gens, skew):
    live = [True] * len(gens)
    step = 0
    while any(live):
        for j, g in enumerate(gens):
            if live[j] and step >= j * skew:
                try:
                    next(g)
                except StopIteration:
                    live[j] = False
        step += 1


def _ada_kernel(c_ref, w_ref, b_ref, o_ref):
    s = _silu(c_ref[...])
    o_ref[...] = jnp.dot(s, w_ref[...], preferred_element_type=F32,
                         precision=lax.Precision.HIGHEST) + b_ref[...]


def _ada_call(c8, w_ada, b_ada):
    blk = 2 * LANES
    n_blk = (3 * D_MODEL) // blk
    return pl.pallas_call(
        _ada_kernel,
        grid=(n_blk,),
        in_specs=[
            pl.BlockSpec((8, D_MODEL), lambda j: (0, 0)),
            pl.BlockSpec((D_MODEL, blk), lambda j: (0, j)),
            pl.BlockSpec((1, blk), lambda j: (0, j)),
        ],
        out_specs=pl.BlockSpec((8, blk), lambda j: (0, j)),
        out_shape=jax.ShapeDtypeStruct((8, 3 * D_MODEL), F32),
        compiler_params=pltpu.CompilerParams(dimension_semantics=("arbitrary",)),
        name="ada_call",
    )(c8, w_ada, b_ada)


def _prep_stages(sub, refs):
    (x_ref, mod_ref, gpre_ref, gkb_ref, gbias_ref, wk_scr, wv_scr, wlr_scr,
     sb_ref, k_ref, v_ref, h_ref, st_scr) = refs
    t = SUB_TILE
    rows = slice(sub * t, (sub + 1) * t)
    shift = mod_ref[0:1, :]
    scale = mod_ref[1:2, :]

    h = _norm_mod(x_ref[rows, :], gpre_ref[...], scale, shift).astype(BF16)
    h_ref[rows, :] = h
    k = _dot(h, wk_scr[...])
    k_ref[rows, :] = k
    lr = _dot(h, wlr_scr[...])[:, 0:LR_WIDTH].astype(BF16)
    yield

    v_bf = _dot(h, wv_scr[...]).astype(BF16)
    v_ref[rows, :] = v_bf
    la = _log_sigmoid(_dot(lr, gkb_ref[...]) + gbias_ref[...]) / GLA_GATE_NORMALIZER
    la_hi, la_lo = _split_hi_lo(la)
    yield

    row = lax.broadcasted_iota(jnp.int32, (t, t), 0)
    col = lax.broadcasted_iota(jnp.int32, (t, t), 1)
    upper = (col >= row).astype(BF16)
    rc = _dot(upper, la_hi) + _dot(upper, la_lo)
    yield

    tot = rc[0:1, :]
    k_end = (k * jnp.exp(tot - rc)).astype(BF16)
    decay = jnp.exp(tot)
    kvs = []
    for hd in range(GLA_HEADS):
        ks = slice(hd * GLA_HEAD_K, (hd + 1) * GLA_HEAD_K)
        vs = slice(hd * GLA_HEAD_V, (hd + 1) * GLA_HEAD_V)
        kvs.append(_dot_tn(v_bf[:, vs], k_end[:, ks]))
    yield

    for hd in range(GLA_HEADS):
        ks = slice(hd * GLA_HEAD_K, (hd + 1) * GLA_HEAD_K)
        st = st_scr[hd]
        sb_ref[sub, hd] = st
        st_scr[hd] = st * decay[:, ks] + kvs[hd]
    yield


def _prep_kernel(x_ref, mod_ref, gpre_ref, gkb_ref, gbias_ref, wk_ref, wv_ref, wlr_ref,
                 win_ref, wpool_ref, wpp_ref, wpgla_ref, wout_ref,
                 sb_ref, k_ref, v_ref, h_ref, winbf_ref, wpoolbf_ref, wppbf_ref, wpglabf_ref, woutbf_ref,
                 st_scr, wk_scr, wv_scr, wlr_scr):
    @pl.when(pl.program_id(0) == 0)
    def _():
        st_scr[...] = jnp.zeros_like(st_scr)
        wk_scr[...] = wk_ref[...].astype(BF16)
        wv_scr[...] = wv_ref[...].astype(BF16)
        wlr_scr[...] = wlr_ref[...].astype(BF16)

    winbf_ref[:, 0:_LR_SRC] = win_ref[:, 0:_LR_SRC].astype(BF16)
    winbf_ref[:, _BG_DST:_LR_DST] = win_ref[:, _BG_SRC:IN_WIDTH].astype(BF16)
    winbf_ref[:, _LR_DST:_LR_DST + LR_WIDTH] = win_ref[:, _LR_SRC:_BG_SRC].astype(BF16)
    winbf_ref[:, _LR_DST + LR_WIDTH:W_BF_WIDTH] = jnp.zeros(
        (winbf_ref.shape[0], W_BF_WIDTH - _LR_DST - LR_WIDTH), BF16)
    wpoolbf_ref[...] = wpool_ref[...].astype(BF16)
    wppbf_ref[...] = wpp_ref[...].astype(BF16)
    wpglabf_ref[...] = wpgla_ref[...].astype(BF16)
    woutbf_ref[...] = wout_ref[...].astype(BF16)

    refs = (x_ref, mod_ref, gpre_ref, gkb_ref, gbias_ref, wk_scr, wv_scr, wlr_scr,
            sb_ref, k_ref, v_ref, h_ref, st_scr)
    gens = [_prep_stages(sub, refs) for sub in reversed(range(PREP_SUBTILES))]
    _run_interleaved(gens, STAGE_SKEW)


def _prep_call(x2, mod, g_pre, gk_b, gbias_b, w_in, pool_w2, w_proj_pool, w_proj_gla, w_out):
    s = x2.shape[0]
    tb = PREP_SUBTILES * SUB_TILE
    nb = s // tb
    slab = D_MODEL // nb
    const = lambda i: (0, 0)
    rev = lambda i: (nb - 1 - i, 0)
    fwd = lambda i: (i, 0)
    one = pl.Buffered(1)
    k_col = sum(IN_SPLITS[:3]) // GLA_KEY_WIDTH
    v_col = sum(IN_SPLITS[:4]) // D_MODEL
    lr_col = _LR_SRC // LANES
    return pl.pallas_call(
        _prep_kernel,
        grid=(nb,),
        in_specs=[
            pl.BlockSpec((tb, D_MODEL), rev),
            pl.BlockSpec((3, D_MODEL), const),
            pl.BlockSpec((1, D_MODEL), const),
            pl.BlockSpec(gk_b.shape, const),
            pl.BlockSpec(gbias_b.shape, const),
            pl.BlockSpec((D_MODEL, GLA_KEY_WIDTH), lambda i: (0, k_col), pipeline_mode=one),
            pl.BlockSpec((D_MODEL, D_MODEL), lambda i: (0, v_col), pipeline_mode=one),
            pl.BlockSpec((D_MODEL, LANES), lambda i: (0, lr_col), pipeline_mode=one),
            pl.BlockSpec((slab, IN_WIDTH), fwd),
            pl.BlockSpec((slab, POOL_GROUP_WIDTH), fwd),
            pl.BlockSpec((slab, D_MODEL), fwd),
            pl.BlockSpec((slab, D_MODEL), fwd),
            pl.BlockSpec((slab, D_MODEL), fwd),
        ],
        out_specs=[
            pl.BlockSpec((PREP_SUBTILES, GLA_HEADS, GLA_HEAD_V, GLA_HEAD_K), lambda i: (nb - 1 - i, 0, 0, 0)),
            pl.BlockSpec((tb, GLA_KEY_WIDTH), rev),
            pl.BlockSpec((tb, D_MODEL), rev),
            pl.BlockSpec((tb, D_MODEL), rev),
            pl.BlockSpec((slab, W_BF_WIDTH), fwd),
            pl.BlockSpec((slab, POOL_GROUP_WIDTH), fwd),
            pl.BlockSpec((slab, D_MODEL), fwd),
            pl.BlockSpec((slab, D_MODEL), fwd),
            pl.BlockSpec((slab, D_MODEL), fwd),
        ],
        out_shape=[
            jax.ShapeDtypeStruct((s // SUB_TILE, GLA_HEADS, GLA_HEAD_V, GLA_HEAD_K), F32),
            jax.ShapeDtypeStruct((s, GLA_KEY_WIDTH), F32),
            jax.ShapeDtypeStruct((s, D_MODEL), BF16),
            jax.ShapeDtypeStruct((s, D_MODEL), BF16),
            jax.ShapeDtypeStruct((D_MODEL, W_BF_WIDTH), BF16),
            jax.ShapeDtypeStruct(pool_w2.shape, BF16),
            jax.ShapeDtypeStruct((D_MODEL, D_MODEL), BF16),
            jax.ShapeDtypeStruct((D_MODEL, D_MODEL), BF16),
            jax.ShapeDtypeStruct((D_MODEL, D_MODEL), BF16),
        ],
        scratch_shapes=[
            pltpu.VMEM((GLA_HEADS, GLA_HEAD_V, GLA_HEAD_K), F32),
            pltpu.VMEM((D_MODEL, GLA_KEY_WIDTH), BF16),
            pltpu.VMEM((D_MODEL, D_MODEL), BF16),
            pltpu.VMEM((D_MODEL, LANES), BF16),
        ],
        compiler_params=pltpu.CompilerParams(dimension_semantics=("arbitrary",),
                                             vmem_limit_bytes=VMEM_LIMIT_BYTES),
        name="prep_call",
    )(x2, mod, g_pre, gk_b, gbias_b, w_in, w_in, w_in, w_in, pool_w2, w_proj_pool, w_proj_gla, w_out)


def _subtile_stages(seq_len, sub, refs):
    (x_ref, mod_ref, gpost_ref, k_ref, v_ref,
     wpin_ref, wpg_ref, wq_ref, wgg_ref, wlr_ref, wbgp_ref, wbgg_ref,
     poolw_ref, pscale_ref, gkcat_ref, gbias_ref, glag_ref,
     wpp_ref, wpgla_ref, wout_ref, sb_ref, o_ref,
     sf_scr, hext_scr, q_scr, cf_scr, rb_scr, qd_scr, ke_scr, kif_scr, kib_scr,
     kv_scr, scat_scr, sc_scr, og_scr) = refs
    t = SUB_TILE
    r0 = sub * t
    rows = slice(r0, r0 + t)
    tile_row0 = pl.program_id(0) * (MAIN_SUBTILES * t) + r0
    n_chunks = t // GLA_CHUNK
    hk, hv = GLA_HEAD_K, GLA_HEAD_V
    gate = mod_ref[2:3, :]

    h_ext = hext_scr[r0:r0 + t + 2 * HALO, :]
    h = hext_scr[r0 + HALO:r0 + HALO + t, :]

    q_scr[rows, :] = _dot(h, wq_ref[...]) * (GLA_HEAD_K ** -0.5)
    lr = _dot(h, wlr_ref[...])[:, 0:LR_WIDTH].astype(BF16)
    te = t + 2 * HALO
    g_row = tile_row0 - HALO + lax.broadcasted_iota(jnp.int32, (te, 1), 0)
    valid = jnp.logical_and(g_row >= 0, g_row < seq_len)
    u_ext = jnp.where(valid, _dot(h_ext, wpin_ref[...]), 0.0)
    yield

    la = _log_sigmoid(_dot(lr, gkcat_ref[...]) + gbias_ref[...]) / GLA_GATE_NORMALIZER
    p_gate = _dot(h, wpg_ref[...])
    u_ext_bf = u_ext.astype(BF16)
    u = u_ext[HALO:HALO + t, :]
    laf_hi, laf_lo = _split_hi_lo(la[:, :GLA_KEY_WIDTH])
    lab_hi, lab_lo = _split_hi_lo(la[:, GLA_KEY_WIDTH:])
    yield

    row = lax.broadcasted_iota(jnp.int32, (t, t), 0)
    col = lax.broadcasted_iota(jnp.int32, (t, t), 1)
    same_chunk = (row // GLA_CHUNK) == (col // GLA_CHUNK)
    lower = jnp.logical_and(same_chunk, col <= row).astype(BF16)
    upper = jnp.logical_and(same_chunk, col >= row).astype(BF16)
    cf_scr[rows, :] = _dot(lower, laf_hi) + _dot(lower, laf_lo)
    rb_scr[rows, :] = _dot(upper, lab_hi) + _dot(upper, lab_lo)
    bg_pool = _sigmoid(_dot(h, wbgp_ref[...]))
    yield

    pos = tile_row0 + lax.broadcasted_iota(jnp.int32, (t, 1), 0)
    rb_ = POOL_ROW_BLOCK
    kb_ = POOL_ROW_BLOCK + 2 * HALO
    band_r = lax.broadcasted_iota(jnp.int32, (rb_, kb_), 0)
    band_c = lax.broadcasted_iota(jnp.int32, (rb_, kb_), 1)
    band_d = band_c - HALO - band_r
    a_parts = []
    for gi, w in enumerate(POOL_WINDOWS):
        hw = w // 2
        cs = slice(gi * POOL_GROUP_WIDTH, (gi + 1) * POOL_GROUP_WIDTH)
        ws = slice(gi * POOL_GROUP_WIDTH, (gi + 1) * POOL_GROUP_WIDTH)
        band = jnp.logical_and(band_d >= -hw, band_d < hw).astype(BF16)
        wsum = jnp.concatenate(
            [_dot(band, u_ext_bf[b * rb_:b * rb_ + kb_, cs]) for b in range(t // rb_)], axis=0)
        count = (jnp.minimum(pos + hw, seq_len) - jnp.maximum(pos - hw, 0)).astype(F32)
        pooled = wsum / count - u[:, cs]
        a_parts.append(_dot(pooled.astype(BF16), poolw_ref[ws, :]))
    a = jnp.concatenate(a_parts, axis=1) * pscale_ref[...]
    z_pool = (a * _silu(p_gate)).astype(BF16)
    gla_gate = _dot(h, wgg_ref[...])
    bg_gla = _sigmoid(_dot(h, wbgg_ref[...]))

    dec_f, dec_b = [], []
    for c in range(n_chunks):
        rs = slice(r0 + c * GLA_CHUNK, r0 + (c + 1) * GLA_CHUNK)
        last = r0 + (c + 1) * GLA_CHUNK - 1
        first = r0 + c * GLA_CHUNK
        qc = q_scr[rs, :]
        kc = k_ref[rs, :]
        cum_f = cf_scr[rs, :]
        tot_f = cf_scr[last:last + 1, :]
        cum_b = rb_scr[rs, :]
        tot_b = rb_scr[first:first + 1, :]
        qd_f = (qc * jnp.exp(cum_f)).astype(BF16)
        qd_b = (qc * jnp.exp(cum_b)).astype(BF16)
        ke_f = (kc * jnp.exp(tot_f - cum_f)).astype(BF16)
        ke_b = (kc * jnp.exp(tot_b - cum_b)).astype(BF16)
        kif_scr[rs, :] = (kc * jnp.exp(-cum_f)).astype(BF16)
        kib_scr[rs, :] = (kc * jnp.exp(-cum_b)).astype(BF16)
        for hd in range(GLA_HEADS):
            ks = slice(hd * hk, (hd + 1) * hk)
            qd_scr[rs, 2 * hd * hk:(2 * hd + 1) * hk] = qd_f[:, ks]
            qd_scr[rs, (2 * hd + 1) * hk:(2 * hd + 2) * hk] = qd_b[:, ks]
            ke_scr[rs, 2 * hd * hk:(2 * hd + 1) * hk] = ke_f[:, ks]
            ke_scr[rs, (2 * hd + 1) * hk:(2 * hd + 2) * hk] = ke_b[:, ks]
        dec_f.append(jnp.exp(tot_f))
        dec_b.append(jnp.exp(tot_b))
    yield

    merged = bg_pool * _dot(z_pool, wpp_ref[...])
    for hd in range(GLA_HEADS):
        vs = slice(hd * hv, (hd + 1) * hv)
        cat = slice(2 * hd * hk, (2 * hd + 2) * hk)
        for c in range(n_chunks):
            rs = slice(r0 + c * GLA_CHUNK, r0 + (c + 1) * GLA_CHUNK)
            kv_scr[sub, hd, c] = _dot_tn(v_ref[rs, vs], ke_scr[rs, cat])
    yield

    crow = lax.broadcasted_iota(jnp.int32, (GLA_CHUNK, GLA_CHUNK), 0)
    ccol = lax.broadcasted_iota(jnp.int32, (GLA_CHUNK, GLA_CHUNK), 1)
    mask_f = ccol <= crow
    for c in range(n_chunks):
        rs = slice(r0 + c * GLA_CHUNK, r0 + (c + 1) * GLA_CHUNK)
        for hd in range(GLA_HEADS):
            ks = slice(hd * hk, (hd + 1) * hk)
            sc_f = _dot_nt(qd_scr[rs, 2 * hd * hk:(2 * hd + 1) * hk], kif_scr[rs, ks])
            sc_b = _dot_nt(qd_scr[rs, (2 * hd + 1) * hk:(2 * hd + 2) * hk], kib_scr[rs, ks])
            sc_scr[sub, c, hd] = jnp.where(mask_f, sc_f, sc_b).astype(BF16)
    for hd in range(GLA_HEADS):
        ks = slice(hd * hk, (hd + 1) * hk)
        sf = sf_scr[hd]
        for c in range(n_chunks):
            scat_scr[sub, c, hd, :, 0:hk] = sf.astype(BF16)
            sf = sf * dec_f[c][:, ks] + kv_scr[sub, hd, c, :, 0:hk]
        sf_scr[hd] = sf
        sb = sb_ref[sub, hd]
        for c in reversed(range(n_chunks)):
            scat_scr[sub, c, hd, :, hk:2 * hk] = sb.astype(BF16)
            if c > 0:
                sb = sb * dec_b[c][:, ks] + kv_scr[sub, hd, c, :, hk:2 * hk]
    yield

    for c in range(n_chunks):
        rs = slice(r0 + c * GLA_CHUNK, r0 + (c + 1) * GLA_CHUNK)
        for hd in range(GLA_HEADS):
            vs = slice(hd * hv, (hd + 1) * hv)
            cat = slice(2 * hd * hk, (2 * hd + 2) * hk)
            og_scr[rs, vs] = (_dot(sc_scr[sub, c, hd], v_ref[rs, vs])
                              + _dot_nt(qd_scr[rs, cat], scat_scr[sub, c, hd]))
    yield

    glag = glag_ref[...]
    z_parts = []
    for hd in range(GLA_HEADS):
        vs = slice(hd * hv, (hd + 1) * hv)
        oh = og_scr[rows, vs]
        ms = jnp.mean(oh * oh, axis=-1, keepdims=True)
        z_parts.append((oh * lax.rsqrt(ms + RMS_EPS)) * glag)
    o_n = jnp.concatenate(z_parts, axis=1)
    y_gla = _dot((o_n * _silu(gla_gate)).astype(BF16), wpgla_ref[...])
    merged = merged + bg_gla * y_gla
    yield

    out = _dot(merged.astype(BF16), wout_ref[...])
    yield

    ms = jnp.mean(out * out, axis=-1, keepdims=True)
    out_n = (out * lax.rsqrt(ms + RMS_EPS)) * gpost_ref[...]
    o_ref[rows, :] = x_ref[rows, :] + gate * out_n
    yield


def _main_kernel(seq_len,
                 x_ref, h_ref, hp_ref, hn_ref, mod_ref, gpost_ref, k_ref, v_ref,
                 wpin_ref, wpg_ref, wq_ref, wgg_ref, wlr_ref, wbgp_ref, wbgg_ref,
                 poolw_ref, pscale_ref, gkcat_ref, gbias_ref, glag_ref,
                 wpp_ref, wpgla_ref, wout_ref, sb_ref,
                 o_ref,
                 sf_scr, hext_scr, q_scr, cf_scr, rb_scr, qd_scr, ke_scr, kif_scr, kib_scr,
                 kv_scr, scat_scr, sc_scr, og_scr):
    tb = MAIN_SUBTILES * SUB_TILE

    @pl.when(pl.program_id(0) == 0)
    def _():
        sf_scr[...] = jnp.zeros_like(sf_scr)

    hext_scr[0:HALO, :] = hp_ref[...]
    hext_scr[HALO:HALO + tb, :] = h_ref[...]
    hext_scr[HALO + tb:HALO + tb + HALO, :] = hn_ref[...]

    refs = (x_ref, mod_ref, gpost_ref, k_ref, v_ref,
            wpin_ref, wpg_ref, wq_ref, wgg_ref, wlr_ref, wbgp_ref, wbgg_ref,
            poolw_ref, pscale_ref, gkcat_ref, gbias_ref, glag_ref,
            wpp_ref, wpgla_ref, wout_ref, sb_ref, o_ref,
            sf_scr, hext_scr, q_scr, cf_scr, rb_scr, qd_scr, ke_scr, kif_scr, kib_scr,
            kv_scr, scat_scr, sc_scr, og_scr)
    gens = [_subtile_stages(seq_len, sub, refs) for sub in range(MAIN_SUBTILES)]
    _run_interleaved(gens, STAGE_SKEW)


def _main_call(x2, h_all, mod, g_post, k_all, v_all, w_bf, small, wpool_bf, wpp_bf, wpgla_bf, wout_bf, sb):
    s = x2.shape[0]
    tb = MAIN_SUBTILES * SUB_TILE
    nt = s // tb
    hb = tb // HALO
    n_hb = s // HALO
    n_chunks = SUB_TILE // GLA_CHUNK
    const2 = lambda i: (0, 0)
    tile = lambda i: (i, 0)
    one = pl.Buffered(1)
    pool_scale, gk_cat, gbias_cat, gla_norm_g = small

    def w_piece(width, col_block):
        return pl.BlockSpec((D_MODEL, width), lambda i: (0, col_block), pipeline_mode=one)

    def resident(arr):
        return pl.BlockSpec(arr.shape, const2, pipeline_mode=one)

    in_specs = [
        pl.BlockSpec((tb, D_MODEL), tile),
        pl.BlockSpec((tb, D_MODEL), tile),
        pl.BlockSpec((HALO, D_MODEL), lambda i: (jnp.maximum(i * hb - 1, 0), 0)),
        pl.BlockSpec((HALO, D_MODEL), lambda i: (jnp.minimum((i + 1) * hb, n_hb - 1), 0)),
        pl.BlockSpec((3, D_MODEL), const2),
        pl.BlockSpec((1, D_MODEL), const2),
        pl.BlockSpec((tb, GLA_KEY_WIDTH), tile),
        pl.BlockSpec((tb, D_MODEL), tile),
        w_piece(D_MODEL, 0),
        w_piece(D_MODEL, 1),
        w_piece(GLA_KEY_WIDTH, 2 * D_MODEL // GLA_KEY_WIDTH),
        w_piece(D_MODEL, 4),
        w_piece(LANES, _LR_DST // LANES),
        w_piece(D_MODEL, _BG_DST // D_MODEL),
        w_piece(D_MODEL, _BG_DST // D_MODEL + 1),
        resident(wpool_bf), resident(pool_scale), resident(gk_cat), resident(gbias_cat), resident(gla_norm_g),
        resident(wpp_bf), resident(wpgla_bf), resident(wout_bf),
        pl.BlockSpec((MAIN_SUBTILES, GLA_HEADS, GLA_HEAD_V, GLA_HEAD_K), lambda i: (i, 0, 0, 0)),
    ]
    scratch = [
        pltpu.VMEM((GLA_HEADS, GLA_HEAD_V, GLA_HEAD_K), F32),
        pltpu.VMEM((tb + 2 * HALO, D_MODEL), BF16),
        pltpu.VMEM((tb, GLA_KEY_WIDTH), F32),
        pltpu.VMEM((tb, GLA_KEY_WIDTH), F32),
        pltpu.VMEM((tb, GLA_KEY_WIDTH), F32),
        pltpu.VMEM((tb, 2 * GLA_KEY_WIDTH), BF16),
        pltpu.VMEM((tb, 2 * GLA_KEY_WIDTH), BF16),
        pltpu.VMEM((tb, GLA_KEY_WIDTH), BF16),
        pltpu.VMEM((tb, GLA_KEY_WIDTH), BF16),
        pltpu.VMEM((MAIN_SUBTILES, GLA_HEADS, n_chunks, GLA_HEAD_V, 2 * GLA_HEAD_K), F32),
        pltpu.VMEM((MAIN_SUBTILES, n_chunks, GLA_HEADS, GLA_HEAD_V, 2 * GLA_HEAD_K), BF16),
        pltpu.VMEM((MAIN_SUBTILES, n_chunks, GLA_HEADS, GLA_CHUNK, GLA_CHUNK), BF16),
        pltpu.VMEM((tb, D_MODEL), F32),
    ]
    return pl.pallas_call(
        functools.partial(_main_kernel, s),
        grid=(nt,),
        in_specs=in_specs,
        out_specs=pl.BlockSpec((tb, D_MODEL), tile),
        out_shape=jax.ShapeDtypeStruct((s, D_MODEL), F32),
        scratch_shapes=scratch,
        compiler_params=pltpu.CompilerParams(dimension_semantics=("arbitrary",),
                                             vmem_limit_bytes=VMEM_LIMIT_BYTES),
        name="main_call",
    )(x2, h_all, h_all, h_all, mod, g_post, k_all, v_all,
      w_bf, w_bf, w_bf, w_bf, w_bf, w_bf, w_bf,
      wpool_bf, pool_scale, gk_cat, gbias_cat, gla_norm_g, wpp_bf, wpgla_bf, wout_bf, sb)


def _layer(x2, c, w_ada, b_ada, g_pre, g_post, w_in, pool_w, pool_scale, gk_up_fwd, gk_bias_fwd,
           gk_up_bwd, gk_bias_bwd, gla_norm_g, w_proj_pool, w_proj_gla, w_out):
    c8 = jnp.broadcast_to(c, (8, D_MODEL))
    ada = _ada_call(c8, w_ada, b_ada.reshape(1, -1))
    mod = ada[0].reshape(3, D_MODEL)

    zeros = jnp.zeros((GLA_GATE_RANK, GLA_KEY_WIDTH), F32)
    gk_cat = jnp.concatenate([jnp.concatenate([gk_up_fwd, zeros], axis=1),
                              jnp.concatenate([zeros, gk_up_bwd], axis=1)], axis=0).astype(BF16)
    gbias_cat = jnp.concatenate([gk_bias_fwd, gk_bias_bwd]).reshape(1, -1)
    gk_b = gk_cat[:, GLA_KEY_WIDTH:]
    gbias_b = gk_bias_bwd.reshape(1, -1)

    pool_w2 = pool_w.reshape(POOL_GROUPS * POOL_GROUP_WIDTH, POOL_GROUP_WIDTH)
    sb, k_all, v_all, h_all, w_bf, wpool_bf, wpp_bf, wpgla_bf, wout_bf = _prep_call(
        x2, mod, g_pre.reshape(1, -1), gk_b, gbias_b, w_in, pool_w2, w_proj_pool, w_proj_gla, w_out)

    small = (pool_scale.reshape(1, -1), gk_cat, gbias_cat, gla_norm_g.reshape(1, -1))
    return _main_call(x2, h_all, mod, g_post.reshape(1, -1), k_all, v_all, w_bf, small,
                      wpool_bf, wpp_bf, wpgla_bf, wout_bf, sb)


def kernel(x, c, w_ada, b_ada, g_pre, g_post, w_in, pool_w, pool_scale, gk_up_fwd, gk_bias_fwd, gk_up_bwd, gk_bias_bwd, gla_norm_g, w_proj_pool, w_proj_gla, w_out):
    b, s, d = x.shape
    depth = w_in.shape[0]
    xf = x.reshape(b * s, d)
    outs = []
    for bi in range(b):
        xb = xf if b == 1 else lax.slice_in_dim(xf, bi * s, (bi + 1) * s, axis=0)
        for l in range(depth):
            xb = _layer(xb, c[bi:bi + 1], w_ada[l], b_ada[l], g_pre[l], g_post[l], w_in[l], pool_w[l],
                        pool_scale[l], gk_up_fwd[l], gk_bias_fwd[l], gk_up_bwd[l], gk_bias_bwd[l],
                        gla_norm_g[l], w_proj_pool[l], w_proj_gla[l], w_out[l])
        outs.append(xb)
    out = outs[0] if b == 1 else jnp.concatenate(outs, axis=0)
    return out.reshape(b, s, d)
```

```python
import functools

import jax
import jax.numpy as jnp
from jax import lax
from jax.experimental import pallas as pl
from jax.experimental.pallas import tpu as pltpu

D_MODEL = 1024
POOL_GROUPS = 4
POOL_GROUP_WIDTH = 256
POOL_WINDOWS = (2, 4, 8, 16)
GLA_HEADS = 4
GLA_KEY_WIDTH = 512
GLA_HEAD_K = 128
GLA_HEAD_V = 256
GLA_GATE_RANK = 16
GLA_GATE_NORMALIZER = 16.0
GLA_CHUNK = 64
RMS_EPS = 1e-6
IN_SPLITS = (1024, 1024, 512, 512, 1024, 1024, 16, 16, 1024, 1024)
IN_WIDTH = sum(IN_SPLITS)

LANES = 128
SUB_TILE = 256
PREP_SUBTILES = 2
MAIN_SUBTILES = 2
STAGE_SKEW = 2
HALO = 16
POOL_ROW_BLOCK = 128
VMEM_LIMIT_BYTES = 60 * 1024 * 1024

_OFFS = [sum(IN_SPLITS[:j]) for j in range(len(IN_SPLITS) + 1)]
(PIN_ROWS, PGATE_ROWS, Q_ROWS, K_ROWS, V_ROWS, GGATE_ROWS) = (slice(_OFFS[j], _OFFS[j + 1]) for j in range(6))
LR_ROWS = slice(_OFFS[6], _OFFS[8])
BGP_ROWS = slice(_OFFS[8], _OFFS[9])
BGG_ROWS = slice(_OFFS[9], _OFFS[10])
LR_WIDTH = 2 * GLA_GATE_RANK
WT_SLAB = 240

F32 = jnp.float32
BF16 = jnp.bfloat16


def _dot(a, b):
    return jnp.dot(a, b, preferred_element_type=F32)


def _dot_nt(a, b):
    return lax.dot_general(a, b, (((1,), (1,)), ((), ())), preferred_element_type=F32)


def _dot_tn(a, b):
    return lax.dot_general(a, b, (((0,), (0,)), ((), ())), preferred_element_type=F32)


def _sigmoid(x):
    return 1.0 / (1.0 + jnp.exp(-x))


def _silu(x):
    return x * _sigmoid(x)


def _log_sigmoid(x):
    return jnp.minimum(x, 0.0) - jnp.log1p(jnp.exp(-jnp.abs(x)))


def _norm_mod(xv, g_pre, scale, shift):
    ms = jnp.mean(xv * xv, axis=-1, keepdims=True)
    hn = (xv * lax.rsqrt(ms + RMS_EPS)) * g_pre
    return hn * (1.0 + scale) + shift


def _split_hi_lo(a):
    hi = a.astype(BF16)
    lo = (a - hi.astype(F32)).astype(BF16)
    return hi, lo


def _run_interleaved(gens, skew):
    live = [True] * len(gens)
    step = 0
    while any(live):
        for j, g in enumerate(gens):
            if live[j] and step >= j * skew:
                try:
                    next(g)
                except StopIteration:
                    live[j] = False
        step += 1


def _ada_kernel(c_ref, w_ref, b_ref, o_ref):
    s = _silu(c_ref[...])
    o_ref[...] = jnp.dot(s, w_ref[...], preferred_element_type=F32,
                         precision=lax.Precision.HIGHEST) + b_ref[...]


def _ada_call(c8, w_ada, b_ada):
    blk = 2 * LANES
    n_blk = (3 * D_MODEL) // blk
    return pl.pallas_call(
        _ada_kernel,
        grid=(n_blk,),
        in_specs=[
            pl.BlockSpec((8, D_MODEL), lambda j: (0, 0)),
            pl.BlockSpec((D_MODEL, blk), lambda j: (0, j)),
            pl.BlockSpec((1, blk), lambda j: (0, j)),
        ],
        out_specs=pl.BlockSpec((8, blk), lambda j: (0, j)),
        out_shape=jax.ShapeDtypeStruct((8, 3 * D_MODEL), F32),
        compiler_params=pltpu.CompilerParams(dimension_semantics=("arbitrary",)),
        name="ada_call",
    )(c8, w_ada, b_ada)


def _prep_stages(sub, refs):
    (x_ref, mod_ref, gpre_ref, gkb_ref, gbias_ref, wk_scr, wv_scr, wlr_scr,
     sb_ref, k_ref, v_ref, h_ref, st_scr) = refs
    t = SUB_TILE
    rows = slice(sub * t, (sub + 1) * t)
    shift = mod_ref[0:1, :]
    scale = mod_ref[1:2, :]

    h = _norm_mod(x_ref[rows, :], gpre_ref[...], scale, shift).astype(BF16)
    h_ref[rows, :] = h
    k = _dot_nt(h, wk_scr[...])
    k_ref[rows, :] = k
    lr = _dot_nt(h, wlr_scr[...]).astype(BF16)
    yield

    v_bf = _dot_nt(h, wv_scr[...]).astype(BF16)
    v_ref[rows, :] = v_bf
    la = _log_sigmoid(_dot(lr, gkb_ref[...]) + gbias_ref[...]) / GLA_GATE_NORMALIZER
    la_hi, la_lo = _split_hi_lo(la)
    yield

    row = lax.broadcasted_iota(jnp.int32, (t, t), 0)
    col = lax.broadcasted_iota(jnp.int32, (t, t), 1)
    upper = (col >= row).astype(BF16)
    rc = _dot(upper, la_hi) + _dot(upper, la_lo)
    yield

    tot = rc[0:1, :]
    k_end = (k * jnp.exp(tot - rc)).astype(BF16)
    decay = jnp.exp(tot)
    kvs = []
    for hd in range(GLA_HEADS):
        ks = slice(hd * GLA_HEAD_K, (hd + 1) * GLA_HEAD_K)
        vs = slice(hd * GLA_HEAD_V, (hd + 1) * GLA_HEAD_V)
        kvs.append(_dot_tn(v_bf[:, vs], k_end[:, ks]))
    yield

    for hd in range(GLA_HEADS):
        ks = slice(hd * GLA_HEAD_K, (hd + 1) * GLA_HEAD_K)
        st = st_scr[hd]
        sb_ref[sub, hd] = st
        st_scr[hd] = st * decay[:, ks] + kvs[hd]
    yield


def _prep_kernel(x_ref, mod_ref, gpre_ref, gkb_ref, gbias_ref, wk_ref, wv_ref, wlr_ref,
                 win_ref, wpool_ref, wpp_ref, wpgla_ref, wout_ref,
                 sb_ref, k_ref, v_ref, h_ref, winbf_ref, wpoolbf_ref, wppbf_ref, wpglabf_ref, woutbf_ref,
                 st_scr, wk_scr, wv_scr, wlr_scr):
    @pl.when(pl.program_id(0) == 0)
    def _():
        st_scr[...] = jnp.zeros_like(st_scr)
        wk_scr[...] = wk_ref[...].astype(BF16)
        wv_scr[...] = wv_ref[...].astype(BF16)
        wlr_scr[...] = wlr_ref[...].astype(BF16)

    winbf_ref[...] = win_ref[...].astype(BF16)
    wpoolbf_ref[...] = wpool_ref[...].astype(BF16)
    wppbf_ref[...] = wpp_ref[...].astype(BF16)
    wpglabf_ref[...] = wpgla_ref[...].astype(BF16)
    woutbf_ref[...] = wout_ref[...].astype(BF16)

    refs = (x_ref, mod_ref, gpre_ref, gkb_ref, gbias_ref, wk_scr, wv_scr, wlr_scr,
            sb_ref, k_ref, v_ref, h_ref, st_scr)
    gens = [_prep_stages(sub, refs) for sub in reversed(range(PREP_SUBTILES))]
    _run_interleaved(gens, STAGE_SKEW)


def _prep_call(x2, mod, g_pre, gk_b, gbias_b, w_in_t, pool_w2, w_proj_pool, w_proj_gla, w_out):
    s = x2.shape[0]
    tb = PREP_SUBTILES * SUB_TILE
    nb = s // tb
    slab = D_MODEL // nb
    n_wt = IN_WIDTH // WT_SLAB
    const = lambda i: (0, 0)
    rev = lambda i: (nb - 1 - i, 0)
    fwd = lambda i: (i, 0)
    wt_slab = lambda i: (jnp.minimum(i, n_wt - 1), 0)
    one = pl.Buffered(1)

    def rows_of(rs):
        n = rs.stop - rs.start
        return pl.BlockSpec((n, D_MODEL), lambda i: (rs.start // n, 0), pipeline_mode=one)

    return pl.pallas_call(
        _prep_kernel,
        grid=(nb,),
        in_specs=[
            pl.BlockSpec((tb, D_MODEL), rev),
            pl.BlockSpec((3, D_MODEL), const),
            pl.BlockSpec((1, D_MODEL), const),
            pl.BlockSpec(gk_b.shape, const),
            pl.BlockSpec(gbias_b.shape, const),
            rows_of(K_ROWS),
            rows_of(V_ROWS),
            rows_of(LR_ROWS),
            pl.BlockSpec((WT_SLAB, D_MODEL), wt_slab),
            pl.BlockSpec((slab, POOL_GROUP_WIDTH), fwd),
            pl.BlockSpec((slab, D_MODEL), fwd),
            pl.BlockSpec((slab, D_MODEL), fwd),
            pl.BlockSpec((slab, D_MODEL), fwd),
        ],
        out_specs=[
            pl.BlockSpec((PREP_SUBTILES, GLA_HEADS, GLA_HEAD_V, GLA_HEAD_K), lambda i: (nb - 1 - i, 0, 0, 0)),
            pl.BlockSpec((tb, GLA_KEY_WIDTH), rev),
            pl.BlockSpec((tb, D_MODEL), rev),
            pl.BlockSpec((tb, D_MODEL), rev),
            pl.BlockSpec((WT_SLAB, D_MODEL), wt_slab),
            pl.BlockSpec((slab, POOL_GROUP_WIDTH), fwd),
            pl.BlockSpec((slab, D_MODEL), fwd),
            pl.BlockSpec((slab, D_MODEL), fwd),
            pl.BlockSpec((slab, D_MODEL), fwd),
        ],
        out_shape=[
            jax.ShapeDtypeStruct((s // SUB_TILE, GLA_HEADS, GLA_HEAD_V, GLA_HEAD_K), F32),
            jax.ShapeDtypeStruct((s, GLA_KEY_WIDTH), F32),
            jax.ShapeDtypeStruct((s, D_MODEL), BF16),
            jax.ShapeDtypeStruct((s, D_MODEL), BF16),
            jax.ShapeDtypeStruct((IN_WIDTH, D_MODEL), BF16),
            jax.ShapeDtypeStruct(pool_w2.shape, BF16),
            jax.ShapeDtypeStruct((D_MODEL, D_MODEL), BF16),
            jax.ShapeDtypeStruct((D_MODEL, D_MODEL), BF16),
            jax.ShapeDtypeStruct((D_MODEL, D_MODEL), BF16),
        ],
        scratch_shapes=[
            pltpu.VMEM((GLA_HEADS, GLA_HEAD_V, GLA_HEAD_K), F32),
            pltpu.VMEM((GLA_KEY_WIDTH, D_MODEL), BF16),
            pltpu.VMEM((D_MODEL, D_MODEL), BF16),
            pltpu.VMEM((LR_WIDTH, D_MODEL), BF16),
        ],
        compiler_params=pltpu.CompilerParams(dimension_semantics=("arbitrary",),
                                             vmem_limit_bytes=VMEM_LIMIT_BYTES),
        name="prep_call",
    )(x2, mod, g_pre, gk_b, gbias_b, w_in_t, w_in_t, w_in_t, w_in_t, pool_w2, w_proj_pool, w_proj_gla, w_out)


def _subtile_stages(seq_len, sub, refs):
    (x_ref, mod_ref, gpost_ref, k_ref, v_ref,
     wt_ref,
     poolw_ref, pscale_ref, gkcat_ref, gbias_ref, glag_ref,
     wpp_ref, wpgla_ref, wout_ref, sb_ref, o_ref,
     sf_scr, hext_scr, q_scr, cf_scr, rb_scr, qd_scr, ke_scr, kif_scr, kib_scr,
     kv_scr, scat_scr, sc_scr, og_scr) = refs
    t = SUB_TILE
    r0 = sub * t
    rows = slice(r0, r0 + t)
    tile_row0 = pl.program_id(0) * (MAIN_SUBTILES * t) + r0
    n_chunks = t // GLA_CHUNK
    hk, hv = GLA_HEAD_K, GLA_HEAD_V
    gate = mod_ref[2:3, :]

    h_ext = hext_scr[r0:r0 + t + 2 * HALO, :]
    h = hext_scr[r0 + HALO:r0 + HALO + t, :]

    q_scr[rows, :] = _dot_nt(h, wt_ref[Q_ROWS, :]) * (GLA_HEAD_K ** -0.5)
    lr = _dot_nt(h, wt_ref[LR_ROWS, :]).astype(BF16)
    te = t + 2 * HALO
    g_row = tile_row0 - HALO + lax.broadcasted_iota(jnp.int32, (te, 1), 0)
    valid = jnp.logical_and(g_row >= 0, g_row < seq_len)
    u_ext = jnp.where(valid, _dot_nt(h_ext, wt_ref[PIN_ROWS, :]), 0.0)
    yield

    la = _log_sigmoid(_dot(lr, gkcat_ref[...]) + gbias_ref[...]) / GLA_GATE_NORMALIZER
    p_gate = _dot_nt(h, wt_ref[PGATE_ROWS, :])
    u_ext_bf = u_ext.astype(BF16)
    u = u_ext[HALO:HALO + t, :]
    laf_hi, laf_lo = _split_hi_lo(la[:, :GLA_KEY_WIDTH])
    lab_hi, lab_lo = _split_hi_lo(la[:, GLA_KEY_WIDTH:])
    yield

    row = lax.broadcasted_iota(jnp.int32, (t, t), 0)
    col = lax.broadcasted_iota(jnp.int32, (t, t), 1)
    same_chunk = (row // GLA_CHUNK) == (col // GLA_CHUNK)
    lower = jnp.logical_and(same_chunk, col <= row).astype(BF16)
    upper = jnp.logical_and(same_chunk, col >= row).astype(BF16)
    cf_scr[rows, :] = _dot(lower, laf_hi) + _dot(lower, laf_lo)
    rb_scr[rows, :] = _dot(upper, lab_hi) + _dot(upper, lab_lo)
    bg_pool = _sigmoid(_dot_nt(h, wt_ref[BGP_ROWS, :]))
    yield

    pos = tile_row0 + lax.broadcasted_iota(jnp.int32, (t, 1), 0)
    rb_ = POOL_ROW_BLOCK
    kb_ = POOL_ROW_BLOCK + 2 * HALO
    band_r = lax.broadcasted_iota(jnp.int32, (rb_, kb_), 0)
    band_c = lax.broadcasted_iota(jnp.int32, (rb_, kb_), 1)
    band_d = band_c - HALO - band_r
    a_parts = []
    for gi, w in enumerate(POOL_WINDOWS):
        hw = w // 2
        cs = slice(gi * POOL_GROUP_WIDTH, (gi + 1) * POOL_GROUP_WIDTH)
        ws = slice(gi * POOL_GROUP_WIDTH, (gi + 1) * POOL_GROUP_WIDTH)
        band = jnp.logical_and(band_d >= -hw, band_d < hw).astype(BF16)
        wsum = jnp.concatenate(
            [_dot(band, u_ext_bf[b * rb_:b * rb_ + kb_, cs]) for b in range(t // rb_)], axis=0)
        count = (jnp.minimum(pos + hw, seq_len) - jnp.maximum(pos - hw, 0)).astype(F32)
        pooled = wsum / count - u[:, cs]
        a_parts.append(_dot(pooled.astype(BF16), poolw_ref[ws, :]))
    a = jnp.concatenate(a_parts, axis=1) * pscale_ref[...]
    z_pool = (a * _silu(p_gate)).astype(BF16)
    gla_gate = _dot_nt(h, wt_ref[GGATE_ROWS, :])
    bg_gla = _sigmoid(_dot_nt(h, wt_ref[BGG_ROWS, :]))

    dec_f, dec_b = [], []
    for c in range(n_chunks):
        rs = slice(r0 + c * GLA_CHUNK, r0 + (c + 1) * GLA_CHUNK)
        last = r0 + (c + 1) * GLA_CHUNK - 1
        first = r0 + c * GLA_CHUNK
        qc = q_scr[rs, :]
        kc = k_ref[rs, :]
        cum_f = cf_scr[rs, :]
        tot_f = cf_scr[last:last + 1, :]
        cum_b = rb_scr[rs, :]
        tot_b = rb_scr[first:first + 1, :]
        qd_f = (qc * jnp.exp(cum_f)).astype(BF16)
        qd_b = (qc * jnp.exp(cum_b)).astype(BF16)
        ke_f = (kc * jnp.exp(tot_f - cum_f)).astype(BF16)
        ke_b = (kc * jnp.exp(tot_b - cum_b)).astype(BF16)
        kif_scr[rs, :] = (kc * jnp.exp(-cum_f)).astype(BF16)
        kib_scr[rs, :] = (kc * jnp.exp(-cum_b)).astype(BF16)
        for hd in range(GLA_HEADS):
            ks = slice(hd * hk, (hd + 1) * hk)
            qd_scr[rs, 2 * hd * hk:(2 * hd + 1) * hk] = qd_f[:, ks]
            qd_scr[rs, (2 * hd + 1) * hk:(2 * hd + 2) * hk] = qd_b[:, ks]
            ke_scr[rs, 2 * hd * hk:(2 * hd + 1) * hk] = ke_f[:, ks]
            ke_scr[rs, (2 * hd + 1) * hk:(2 * hd + 2) * hk] = ke_b[:, ks]
        dec_f.append(jnp.exp(tot_f))
        dec_b.append(jnp.exp(tot_b))
    yield

    merged = bg_pool * _dot(z_pool, wpp_ref[...])
    for hd in range(GLA_HEADS):
        vs = slice(hd * hv, (hd + 1) * hv)
        cat = slice(2 * hd * hk, (2 * hd + 2) * hk)
        for c in range(n_chunks):
            rs = slice(r0 + c * GLA_CHUNK, r0 + (c + 1) * GLA_CHUNK)
            kv_scr[sub, hd, c] = _dot_tn(v_ref[rs, vs], ke_scr[rs, cat])
    yield

    crow = lax.broadcasted_iota(jnp.int32, (GLA_CHUNK, GLA_CHUNK), 0)
    ccol = lax.broadcasted_iota(jnp.int32, (GLA_CHUNK, GLA_CHUNK), 1)
    mask_f = ccol <= crow
    for c in range(n_chunks):
        rs = slice(r0 + c * GLA_CHUNK, r0 + (c + 1) * GLA_CHUNK)
        for hd in range(GLA_HEADS):
            ks = slice(hd * hk, (hd + 1) * hk)
            sc_f = _dot_nt(qd_scr[rs, 2 * hd * hk:(2 * hd + 1) * hk], kif_scr[rs, ks])
            sc_b = _dot_nt(qd_scr[rs, (2 * hd + 1) * hk:(2 * hd + 2) * hk], kib_scr[rs, ks])
            sc_scr[sub, c, hd] = jnp.where(mask_f, sc_f, sc_b).astype(BF16)
    for hd in range(GLA_HEADS):
        ks = slice(hd * hk, (hd + 1) * hk)
        sf = sf_scr[hd]
        for c in range(n_chunks):
            scat_scr[sub, c, hd, :, 0:hk] = sf.astype(BF16)
            sf = sf * dec_f[c][:, ks] + kv_scr[sub, hd, c, :, 0:hk]
        sf_scr[hd] = sf
        sb = sb_ref[sub, hd]
        for c in reversed(range(n_chunks)):
            scat_scr[sub, c, hd, :, hk:2 * hk] = sb.astype(BF16)
            if c > 0:
                sb = sb * dec_b[c][:, ks] + kv_scr[sub, hd, c, :, hk:2 * hk]
    yield

    for c in range(n_chunks):
        rs = slice(r0 + c * GLA_CHUNK, r0 + (c + 1) * GLA_CHUNK)
        for hd in range(GLA_HEADS):
            vs = slice(hd * hv, (hd + 1) * hv)
            cat = slice(2 * hd * hk, (2 * hd + 2) * hk)
            og_scr[rs, vs] = (_dot(sc_scr[sub, c, hd], v_ref[rs, vs])
                              + _dot_nt(qd_scr[rs, cat], scat_scr[sub, c, hd]))
    yield

    glag = glag_ref[...]
    z_parts = []
    for hd in range(GLA_HEADS):
        vs = slice(hd * hv, (hd + 1) * hv)
        oh = og_scr[rows, vs]
        ms = jnp.mean(oh * oh, axis=-1, keepdims=True)
        z_parts.append((oh * lax.rsqrt(ms + RMS_EPS)) * glag)
    o_n = jnp.concatenate(z_parts, axis=1)
    y_gla = _dot((o_n * _silu(gla_gate)).astype(BF16), wpgla_ref[...])
    merged = merged + bg_gla * y_gla
    yield

    out = _dot(merged.astype(BF16), wout_ref[...])
    yield

    ms = jnp.mean(out * out, axis=-1, keepdims=True)
    out_n = (out * lax.rsqrt(ms + RMS_EPS)) * gpost_ref[...]
    o_ref[rows, :] = x_ref[rows, :] + gate * out_n
    yield


def _main_kernel(seq_len,
                 x_ref, h_ref, hp_ref, hn_ref, mod_ref, gpost_ref, k_ref, v_ref,
                 wt_ref,
                 poolw_ref, pscale_ref, gkcat_ref, gbias_ref, glag_ref,
                 wpp_ref, wpgla_ref, wout_ref, sb_ref,
                 o_ref,
                 sf_scr, hext_scr, q_scr, cf_scr, rb_scr, qd_scr, ke_scr, kif_scr, kib_scr,
                 kv_scr, scat_scr, sc_scr, og_scr):
    tb = MAIN_SUBTILES * SUB_TILE

    @pl.when(pl.program_id(0) == 0)
    def _():
        sf_scr[...] = jnp.zeros_like(sf_scr)

    hext_scr[0:HALO, :] = hp_ref[...]
    hext_scr[HALO:HALO + tb, :] = h_ref[...]
    hext_scr[HALO + tb:HALO + tb + HALO, :] = hn_ref[...]

    refs = (x_ref, mod_ref, gpost_ref, k_ref, v_ref,
            wt_ref,
            poolw_ref, pscale_ref, gkcat_ref, gbias_ref, glag_ref,
            wpp_ref, wpgla_ref, wout_ref, sb_ref, o_ref,
            sf_scr, hext_scr, q_scr, cf_scr, rb_scr, qd_scr, ke_scr, kif_scr, kib_scr,
            kv_scr, scat_scr, sc_scr, og_scr)
    gens = [_subtile_stages(seq_len, sub, refs) for sub in range(MAIN_SUBTILES)]
    _run_interleaved(gens, STAGE_SKEW)


def _main_call(x2, h_all, mod, g_post, k_all, v_all, w_bf, small, wpool_bf, wpp_bf, wpgla_bf, wout_bf, sb):
    s = x2.shape[0]
    tb = MAIN_SUBTILES * SUB_TILE
    nt = s // tb
    hb = tb // HALO
    n_hb = s // HALO
    n_chunks = SUB_TILE // GLA_CHUNK
    const2 = lambda i: (0, 0)
    tile = lambda i: (i, 0)
    one = pl.Buffered(1)
    pool_scale, gk_cat, gbias_cat, gla_norm_g = small

    def resident(arr):
        return pl.BlockSpec(arr.shape, const2, pipeline_mode=one)

    in_specs = [
        pl.BlockSpec((tb, D_MODEL), tile),
        pl.BlockSpec((tb, D_MODEL), tile),
        pl.BlockSpec((HALO, D_MODEL), lambda i: (jnp.maximum(i * hb - 1, 0), 0)),
        pl.BlockSpec((HALO, D_MODEL), lambda i: (jnp.minimum((i + 1) * hb, n_hb - 1), 0)),
        pl.BlockSpec((3, D_MODEL), const2),
        pl.BlockSpec((1, D_MODEL), const2),
        pl.BlockSpec((tb, GLA_KEY_WIDTH), tile),
        pl.BlockSpec((tb, D_MODEL), tile),
        resident(w_bf),
        resident(wpool_bf), resident(pool_scale), resident(gk_cat), resident(gbias_cat), resident(gla_norm_g),
        resident(wpp_bf), resident(wpgla_bf), resident(wout_bf),
        pl.BlockSpec((MAIN_SUBTILES, GLA_HEADS, GLA_HEAD_V, GLA_HEAD_K), lambda i: (i, 0, 0, 0)),
    ]
    scratch = [
        pltpu.VMEM((GLA_HEADS, GLA_HEAD_V, GLA_HEAD_K), F32),
        pltpu.VMEM((tb + 2 * HALO, D_MODEL), BF16),
        pltpu.VMEM((tb, GLA_KEY_WIDTH), F32),
        pltpu.VMEM((tb, GLA_KEY_WIDTH), F32),
        pltpu.VMEM((tb, GLA_KEY_WIDTH), F32),
        pltpu.VMEM((tb, 2 * GLA_KEY_WIDTH), BF16),
        pltpu.VMEM((tb, 2 * GLA_KEY_WIDTH), BF16),
        pltpu.VMEM((tb, GLA_KEY_WIDTH), BF16),
        pltpu.VMEM((tb, GLA_KEY_WIDTH), BF16),
        pltpu.VMEM((MAIN_SUBTILES, GLA_HEADS, n_chunks, GLA_HEAD_V, 2 * GLA_HEAD_K), F32),
        pltpu.VMEM((MAIN_SUBTILES, n_chunks, GLA_HEADS, GLA_HEAD_V, 2 * GLA_HEAD_K), BF16),
        pltpu.VMEM((MAIN_SUBTILES, n_chunks, GLA_HEADS, GLA_CHUNK, GLA_CHUNK), BF16),
        pltpu.VMEM((tb, D_MODEL), F32),
    ]
    return pl.pallas_call(
        functools.partial(_main_kernel, s),
        grid=(nt,),
        in_specs=in_specs,
        out_specs=pl.BlockSpec((tb, D_MODEL), tile),
        out_shape=jax.ShapeDtypeStruct((s, D_MODEL), F32),
        scratch_shapes=scratch,
        compiler_params=pltpu.CompilerParams(dimension_semantics=("arbitrary",),
                                             vmem_limit_bytes=VMEM_LIMIT_BYTES),
        name="main_call",
    )(x2, h_all, h_all, h_all, mod, g_post, k_all, v_all,
      w_bf,
      wpool_bf, pool_scale, gk_cat, gbias_cat, gla_norm_g, wpp_bf, wpgla_bf, wout_bf, sb)


def _layer(x2, c, w_ada, b_ada, g_pre, g_post, w_in, pool_w, pool_scale, gk_up_fwd, gk_bias_fwd,
           gk_up_bwd, gk_bias_bwd, gla_norm_g, w_proj_pool, w_proj_gla, w_out):
    c8 = jnp.broadcast_to(c, (8, D_MODEL))
    ada = _ada_call(c8, w_ada, b_ada.reshape(1, -1))
    mod = ada[0].reshape(3, D_MODEL)

    zeros = jnp.zeros((GLA_GATE_RANK, GLA_KEY_WIDTH), F32)
    gk_cat = jnp.concatenate([jnp.concatenate([gk_up_fwd, zeros], axis=1),
                              jnp.concatenate([zeros, gk_up_bwd], axis=1)], axis=0).astype(BF16)
    gbias_cat = jnp.concatenate([gk_bias_fwd, gk_bias_bwd]).reshape(1, -1)
    gk_b = gk_cat[:, GLA_KEY_WIDTH:]
    gbias_b = gk_bias_bwd.reshape(1, -1)

    pool_w2 = pool_w.reshape(POOL_GROUPS * POOL_GROUP_WIDTH, POOL_GROUP_WIDTH)
    sb, k_all, v_all, h_all, w_bf, wpool_bf, wpp_bf, wpgla_bf, wout_bf = _prep_call(
        x2, mod, g_pre.reshape(1, -1), gk_b, gbias_b, w_in.T, pool_w2, w_proj_pool, w_proj_gla, w_out)

    small = (pool_scale.reshape(1, -1), gk_cat, gbias_cat, gla_norm_g.reshape(1, -1))
    return _main_call(x2, h_all, mod, g_post.reshape(1, -1), k_all, v_all, w_bf, small,
                      wpool_bf, wpp_bf, wpgla_bf, wout_bf, sb)


def kernel(x, c, w_ada, b_ada, g_pre, g_post, w_in, pool_w, pool_scale, gk_up_fwd, gk_bias_fwd, gk_up_bwd, gk_bias_bwd, gla_norm_g, w_proj_pool, w_proj_gla, w_out):
    b, s, d = x.shape
    depth = w_in.shape[0]
    xf = x.reshape(b * s, d)
    outs = []
    for bi in range(b):
        xb = xf if b == 1 else lax.slice_in_dim(xf, bi * s, (bi + 1) * s, axis=0)
        for l in range(depth):
            xb = _layer(xb, c[bi:bi + 1], w_ada[l], b_ada[l], g_pre[l], g_post[l], w_in[l], pool_w[l],
                        pool_scale[l], gk_up_fwd[l], gk_bias_fwd[l], gk_up_bwd[l], gk_bias_bwd[l],
                        gla_norm_g[l], w_proj_pool[l], w_proj_gla[l], w_out[l])
        outs.append(xb)
    out = outs[0] if b == 1 else jnp.concatenate(outs, axis=0)
    return out.reshape(b, s, d)
```

```python
import functools

import jax
import jax.numpy as jnp
from jax import lax
from jax.experimental import pallas as pl
from jax.experimental.pallas import tpu as pltpu

D_MODEL = 1024
POOL_GROUPS = 4
POOL_GROUP_WIDTH = 256
POOL_WINDOWS = (2, 4, 8, 16)
GLA_HEADS = 4
GLA_KEY_WIDTH = 512
GLA_HEAD_K = 128
GLA_HEAD_V = 256
GLA_GATE_RANK = 16
GLA_GATE_NORMALIZER = 16.0
GLA_CHUNK = 64
RMS_EPS = 1e-6
IN_SPLITS = (1024, 1024, 512, 512, 1024, 1024, 16, 16, 1024, 1024)
IN_WIDTH = sum(IN_SPLITS)

LANES = 128
SUB_TILE = 256
PREP_SUBTILES = 2
MAIN_SUBTILES = 2
STAGE_SKEW = 2
HALO = 16
POOL_ROW_BLOCK = 128
VMEM_LIMIT_BYTES = 60 * 1024 * 1024

_OFFS = [sum(IN_SPLITS[:j]) for j in range(len(IN_SPLITS) + 1)]
(PIN_ROWS, PGATE_ROWS, Q_ROWS, K_ROWS, V_ROWS, GGATE_ROWS) = (slice(_OFFS[j], _OFFS[j + 1]) for j in range(6))
LR_ROWS = slice(_OFFS[6], _OFFS[8])
BGP_ROWS = slice(_OFFS[8], _OFFS[9])
BGG_ROWS = slice(_OFFS[9], _OFFS[10])
LR_WIDTH = 2 * GLA_GATE_RANK
WT_SLAB = 240

F32 = jnp.float32
BF16 = jnp.bfloat16


def _dot(a, b):
    return jnp.dot(a, b, preferred_element_type=F32)


def _dot_nt(a, b):
    return lax.dot_general(a, b, (((1,), (1,)), ((), ())), preferred_element_type=F32)


def _dot_tn(a, b):
    return lax.dot_general(a, b, (((0,), (0,)), ((), ())), preferred_element_type=F32)


def _sigmoid(x):
    return 0.5 * jnp.tanh(0.5 * x) + 0.5


def _silu(x):
    return x * _sigmoid(x)


def _log_sigmoid(x):
    return jnp.minimum(x, 0.0) - jnp.log(1.0 + jnp.exp(-jnp.abs(x)))


def _norm_mod(xv, g_pre, scale, shift):
    ms = jnp.mean(xv * xv, axis=-1, keepdims=True)
    hn = (xv * lax.rsqrt(ms + RMS_EPS)) * g_pre
    return hn * (1.0 + scale) + shift


def _split_hi_lo(a):
    hi = a.astype(BF16)
    lo = (a - hi.astype(F32)).astype(BF16)
    return hi, lo


def _run_interleaved(gens, skew):
    live = [True] * len(gens)
    step = 0
    while any(live):
        for j, g in enumerate(gens):
            if live[j] and step >= j * skew:
                try:
                    next(g)
                except StopIteration:
                    live[j] = False
        step += 1


def _ada_kernel(c_ref, w_ref, b_ref, o_ref):
    s = _silu(c_ref[...])
    o_ref[...] = jnp.sum(w_ref[...] * s, axis=0, keepdims=True) + b_ref[...]


def _ada_call(c_col, w_ada, b_ada):
    blk = 4 * LANES
    n_blk = (3 * D_MODEL) // blk
    return pl.pallas_call(
        _ada_kernel,
        grid=(n_blk,),
        in_specs=[
            pl.BlockSpec((D_MODEL, 1), lambda j: (0, 0)),
            pl.BlockSpec((D_MODEL, blk), lambda j: (0, j)),
            pl.BlockSpec((1, blk), lambda j: (0, j)),
        ],
        out_specs=pl.BlockSpec((1, blk), lambda j: (0, j)),
        out_shape=jax.ShapeDtypeStruct((1, 3 * D_MODEL), F32),
        compiler_params=pltpu.CompilerParams(dimension_semantics=("arbitrary",)),
        name="ada_call",
    )(c_col, w_ada, b_ada)


def _prep_stages(sub, refs):
    (x_ref, mod_ref, gpre_ref, gkb_ref, gbias_ref, wk_scr, wv_scr, wlr_scr,
     sb_ref, k_ref, v_ref, h_ref, lr_ref, st_scr) = refs
    t = SUB_TILE
    rows = slice(sub * t, (sub + 1) * t)
    shift = mod_ref[0:1, :]
    scale = mod_ref[1:2, :]

    h = _norm_mod(x_ref[rows, :], gpre_ref[...], scale, shift).astype(BF16)
    h_ref[rows, :] = h
    k = _dot_nt(h, wk_scr[...])
    k_ref[rows, :] = k
    lr = _dot_nt(h, wlr_scr[...]).astype(BF16)
    lr_ref[rows, :] = lr
    yield

    v_bf = _dot_nt(h, wv_scr[...]).astype(BF16)
    v_ref[rows, :] = v_bf
    la = _log_sigmoid(_dot(lr, gkb_ref[...]) + gbias_ref[...]) / GLA_GATE_NORMALIZER
    la_hi, la_lo = _split_hi_lo(la)
    yield

    row = lax.broadcasted_iota(jnp.int32, (t, t), 0)
    col = lax.broadcasted_iota(jnp.int32, (t, t), 1)
    upper = (col >= row).astype(BF16)
    rc = _dot(upper, la_hi) + _dot(upper, la_lo)
    yield

    tot = rc[0:1, :]
    k_end = (k * jnp.exp(tot - rc)).astype(BF16)
    decay = jnp.exp(tot)
    kvs = []
    for hd in range(GLA_HEADS):
        ks = slice(hd * GLA_HEAD_K, (hd + 1) * GLA_HEAD_K)
        vs = slice(hd * GLA_HEAD_V, (hd + 1) * GLA_HEAD_V)
        kvs.append(_dot_tn(v_bf[:, vs], k_end[:, ks]))
    yield

    for hd in range(GLA_HEADS):
        ks = slice(hd * GLA_HEAD_K, (hd + 1) * GLA_HEAD_K)
        st = st_scr[hd]
        sb_ref[sub, hd] = st
        st_scr[hd] = st * decay[:, ks] + kvs[hd]
    yield


def _prep_kernel(x_ref, mod_ref, gpre_ref, gkb_ref, gbias_ref, wk_ref, wv_ref, wlr_ref,
                 win_ref, wpool_ref, wpp_ref, wpgla_ref, wout_ref,
                 sb_ref, k_ref, v_ref, h_ref, lr_ref, winbf_ref, wpoolbf_ref, wppbf_ref, wpglabf_ref, woutbf_ref,
                 st_scr, wk_scr, wv_scr, wlr_scr):
    @pl.when(pl.program_id(0) == 0)
    def _():
        st_scr[...] = jnp.zeros_like(st_scr)
        wk_scr[...] = wk_ref[...].astype(BF16)
        wv_scr[...] = wv_ref[...].astype(BF16)
        wlr_scr[...] = wlr_ref[...].astype(BF16)

    winbf_ref[...] = win_ref[...].astype(BF16)
    wpoolbf_ref[...] = wpool_ref[...].astype(BF16)
    wppbf_ref[...] = wpp_ref[...].astype(BF16)
    wpglabf_ref[...] = wpgla_ref[...].astype(BF16)
    woutbf_ref[...] = wout_ref[...].astype(BF16)

    refs = (x_ref, mod_ref, gpre_ref, gkb_ref, gbias_ref, wk_scr, wv_scr, wlr_scr,
            sb_ref, k_ref, v_ref, h_ref, lr_ref, st_scr)
    gens = [_prep_stages(sub, refs) for sub in reversed(range(PREP_SUBTILES))]
    _run_interleaved(gens, STAGE_SKEW)


def _prep_call(x2, mod, g_pre, gk_b, gbias_b, w_in_t, pool_w2, w_proj_pool, w_proj_gla, w_out):
    s = x2.shape[0]
    tb = PREP_SUBTILES * SUB_TILE
    nb = s // tb
    slab = D_MODEL // nb
    n_wt = IN_WIDTH // WT_SLAB
    const = lambda i: (0, 0)
    rev = lambda i: (nb - 1 - i, 0)
    fwd = lambda i: (i, 0)
    wt_slab = lambda i: (jnp.minimum(i, n_wt - 1), 0)
    one = pl.Buffered(1)

    def rows_of(rs):
        n = rs.stop - rs.start
        return pl.BlockSpec((n, D_MODEL), lambda i: (rs.start // n, 0), pipeline_mode=one)

    return pl.pallas_call(
        _prep_kernel,
        grid=(nb,),
        in_specs=[
            pl.BlockSpec((tb, D_MODEL), rev),
            pl.BlockSpec((3, D_MODEL), const),
            pl.BlockSpec((1, D_MODEL), const),
            pl.BlockSpec(gk_b.shape, const),
            pl.BlockSpec(gbias_b.shape, const),
            rows_of(K_ROWS),
            rows_of(V_ROWS),
            rows_of(LR_ROWS),
            pl.BlockSpec((WT_SLAB, D_MODEL), wt_slab),
            pl.BlockSpec((slab, POOL_GROUP_WIDTH), fwd),
            pl.BlockSpec((slab, D_MODEL), fwd),
            pl.BlockSpec((slab, D_MODEL), fwd),
            pl.BlockSpec((slab, D_MODEL), fwd),
        ],
        out_specs=[
            pl.BlockSpec((PREP_SUBTILES, GLA_HEADS, GLA_HEAD_V, GLA_HEAD_K), lambda i: (nb - 1 - i, 0, 0, 0)),
            pl.BlockSpec((tb, GLA_KEY_WIDTH), rev),
            pl.BlockSpec((tb, D_MODEL), rev),
            pl.BlockSpec((tb, D_MODEL), rev),
            pl.BlockSpec((tb, LR_WIDTH), rev),
            pl.BlockSpec((WT_SLAB, D_MODEL), wt_slab),
            pl.BlockSpec((slab, POOL_GROUP_WIDTH), fwd),
            pl.BlockSpec((slab, D_MODEL), fwd),
            pl.BlockSpec((slab, D_MODEL), fwd),
            pl.BlockSpec((slab, D_MODEL), fwd),
        ],
        out_shape=[
            jax.ShapeDtypeStruct((s // SUB_TILE, GLA_HEADS, GLA_HEAD_V, GLA_HEAD_K), F32),
            jax.ShapeDtypeStruct((s, GLA_KEY_WIDTH), F32),
            jax.ShapeDtypeStruct((s, D_MODEL), BF16),
            jax.ShapeDtypeStruct((s, D_MODEL), BF16),
            jax.ShapeDtypeStruct((s, LR_WIDTH), BF16),
            jax.ShapeDtypeStruct((IN_WIDTH, D_MODEL), BF16),
            jax.ShapeDtypeStruct(pool_w2.shape, BF16),
            jax.ShapeDtypeStruct((D_MODEL, D_MODEL), BF16),
            jax.ShapeDtypeStruct((D_MODEL, D_MODEL), BF16),
            jax.ShapeDtypeStruct((D_MODEL, D_MODEL), BF16),
        ],
        scratch_shapes=[
            pltpu.VMEM((GLA_HEADS, GLA_HEAD_V, GLA_HEAD_K), F32),
            pltpu.VMEM((GLA_KEY_WIDTH, D_MODEL), BF16),
            pltpu.VMEM((D_MODEL, D_MODEL), BF16),
            pltpu.VMEM((LR_WIDTH, D_MODEL), BF16),
        ],
        compiler_params=pltpu.CompilerParams(dimension_semantics=("arbitrary",),
                                             vmem_limit_bytes=VMEM_LIMIT_BYTES),
        name="prep_call",
    )(x2, mod, g_pre, gk_b, gbias_b, w_in_t, w_in_t, w_in_t, w_in_t, pool_w2, w_proj_pool, w_proj_gla, w_out)


def _subtile_stages(seq_len, sub, refs):
    (x_ref, mod_ref, gpost_ref, k_ref, v_ref, lr_ref,
     wt_ref,
     poolw_ref, pscale_ref, gkcat_ref, gbias_ref, glag_ref,
     wpp_ref, wpgla_ref, wout_ref, sb_ref, o_ref,
     sf_scr, hext_scr, q_scr, cf_scr, rb_scr, qd_scr, ke_scr, kif_scr, kib_scr,
     kv_scr, scat_scr, sc_scr, og_scr) = refs
    t = SUB_TILE
    r0 = sub * t
    rows = slice(r0, r0 + t)
    tile_row0 = pl.program_id(0) * (MAIN_SUBTILES * t) + r0
    n_chunks = t // GLA_CHUNK
    hk, hv = GLA_HEAD_K, GLA_HEAD_V
    gate = mod_ref[2:3, :]

    h_ext = hext_scr[r0:r0 + t + 2 * HALO, :]
    h = hext_scr[r0 + HALO:r0 + HALO + t, :]

    q_scr[rows, :] = _dot_nt(h, wt_ref[Q_ROWS, :]) * (GLA_HEAD_K ** -0.5)
    la_pre = _dot(lr_ref[rows, :], gkcat_ref[...]) + gbias_ref[...]
    te = t + 2 * HALO
    g_row = tile_row0 - HALO + lax.broadcasted_iota(jnp.int32, (te, 1), 0)
    valid = jnp.logical_and(g_row >= 0, g_row < seq_len)
    u_ext = jnp.where(valid, _dot_nt(h_ext, wt_ref[PIN_ROWS, :]), 0.0)
    yield

    la = _log_sigmoid(la_pre) / GLA_GATE_NORMALIZER
    p_gate = _dot_nt(h, wt_ref[PGATE_ROWS, :])
    u_ext_bf = u_ext.astype(BF16)
    u = u_ext[HALO:HALO + t, :]
    laf_hi, laf_lo = _split_hi_lo(la[:, :GLA_KEY_WIDTH])
    lab_hi, lab_lo = _split_hi_lo(la[:, GLA_KEY_WIDTH:])
    yield

    row = lax.broadcasted_iota(jnp.int32, (t, t), 0)
    col = lax.broadcasted_iota(jnp.int32, (t, t), 1)
    same_chunk = (row // GLA_CHUNK) == (col // GLA_CHUNK)
    lower = jnp.logical_and(same_chunk, col <= row).astype(BF16)
    upper = jnp.logical_and(same_chunk, col >= row).astype(BF16)
    cf_scr[rows, :] = _dot(lower, laf_hi) + _dot(lower, laf_lo)
    rb_scr[rows, :] = _dot(upper, lab_hi) + _dot(upper, lab_lo)
    bg_pool = _sigmoid(_dot_nt(h, wt_ref[BGP_ROWS, :]))
    yield

    pos = tile_row0 + lax.broadcasted_iota(jnp.int32, (t, 1), 0)
    rb_ = POOL_ROW_BLOCK
    kb_ = POOL_ROW_BLOCK + 2 * HALO
    band_r = lax.broadcasted_iota(jnp.int32, (rb_, kb_), 0)
    band_c = lax.broadcasted_iota(jnp.int32, (rb_, kb_), 1)
    band_d = band_c - HALO - band_r
    a_parts = []
    for gi, w in enumerate(POOL_WINDOWS):
        hw = w // 2
        cs = slice(gi * POOL_GROUP_WIDTH, (gi + 1) * POOL_GROUP_WIDTH)
        ws = slice(gi * POOL_GROUP_WIDTH, (gi + 1) * POOL_GROUP_WIDTH)
        band = jnp.logical_and(band_d >= -hw, band_d < hw).astype(BF16)
        wsum = jnp.concatenate(
            [_dot(band, u_ext_bf[b * rb_:b * rb_ + kb_, cs]) for b in range(t // rb_)], axis=0)
        count = (jnp.minimum(pos + hw, seq_len) - jnp.maximum(pos - hw, 0)).astype(F32)
        pooled = wsum / count - u[:, cs]
        a_parts.append(_dot(pooled.astype(BF16), poolw_ref[ws, :]))
    a = jnp.concatenate(a_parts, axis=1) * pscale_ref[...]
    z_pool = (a * _silu(p_gate)).astype(BF16)
    gla_gate = _dot_nt(h, wt_ref[GGATE_ROWS, :])
    bg_gla = _sigmoid(_dot_nt(h, wt_ref[BGG_ROWS, :]))

    dec_f, dec_b = [], []
    for c in range(n_chunks):
        rs = slice(r0 + c * GLA_CHUNK, r0 + (c + 1) * GLA_CHUNK)
        last = r0 + (c + 1) * GLA_CHUNK - 1
        first = r0 + c * GLA_CHUNK
        qc = q_scr[rs, :]
        kc = k_ref[rs, :]
        cum_f = cf_scr[rs, :]
        tot_f = cf_scr[last:last + 1, :]
        cum_b = rb_scr[rs, :]
        tot_b = rb_scr[first:first + 1, :]
        qd_f = (qc * jnp.exp(cum_f)).astype(BF16)
        qd_b = (qc * jnp.exp(cum_b)).astype(BF16)
        ke_f = (kc * jnp.exp(tot_f - cum_f)).astype(BF16)
        ke_b = (kc * jnp.exp(tot_b - cum_b)).astype(BF16)
        kif_scr[rs, :] = (kc * jnp.exp(-cum_f)).astype(BF16)
        kib_scr[rs, :] = (kc * jnp.exp(-cum_b)).astype(BF16)
        for hd in range(GLA_HEADS):
            ks = slice(hd * hk, (hd + 1) * hk)
            qd_scr[rs, 2 * hd * hk:(2 * hd + 1) * hk] = qd_f[:, ks]
            qd_scr[rs, (2 * hd + 1) * hk:(2 * hd + 2) * hk] = qd_b[:, ks]
            ke_scr[rs, 2 * hd * hk:(2 * hd + 1) * hk] = ke_f[:, ks]
            ke_scr[rs, (2 * hd + 1) * hk:(2 * hd + 2) * hk] = ke_b[:, ks]
        dec_f.append(jnp.exp(tot_f))
        dec_b.append(jnp.exp(tot_b))
    yield

    merged = bg_pool * _dot(z_pool, wpp_ref[...])
    for hd in range(GLA_HEADS):
        vs = slice(hd * hv, (hd + 1) * hv)
        cat = slice(2 * hd * hk, (2 * hd + 2) * hk)
        for c in range(n_chunks):
            rs = slice(r0 + c * GLA_CHUNK, r0 + (c + 1) * GLA_CHUNK)
            kv_scr[sub, hd, c] = _dot_tn(v_ref[rs, vs], ke_scr[rs, cat])
    yield

    crow = lax.broadcasted_iota(jnp.int32, (GLA_CHUNK, GLA_CHUNK), 0)
    ccol = lax.broadcasted_iota(jnp.int32, (GLA_CHUNK, GLA_CHUNK), 1)
    mask_f = ccol <= crow
    for c in range(n_chunks):
        rs = slice(r0 + c * GLA_CHUNK, r0 + (c + 1) * GLA_CHUNK)
        for hd in range(GLA_HEADS):
            ks = slice(hd * hk, (hd + 1) * hk)
            sc_f = _dot_nt(qd_scr[rs, 2 * hd * hk:(2 * hd + 1) * hk], kif_scr[rs, ks])
            sc_b = _dot_nt(qd_scr[rs, (2 * hd + 1) * hk:(2 * hd + 2) * hk], kib_scr[rs, ks])
            sc_scr[sub, c, hd] = jnp.where(mask_f, sc_f, sc_b).astype(BF16)
    for hd in range(GLA_HEADS):
        ks = slice(hd * hk, (hd + 1) * hk)
        sf = sf_scr[hd]
        for c in range(n_chunks):
            scat_scr[sub, c, hd, :, 0:hk] = sf.astype(BF16)
            sf = sf * dec_f[c][:, ks] + kv_scr[sub, hd, c, :, 0:hk]
        sf_scr[hd] = sf
        sb = sb_ref[sub, hd]
        for c in reversed(range(n_chunks)):
            scat_scr[sub, c, hd, :, hk:2 * hk] = sb.astype(BF16)
            if c > 0:
                sb = sb * dec_b[c][:, ks] + kv_scr[sub, hd, c, :, hk:2 * hk]
    yield

    for c in range(n_chunks):
        rs = slice(r0 + c * GLA_CHUNK, r0 + (c + 1) * GLA_CHUNK)
        for hd in range(GLA_HEADS):
            vs = slice(hd * hv, (hd + 1) * hv)
            cat = slice(2 * hd * hk, (2 * hd + 2) * hk)
            og_scr[rs, vs] = (_dot(sc_scr[sub, c, hd], v_ref[rs, vs])
                              + _dot_nt(qd_scr[rs, cat], scat_scr[sub, c, hd]))
    yield

    glag = glag_ref[...]
    z_parts = []
    for hd in range(GLA_HEADS):
        vs = slice(hd * hv, (hd + 1) * hv)
        oh = og_scr[rows, vs]
        ms = jnp.mean(oh * oh, axis=-1, keepdims=True)
        z_parts.append((oh * lax.rsqrt(ms + RMS_EPS)) * glag)
    o_n = jnp.concatenate(z_parts, axis=1)
    y_gla = _dot((o_n * _silu(gla_gate)).astype(BF16), wpgla_ref[...])
    merged = merged + bg_gla * y_gla
    yield

    out = _dot(merged.astype(BF16), wout_ref[...])
    yield

    ms = jnp.mean(out * out, axis=-1, keepdims=True)
    out_n = (out * lax.rsqrt(ms + RMS_EPS)) * gpost_ref[...]
    o_ref[rows, :] = x_ref[rows, :] + gate * out_n
    yield


def _main_kernel(seq_len,
                 x_ref, h_ref, hp_ref, hn_ref, mod_ref, gpost_ref, k_ref, v_ref, lr_ref,
                 wt_ref,
                 poolw_ref, pscale_ref, gkcat_ref, gbias_ref, glag_ref,
                 wpp_ref, wpgla_ref, wout_ref, sb_ref,
                 o_ref,
                 sf_scr, hext_scr, q_scr, cf_scr, rb_scr, qd_scr, ke_scr, kif_scr, kib_scr,
                 kv_scr, scat_scr, sc_scr, og_scr):
    tb = MAIN_SUBTILES * SUB_TILE

    @pl.when(pl.program_id(0) == 0)
    def _():
        sf_scr[...] = jnp.zeros_like(sf_scr)

    hext_scr[0:HALO, :] = hp_ref[...]
    hext_scr[HALO:HALO + tb, :] = h_ref[...]
    hext_scr[HALO + tb:HALO + tb + HALO, :] = hn_ref[...]

    refs = (x_ref, mod_ref, gpost_ref, k_ref, v_ref, lr_ref,
            wt_ref,
            poolw_ref, pscale_ref, gkcat_ref, gbias_ref, glag_ref,
            wpp_ref, wpgla_ref, wout_ref, sb_ref, o_ref,
            sf_scr, hext_scr, q_scr, cf_scr, rb_scr, qd_scr, ke_scr, kif_scr, kib_scr,
            kv_scr, scat_scr, sc_scr, og_scr)
    gens = [_subtile_stages(seq_len, sub, refs) for sub in range(MAIN_SUBTILES)]
    _run_interleaved(gens, STAGE_SKEW)


def _main_call(x2, h_all, mod, g_post, k_all, v_all, lr_all, w_bf, small, wpool_bf, wpp_bf, wpgla_bf, wout_bf, sb):
    s = x2.shape[0]
    tb = MAIN_SUBTILES * SUB_TILE
    nt = s // tb
    hb = tb // HALO
    n_hb = s // HALO
    n_chunks = SUB_TILE // GLA_CHUNK
    const2 = lambda i: (0, 0)
    tile = lambda i: (i, 0)
    one = pl.Buffered(1)
    pool_scale, gk_cat, gbias_cat, gla_norm_g = small

    def resident(arr):
        return pl.BlockSpec(arr.shape, const2, pipeline_mode=one)

    in_specs = [
        pl.BlockSpec((tb, D_MODEL), tile),
        pl.BlockSpec((tb, D_MODEL), tile),
        pl.BlockSpec((HALO, D_MODEL), lambda i: (jnp.maximum(i * hb - 1, 0), 0)),
        pl.BlockSpec((HALO, D_MODEL), lambda i: (jnp.minimum((i + 1) * hb, n_hb - 1), 0)),
        pl.BlockSpec((3, D_MODEL), const2),
        pl.BlockSpec((1, D_MODEL), const2),
        pl.BlockSpec((tb, GLA_KEY_WIDTH), tile),
        pl.BlockSpec((tb, D_MODEL), tile),
        pl.BlockSpec((tb, LR_WIDTH), tile),
        resident(w_bf),
        resident(wpool_bf), resident(pool_scale), resident(gk_cat), resident(gbias_cat), resident(gla_norm_g),
        resident(wpp_bf), resident(wpgla_bf), resident(wout_bf),
        pl.BlockSpec((MAIN_SUBTILES, GLA_HEADS, GLA_HEAD_V, GLA_HEAD_K), lambda i: (i, 0, 0, 0)),
    ]
    scratch = [
        pltpu.VMEM((GLA_HEADS, GLA_HEAD_V, GLA_HEAD_K), F32),
        pltpu.VMEM((tb + 2 * HALO, D_MODEL), BF16),
        pltpu.VMEM((tb, GLA_KEY_WIDTH), F32),
        pltpu.VMEM((tb, GLA_KEY_WIDTH), F32),
        pltpu.VMEM((tb, GLA_KEY_WIDTH), F32),
        pltpu.VMEM((tb, 2 * GLA_KEY_WIDTH), BF16),
        pltpu.VMEM((tb, 2 * GLA_KEY_WIDTH), BF16),
        pltpu.VMEM((tb, GLA_KEY_WIDTH), BF16),
        pltpu.VMEM((tb, GLA_KEY_WIDTH), BF16),
        pltpu.VMEM((MAIN_SUBTILES, GLA_HEADS, n_chunks, GLA_HEAD_V, 2 * GLA_HEAD_K), F32),
        pltpu.VMEM((MAIN_SUBTILES, n_chunks, GLA_HEADS, GLA_HEAD_V, 2 * GLA_HEAD_K), BF16),
        pltpu.VMEM((MAIN_SUBTILES, n_chunks, GLA_HEADS, GLA_CHUNK, GLA_CHUNK), BF16),
        pltpu.VMEM((tb, D_MODEL), F32),
    ]
    return pl.pallas_call(
        functools.partial(_main_kernel, s),
        grid=(nt,),
        in_specs=in_specs,
        out_specs=pl.BlockSpec((tb, D_MODEL), tile),
        out_shape=jax.ShapeDtypeStruct((s, D_MODEL), F32),
        scratch_shapes=scratch,
        compiler_params=pltpu.CompilerParams(dimension_semantics=("arbitrary",),
                                             vmem_limit_bytes=VMEM_LIMIT_BYTES),
        name="main_call",
    )(x2, h_all, h_all, h_all, mod, g_post, k_all, v_all, lr_all,
      w_bf,
      wpool_bf, pool_scale, gk_cat, gbias_cat, gla_norm_g, wpp_bf, wpgla_bf, wout_bf, sb)


def _layer(x2, c, w_ada, b_ada, g_pre, g_post, w_in, pool_w, pool_scale, gk_up_fwd, gk_bias_fwd,
           gk_up_bwd, gk_bias_bwd, gla_norm_g, w_proj_pool, w_proj_gla, w_out):
    ada = _ada_call(c.reshape(D_MODEL, 1), w_ada, b_ada.reshape(1, -1))
    mod = ada.reshape(3, D_MODEL)

    zeros = jnp.zeros((GLA_GATE_RANK, GLA_KEY_WIDTH), F32)
    gk_cat = jnp.concatenate([jnp.concatenate([gk_up_fwd, zeros], axis=1),
                              jnp.concatenate([zeros, gk_up_bwd], axis=1)], axis=0).astype(BF16)
    gbias_cat = jnp.concatenate([gk_bias_fwd, gk_bias_bwd]).reshape(1, -1)
    gk_b = gk_cat[:, GLA_KEY_WIDTH:]
    gbias_b = gk_bias_bwd.reshape(1, -1)

    pool_w2 = pool_w.reshape(POOL_GROUPS * POOL_GROUP_WIDTH, POOL_GROUP_WIDTH)
    sb, k_all, v_all, h_all, lr_all, w_bf, wpool_bf, wpp_bf, wpgla_bf, wout_bf = _prep_call(
        x2, mod, g_pre.reshape(1, -1), gk_b, gbias_b, w_in.T, pool_w2, w_proj_pool, w_proj_gla, w_out)

    small = (pool_scale.reshape(1, -1), gk_cat, gbias_cat, gla_norm_g.reshape(1, -1))
    return _main_call(x2, h_all, mod, g_post.reshape(1, -1), k_all, v_all, lr_all, w_bf, small,
                      wpool_bf, wpp_bf, wpgla_bf, wout_bf, sb)


def kernel(x, c, w_ada, b_ada, g_pre, g_post, w_in, pool_w, pool_scale, gk_up_fwd, gk_bias_fwd, gk_up_bwd, gk_bias_bwd, gla_norm_g, w_proj_pool, w_proj_gla, w_out):
    b, s, d = x.shape
    depth = w_in.shape[0]
    xf = x.reshape(b * s, d)
    outs = []
    for bi in range(b):
        xb = xf if b == 1 else lax.slice_in_dim(xf, bi * s, (bi + 1) * s, axis=0)
        for l in range(depth):
            xb = _layer(xb, c[bi:bi + 1], w_ada[l], b_ada[l], g_pre[l], g_post[l], w_in[l], pool_w[l],
                        pool_scale[l], gk_up_fwd[l], gk_bias_fwd[l], gk_up_bwd[l], gk_bias_bwd[l],
                        gla_norm_g[l], w_proj_pool[l], w_proj_gla[l], w_out[l])
        outs.append(xb)
    out = outs[0] if b == 1 else jnp.concatenate(outs, axis=0)
    return out.reshape(b, s, d)
```

```python
import functools

import jax
import jax.numpy as jnp
from jax import lax
from jax.experimental import pallas as pl
from jax.experimental.pallas import tpu as pltpu

D_MODEL = 1024
POOL_GROUPS = 4
POOL_GROUP_WIDTH = 256
POOL_WINDOWS = (2, 4, 8, 16)
GLA_HEADS = 4
GLA_KEY_WIDTH = 512
GLA_HEAD_K = 128
GLA_HEAD_V = 256
GLA_GATE_RANK = 16
GLA_GATE_NORMALIZER = 16.0
GLA_CHUNK = 64
GLA_PAIR = 2 * GLA_CHUNK
RMS_EPS = 1e-6
IN_SPLITS = (1024, 1024, 512, 512, 1024, 1024, 16, 16, 1024, 1024)
IN_WIDTH = sum(IN_SPLITS)

LANES = 128
SUB_TILE = 256
PREP_SUBTILES = 2
MAIN_TILE = 512
MAIN_SUBTILES = 1
CUMSUM_BLOCK = 256
STAGE_SKEW = 2
HALO = 16
POOL_ROW_BLOCK = 128
VMEM_LIMIT_BYTES = 60 * 1024 * 1024

_OFFS = [sum(IN_SPLITS[:j]) for j in range(len(IN_SPLITS) + 1)]
(PIN_ROWS, PGATE_ROWS, Q_ROWS, K_ROWS, V_ROWS, GGATE_ROWS) = (slice(_OFFS[j], _OFFS[j + 1]) for j in range(6))
LR_ROWS = slice(_OFFS[6], _OFFS[8])
BGP_ROWS = slice(_OFFS[8], _OFFS[9])
BGG_ROWS = slice(_OFFS[9], _OFFS[10])
LR_WIDTH = 2 * GLA_GATE_RANK
WT_SLAB = 240

F32 = jnp.float32
BF16 = jnp.bfloat16


def _dot(a, b):
    return jnp.dot(a, b, preferred_element_type=F32)


def _dot_nt(a, b):
    return lax.dot_general(a, b, (((1,), (1,)), ((), ())), preferred_element_type=F32)


def _dot_tn(a, b):
    return lax.dot_general(a, b, (((0,), (0,)), ((), ())), preferred_element_type=F32)


def _sigmoid(x):
    return 0.5 * jnp.tanh(0.5 * x) + 0.5


def _silu(x):
    return x * _sigmoid(x)


def _log_sigmoid(x):
    return jnp.minimum(x, 0.0) - jnp.log(1.0 + jnp.exp(-jnp.abs(x)))


def _norm_mod(xv, g_pre, scale, shift):
    ms = jnp.mean(xv * xv, axis=-1, keepdims=True)
    hn = (xv * lax.rsqrt(ms + RMS_EPS)) * g_pre
    return hn * (1.0 + scale) + shift


def _split_hi_lo(a):
    hi = a.astype(BF16)
    lo = (a - hi.astype(F32)).astype(BF16)
    return hi, lo


def _run_interleaved(gens, skew):
    live = [True] * len(gens)
    step = 0
    while any(live):
        for j, g in enumerate(gens):
            if live[j] and step >= j * skew:
                try:
                    next(g)
                except StopIteration:
                    live[j] = False
        step += 1


def _ada_kernel(c_ref, w_ref, b_ref, o_ref):
    s = _silu(c_ref[...])
    o_ref[...] = jnp.sum(w_ref[...] * s, axis=0, keepdims=True) + b_ref[...]


def _ada_call(c_col, w_ada, b_ada):
    blk = 4 * LANES
    n_blk = (3 * D_MODEL) // blk
    return pl.pallas_call(
        _ada_kernel,
        grid=(n_blk,),
        in_specs=[
            pl.BlockSpec((D_MODEL, 1), lambda j: (0, 0)),
            pl.BlockSpec((D_MODEL, blk), lambda j: (0, j)),
            pl.BlockSpec((1, blk), lambda j: (0, j)),
        ],
        out_specs=pl.BlockSpec((1, blk), lambda j: (0, j)),
        out_shape=jax.ShapeDtypeStruct((1, 3 * D_MODEL), F32),
        compiler_params=pltpu.CompilerParams(dimension_semantics=("arbitrary",)),
        name="ada_call",
    )(c_col, w_ada, b_ada)


def _prep_stages(sub, refs):
    (x_ref, mod_ref, gpre_ref, gkb_ref, gbias_ref, wk_scr, wv_scr, wlr_scr,
     sb_ref, k_ref, v_ref, h_ref, lr_ref, st_scr) = refs
    t = SUB_TILE
    rows = slice(sub * t, (sub + 1) * t)
    k = k_ref[rows, :]
    v_bf = v_ref[rows, :]

    la = _log_sigmoid(_dot(lr_ref[rows, :], gkb_ref[...]) + gbias_ref[...]) / GLA_GATE_NORMALIZER
    la_hi, la_lo = _split_hi_lo(la)
    yield

    row = lax.broadcasted_iota(jnp.int32, (t, t), 0)
    col = lax.broadcasted_iota(jnp.int32, (t, t), 1)
    upper = (col >= row).astype(BF16)
    rc = _dot(upper, la_hi) + _dot(upper, la_lo)
    yield

    tot = rc[0:1, :]
    k_end = (k * jnp.exp(tot - rc)).astype(BF16)
    decay = jnp.exp(tot)
    kvs = []
    for hd in range(GLA_HEADS):
        ks = slice(hd * GLA_HEAD_K, (hd + 1) * GLA_HEAD_K)
        vs = slice(hd * GLA_HEAD_V, (hd + 1) * GLA_HEAD_V)
        kvs.append(_dot_tn(v_bf[:, vs], k_end[:, ks]))
    yield

    for hd in range(GLA_HEADS):
        ks = slice(hd * GLA_HEAD_K, (hd + 1) * GLA_HEAD_K)
        st = st_scr[hd]
        sb_ref[sub, hd] = st
        st_scr[hd] = st * decay[:, ks] + kvs[hd]
    yield


def _prep_kernel(x_ref, mod_ref, gpre_ref, gkb_ref, gbias_ref, wk_ref, wv_ref, wlr_ref,
                 win_ref, wpool_ref, wpp_ref, wpgla_ref, wout_ref,
                 sb_ref, k_ref, v_ref, h_ref, lr_ref, winbf_ref, wpoolbf_ref, wppbf_ref, wpglabf_ref, woutbf_ref,
                 st_scr, wk_scr, wv_scr, wlr_scr):
    @pl.when(pl.program_id(0) == 0)
    def _():
        st_scr[...] = jnp.zeros_like(st_scr)
        wk_scr[...] = wk_ref[...].astype(BF16)
        wv_scr[...] = wv_ref[...].astype(BF16)
        wlr_scr[...] = wlr_ref[...].astype(BF16)

    winbf_ref[...] = win_ref[...].astype(BF16)
    wpoolbf_ref[...] = wpool_ref[...].astype(BF16)
    wppbf_ref[...] = wpp_ref[...].astype(BF16)
    wpglabf_ref[...] = wpgla_ref[...].astype(BF16)
    woutbf_ref[...] = wout_ref[...].astype(BF16)

    h = _norm_mod(x_ref[...], gpre_ref[...], mod_ref[1:2, :], mod_ref[0:1, :]).astype(BF16)
    h_ref[...] = h
    k_ref[...] = _dot_nt(h, wk_scr[...])
    lr_ref[...] = _dot_nt(h, wlr_scr[...]).astype(BF16)
    v_ref[...] = _dot_nt(h, wv_scr[...]).astype(BF16)

    refs = (x_ref, mod_ref, gpre_ref, gkb_ref, gbias_ref, wk_scr, wv_scr, wlr_scr,
            sb_ref, k_ref, v_ref, h_ref, lr_ref, st_scr)
    gens = [_prep_stages(sub, refs) for sub in reversed(range(PREP_SUBTILES))]
    _run_interleaved(gens, STAGE_SKEW)


def _prep_call(x2, mod, g_pre, gk_b, gbias_b, w_in_t, pool_w2, w_proj_pool, w_proj_gla, w_out):
    s = x2.shape[0]
    tb = PREP_SUBTILES * SUB_TILE
    nb = s // tb
    slab = D_MODEL // nb
    n_wt = IN_WIDTH // WT_SLAB
    const = lambda i: (0, 0)
    rev = lambda i: (nb - 1 - i, 0)
    fwd = lambda i: (i, 0)
    wt_slab = lambda i: (jnp.minimum(i, n_wt - 1), 0)
    one = pl.Buffered(1)

    def rows_of(rs):
        n = rs.stop - rs.start
        return pl.BlockSpec((n, D_MODEL), lambda i: (rs.start // n, 0), pipeline_mode=one)

    return pl.pallas_call(
        _prep_kernel,
        grid=(nb,),
        in_specs=[
            pl.BlockSpec((tb, D_MODEL), rev),
            pl.BlockSpec((3, D_MODEL), const),
            pl.BlockSpec((1, D_MODEL), const),
            pl.BlockSpec(gk_b.shape, const),
            pl.BlockSpec(gbias_b.shape, const),
            rows_of(K_ROWS),
            rows_of(V_ROWS),
            rows_of(LR_ROWS),
            pl.BlockSpec((WT_SLAB, D_MODEL), wt_slab),
            pl.BlockSpec((slab, POOL_GROUP_WIDTH), fwd),
            pl.BlockSpec((slab, D_MODEL), fwd),
            pl.BlockSpec((slab, D_MODEL), fwd),
            pl.BlockSpec((slab, D_MODEL), fwd),
        ],
        out_specs=[
            pl.BlockSpec((PREP_SUBTILES, GLA_HEADS, GLA_HEAD_V, GLA_HEAD_K), lambda i: (nb - 1 - i, 0, 0, 0)),
            pl.BlockSpec((tb, GLA_KEY_WIDTH), rev),
            pl.BlockSpec((tb, D_MODEL), rev),
            pl.BlockSpec((tb, D_MODEL), rev),
            pl.BlockSpec((tb, LR_WIDTH), rev),
            pl.BlockSpec((WT_SLAB, D_MODEL), wt_slab),
            pl.BlockSpec((slab, POOL_GROUP_WIDTH), fwd),
            pl.BlockSpec((slab, D_MODEL), fwd),
            pl.BlockSpec((slab, D_MODEL), fwd),
            pl.BlockSpec((slab, D_MODEL), fwd),
        ],
        out_shape=[
            jax.ShapeDtypeStruct((s // SUB_TILE, GLA_HEADS, GLA_HEAD_V, GLA_HEAD_K), F32),
            jax.ShapeDtypeStruct((s, GLA_KEY_WIDTH), F32),
            jax.ShapeDtypeStruct((s, D_MODEL), BF16),
            jax.ShapeDtypeStruct((s, D_MODEL), BF16),
            jax.ShapeDtypeStruct((s, LR_WIDTH), BF16),
            jax.ShapeDtypeStruct((IN_WIDTH, D_MODEL), BF16),
            jax.ShapeDtypeStruct(pool_w2.shape, BF16),
            jax.ShapeDtypeStruct((D_MODEL, D_MODEL), BF16),
            jax.ShapeDtypeStruct((D_MODEL, D_MODEL), BF16),
            jax.ShapeDtypeStruct((D_MODEL, D_MODEL), BF16),
        ],
        scratch_shapes=[
            pltpu.VMEM((GLA_HEADS, GLA_HEAD_V, GLA_HEAD_K), F32),
            pltpu.VMEM((GLA_KEY_WIDTH, D_MODEL), BF16),
            pltpu.VMEM((D_MODEL, D_MODEL), BF16),
            pltpu.VMEM((LR_WIDTH, D_MODEL), BF16),
        ],
        compiler_params=pltpu.CompilerParams(dimension_semantics=("arbitrary",),
                                             vmem_limit_bytes=VMEM_LIMIT_BYTES),
        name="prep_call",
    )(x2, mod, g_pre, gk_b, gbias_b, w_in_t, w_in_t, w_in_t, w_in_t, pool_w2, w_proj_pool, w_proj_gla, w_out)


def _subtile_stages(seq_len, sub, refs):
    (x_ref, mod_ref, gpost_ref, k_ref, v_ref, lr_ref,
     wt_ref,
     poolw_ref, pscale_ref, gkcat_ref, gbias_ref, glag_ref,
     wpp_ref, wpgla_ref, wout_ref, sb_ref, o_ref,
     sf_scr, hext_scr, q_scr, cf_scr, rb_scr, qs_scr, qd_scr, ke_scr, kif_scr, kib_scr,
     kv_scr, scat_scr, sc_scr, og_scr) = refs
    t = MAIN_TILE
    r0 = sub * t
    rows = slice(r0, r0 + t)
    tile_row0 = pl.program_id(0) * (MAIN_SUBTILES * t) + r0
    sb_idx = (sub + 1) * (MAIN_TILE // SUB_TILE) - 1
    n_pairs = t // GLA_PAIR
    hk, hv = GLA_HEAD_K, GLA_HEAD_V
    gate = mod_ref[2:3, :]

    h_ext = hext_scr[r0:r0 + t + 2 * HALO, :]
    h = hext_scr[r0 + HALO:r0 + HALO + t, :]

    q_scr[rows, :] = _dot_nt(h, wt_ref[Q_ROWS, :]) * (GLA_HEAD_K ** -0.5)
    la_pre = _dot(lr_ref[rows, :], gkcat_ref[...]) + gbias_ref[...]
    te = t + 2 * HALO
    g_row = tile_row0 - HALO + lax.broadcasted_iota(jnp.int32, (te, 1), 0)
    valid = jnp.logical_and(g_row >= 0, g_row < seq_len)
    u_ext = jnp.where(valid, _dot_nt(h_ext, wt_ref[PIN_ROWS, :]), 0.0)
    yield

    la = _log_sigmoid(la_pre) / GLA_GATE_NORMALIZER
    p_gate = _dot_nt(h, wt_ref[PGATE_ROWS, :])
    u_ext_bf = u_ext.astype(BF16)
    u = u_ext[HALO:HALO + t, :]
    laf_hi, laf_lo = _split_hi_lo(la[:, :GLA_KEY_WIDTH])
    lab_hi, lab_lo = _split_hi_lo(la[:, GLA_KEY_WIDTH:])
    yield

    cb = CUMSUM_BLOCK
    row = lax.broadcasted_iota(jnp.int32, (cb, cb), 0)
    col = lax.broadcasted_iota(jnp.int32, (cb, cb), 1)
    same_chunk = (row // GLA_CHUNK) == (col // GLA_CHUNK)
    lower = jnp.logical_and(same_chunk, col <= row).astype(BF16)
    upper = jnp.logical_and(same_chunk, col >= row).astype(BF16)
    for b in range(t // cb):
        rb_rows = slice(b * cb, (b + 1) * cb)
        dst = slice(r0 + b * cb, r0 + (b + 1) * cb)
        cf_scr[dst, :] = _dot(lower, laf_hi[rb_rows]) + _dot(lower, laf_lo[rb_rows])
        rb_scr[dst, :] = _dot(upper, lab_hi[rb_rows]) + _dot(upper, lab_lo[rb_rows])
    bg_pool = _sigmoid(_dot_nt(h, wt_ref[BGP_ROWS, :]))
    yield

    pos = tile_row0 + lax.broadcasted_iota(jnp.int32, (t, 1), 0)
    rb_ = POOL_ROW_BLOCK
    kb_ = POOL_ROW_BLOCK + 2 * HALO
    band_r = lax.broadcasted_iota(jnp.int32, (rb_, kb_), 0)
    band_c = lax.broadcasted_iota(jnp.int32, (rb_, kb_), 1)
    band_d = band_c - HALO - band_r
    a_parts = []
    for gi, w in enumerate(POOL_WINDOWS):
        hw = w // 2
        cs = slice(gi * POOL_GROUP_WIDTH, (gi + 1) * POOL_GROUP_WIDTH)
        ws = slice(gi * POOL_GROUP_WIDTH, (gi + 1) * POOL_GROUP_WIDTH)
        band = jnp.logical_and(band_d >= -hw, band_d < hw).astype(BF16)
        wsum = jnp.concatenate(
            [_dot(band, u_ext_bf[b * rb_:b * rb_ + kb_, cs]) for b in range(t // rb_)], axis=0)
        count = (jnp.minimum(pos + hw, seq_len) - jnp.maximum(pos - hw, 0)).astype(F32)
        pooled = wsum / count - u[:, cs]
        a_parts.append(_dot(pooled.astype(BF16), poolw_ref[ws, :]))
    a = jnp.concatenate(a_parts, axis=1) * pscale_ref[...]
    z_pool = (a * _silu(p_gate)).astype(BF16)
    gla_gate = _dot_nt(h, wt_ref[GGATE_ROWS, :])
    bg_gla = _sigmoid(_dot_nt(h, wt_ref[BGG_ROWS, :]))

    def put_heads(dst, rs, fwd, bwd):
        for hd in range(GLA_HEADS):
            ks = slice(hd * hk, (hd + 1) * hk)
            dst[rs, 2 * hd * hk:(2 * hd + 1) * hk] = fwd[:, ks].astype(BF16)
            dst[rs, (2 * hd + 1) * hk:(2 * hd + 2) * hk] = bwd[:, ks].astype(BF16)

    dec_f, dec_b = [], []
    for p in range(n_pairs):
        c0 = r0 + p * GLA_PAIR
        c1 = c0 + GLA_CHUNK
        ra = slice(c0, c1)
        rb = slice(c1, c1 + GLA_CHUNK)
        q0, q1 = q_scr[ra, :], q_scr[rb, :]
        k0, k1 = k_ref[ra, :], k_ref[rb, :]
        cum0, cum1 = cf_scr[ra, :], cf_scr[rb, :]
        tot0 = cf_scr[c1 - 1:c1, :]
        tot1 = cf_scr[c1 + GLA_CHUNK - 1:c1 + GLA_CHUNK, :]
        rc0, rc1 = rb_scr[ra, :], rb_scr[rb, :]
        tb0 = rb_scr[c0:c0 + 1, :]
        tb1 = rb_scr[c1:c1 + 1, :]
        e_tot0, e_tot1 = jnp.exp(tot0), jnp.exp(tot1)
        e_tb0, e_tb1 = jnp.exp(tb0), jnp.exp(tb1)

        a0 = cum0 - tot0
        ks_f0 = k0 * jnp.exp(-a0)
        qs_f1 = q1 * jnp.exp(cum1)
        qs_b0 = q0 * jnp.exp(rc0)
        put_heads(qs_scr, ra, q0 * jnp.exp(a0), qs_b0)
        put_heads(qd_scr, ra, q0 * jnp.exp(cum0), qs_b0 * e_tb1)
        kif_scr[ra, :] = ks_f0.astype(BF16)
        kif_scr[rb, :] = (k1 * jnp.exp(-cum1)).astype(BF16)
        a1 = rc1 - tb1
        ks_b1 = k1 * jnp.exp(-a1)
        put_heads(qs_scr, rb, qs_f1, q1 * jnp.exp(a1))
        put_heads(qd_scr, rb, qs_f1 * e_tot0, q1 * jnp.exp(rc1))
        kib_scr[ra, :] = (k0 * jnp.exp(-rc0)).astype(BF16)
        kib_scr[rb, :] = ks_b1.astype(BF16)
        put_heads(ke_scr, ra, ks_f0 * e_tot1, k0 * jnp.exp(tb0 - rc0))
        put_heads(ke_scr, rb, k1 * jnp.exp(tot1 - cum1), ks_b1 * e_tb0)
        dec_f.append(e_tot0 * e_tot1)
        dec_b.append(e_tb0 * e_tb1)
    yield

    merged = bg_pool * _dot(z_pool, wpp_ref[...])
    for hd in range(GLA_HEADS):
        vs = slice(hd * hv, (hd + 1) * hv)
        cat = slice(2 * hd * hk, (2 * hd + 2) * hk)
        for p in range(n_pairs):
            rs = slice(r0 + p * GLA_PAIR, r0 + (p + 1) * GLA_PAIR)
            kv_scr[sub, hd, p] = _dot_tn(v_ref[rs, vs], ke_scr[rs, cat])
    yield

    prow = lax.broadcasted_iota(jnp.int32, (GLA_PAIR, GLA_PAIR), 0)
    pcol = lax.broadcasted_iota(jnp.int32, (GLA_PAIR, GLA_PAIR), 1)
    mask_f = pcol <= prow
    for p in range(n_pairs):
        rs = slice(r0 + p * GLA_PAIR, r0 + (p + 1) * GLA_PAIR)
        for hd in range(GLA_HEADS):
            ks = slice(hd * hk, (hd + 1) * hk)
            sc_f = _dot_nt(qs_scr[rs, 2 * hd * hk:(2 * hd + 1) * hk], kif_scr[rs, ks])
            sc_b = _dot_nt(qs_scr[rs, (2 * hd + 1) * hk:(2 * hd + 2) * hk], kib_scr[rs, ks])
            sc_scr[sub, p, hd] = jnp.where(mask_f, sc_f, sc_b).astype(BF16)
    for hd in range(GLA_HEADS):
        ks = slice(hd * hk, (hd + 1) * hk)
        sf = sf_scr[hd]
        for p in range(n_pairs):
            scat_scr[sub, p, hd, 0:hk, :] = sf.T.astype(BF16)
            sf = sf * dec_f[p][:, ks] + kv_scr[sub, hd, p, :, 0:hk]
        sf_scr[hd] = sf
        sb = sb_ref[sb_idx, hd]
        for p in reversed(range(n_pairs)):
            scat_scr[sub, p, hd, hk:2 * hk, :] = sb.T.astype(BF16)
            if p > 0:
                sb = sb * dec_b[p][:, ks] + kv_scr[sub, hd, p, :, hk:2 * hk]
    yield

    for p in range(n_pairs):
        rs = slice(r0 + p * GLA_PAIR, r0 + (p + 1) * GLA_PAIR)
        for hd in range(GLA_HEADS):
            vs = slice(hd * hv, (hd + 1) * hv)
            cat = slice(2 * hd * hk, (2 * hd + 2) * hk)
            og_scr[rs, vs] = (_dot(sc_scr[sub, p, hd], v_ref[rs, vs])
                              + _dot(qd_scr[rs, cat], scat_scr[sub, p, hd]))
    yield

    glag = glag_ref[...]
    z_parts = []
    for hd in range(GLA_HEADS):
        vs = slice(hd * hv, (hd + 1) * hv)
        oh = og_scr[rows, vs]
        ms = jnp.mean(oh * oh, axis=-1, keepdims=True)
        z_parts.append((oh * lax.rsqrt(ms + RMS_EPS)) * glag)
    o_n = jnp.concatenate(z_parts, axis=1)
    y_gla = _dot((o_n * _silu(gla_gate)).astype(BF16), wpgla_ref[...])
    merged = merged + bg_gla * y_gla
    yield

    out = _dot(merged.astype(BF16), wout_ref[...])
    yield

    ms = jnp.mean(out * out, axis=-1, keepdims=True)
    out_n = (out * lax.rsqrt(ms + RMS_EPS)) * gpost_ref[...]
    o_ref[rows, :] = x_ref[rows, :] + gate * out_n
    yield


def _main_kernel(seq_len,
                 x_ref, h_ref, hp_ref, hn_ref, mod_ref, gpost_ref, k_ref, v_ref, lr_ref,
                 wt_ref,
                 poolw_ref, pscale_ref, gkcat_ref, gbias_ref, glag_ref,
                 wpp_ref, wpgla_ref, wout_ref, sb_ref,
                 o_ref,
                 sf_scr, hext_scr, q_scr, cf_scr, rb_scr, qs_scr, qd_scr, ke_scr, kif_scr, kib_scr,
                 kv_scr, scat_scr, sc_scr, og_scr):
    tb = MAIN_SUBTILES * MAIN_TILE

    @pl.when(pl.program_id(0) == 0)
    def _():
        sf_scr[...] = jnp.zeros_like(sf_scr)

    hext_scr[0:HALO, :] = hp_ref[...]
    hext_scr[HALO:HALO + tb, :] = h_ref[...]
    hext_scr[HALO + tb:HALO + tb + HALO, :] = hn_ref[...]

    refs = (x_ref, mod_ref, gpost_ref, k_ref, v_ref, lr_ref,
            wt_ref,
            poolw_ref, pscale_ref, gkcat_ref, gbias_ref, glag_ref,
            wpp_ref, wpgla_ref, wout_ref, sb_ref, o_ref,
            sf_scr, hext_scr, q_scr, cf_scr, rb_scr, qs_scr, qd_scr, ke_scr, kif_scr, kib_scr,
            kv_scr, scat_scr, sc_scr, og_scr)
    gens = [_subtile_stages(seq_len, sub, refs) for sub in range(MAIN_SUBTILES)]
    _run_interleaved(gens, STAGE_SKEW)


def _main_call(x2, h_all, mod, g_post, k_all, v_all, lr_all, w_bf, small, wpool_bf, wpp_bf, wpgla_bf, wout_bf, sb):
    s = x2.shape[0]
    tb = MAIN_SUBTILES * MAIN_TILE
    nt = s // tb
    hb = tb // HALO
    n_hb = s // HALO
    n_pairs = MAIN_TILE // GLA_PAIR
    const2 = lambda i: (0, 0)
    tile = lambda i: (i, 0)
    one = pl.Buffered(1)
    pool_scale, gk_cat, gbias_cat, gla_norm_g = small

    def resident(arr):
        return pl.BlockSpec(arr.shape, const2, pipeline_mode=one)

    in_specs = [
        pl.BlockSpec((tb, D_MODEL), tile),
        pl.BlockSpec((tb, D_MODEL), tile),
        pl.BlockSpec((HALO, D_MODEL), lambda i: (jnp.maximum(i * hb - 1, 0), 0)),
        pl.BlockSpec((HALO, D_MODEL), lambda i: (jnp.minimum((i + 1) * hb, n_hb - 1), 0)),
        pl.BlockSpec((3, D_MODEL), const2),
        pl.BlockSpec((1, D_MODEL), const2),
        pl.BlockSpec((tb, GLA_KEY_WIDTH), tile),
        pl.BlockSpec((tb, D_MODEL), tile),
        pl.BlockSpec((tb, LR_WIDTH), tile),
        resident(w_bf),
        resident(wpool_bf), resident(pool_scale), resident(gk_cat), resident(gbias_cat), resident(gla_norm_g),
        resident(wpp_bf), resident(wpgla_bf), resident(wout_bf),
        pl.BlockSpec((tb // SUB_TILE, GLA_HEADS, GLA_HEAD_V, GLA_HEAD_K), lambda i: (i, 0, 0, 0)),
    ]
    scratch = [
        pltpu.VMEM((GLA_HEADS, GLA_HEAD_V, GLA_HEAD_K), F32),
        pltpu.VMEM((tb + 2 * HALO, D_MODEL), BF16),
        pltpu.VMEM((tb, GLA_KEY_WIDTH), F32),
        pltpu.VMEM((tb, GLA_KEY_WIDTH), F32),
        pltpu.VMEM((tb, GLA_KEY_WIDTH), F32),
        pltpu.VMEM((tb, 2 * GLA_KEY_WIDTH), BF16),
        pltpu.VMEM((tb, 2 * GLA_KEY_WIDTH), BF16),
        pltpu.VMEM((tb, 2 * GLA_KEY_WIDTH), BF16),
        pltpu.VMEM((tb, GLA_KEY_WIDTH), BF16),
        pltpu.VMEM((tb, GLA_KEY_WIDTH), BF16),
        pltpu.VMEM((MAIN_SUBTILES, GLA_HEADS, n_pairs, GLA_HEAD_V, 2 * GLA_HEAD_K), F32),
        pltpu.VMEM((MAIN_SUBTILES, n_pairs, GLA_HEADS, 2 * GLA_HEAD_K, GLA_HEAD_V), BF16),
        pltpu.VMEM((MAIN_SUBTILES, n_pairs, GLA_HEADS, GLA_PAIR, GLA_PAIR), BF16),
        pltpu.VMEM((tb, D_MODEL), F32),
    ]
    return pl.pallas_call(
        functools.partial(_main_kernel, s),
        grid=(nt,),
        in_specs=in_specs,
        out_specs=pl.BlockSpec((tb, D_MODEL), tile),
        out_shape=jax.ShapeDtypeStruct((s, D_MODEL), F32),
        scratch_shapes=scratch,
        compiler_params=pltpu.CompilerParams(dimension_semantics=("arbitrary",),
                                             vmem_limit_bytes=VMEM_LIMIT_BYTES),
        name="main_call",
    )(x2, h_all, h_all, h_all, mod, g_post, k_all, v_all, lr_all,
      w_bf,
      wpool_bf, pool_scale, gk_cat, gbias_cat, gla_norm_g, wpp_bf, wpgla_bf, wout_bf, sb)


def _layer(x2, c, w_ada, b_ada, g_pre, g_post, w_in, pool_w, pool_scale, gk_up_fwd, gk_bias_fwd,
           gk_up_bwd, gk_bias_bwd, gla_norm_g, w_proj_pool, w_proj_gla, w_out):
    ada = _ada_call(c.reshape(D_MODEL, 1), w_ada, b_ada.reshape(1, -1))
    mod = ada.reshape(3, D_MODEL)

    zeros = jnp.zeros((GLA_GATE_RANK, GLA_KEY_WIDTH), F32)
    gk_cat = jnp.concatenate([jnp.concatenate([gk_up_fwd, zeros], axis=1),
                              jnp.concatenate([zeros, gk_up_bwd], axis=1)], axis=0).astype(BF16)
    gbias_cat = jnp.concatenate([gk_bias_fwd, gk_bias_bwd]).reshape(1, -1)
    gk_b = gk_cat[:, GLA_KEY_WIDTH:]
    gbias_b = gk_bias_bwd.reshape(1, -1)

    pool_w2 = pool_w.reshape(POOL_GROUPS * POOL_GROUP_WIDTH, POOL_GROUP_WIDTH)
    sb, k_all, v_all, h_all, lr_all, w_bf, wpool_bf, wpp_bf, wpgla_bf, wout_bf = _prep_call(
        x2, mod, g_pre.reshape(1, -1), gk_b, gbias_b, w_in.T, pool_w2, w_proj_pool, w_proj_gla, w_out)

    small = (pool_scale.reshape(1, -1), gk_cat, gbias_cat, gla_norm_g.reshape(1, -1))
    return _main_call(x2, h_all, mod, g_post.reshape(1, -1), k_all, v_all, lr_all, w_bf, small,
                      wpool_bf, wpp_bf, wpgla_bf, wout_bf, sb)


def kernel(x, c, w_ada, b_ada, g_pre, g_post, w_in, pool_w, pool_scale, gk_up_fwd, gk_bias_fwd, gk_up_bwd, gk_bias_bwd, gla_norm_g, w_proj_pool, w_proj_gla, w_out):
    b, s, d = x.shape
    depth = w_in.shape[0]
    xf = x.reshape(b * s, d)
    outs = []
    for bi in range(b):
        xb = xf if b == 1 else lax.slice_in_dim(xf, bi * s, (bi + 1) * s, axis=0)
        for l in range(depth):
            xb = _layer(xb, c[bi:bi + 1], w_ada[l], b_ada[l], g_pre[l], g_post[l], w_in[l], pool_w[l],
                        pool_scale[l], gk_up_fwd[l], gk_bias_fwd[l], gk_up_bwd[l], gk_bias_bwd[l],
                        gla_norm_g[l], w_proj_pool[l], w_proj_gla[l], w_out[l])
        outs.append(xb)
    out = outs[0] if b == 1 else jnp.concatenate(outs, axis=0)
    return out.reshape(b, s, d)
```

```python
import functools

import jax
import jax.numpy as jnp
from jax import lax
from jax.experimental import pallas as pl
from jax.experimental.pallas import tpu as pltpu

D_MODEL = 1024
POOL_GROUPS = 4
POOL_GROUP_WIDTH = 256
POOL_WINDOWS = (2, 4, 8, 16)
GLA_HEADS = 4
GLA_KEY_WIDTH = 512
GLA_HEAD_K = 128
GLA_HEAD_V = 256
GLA_GATE_RANK = 16
GLA_GATE_NORMALIZER = 16.0
GLA_CHUNK = 64
GLA_PAIR = 2 * GLA_CHUNK
RMS_EPS = 1e-6
IN_SPLITS = (1024, 1024, 512, 512, 1024, 1024, 16, 16, 1024, 1024)
IN_WIDTH = sum(IN_SPLITS)

LANES = 128
SUB_TILE = 256
PREP_SUBTILES = 2
MAIN_TILE = 512
MAIN_SUBTILES = 1
CUMSUM_BLOCK = 256
STAGE_SKEW = 2
HALO = 16
POOL_ROW_BLOCK = 128
VMEM_LIMIT_BYTES = 60 * 1024 * 1024

_OFFS = [sum(IN_SPLITS[:j]) for j in range(len(IN_SPLITS) + 1)]
(PIN_ROWS, PGATE_ROWS, Q_ROWS, K_ROWS, V_ROWS, GGATE_ROWS) = (slice(_OFFS[j], _OFFS[j + 1]) for j in range(6))
LR_ROWS = slice(_OFFS[6], _OFFS[8])
BGP_ROWS = slice(_OFFS[8], _OFFS[9])
BGG_ROWS = slice(_OFFS[9], _OFFS[10])
LR_WIDTH = 2 * GLA_GATE_RANK
WT_SLAB = 240

F32 = jnp.float32
BF16 = jnp.bfloat16


def _dot(a, b):
    return jnp.dot(a, b, preferred_element_type=F32)


def _dot_nt(a, b):
    return lax.dot_general(a, b, (((1,), (1,)), ((), ())), preferred_element_type=F32)


def _dot_tn(a, b):
    return lax.dot_general(a, b, (((0,), (0,)), ((), ())), preferred_element_type=F32)


def _sigmoid(x):
    return 0.5 * jnp.tanh(0.5 * x) + 0.5


def _silu(x):
    return x * _sigmoid(x)


def _tanh1(xh):
    return jnp.tanh(xh) + 1.0


def _log_sigmoid(x):
    return jnp.minimum(x, 0.0) - jnp.log(1.0 + jnp.exp(-jnp.abs(x)))


def _norm_mod(xv, g_pre, scale, shift):
    ms = jnp.mean(xv * xv, axis=-1, keepdims=True)
    hn = (xv * lax.rsqrt(ms + RMS_EPS)) * g_pre
    return hn * (1.0 + scale) + shift


def _split_hi_lo(a):
    hi = a.astype(BF16)
    lo = (a - hi.astype(F32)).astype(BF16)
    return hi, lo


def _run_interleaved(gens, skew):
    live = [True] * len(gens)
    step = 0
    while any(live):
        for j, g in enumerate(gens):
            if live[j] and step >= j * skew:
                try:
                    next(g)
                except StopIteration:
                    live[j] = False
        step += 1


def _ada_kernel(c_ref, w_ref, b_ref, o_ref):
    s = _silu(c_ref[...])
    o_ref[...] = jnp.sum(w_ref[...] * s, axis=0, keepdims=True) + b_ref[...]


def _ada_call(c_col, w_ada, b_ada):
    blk = 4 * LANES
    n_blk = (3 * D_MODEL) // blk
    return pl.pallas_call(
        _ada_kernel,
        grid=(n_blk,),
        in_specs=[
            pl.BlockSpec((D_MODEL, 1), lambda j: (0, 0)),
            pl.BlockSpec((D_MODEL, blk), lambda j: (0, j)),
            pl.BlockSpec((1, blk), lambda j: (0, j)),
        ],
        out_specs=pl.BlockSpec((1, blk), lambda j: (0, j)),
        out_shape=jax.ShapeDtypeStruct((1, 3 * D_MODEL), F32),
        compiler_params=pltpu.CompilerParams(dimension_semantics=("arbitrary",)),
        name="ada_call",
    )(c_col, w_ada, b_ada)


def _prep_stages(sub, refs):
    (x_ref, mod_ref, gpre_ref, gkb_ref, gbias_ref, wk_scr, wv_scr, wlr_scr,
     sb_ref, k_ref, v_ref, h_ref, lr_ref, st_scr) = refs
    t = SUB_TILE
    rows = slice(sub * t, (sub + 1) * t)
    k = k_ref[rows, :]
    v_bf = v_ref[rows, :]

    la = _log_sigmoid(_dot(lr_ref[rows, :], gkb_ref[...]) + gbias_ref[...]) / GLA_GATE_NORMALIZER
    la_hi, la_lo = _split_hi_lo(la)
    yield

    row = lax.broadcasted_iota(jnp.int32, (t, t), 0)
    col = lax.broadcasted_iota(jnp.int32, (t, t), 1)
    upper = (col >= row).astype(BF16)
    rc = _dot(upper, la_hi) + _dot(upper, la_lo)
    yield

    tot = rc[0:1, :]
    k_end = (k * jnp.exp(tot - rc)).astype(BF16)
    decay = jnp.exp(tot)
    kvs = []
    for hd in range(GLA_HEADS):
        ks = slice(hd * GLA_HEAD_K, (hd + 1) * GLA_HEAD_K)
        vs = slice(hd * GLA_HEAD_V, (hd + 1) * GLA_HEAD_V)
        kvs.append(_dot_tn(v_bf[:, vs], k_end[:, ks]))
    yield

    for hd in range(GLA_HEADS):
        ks = slice(hd * GLA_HEAD_K, (hd + 1) * GLA_HEAD_K)
        st = st_scr[hd]
        sb_ref[sub, hd] = st
        st_scr[hd] = st * decay[:, ks] + kvs[hd]
    yield


def _prep_kernel(x_ref, mod_ref, gpre_ref, gkb_ref, gbias_ref, wk_ref, wv_ref, wlr_ref,
                 win_ref, wpool_ref, wpp_ref, wpgla_ref, wout_ref,
                 sb_ref, k_ref, v_ref, h_ref, lr_ref, winbf_ref, wpoolbf_ref, wppbf_ref, wpglabf_ref, woutbf_ref,
                 st_scr, wk_scr, wv_scr, wlr_scr):
    @pl.when(pl.program_id(0) == 0)
    def _():
        st_scr[...] = jnp.zeros_like(st_scr)
        wk_scr[...] = wk_ref[...].astype(BF16)
        wv_scr[...] = wv_ref[...].astype(BF16)
        wlr_scr[...] = wlr_ref[...].astype(BF16)

    n_wt = IN_WIDTH // WT_SLAB
    wrow = (jnp.minimum(pl.program_id(0), n_wt - 1) * WT_SLAB
            + lax.broadcasted_iota(jnp.int32, (WT_SLAB, 1), 0))
    halved = functools.reduce(jnp.logical_or, [
        jnp.logical_and(wrow >= rs.start, wrow < rs.stop) for rs in (PGATE_ROWS, GGATE_ROWS, BGP_ROWS, BGG_ROWS)])
    winbf_ref[...] = (win_ref[...] * jnp.where(halved, 0.5, 1.0)).astype(BF16)
    wpoolbf_ref[...] = wpool_ref[...].astype(BF16)
    wppbf_ref[...] = wpp_ref[...].astype(BF16)
    wpglabf_ref[...] = wpgla_ref[...].astype(BF16)
    woutbf_ref[...] = (wout_ref[...] * 0.5).astype(BF16)

    h = _norm_mod(x_ref[...], gpre_ref[...], mod_ref[1:2, :], mod_ref[0:1, :]).astype(BF16)
    h_ref[...] = h
    k_ref[...] = _dot_nt(h, wk_scr[...])
    lr_ref[...] = _dot_nt(h, wlr_scr[...]).astype(BF16)
    v_ref[...] = _dot_nt(h, wv_scr[...]).astype(BF16)

    refs = (x_ref, mod_ref, gpre_ref, gkb_ref, gbias_ref, wk_scr, wv_scr, wlr_scr,
            sb_ref, k_ref, v_ref, h_ref, lr_ref, st_scr)
    gens = [_prep_stages(sub, refs) for sub in reversed(range(PREP_SUBTILES))]
    _run_interleaved(gens, STAGE_SKEW)


def _prep_call(x2, mod, g_pre, gk_b, gbias_b, w_in_t, pool_w2, w_proj_pool, w_proj_gla, w_out):
    s = x2.shape[0]
    tb = PREP_SUBTILES * SUB_TILE
    nb = s // tb
    slab = D_MODEL // nb
    n_wt = IN_WIDTH // WT_SLAB
    const = lambda i: (0, 0)
    rev = lambda i: (nb - 1 - i, 0)
    fwd = lambda i: (i, 0)
    wt_slab = lambda i: (jnp.minimum(i, n_wt - 1), 0)
    one = pl.Buffered(1)

    def rows_of(rs):
        n = rs.stop - rs.start
        return pl.BlockSpec((n, D_MODEL), lambda i: (rs.start // n, 0), pipeline_mode=one)

    return pl.pallas_call(
        _prep_kernel,
        grid=(nb,),
        in_specs=[
            pl.BlockSpec((tb, D_MODEL), rev),
            pl.BlockSpec((3, D_MODEL), const),
            pl.BlockSpec((1, D_MODEL), const),
            pl.BlockSpec(gk_b.shape, const),
            pl.BlockSpec(gbias_b.shape, const),
            rows_of(K_ROWS),
            rows_of(V_ROWS),
            rows_of(LR_ROWS),
            pl.BlockSpec((WT_SLAB, D_MODEL), wt_slab),
            pl.BlockSpec((slab, POOL_GROUP_WIDTH), fwd),
            pl.BlockSpec((slab, D_MODEL), fwd),
            pl.BlockSpec((slab, D_MODEL), fwd),
            pl.BlockSpec((slab, D_MODEL), fwd),
        ],
        out_specs=[
            pl.BlockSpec((PREP_SUBTILES, GLA_HEADS, GLA_HEAD_V, GLA_HEAD_K), lambda i: (nb - 1 - i, 0, 0, 0)),
            pl.BlockSpec((tb, GLA_KEY_WIDTH), rev),
            pl.BlockSpec((tb, D_MODEL), rev),
            pl.BlockSpec((tb, D_MODEL), rev),
            pl.BlockSpec((tb, LR_WIDTH), rev),
            pl.BlockSpec((WT_SLAB, D_MODEL), wt_slab),
            pl.BlockSpec((slab, POOL_GROUP_WIDTH), fwd),
            pl.BlockSpec((slab, D_MODEL), fwd),
            pl.BlockSpec((slab, D_MODEL), fwd),
            pl.BlockSpec((slab, D_MODEL), fwd),
        ],
        out_shape=[
            jax.ShapeDtypeStruct((s // SUB_TILE, GLA_HEADS, GLA_HEAD_V, GLA_HEAD_K), F32),
            jax.ShapeDtypeStruct((s, GLA_KEY_WIDTH), F32),
            jax.ShapeDtypeStruct((s, D_MODEL), BF16),
            jax.ShapeDtypeStruct((s, D_MODEL), BF16),
            jax.ShapeDtypeStruct((s, LR_WIDTH), BF16),
            jax.ShapeDtypeStruct((IN_WIDTH, D_MODEL), BF16),
            jax.ShapeDtypeStruct(pool_w2.shape, BF16),
            jax.ShapeDtypeStruct((D_MODEL, D_MODEL), BF16),
            jax.ShapeDtypeStruct((D_MODEL, D_MODEL), BF16),
            jax.ShapeDtypeStruct((D_MODEL, D_MODEL), BF16),
        ],
        scratch_shapes=[
            pltpu.VMEM((GLA_HEADS, GLA_HEAD_V, GLA_HEAD_K), F32),
            pltpu.VMEM((GLA_KEY_WIDTH, D_MODEL), BF16),
            pltpu.VMEM((D_MODEL, D_MODEL), BF16),
            pltpu.VMEM((LR_WIDTH, D_MODEL), BF16),
        ],
        compiler_params=pltpu.CompilerParams(dimension_semantics=("arbitrary",),
                                             vmem_limit_bytes=VMEM_LIMIT_BYTES),
        name="prep_call",
    )(x2, mod, g_pre, gk_b, gbias_b, w_in_t, w_in_t, w_in_t, w_in_t, pool_w2, w_proj_pool, w_proj_gla, w_out)


def _subtile_stages(seq_len, sub, refs):
    (x_ref, mod_ref, gpost_ref, k_ref, v_ref, lr_ref,
     wt_ref,
     poolw_ref, pscale_ref, gkcat_ref, gbias_ref, glag_ref,
     wpp_ref, wpgla_ref, wout_ref, sb_ref, o_ref,
     sf_scr, hext_scr, q_scr, cf_scr, rb_scr, qs_scr, qd_scr, ke_scr, kif_scr, kib_scr,
     kv_scr, scat_scr, sc_scr, og_scr) = refs
    t = MAIN_TILE
    r0 = sub * t
    rows = slice(r0, r0 + t)
    tile_row0 = pl.program_id(0) * (MAIN_SUBTILES * t) + r0
    sb_idx = (sub + 1) * (MAIN_TILE // SUB_TILE) - 1
    n_pairs = t // GLA_PAIR
    hk, hv = GLA_HEAD_K, GLA_HEAD_V
    gate = mod_ref[2:3, :]

    h_ext = hext_scr[r0:r0 + t + 2 * HALO, :]
    h = hext_scr[r0 + HALO:r0 + HALO + t, :]

    q_scr[rows, :] = _dot_nt(h, wt_ref[Q_ROWS, :]) * (GLA_HEAD_K ** -0.5)
    la_pre = _dot(lr_ref[rows, :], gkcat_ref[...]) + gbias_ref[...]
    u_raw = _dot_nt(h_ext, wt_ref[PIN_ROWS, :])
    halo_row = lax.broadcasted_iota(jnp.int32, (HALO, 1), 0)
    top_ok = tile_row0 - HALO + halo_row >= 0
    bot_ok = tile_row0 + t + halo_row < seq_len
    u = u_raw[HALO:HALO + t, :]
    u_ext = jnp.concatenate([jnp.where(top_ok, u_raw[0:HALO, :], 0.0), u,
                             jnp.where(bot_ok, u_raw[HALO + t:, :], 0.0)], axis=0)
    yield

    la = _log_sigmoid(la_pre) / GLA_GATE_NORMALIZER
    p_gate_h = _dot_nt(h, wt_ref[PGATE_ROWS, :])
    u_ext_bf = u_ext.astype(BF16)
    laf_hi, laf_lo = _split_hi_lo(la[:, :GLA_KEY_WIDTH])
    lab_hi, lab_lo = _split_hi_lo(la[:, GLA_KEY_WIDTH:])
    yield

    cb = CUMSUM_BLOCK
    row = lax.broadcasted_iota(jnp.int32, (cb, cb), 0)
    col = lax.broadcasted_iota(jnp.int32, (cb, cb), 1)
    same_chunk = (row // GLA_CHUNK) == (col // GLA_CHUNK)
    lower = jnp.logical_and(same_chunk, col <= row).astype(BF16)
    upper = jnp.logical_and(same_chunk, col >= row).astype(BF16)
    for b in range(t // cb):
        rb_rows = slice(b * cb, (b + 1) * cb)
        dst = slice(r0 + b * cb, r0 + (b + 1) * cb)
        cf_scr[dst, :] = _dot(lower, laf_hi[rb_rows]) + _dot(lower, laf_lo[rb_rows])
        rb_scr[dst, :] = _dot(upper, lab_hi[rb_rows]) + _dot(upper, lab_lo[rb_rows])
    bg_pool2 = _tanh1(_dot_nt(h, wt_ref[BGP_ROWS, :]))
    yield

    pos = tile_row0 + lax.broadcasted_iota(jnp.int32, (t, LANES), 0)
    lane = lax.broadcasted_iota(jnp.int32, (t, LANES), 1)
    hw_lane = jnp.zeros((t, LANES), jnp.int32)
    for gi, w in enumerate(POOL_WINDOWS):
        hw_lane = jnp.where(lane == gi, w // 2, hw_lane)
    inv_count = 1.0 / (jnp.minimum(pos + hw_lane, seq_len) - jnp.maximum(pos - hw_lane, 0)).astype(F32)
    rb_ = POOL_ROW_BLOCK
    kb_ = POOL_ROW_BLOCK + 2 * HALO
    band_r = lax.broadcasted_iota(jnp.int32, (rb_, kb_), 0)
    band_c = lax.broadcasted_iota(jnp.int32, (rb_, kb_), 1)
    band_d = band_c - HALO - band_r
    a_parts = []
    for gi, w in enumerate(POOL_WINDOWS):
        hw = w // 2
        cs = slice(gi * POOL_GROUP_WIDTH, (gi + 1) * POOL_GROUP_WIDTH)
        ws = slice(gi * POOL_GROUP_WIDTH, (gi + 1) * POOL_GROUP_WIDTH)
        band = jnp.logical_and(band_d >= -hw, band_d < hw).astype(BF16)
        wsum = jnp.concatenate(
            [_dot(band, u_ext_bf[b * rb_:b * rb_ + kb_, cs]) for b in range(t // rb_)], axis=0)
        pooled = wsum * inv_count[:, gi:gi + 1] - u[:, cs]
        a_parts.append(_dot(pooled.astype(BF16), poolw_ref[ws, :]))
    a = jnp.concatenate(a_parts, axis=1) * pscale_ref[...]
    z_pool = (a * (p_gate_h * _tanh1(p_gate_h))).astype(BF16)
    gla_gate_h = _dot_nt(h, wt_ref[GGATE_ROWS, :])
    bg_gla2 = _tanh1(_dot_nt(h, wt_ref[BGG_ROWS, :]))

    def put_heads(dst, rs, fwd, bwd):
        for hd in range(GLA_HEADS):
            ks = slice(hd * hk, (hd + 1) * hk)
            dst[rs, 2 * hd * hk:(2 * hd + 1) * hk] = fwd[:, ks].astype(BF16)
            dst[rs, (2 * hd + 1) * hk:(2 * hd + 2) * hk] = bwd[:, ks].astype(BF16)

    dec_f, dec_b = [], []
    for p in range(n_pairs):
        c0 = r0 + p * GLA_PAIR
        c1 = c0 + GLA_CHUNK
        ra = slice(c0, c1)
        rb = slice(c1, c1 + GLA_CHUNK)
        q0, q1 = q_scr[ra, :], q_scr[rb, :]
        k0, k1 = k_ref[ra, :], k_ref[rb, :]
        cum0, cum1 = cf_scr[ra, :], cf_scr[rb, :]
        tot0 = cf_scr[c1 - 1:c1, :]
        tot1 = cf_scr[c1 + GLA_CHUNK - 1:c1 + GLA_CHUNK, :]
        rc0, rc1 = rb_scr[ra, :], rb_scr[rb, :]
        tb0 = rb_scr[c0:c0 + 1, :]
        tb1 = rb_scr[c1:c1 + 1, :]
        e_tot0, e_tot1 = jnp.exp(tot0), jnp.exp(tot1)
        e_tb0, e_tb1 = jnp.exp(tb0), jnp.exp(tb1)

        a0 = cum0 - tot0
        ks_f0 = k0 * jnp.exp(-a0)
        qs_f1 = q1 * jnp.exp(cum1)
        qs_b0 = q0 * jnp.exp(rc0)
        put_heads(qs_scr, ra, q0 * jnp.exp(a0), qs_b0)
        put_heads(qd_scr, ra, q0 * jnp.exp(cum0), qs_b0 * e_tb1)
        kif_scr[ra, :] = ks_f0.astype(BF16)
        kif_scr[rb, :] = (k1 * jnp.exp(-cum1)).astype(BF16)
        a1 = rc1 - tb1
        ks_b1 = k1 * jnp.exp(-a1)
        put_heads(qs_scr, rb, qs_f1, q1 * jnp.exp(a1))
        put_heads(qd_scr, rb, qs_f1 * e_tot0, q1 * jnp.exp(rc1))
        kib_scr[ra, :] = (k0 * jnp.exp(-rc0)).astype(BF16)
        kib_scr[rb, :] = ks_b1.astype(BF16)
        put_heads(ke_scr, ra, ks_f0 * e_tot1, k0 * jnp.exp(tb0 - rc0))
        put_heads(ke_scr, rb, k1 * jnp.exp(tot1 - cum1), ks_b1 * e_tb0)
        dec_f.append(e_tot0 * e_tot1)
        dec_b.append(e_tb0 * e_tb1)
    yield

    merged2 = bg_pool2 * _dot(z_pool, wpp_ref[...])
    for hd in range(GLA_HEADS):
        vs = slice(hd * hv, (hd + 1) * hv)
        cat = slice(2 * hd * hk, (2 * hd + 2) * hk)
        for p in range(n_pairs):
            rs = slice(r0 + p * GLA_PAIR, r0 + (p + 1) * GLA_PAIR)
            kv_scr[sub, hd, p] = _dot_tn(v_ref[rs, vs], ke_scr[rs, cat])
    yield

    prow = lax.broadcasted_iota(jnp.int32, (GLA_PAIR, GLA_PAIR), 0)
    pcol = lax.broadcasted_iota(jnp.int32, (GLA_PAIR, GLA_PAIR), 1)
    mask_f = pcol <= prow
    for p in range(n_pairs):
        rs = slice(r0 + p * GLA_PAIR, r0 + (p + 1) * GLA_PAIR)
        for hd in range(GLA_HEADS):
            ks = slice(hd * hk, (hd + 1) * hk)
            sc_f = _dot_nt(qs_scr[rs, 2 * hd * hk:(2 * hd + 1) * hk], kif_scr[rs, ks])
            sc_b = _dot_nt(qs_scr[rs, (2 * hd + 1) * hk:(2 * hd + 2) * hk], kib_scr[rs, ks])
            sc_scr[sub, p, hd] = jnp.where(mask_f, sc_f, sc_b).astype(BF16)
    for hd in range(GLA_HEADS):
        ks = slice(hd * hk, (hd + 1) * hk)
        sf = sf_scr[hd]
        for p in range(n_pairs):
            scat_scr[sub, p, hd, 0:hk, :] = sf.T.astype(BF16)
            sf = sf * dec_f[p][:, ks] + kv_scr[sub, hd, p, :, 0:hk]
        sf_scr[hd] = sf
        sb = sb_ref[sb_idx, hd]
        for p in reversed(range(n_pairs)):
            scat_scr[sub, p, hd, hk:2 * hk, :] = sb.T.astype(BF16)
            if p > 0:
                sb = sb * dec_b[p][:, ks] + kv_scr[sub, hd, p, :, hk:2 * hk]
    yield

    for p in range(n_pairs):
        rs = slice(r0 + p * GLA_PAIR, r0 + (p + 1) * GLA_PAIR)
        for hd in range(GLA_HEADS):
            vs = slice(hd * hv, (hd + 1) * hv)
            cat = slice(2 * hd * hk, (2 * hd + 2) * hk)
            og_scr[rs, vs] = (_dot(sc_scr[sub, p, hd], v_ref[rs, vs])
                              + _dot(qd_scr[rs, cat], scat_scr[sub, p, hd]))
    yield

    glag = glag_ref[...]
    z_parts = []
    for hd in range(GLA_HEADS):
        vs = slice(hd * hv, (hd + 1) * hv)
        oh = og_scr[rows, vs]
        ms = jnp.mean(oh * oh, axis=-1, keepdims=True)
        z_parts.append((oh * lax.rsqrt(ms + RMS_EPS)) * glag)
    o_n = jnp.concatenate(z_parts, axis=1)
    y_gla = _dot((o_n * (gla_gate_h * _tanh1(gla_gate_h))).astype(BF16), wpgla_ref[...])
    merged2 = merged2 + bg_gla2 * y_gla
    yield

    out = _dot(merged2.astype(BF16), wout_ref[...])
    yield

    ms = jnp.mean(out * out, axis=-1, keepdims=True)
    out_n = (out * lax.rsqrt(ms + RMS_EPS)) * gpost_ref[...]
    o_ref[rows, :] = x_ref[rows, :] + gate * out_n
    yield


def _main_kernel(seq_len,
                 x_ref, h_ref, hp_ref, hn_ref, mod_ref, gpost_ref, k_ref, v_ref, lr_ref,
                 wt_ref,
                 poolw_ref, pscale_ref, gkcat_ref, gbias_ref, glag_ref,
                 wpp_ref, wpgla_ref, wout_ref, sb_ref,
                 o_ref,
                 sf_scr, hext_scr, q_scr, cf_scr, rb_scr, qs_scr, qd_scr, ke_scr, kif_scr, kib_scr,
                 kv_scr, scat_scr, sc_scr, og_scr):
    tb = MAIN_SUBTILES * MAIN_TILE

    @pl.when(pl.program_id(0) == 0)
    def _():
        sf_scr[...] = jnp.zeros_like(sf_scr)

    hext_scr[0:HALO, :] = hp_ref[...]
    hext_scr[HALO:HALO + tb, :] = h_ref[...]
    hext_scr[HALO + tb:HALO + tb + HALO, :] = hn_ref[...]

    refs = (x_ref, mod_ref, gpost_ref, k_ref, v_ref, lr_ref,
            wt_ref,
            poolw_ref, pscale_ref, gkcat_ref, gbias_ref, glag_ref,
            wpp_ref, wpgla_ref, wout_ref, sb_ref, o_ref,
            sf_scr, hext_scr, q_scr, cf_scr, rb_scr, qs_scr, qd_scr, ke_scr, kif_scr, kib_scr,
            kv_scr, scat_scr, sc_scr, og_scr)
    gens = [_subtile_stages(seq_len, sub, refs) for sub in range(MAIN_SUBTILES)]
    _run_interleaved(gens, STAGE_SKEW)


def _main_call(x2, h_all, mod, g_post, k_all, v_all, lr_all, w_bf, small, wpool_bf, wpp_bf, wpgla_bf, wout_bf, sb):
    s = x2.shape[0]
    tb = MAIN_SUBTILES * MAIN_TILE
    nt = s // tb
    hb = tb // HALO
    n_hb = s // HALO
    n_pairs = MAIN_TILE // GLA_PAIR
    const2 = lambda i: (0, 0)
    tile = lambda i: (i, 0)
    one = pl.Buffered(1)
    pool_scale, gk_cat, gbias_cat, gla_norm_g = small

    def resident(arr):
        return pl.BlockSpec(arr.shape, const2, pipeline_mode=one)

    in_specs = [
        pl.BlockSpec((tb, D_MODEL), tile),
        pl.BlockSpec((tb, D_MODEL), tile),
        pl.BlockSpec((HALO, D_MODEL), lambda i: (jnp.maximum(i * hb - 1, 0), 0)),
        pl.BlockSpec((HALO, D_MODEL), lambda i: (jnp.minimum((i + 1) * hb, n_hb - 1), 0)),
        pl.BlockSpec((3, D_MODEL), const2),
        pl.BlockSpec((1, D_MODEL), const2),
        pl.BlockSpec((tb, GLA_KEY_WIDTH), tile),
        pl.BlockSpec((tb, D_MODEL), tile),
        pl.BlockSpec((tb, LR_WIDTH), tile),
        resident(w_bf),
        resident(wpool_bf), resident(pool_scale), resident(gk_cat), resident(gbias_cat), resident(gla_norm_g),
        resident(wpp_bf), resident(wpgla_bf), resident(wout_bf),
        pl.BlockSpec((tb // SUB_TILE, GLA_HEADS, GLA_HEAD_V, GLA_HEAD_K), lambda i: (i, 0, 0, 0)),
    ]
    scratch = [
        pltpu.VMEM((GLA_HEADS, GLA_HEAD_V, GLA_HEAD_K), F32),
        pltpu.VMEM((tb + 2 * HALO, D_MODEL), BF16),
        pltpu.VMEM((tb, GLA_KEY_WIDTH), F32),
        pltpu.VMEM((tb, GLA_KEY_WIDTH), F32),
        pltpu.VMEM((tb, GLA_KEY_WIDTH), F32),
        pltpu.VMEM((tb, 2 * GLA_KEY_WIDTH), BF16),
        pltpu.VMEM((tb, 2 * GLA_KEY_WIDTH), BF16),
        pltpu.VMEM((tb, 2 * GLA_KEY_WIDTH), BF16),
        pltpu.VMEM((tb, GLA_KEY_WIDTH), BF16),
        pltpu.VMEM((tb, GLA_KEY_WIDTH), BF16),
        pltpu.VMEM((MAIN_SUBTILES, GLA_HEADS, n_pairs, GLA_HEAD_V, 2 * GLA_HEAD_K), F32),
        pltpu.VMEM((MAIN_SUBTILES, n_pairs, GLA_HEADS, 2 * GLA_HEAD_K, GLA_HEAD_V), BF16),
        pltpu.VMEM((MAIN_SUBTILES, n_pairs, GLA_HEADS, GLA_PAIR, GLA_PAIR), BF16),
        pltpu.VMEM((tb, D_MODEL), F32),
    ]
    return pl.pallas_call(
        functools.partial(_main_kernel, s),
        grid=(nt,),
        in_specs=in_specs,
        out_specs=pl.BlockSpec((tb, D_MODEL), tile),
        out_shape=jax.ShapeDtypeStruct((s, D_MODEL), F32),
        scratch_shapes=scratch,
        compiler_params=pltpu.CompilerParams(dimension_semantics=("arbitrary",),
                                             vmem_limit_bytes=VMEM_LIMIT_BYTES),
        name="main_call",
    )(x2, h_all, h_all, h_all, mod, g_post, k_all, v_all, lr_all,
      w_bf,
      wpool_bf, pool_scale, gk_cat, gbias_cat, gla_norm_g, wpp_bf, wpgla_bf, wout_bf, sb)


def _layer(x2, c, w_ada, b_ada, g_pre, g_post, w_in, pool_w, pool_scale, gk_up_fwd, gk_bias_fwd,
           gk_up_bwd, gk_bias_bwd, gla_norm_g, w_proj_pool, w_proj_gla, w_out):
    ada = _ada_call(c.reshape(D_MODEL, 1), w_ada, b_ada.reshape(1, -1))
    mod = ada.reshape(3, D_MODEL)

    zeros = jnp.zeros((GLA_GATE_RANK, GLA_KEY_WIDTH), F32)
    gk_cat = jnp.concatenate([jnp.concatenate([gk_up_fwd, zeros], axis=1),
                              jnp.concatenate([zeros, gk_up_bwd], axis=1)], axis=0).astype(BF16)
    gbias_cat = jnp.concatenate([gk_bias_fwd, gk_bias_bwd]).reshape(1, -1)
    gk_b = gk_cat[:, GLA_KEY_WIDTH:]
    gbias_b = gk_bias_bwd.reshape(1, -1)

    pool_w2 = pool_w.reshape(POOL_GROUPS * POOL_GROUP_WIDTH, POOL_GROUP_WIDTH)
    sb, k_all, v_all, h_all, lr_all, w_bf, wpool_bf, wpp_bf, wpgla_bf, wout_bf = _prep_call(
        x2, mod, g_pre.reshape(1, -1), gk_b, gbias_b, w_in.T, pool_w2, w_proj_pool, w_proj_gla, w_out)

    small = (pool_scale.reshape(1, -1), gk_cat, gbias_cat, gla_norm_g.reshape(1, -1))
    return _main_call(x2, h_all, mod, g_post.reshape(1, -1), k_all, v_all, lr_all, w_bf, small,
                      wpool_bf, wpp_bf, wpgla_bf, wout_bf, sb)


def kernel(x, c, w_ada, b_ada, g_pre, g_post, w_in, pool_w, pool_scale, gk_up_fwd, gk_bias_fwd, gk_up_bwd, gk_bias_bwd, gla_norm_g, w_proj_pool, w_proj_gla, w_out):
    b, s, d = x.shape
    depth = w_in.shape[0]
    xf = x.reshape(b * s, d)
    outs = []
    for bi in range(b):
        xb = xf if b == 1 else lax.slice_in_dim(xf, bi * s, (bi + 1) * s, axis=0)
        for l in range(depth):
            xb = _layer(xb, c[bi:bi + 1], w_ada[l], b_ada[l], g_pre[l], g_post[l], w_in[l], pool_w[l],
                        pool_scale[l], gk_up_fwd[l], gk_bias_fwd[l], gk_up_bwd[l], gk_bias_bwd[l],
                        gla_norm_g[l], w_proj_pool[l], w_proj_gla[l], w_out[l])
        outs.append(xb)
    out = outs[0] if b == 1 else jnp.concatenate(outs, axis=0)
    return out.reshape(b, s, d)
```

```python
import functools

import jax
import jax.numpy as jnp
from jax import lax
from jax.experimental import pallas as pl
from jax.experimental.pallas import tpu as pltpu

D_MODEL = 1024
POOL_GROUPS = 4
POOL_GROUP_WIDTH = 256
POOL_WINDOWS = (2, 4, 8, 16)
GLA_HEADS = 4
GLA_KEY_WIDTH = 512
GLA_HEAD_K = 128
GLA_HEAD_V = 256
GLA_GATE_RANK = 16
GLA_GATE_NORMALIZER = 16.0
GLA_CHUNK = 64
GLA_PAIR = 2 * GLA_CHUNK
RMS_EPS = 1e-6
IN_SPLITS = (1024, 1024, 512, 512, 1024, 1024, 16, 16, 1024, 1024)
IN_WIDTH = sum(IN_SPLITS)

LANES = 128
SUB_TILE = 256
PREP_SUBTILES = 2
MAIN_TILE = 512
MAIN_SUBTILES = 1
STAGE_SKEW = 2
HALO = 16
POOL_ROW_BLOCK = 128
VMEM_LIMIT_BYTES = 60 * 1024 * 1024

_OFFS = [sum(IN_SPLITS[:j]) for j in range(len(IN_SPLITS) + 1)]
(PIN_ROWS, PGATE_ROWS, Q_ROWS, K_ROWS, V_ROWS, GGATE_ROWS) = (slice(_OFFS[j], _OFFS[j + 1]) for j in range(6))
LR_ROWS = slice(_OFFS[6], _OFFS[8])
BGP_ROWS = slice(_OFFS[8], _OFFS[9])
BGG_ROWS = slice(_OFFS[9], _OFFS[10])
LR_WIDTH = 2 * GLA_GATE_RANK
WT_SLAB = 240

F32 = jnp.float32
BF16 = jnp.bfloat16


def _dot(a, b):
    return jnp.dot(a, b, preferred_element_type=F32)


def _dot_nt(a, b):
    return lax.dot_general(a, b, (((1,), (1,)), ((), ())), preferred_element_type=F32)


def _dot_tn(a, b):
    return lax.dot_general(a, b, (((0,), (0,)), ((), ())), preferred_element_type=F32)


def _sigmoid(x):
    return 0.5 * jnp.tanh(0.5 * x) + 0.5


def _silu(x):
    return x * _sigmoid(x)


def _log_sigmoid(x):
    return jnp.minimum(x, 0.0) - jnp.log(1.0 + jnp.exp(-jnp.abs(x)))


def _norm_mod(xv, g_pre, scale, shift):
    ms = jnp.mean(xv * xv, axis=-1, keepdims=True)
    hn = (xv * lax.rsqrt(ms + RMS_EPS)) * g_pre
    return hn * (1.0 + scale) + shift


def _split_hi_lo(a):
    hi = a.astype(BF16)
    lo = (a - hi.astype(F32)).astype(BF16)
    return hi, lo


def _run_interleaved(gens, skew):
    live = [True] * len(gens)
    step = 0
    while any(live):
        for j, g in enumerate(gens):
            if live[j] and step >= j * skew:
                try:
                    next(g)
                except StopIteration:
                    live[j] = False
        step += 1


def _ada_kernel(c_ref, w_ref, b_ref, o_ref):
    s = _silu(c_ref[...])
    o_ref[...] = jnp.sum(w_ref[...] * s, axis=0, keepdims=True) + b_ref[...]


def _ada_call(c_col, w_ada, b_ada):
    blk = 4 * LANES
    n_blk = (3 * D_MODEL) // blk
    return pl.pallas_call(
        _ada_kernel,
        grid=(n_blk,),
        in_specs=[
            pl.BlockSpec((D_MODEL, 1), lambda j: (0, 0)),
            pl.BlockSpec((D_MODEL, blk), lambda j: (0, j)),
            pl.BlockSpec((1, blk), lambda j: (0, j)),
        ],
        out_specs=pl.BlockSpec((1, blk), lambda j: (0, j)),
        out_shape=jax.ShapeDtypeStruct((1, 3 * D_MODEL), F32),
        compiler_params=pltpu.CompilerParams(dimension_semantics=("arbitrary",)),
        name="ada_call",
    )(c_col, w_ada, b_ada)


def _prep_stages(sub, refs):
    (gkcat_ref, gbias_ref, sb_ref, k_ref, v_ref, cf_ref, rb_ref, lr_scr, st_scr) = refs
    t = SUB_TILE
    rows = slice(sub * t, (sub + 1) * t)
    k = k_ref[rows, :]
    v_bf = v_ref[rows, :]

    la = _log_sigmoid(_dot(lr_scr[rows, :], gkcat_ref[...]) + gbias_ref[...]) / GLA_GATE_NORMALIZER
    laf_hi, laf_lo = _split_hi_lo(la[:, :GLA_KEY_WIDTH])
    lab_hi, lab_lo = _split_hi_lo(la[:, GLA_KEY_WIDTH:])
    yield

    row = lax.broadcasted_iota(jnp.int32, (t, t), 0)
    col = lax.broadcasted_iota(jnp.int32, (t, t), 1)
    same_chunk = (row // GLA_CHUNK) == (col // GLA_CHUNK)
    lower = jnp.logical_and(same_chunk, col <= row).astype(BF16)
    upper = jnp.logical_and(same_chunk, col >= row).astype(BF16)
    cf_ref[rows, :] = _dot(lower, laf_hi) + _dot(lower, laf_lo)
    rb = _dot(upper, lab_hi) + _dot(upper, lab_lo)
    rb_ref[rows, :] = rb
    yield

    n_chunks = t // GLA_CHUNK
    run = None
    parts = []
    for c in range(n_chunks):
        first = c * GLA_CHUNK
        tot_c = rb[first:first + 1, :]
        run = tot_c if run is None else run + tot_c
        parts.append(run - rb[first:first + GLA_CHUNK, :])
    k_end = (k * jnp.exp(jnp.concatenate(parts, axis=0))).astype(BF16)
    decay = jnp.exp(run)
    kvs = []
    for hd in range(GLA_HEADS):
        ks = slice(hd * GLA_HEAD_K, (hd + 1) * GLA_HEAD_K)
        vs = slice(hd * GLA_HEAD_V, (hd + 1) * GLA_HEAD_V)
        kvs.append(_dot_tn(v_bf[:, vs], k_end[:, ks]))
    yield

    for hd in range(GLA_HEADS):
        ks = slice(hd * GLA_HEAD_K, (hd + 1) * GLA_HEAD_K)
        st = st_scr[hd]
        sb_ref[sub, hd] = st
        st_scr[hd] = st * decay[:, ks] + kvs[hd]
    yield


def _prep_kernel(x_ref, mod_ref, gpre_ref, gkcat_ref, gbias_ref, wk_ref, wv_ref, wlr_ref,
                 win_ref, wpool_ref, wpp_ref, wpgla_ref, wout_ref,
                 sb_ref, k_ref, v_ref, h_ref, cf_ref, rb_ref,
                 winbf_ref, wpoolbf_ref, wppbf_ref, wpglabf_ref, woutbf_ref,
                 st_scr, wk_scr, wv_scr, wlr_scr, lr_scr):
    @pl.when(pl.program_id(0) == 0)
    def _():
        st_scr[...] = jnp.zeros_like(st_scr)
        wk_scr[...] = wk_ref[...].astype(BF16)
        wv_scr[...] = wv_ref[...].astype(BF16)
        wlr_scr[...] = wlr_ref[...].astype(BF16)

    winbf_ref[...] = win_ref[...].astype(BF16)
    wpoolbf_ref[...] = wpool_ref[...].astype(BF16)
    wppbf_ref[...] = wpp_ref[...].astype(BF16)
    wpglabf_ref[...] = wpgla_ref[...].astype(BF16)
    woutbf_ref[...] = wout_ref[...].astype(BF16)

    h = _norm_mod(x_ref[...], gpre_ref[...], mod_ref[1:2, :], mod_ref[0:1, :]).astype(BF16)
    h_ref[...] = h
    k_ref[...] = _dot_nt(h, wk_scr[...])
    lr_scr[...] = _dot_nt(h, wlr_scr[...]).astype(BF16)
    v_ref[...] = _dot_nt(h, wv_scr[...]).astype(BF16)

    refs = (gkcat_ref, gbias_ref, sb_ref, k_ref, v_ref, cf_ref, rb_ref, lr_scr, st_scr)
    gens = [_prep_stages(sub, refs) for sub in reversed(range(PREP_SUBTILES))]
    _run_interleaved(gens, STAGE_SKEW)


def _prep_call(x2, mod, g_pre, gk_cat, gbias_cat, w_in_t, pool_w2, w_proj_pool, w_proj_gla, w_out):
    s = x2.shape[0]
    tb = PREP_SUBTILES * SUB_TILE
    nb = s // tb
    slab = D_MODEL // nb
    n_wt = IN_WIDTH // WT_SLAB
    const = lambda i: (0, 0)
    rev = lambda i: (nb - 1 - i, 0)
    fwd = lambda i: (i, 0)
    wt_slab = lambda i: (jnp.minimum(i, n_wt - 1), 0)
    one = pl.Buffered(1)

    def rows_of(rs):
        n = rs.stop - rs.start
        return pl.BlockSpec((n, D_MODEL), lambda i: (rs.start // n, 0), pipeline_mode=one)

    return pl.pallas_call(
        _prep_kernel,
        grid=(nb,),
        in_specs=[
            pl.BlockSpec((tb, D_MODEL), rev),
            pl.BlockSpec((3, D_MODEL), const),
            pl.BlockSpec((1, D_MODEL), const),
            pl.BlockSpec(gk_cat.shape, const),
            pl.BlockSpec(gbias_cat.shape, const),
            rows_of(K_ROWS),
            rows_of(V_ROWS),
            rows_of(LR_ROWS),
            pl.BlockSpec((WT_SLAB, D_MODEL), wt_slab),
            pl.BlockSpec((slab, POOL_GROUP_WIDTH), fwd),
            pl.BlockSpec((slab, D_MODEL), fwd),
            pl.BlockSpec((slab, D_MODEL), fwd),
            pl.BlockSpec((slab, D_MODEL), fwd),
        ],
        out_specs=[
            pl.BlockSpec((PREP_SUBTILES, GLA_HEADS, GLA_HEAD_V, GLA_HEAD_K), lambda i: (nb - 1 - i, 0, 0, 0)),
            pl.BlockSpec((tb, GLA_KEY_WIDTH), rev),
            pl.BlockSpec((tb, D_MODEL), rev),
            pl.BlockSpec((tb, D_MODEL), rev),
            pl.BlockSpec((tb, GLA_KEY_WIDTH), rev),
            pl.BlockSpec((tb, GLA_KEY_WIDTH), rev),
            pl.BlockSpec((WT_SLAB, D_MODEL), wt_slab),
            pl.BlockSpec((slab, POOL_GROUP_WIDTH), fwd),
            pl.BlockSpec((slab, D_MODEL), fwd),
            pl.BlockSpec((slab, D_MODEL), fwd),
            pl.BlockSpec((slab, D_MODEL), fwd),
        ],
        out_shape=[
            jax.ShapeDtypeStruct((s // SUB_TILE, GLA_HEADS, GLA_HEAD_V, GLA_HEAD_K), F32),
            jax.ShapeDtypeStruct((s, GLA_KEY_WIDTH), F32),
            jax.ShapeDtypeStruct((s, D_MODEL), BF16),
            jax.ShapeDtypeStruct((s, D_MODEL), BF16),
            jax.ShapeDtypeStruct((s, GLA_KEY_WIDTH), F32),
            jax.ShapeDtypeStruct((s, GLA_KEY_WIDTH), F32),
            jax.ShapeDtypeStruct((IN_WIDTH, D_MODEL), BF16),
            jax.ShapeDtypeStruct(pool_w2.shape, BF16),
            jax.ShapeDtypeStruct((D_MODEL, D_MODEL), BF16),
            jax.ShapeDtypeStruct((D_MODEL, D_MODEL), BF16),
            jax.ShapeDtypeStruct((D_MODEL, D_MODEL), BF16),
        ],
        scratch_shapes=[
            pltpu.VMEM((GLA_HEADS, GLA_HEAD_V, GLA_HEAD_K), F32),
            pltpu.VMEM((GLA_KEY_WIDTH, D_MODEL), BF16),
            pltpu.VMEM((D_MODEL, D_MODEL), BF16),
            pltpu.VMEM((LR_WIDTH, D_MODEL), BF16),
            pltpu.VMEM((tb, LR_WIDTH), BF16),
        ],
        compiler_params=pltpu.CompilerParams(dimension_semantics=("arbitrary",),
                                             vmem_limit_bytes=VMEM_LIMIT_BYTES),
        name="prep_call",
    )(x2, mod, g_pre, gk_cat, gbias_cat, w_in_t, w_in_t, w_in_t, w_in_t, pool_w2, w_proj_pool, w_proj_gla, w_out)


def _subtile_stages(seq_len, sub, refs):
    (x_ref, mod_ref, gpost_ref, k_ref, v_ref, cf_ref, rb_ref,
     wt_ref,
     poolw_ref, pscale_ref, glag_ref,
     wpp_ref, wpgla_ref, wout_ref, sb_ref, o_ref,
     sf_scr, hext_scr, q_scr, qs_scr, qd_scr, ke_scr, kif_scr, kib_scr,
     kv_scr, scat_scr, sc_scr, og_scr) = refs
    t = MAIN_TILE
    r0 = sub * t
    rows = slice(r0, r0 + t)
    tile_row0 = pl.program_id(0) * (MAIN_SUBTILES * t) + r0
    sb_idx = (sub + 1) * (MAIN_TILE // SUB_TILE) - 1
    n_pairs = t // GLA_PAIR
    hk, hv = GLA_HEAD_K, GLA_HEAD_V
    gate = mod_ref[2:3, :]

    h_ext = hext_scr[r0:r0 + t + 2 * HALO, :]
    h = hext_scr[r0 + HALO:r0 + HALO + t, :]

    q_scr[rows, :] = _dot_nt(h, wt_ref[Q_ROWS, :]) * (GLA_HEAD_K ** -0.5)
    te = t + 2 * HALO
    g_row = tile_row0 - HALO + lax.broadcasted_iota(jnp.int32, (te, 1), 0)
    valid = jnp.logical_and(g_row >= 0, g_row < seq_len)
    u_ext = jnp.where(valid, _dot_nt(h_ext, wt_ref[PIN_ROWS, :]), 0.0)
    yield

    p_gate = _dot_nt(h, wt_ref[PGATE_ROWS, :])
    u_ext_bf = u_ext.astype(BF16)
    u = u_ext[HALO:HALO + t, :]
    bg_pool = _sigmoid(_dot_nt(h, wt_ref[BGP_ROWS, :]))
    yield

    pos = tile_row0 + lax.broadcasted_iota(jnp.int32, (t, 1), 0)
    rb_ = POOL_ROW_BLOCK
    kb_ = POOL_ROW_BLOCK + 2 * HALO
    band_r = lax.broadcasted_iota(jnp.int32, (rb_, kb_), 0)
    band_c = lax.broadcasted_iota(jnp.int32, (rb_, kb_), 1)
    band_d = band_c - HALO - band_r
    a_parts = []
    for gi, w in enumerate(POOL_WINDOWS):
        hw = w // 2
        cs = slice(gi * POOL_GROUP_WIDTH, (gi + 1) * POOL_GROUP_WIDTH)
        ws = slice(gi * POOL_GROUP_WIDTH, (gi + 1) * POOL_GROUP_WIDTH)
        band = jnp.logical_and(band_d >= -hw, band_d < hw).astype(BF16)
        wsum = jnp.concatenate(
            [_dot(band, u_ext_bf[b * rb_:b * rb_ + kb_, cs]) for b in range(t // rb_)], axis=0)
        count = (jnp.minimum(pos + hw, seq_len) - jnp.maximum(pos - hw, 0)).astype(F32)
        pooled = wsum / count - u[:, cs]
        a_parts.append(_dot(pooled.astype(BF16), poolw_ref[ws, :]))
    a = jnp.concatenate(a_parts, axis=1) * pscale_ref[...]
    z_pool = (a * _silu(p_gate)).astype(BF16)
    gla_gate = _dot_nt(h, wt_ref[GGATE_ROWS, :])
    bg_gla = _sigmoid(_dot_nt(h, wt_ref[BGG_ROWS, :]))

    def put_heads(dst, rs, fwd, bwd):
        for hd in range(GLA_HEADS):
            ks = slice(hd * hk, (hd + 1) * hk)
            dst[rs, 2 * hd * hk:(2 * hd + 1) * hk] = fwd[:, ks].astype(BF16)
            dst[rs, (2 * hd + 1) * hk:(2 * hd + 2) * hk] = bwd[:, ks].astype(BF16)

    dec_f, dec_b = [], []
    for p in range(n_pairs):
        c0 = r0 + p * GLA_PAIR
        c1 = c0 + GLA_CHUNK
        ra = slice(c0, c1)
        rb = slice(c1, c1 + GLA_CHUNK)
        q0, q1 = q_scr[ra, :], q_scr[rb, :]
        k0, k1 = k_ref[ra, :], k_ref[rb, :]
        cum0, cum1 = cf_ref[ra, :], cf_ref[rb, :]
        tot0 = cf_ref[c1 - 1:c1, :]
        tot1 = cf_ref[c1 + GLA_CHUNK - 1:c1 + GLA_CHUNK, :]
        rc0, rc1 = rb_ref[ra, :], rb_ref[rb, :]
        tb0 = rb_ref[c0:c0 + 1, :]
        tb1 = rb_ref[c1:c1 + 1, :]
        e_tot0, e_tot1 = jnp.exp(tot0), jnp.exp(tot1)
        e_tb0, e_tb1 = jnp.exp(tb0), jnp.exp(tb1)

        a0 = cum0 - tot0
        ks_f0 = k0 * jnp.exp(-a0)
        qs_f1 = q1 * jnp.exp(cum1)
        qs_b0 = q0 * jnp.exp(rc0)
        put_heads(qs_scr, ra, q0 * jnp.exp(a0), qs_b0)
        put_heads(qd_scr, ra, q0 * jnp.exp(cum0), qs_b0 * e_tb1)
        kif_scr[ra, :] = ks_f0.astype(BF16)
        kif_scr[rb, :] = (k1 * jnp.exp(-cum1)).astype(BF16)
        a1 = rc1 - tb1
        ks_b1 = k1 * jnp.exp(-a1)
        put_heads(qs_scr, rb, qs_f1, q1 * jnp.exp(a1))
        put_heads(qd_scr, rb, qs_f1 * e_tot0, q1 * jnp.exp(rc1))
        kib_scr[ra, :] = (k0 * jnp.exp(-rc0)).astype(BF16)
        kib_scr[rb, :] = ks_b1.astype(BF16)
        put_heads(ke_scr, ra, ks_f0 * e_tot1, k0 * jnp.exp(tb0 - rc0))
        put_heads(ke_scr, rb, k1 * jnp.exp(tot1 - cum1), ks_b1 * e_tb0)
        dec_f.append(e_tot0 * e_tot1)
        dec_b.append(e_tb0 * e_tb1)
    yield

    merged = bg_pool * _dot(z_pool, wpp_ref[...])
    for hd in range(GLA_HEADS):
        vs = slice(hd * hv, (hd + 1) * hv)
        cat = slice(2 * hd * hk, (2 * hd + 2) * hk)
        for p in range(n_pairs):
            rs = slice(r0 + p * GLA_PAIR, r0 + (p + 1) * GLA_PAIR)
            kv_scr[sub, hd, p] = _dot_tn(v_ref[rs, vs], ke_scr[rs, cat])
    yield

    prow = lax.broadcasted_iota(jnp.int32, (GLA_PAIR, GLA_PAIR), 0)
    pcol = lax.broadcasted_iota(jnp.int32, (GLA_PAIR, GLA_PAIR), 1)
    mask_f = pcol <= prow
    for p in range(n_pairs):
        rs = slice(r0 + p * GLA_PAIR, r0 + (p + 1) * GLA_PAIR)
        for hd in range(GLA_HEADS):
            ks = slice(hd * hk, (hd + 1) * hk)
            sc_f = _dot_nt(qs_scr[rs, 2 * hd * hk:(2 * hd + 1) * hk], kif_scr[rs, ks])
            sc_b = _dot_nt(qs_scr[rs, (2 * hd + 1) * hk:(2 * hd + 2) * hk], kib_scr[rs, ks])
            sc_scr[sub, p, hd] = jnp.where(mask_f, sc_f, sc_b).astype(BF16)
    for hd in range(GLA_HEADS):
        ks = slice(hd * hk, (hd + 1) * hk)
        sf = sf_scr[hd]
        for p in range(n_pairs):
            scat_scr[sub, p, hd, 0:hk, :] = sf.T.astype(BF16)
            sf = sf * dec_f[p][:, ks] + kv_scr[sub, hd, p, :, 0:hk]
        sf_scr[hd] = sf
        sb = sb_ref[sb_idx, hd]
        for p in reversed(range(n_pairs)):
            scat_scr[sub, p, hd, hk:2 * hk, :] = sb.T.astype(BF16)
            if p > 0:
                sb = sb * dec_b[p][:, ks] + kv_scr[sub, hd, p, :, hk:2 * hk]
    yield

    for p in range(n_pairs):
        rs = slice(r0 + p * GLA_PAIR, r0 + (p + 1) * GLA_PAIR)
        for hd in range(GLA_HEADS):
            vs = slice(hd * hv, (hd + 1) * hv)
            cat = slice(2 * hd * hk, (2 * hd + 2) * hk)
            og_scr[rs, vs] = (_dot(sc_scr[sub, p, hd], v_ref[rs, vs])
                              + _dot(qd_scr[rs, cat], scat_scr[sub, p, hd]))
    yield

    glag = glag_ref[...]
    z_parts = []
    for hd in range(GLA_HEADS):
        vs = slice(hd * hv, (hd + 1) * hv)
        oh = og_scr[rows, vs]
        ms = jnp.mean(oh * oh, axis=-1, keepdims=True)
        z_parts.append((oh * lax.rsqrt(ms + RMS_EPS)) * glag)
    o_n = jnp.concatenate(z_parts, axis=1)
    y_gla = _dot((o_n * _silu(gla_gate)).astype(BF16), wpgla_ref[...])
    merged = merged + bg_gla * y_gla
    yield

    out = _dot(merged.astype(BF16), wout_ref[...])
    yield

    ms = jnp.mean(out * out, axis=-1, keepdims=True)
    out_n = (out * lax.rsqrt(ms + RMS_EPS)) * gpost_ref[...]
    o_ref[rows, :] = x_ref[rows, :] + gate * out_n
    yield


def _main_kernel(seq_len,
                 x_ref, h_ref, hp_ref, hn_ref, mod_ref, gpost_ref, k_ref, v_ref, cf_ref, rb_ref,
                 wt_ref,
                 poolw_ref, pscale_ref, glag_ref,
                 wpp_ref, wpgla_ref, wout_ref, sb_ref,
                 o_ref,
                 sf_scr, hext_scr, q_scr, qs_scr, qd_scr, ke_scr, kif_scr, kib_scr,
                 kv_scr, scat_scr, sc_scr, og_scr):
    tb = MAIN_SUBTILES * MAIN_TILE

    @pl.when(pl.program_id(0) == 0)
    def _():
        sf_scr[...] = jnp.zeros_like(sf_scr)

    hext_scr[0:HALO, :] = hp_ref[...]
    hext_scr[HALO:HALO + tb, :] = h_ref[...]
    hext_scr[HALO + tb:HALO + tb + HALO, :] = hn_ref[...]

    refs = (x_ref, mod_ref, gpost_ref, k_ref, v_ref, cf_ref, rb_ref,
            wt_ref,
            poolw_ref, pscale_ref, glag_ref,
            wpp_ref, wpgla_ref, wout_ref, sb_ref, o_ref,
            sf_scr, hext_scr, q_scr, qs_scr, qd_scr, ke_scr, kif_scr, kib_scr,
            kv_scr, scat_scr, sc_scr, og_scr)
    gens = [_subtile_stages(seq_len, sub, refs) for sub in range(MAIN_SUBTILES)]
    _run_interleaved(gens, STAGE_SKEW)


def _main_call(x2, h_all, mod, g_post, k_all, v_all, cf_all, rb_all, w_bf, small, wpool_bf, wpp_bf, wpgla_bf, wout_bf, sb):
    s = x2.shape[0]
    tb = MAIN_SUBTILES * MAIN_TILE
    nt = s // tb
    hb = tb // HALO
    n_hb = s // HALO
    n_pairs = MAIN_TILE // GLA_PAIR
    const2 = lambda i: (0, 0)
    tile = lambda i: (i, 0)
    one = pl.Buffered(1)
    pool_scale, gla_norm_g = small

    def resident(arr):
        return pl.BlockSpec(arr.shape, const2, pipeline_mode=one)

    in_specs = [
        pl.BlockSpec((tb, D_MODEL), tile),
        pl.BlockSpec((tb, D_MODEL), tile),
        pl.BlockSpec((HALO, D_MODEL), lambda i: (jnp.maximum(i * hb - 1, 0), 0)),
        pl.BlockSpec((HALO, D_MODEL), lambda i: (jnp.minimum((i + 1) * hb, n_hb - 1), 0)),
        pl.BlockSpec((3, D_MODEL), const2),
        pl.BlockSpec((1, D_MODEL), const2),
        pl.BlockSpec((tb, GLA_KEY_WIDTH), tile),
        pl.BlockSpec((tb, D_MODEL), tile),
        pl.BlockSpec((tb, GLA_KEY_WIDTH), tile),
        pl.BlockSpec((tb, GLA_KEY_WIDTH), tile),
        resident(w_bf),
        resident(wpool_bf), resident(pool_scale), resident(gla_norm_g),
        resident(wpp_bf), resident(wpgla_bf), resident(wout_bf),
        pl.BlockSpec((tb // SUB_TILE, GLA_HEADS, GLA_HEAD_V, GLA_HEAD_K), lambda i: (i, 0, 0, 0)),
    ]
    scratch = [
        pltpu.VMEM((GLA_HEADS, GLA_HEAD_V, GLA_HEAD_K), F32),
        pltpu.VMEM((tb + 2 * HALO, D_MODEL), BF16),
        pltpu.VMEM((tb, GLA_KEY_WIDTH), F32),
        pltpu.VMEM((tb, 2 * GLA_KEY_WIDTH), BF16),
        pltpu.VMEM((tb, 2 * GLA_KEY_WIDTH), BF16),
        pltpu.VMEM((tb, 2 * GLA_KEY_WIDTH), BF16),
        pltpu.VMEM((tb, GLA_KEY_WIDTH), BF16),
        pltpu.VMEM((tb, GLA_KEY_WIDTH), BF16),
        pltpu.VMEM((MAIN_SUBTILES, GLA_HEADS, n_pairs, GLA_HEAD_V, 2 * GLA_HEAD_K), F32),
        pltpu.VMEM((MAIN_SUBTILES, n_pairs, GLA_HEADS, 2 * GLA_HEAD_K, GLA_HEAD_V), BF16),
        pltpu.VMEM((MAIN_SUBTILES, n_pairs, GLA_HEADS, GLA_PAIR, GLA_PAIR), BF16),
        pltpu.VMEM((tb, D_MODEL), F32),
    ]
    return pl.pallas_call(
        functools.partial(_main_kernel, s),
        grid=(nt,),
        in_specs=in_specs,
        out_specs=pl.BlockSpec((tb, D_MODEL), tile),
        out_shape=jax.ShapeDtypeStruct((s, D_MODEL), F32),
        scratch_shapes=scratch,
        compiler_params=pltpu.CompilerParams(dimension_semantics=("arbitrary",),
                                             vmem_limit_bytes=VMEM_LIMIT_BYTES),
        name="main_call",
    )(x2, h_all, h_all, h_all, mod, g_post, k_all, v_all, cf_all, rb_all,
      w_bf,
      wpool_bf, pool_scale, gla_norm_g, wpp_bf, wpgla_bf, wout_bf, sb)


def _layer(x2, c, w_ada, b_ada, g_pre, g_post, w_in, pool_w, pool_scale, gk_up_fwd, gk_bias_fwd,
           gk_up_bwd, gk_bias_bwd, gla_norm_g, w_proj_pool, w_proj_gla, w_out):
    ada = _ada_call(c.reshape(D_MODEL, 1), w_ada, b_ada.reshape(1, -1))
    mod = ada.reshape(3, D_MODEL)

    zeros = jnp.zeros((GLA_GATE_RANK, GLA_KEY_WIDTH), F32)
    gk_cat = jnp.concatenate([jnp.concatenate([gk_up_fwd, zeros], axis=1),
                              jnp.concatenate([zeros, gk_up_bwd], axis=1)], axis=0).astype(BF16)
    gbias_cat = jnp.concatenate([gk_bias_fwd, gk_bias_bwd]).reshape(1, -1)

    pool_w2 = pool_w.reshape(POOL_GROUPS * POOL_GROUP_WIDTH, POOL_GROUP_WIDTH)
    sb, k_all, v_all, h_all, cf_all, rb_all, w_bf, wpool_bf, wpp_bf, wpgla_bf, wout_bf = _prep_call(
        x2, mod, g_pre.reshape(1, -1), gk_cat, gbias_cat, w_in.T, pool_w2, w_proj_pool, w_proj_gla, w_out)

    small = (pool_scale.reshape(1, -1), gla_norm_g.reshape(1, -1))
    return _main_call(x2, h_all, mod, g_post.reshape(1, -1), k_all, v_all, cf_all, rb_all, w_bf, small,
                      wpool_bf, wpp_bf, wpgla_bf, wout_bf, sb)


def kernel(x, c, w_ada, b_ada, g_pre, g_post, w_in, pool_w, pool_scale, gk_up_fwd, gk_bias_fwd, gk_up_bwd, gk_bias_bwd, gla_norm_g, w_proj_pool, w_proj_gla, w_out):
    b, s, d = x.shape
    depth = w_in.shape[0]
    xf = x.reshape(b * s, d)
    outs = []
    for bi in range(b):
        xb = xf if b == 1 else lax.slice_in_dim(xf, bi * s, (bi + 1) * s, axis=0)
        for l in range(depth):
            xb = _layer(xb, c[bi:bi + 1], w_ada[l], b_ada[l], g_pre[l], g_post[l], w_in[l], pool_w[l],
                        pool_scale[l], gk_up_fwd[l], gk_bias_fwd[l], gk_up_bwd[l], gk_bias_bwd[l],
                        gla_norm_g[l], w_proj_pool[l], w_proj_gla[l], w_out[l])
        outs.append(xb)
    out = outs[0] if b == 1 else jnp.concatenate(outs, axis=0)
    return out.reshape(b, s, d)
```

```python
import functools

import jax
import jax.numpy as jnp
from jax import lax
from jax.experimental import pallas as pl
from jax.experimental.pallas import tpu as pltpu

D_MODEL = 1024
POOL_GROUPS = 4
POOL_GROUP_WIDTH = 256
POOL_WINDOWS = (2, 4, 8, 16)
GLA_HEADS = 4
GLA_KEY_WIDTH = 512
GLA_HEAD_K = 128
GLA_HEAD_V = 256
GLA_GATE_RANK = 16
GLA_GATE_NORMALIZER = 16.0
GLA_CHUNK = 64
GLA_PAIR = 2 * GLA_CHUNK
RMS_EPS = 1e-6
IN_SPLITS = (1024, 1024, 512, 512, 1024, 1024, 16, 16, 1024, 1024)
IN_WIDTH = sum(IN_SPLITS)

LANES = 128
SUB_TILE = 256
PREP_SUBTILES = 4
MAIN_TILE = 512
MAIN_SUBTILES = 1
CUMSUM_BLOCK = 256
HALO = 16
POOL_ROW_BLOCK = 128
VMEM_LIMIT_BYTES = 60 * 1024 * 1024

_OFFS = [sum(IN_SPLITS[:j]) for j in range(len(IN_SPLITS) + 1)]
(PIN_ROWS, PGATE_ROWS, Q_ROWS, K_ROWS, V_ROWS, GGATE_ROWS) = (slice(_OFFS[j], _OFFS[j + 1]) for j in range(6))
LR_ROWS = slice(_OFFS[6], _OFFS[8])
BGP_ROWS = slice(_OFFS[8], _OFFS[9])
BGG_ROWS = slice(_OFFS[9], _OFFS[10])
LR_WIDTH = 2 * GLA_GATE_RANK
WT_SLAB = 480

F32 = jnp.float32
BF16 = jnp.bfloat16


def _dot(a, b):
    return jnp.dot(a, b, preferred_element_type=F32)


def _dot_nt(a, b):
    return lax.dot_general(a, b, (((1,), (1,)), ((), ())), preferred_element_type=F32)


def _dot_tn(a, b):
    return lax.dot_general(a, b, (((0,), (0,)), ((), ())), preferred_element_type=F32)


def _sigmoid(x):
    return 0.5 * jnp.tanh(0.5 * x) + 0.5


def _silu(x):
    return x * _sigmoid(x)


def _log_sigmoid(x):
    return jnp.minimum(x, 0.0) - jnp.log(1.0 + jnp.exp(-jnp.abs(x)))


def _norm_mod(xv, g_pre, scale, shift):
    ms = jnp.mean(xv * xv, axis=-1, keepdims=True)
    hn = (xv * lax.rsqrt(ms + RMS_EPS)) * g_pre
    return hn * (1.0 + scale) + shift


def _split_hi_lo(a):
    hi = a.astype(BF16)
    lo = (a - hi.astype(F32)).astype(BF16)
    return hi, lo


def _run_round_robin(gens):
    live = list(gens)
    while live:
        for g in list(live):
            try:
                next(g)
            except StopIteration:
                live.remove(g)


def _ada_kernel(c_ref, w_ref, b_ref, o_ref):
    s = _silu(c_ref[...])
    o_ref[...] = jnp.sum(w_ref[...] * s, axis=0, keepdims=True) + b_ref[...]


def _ada_call(c_col, w_ada, b_ada):
    blk = 4 * LANES
    n_blk = (3 * D_MODEL) // blk
    return pl.pallas_call(
        _ada_kernel,
        grid=(n_blk,),
        in_specs=[
            pl.BlockSpec((D_MODEL, 1), lambda j: (0, 0)),
            pl.BlockSpec((D_MODEL, blk), lambda j: (0, j)),
            pl.BlockSpec((1, blk), lambda j: (0, j)),
        ],
        out_specs=pl.BlockSpec((1, blk), lambda j: (0, j)),
        out_shape=jax.ShapeDtypeStruct((1, 3 * D_MODEL), F32),
        compiler_params=pltpu.CompilerParams(dimension_semantics=("arbitrary",)),
        name="ada_call",
    )(c_col, w_ada, b_ada)


def _prep_stages(sub, refs):
    (gkb_ref, gbias_ref, sb_ref, k_ref, v_ref, lr_ref, st_scr) = refs
    t = SUB_TILE
    rows = slice(sub * t, (sub + 1) * t)
    k = k_ref[rows, :]
    v_bf = v_ref[rows, :]

    la = _log_sigmoid(_dot(lr_ref[rows, :], gkb_ref[...]) + gbias_ref[...]) / GLA_GATE_NORMALIZER
    la_hi, la_lo = _split_hi_lo(la)
    yield

    row = lax.broadcasted_iota(jnp.int32, (t, t), 0)
    col = lax.broadcasted_iota(jnp.int32, (t, t), 1)
    upper = (col >= row).astype(BF16)
    rc = _dot(upper, la_hi) + _dot(upper, la_lo)
    yield

    tot = rc[0:1, :]
    k_end = (k * jnp.exp(tot - rc)).astype(BF16)
    decay = jnp.exp(tot)
    kvs = []
    for hd in range(GLA_HEADS):
        ks = slice(hd * GLA_HEAD_K, (hd + 1) * GLA_HEAD_K)
        vs = slice(hd * GLA_HEAD_V, (hd + 1) * GLA_HEAD_V)
        kvs.append(_dot_tn(v_bf[:, vs], k_end[:, ks]))
    yield

    for hd in range(GLA_HEADS):
        ks = slice(hd * GLA_HEAD_K, (hd + 1) * GLA_HEAD_K)
        st = st_scr[hd]
        sb_ref[sub, hd] = st
        st_scr[hd] = st * decay[:, ks] + kvs[hd]
    yield


def _prep_kernel(x_ref, mod_ref, gpre_ref, gkb_ref, gbias_ref, wk_ref, wv_ref, wlr_ref,
                 win_ref, wpool_ref, wpp_ref, wpgla_ref, wout_ref,
                 sb_ref, k_ref, v_ref, h_ref, lr_ref, winbf_ref, wpoolbf_ref, wppbf_ref, wpglabf_ref, woutbf_ref,
                 st_scr, wk_scr, wv_scr, wlr_scr):
    @pl.when(pl.program_id(0) == 0)
    def _():
        st_scr[...] = jnp.zeros_like(st_scr)
        wk_scr[...] = wk_ref[...].astype(BF16)
        wv_scr[...] = wv_ref[...].astype(BF16)
        wlr_scr[...] = wlr_ref[...].astype(BF16)

    winbf_ref[...] = win_ref[...].astype(BF16)
    wpoolbf_ref[...] = wpool_ref[...].astype(BF16)
    wppbf_ref[...] = wpp_ref[...].astype(BF16)
    wpglabf_ref[...] = wpgla_ref[...].astype(BF16)
    woutbf_ref[...] = wout_ref[...].astype(BF16)

    h = _norm_mod(x_ref[...], gpre_ref[...], mod_ref[1:2, :], mod_ref[0:1, :]).astype(BF16)
    h_ref[...] = h
    k_ref[...] = _dot_nt(h, wk_scr[...])
    lr_ref[...] = _dot_nt(h, wlr_scr[...]).astype(BF16)
    v_ref[...] = _dot_nt(h, wv_scr[...]).astype(BF16)

    refs = (gkb_ref, gbias_ref, sb_ref, k_ref, v_ref, lr_ref, st_scr)
    gens = [_prep_stages(sub, refs) for sub in reversed(range(PREP_SUBTILES))]
    _run_round_robin(gens)


def _prep_call(x2, mod, g_pre, gk_b, gbias_b, w_in_t, pool_w2, w_proj_pool, w_proj_gla, w_out):
    s = x2.shape[0]
    tb = PREP_SUBTILES * SUB_TILE
    nb = s // tb
    slab = D_MODEL // nb
    n_wt = IN_WIDTH // WT_SLAB
    const = lambda i: (0, 0)
    rev = lambda i: (nb - 1 - i, 0)
    fwd = lambda i: (i, 0)
    wt_slab = lambda i: (jnp.minimum(i, n_wt - 1), 0)
    one = pl.Buffered(1)

    def rows_of(rs):
        n = rs.stop - rs.start
        return pl.BlockSpec((n, D_MODEL), lambda i: (rs.start // n, 0), pipeline_mode=one)

    return pl.pallas_call(
        _prep_kernel,
        grid=(nb,),
        in_specs=[
            pl.BlockSpec((tb, D_MODEL), rev),
            pl.BlockSpec((3, D_MODEL), const),
            pl.BlockSpec((1, D_MODEL), const),
            pl.BlockSpec(gk_b.shape, const),
            pl.BlockSpec(gbias_b.shape, const),
            rows_of(K_ROWS),
            rows_of(V_ROWS),
            rows_of(LR_ROWS),
            pl.BlockSpec((WT_SLAB, D_MODEL), wt_slab),
            pl.BlockSpec((slab, POOL_GROUP_WIDTH), fwd),
            pl.BlockSpec((slab, D_MODEL), fwd),
            pl.BlockSpec((slab, D_MODEL), fwd),
            pl.BlockSpec((slab, D_MODEL), fwd),
        ],
        out_specs=[
            pl.BlockSpec((PREP_SUBTILES, GLA_HEADS, GLA_HEAD_V, GLA_HEAD_K), lambda i: (nb - 1 - i, 0, 0, 0)),
            pl.BlockSpec((tb, GLA_KEY_WIDTH), rev),
            pl.BlockSpec((tb, D_MODEL), rev),
            pl.BlockSpec((tb, D_MODEL), rev),
            pl.BlockSpec((tb, LR_WIDTH), rev),
            pl.BlockSpec((WT_SLAB, D_MODEL), wt_slab),
            pl.BlockSpec((slab, POOL_GROUP_WIDTH), fwd),
            pl.BlockSpec((slab, D_MODEL), fwd),
            pl.BlockSpec((slab, D_MODEL), fwd),
            pl.BlockSpec((slab, D_MODEL), fwd),
        ],
        out_shape=[
            jax.ShapeDtypeStruct((s // SUB_TILE, GLA_HEADS, GLA_HEAD_V, GLA_HEAD_K), F32),
            jax.ShapeDtypeStruct((s, GLA_KEY_WIDTH), F32),
            jax.ShapeDtypeStruct((s, D_MODEL), BF16),
            jax.ShapeDtypeStruct((s, D_MODEL), BF16),
            jax.ShapeDtypeStruct((s, LR_WIDTH), BF16),
            jax.ShapeDtypeStruct((IN_WIDTH, D_MODEL), BF16),
            jax.ShapeDtypeStruct(pool_w2.shape, BF16),
            jax.ShapeDtypeStruct((D_MODEL, D_MODEL), BF16),
            jax.ShapeDtypeStruct((D_MODEL, D_MODEL), BF16),
            jax.ShapeDtypeStruct((D_MODEL, D_MODEL), BF16),
        ],
        scratch_shapes=[
            pltpu.VMEM((GLA_HEADS, GLA_HEAD_V, GLA_HEAD_K), F32),
            pltpu.VMEM((GLA_KEY_WIDTH, D_MODEL), BF16),
            pltpu.VMEM((D_MODEL, D_MODEL), BF16),
            pltpu.VMEM((LR_WIDTH, D_MODEL), BF16),
        ],
        compiler_params=pltpu.CompilerParams(dimension_semantics=("arbitrary",),
                                             vmem_limit_bytes=VMEM_LIMIT_BYTES),
        name="prep_call",
    )(x2, mod, g_pre, gk_b, gbias_b, w_in_t, w_in_t, w_in_t, w_in_t, pool_w2, w_proj_pool, w_proj_gla, w_out)


def _subtile_stages(seq_len, sub, refs):
    (x_ref, mod_ref, gpost_ref, k_ref, v_ref, lr_ref,
     wt_ref,
     poolw_ref, pscale_ref, gkcat_ref, gbias_ref, glag_ref,
     wpp_ref, wpgla_ref, wout_ref, sb_ref, o_ref,
     sf_scr, hext_scr, q_scr, cf_scr, rb_scr, qs_scr, qd_scr, ke_scr, kif_scr, kib_scr,
     kv_scr, scat_scr, sc_scr, og_scr) = refs
    t = MAIN_TILE
    r0 = sub * t
    rows = slice(r0, r0 + t)
    tile_row0 = pl.program_id(0) * (MAIN_SUBTILES * t) + r0
    sb_idx = (sub + 1) * (MAIN_TILE // SUB_TILE) - 1
    n_pairs = t // GLA_PAIR
    hk, hv = GLA_HEAD_K, GLA_HEAD_V
    gate = mod_ref[2:3, :]

    h_ext = hext_scr[r0:r0 + t + 2 * HALO, :]
    h = hext_scr[r0 + HALO:r0 + HALO + t, :]

    q_scr[rows, :] = _dot_nt(h, wt_ref[Q_ROWS, :]) * (GLA_HEAD_K ** -0.5)
    la_pre = _dot(lr_ref[rows, :], gkcat_ref[...]) + gbias_ref[...]
    te = t + 2 * HALO
    g_row = tile_row0 - HALO + lax.broadcasted_iota(jnp.int32, (te, 1), 0)
    valid = jnp.logical_and(g_row >= 0, g_row < seq_len)
    u_ext = jnp.where(valid, _dot_nt(h_ext, wt_ref[PIN_ROWS, :]), 0.0)
    yield

    la = _log_sigmoid(la_pre) / GLA_GATE_NORMALIZER
    p_gate = _dot_nt(h, wt_ref[PGATE_ROWS, :])
    u_ext_bf = u_ext.astype(BF16)
    u = u_ext[HALO:HALO + t, :]
    laf_hi, laf_lo = _split_hi_lo(la[:, :GLA_KEY_WIDTH])
    lab_hi, lab_lo = _split_hi_lo(la[:, GLA_KEY_WIDTH:])
    yield

    cb = CUMSUM_BLOCK
    row = lax.broadcasted_iota(jnp.int32, (cb, cb), 0)
    col = lax.broadcasted_iota(jnp.int32, (cb, cb), 1)
    same_chunk = (row // GLA_CHUNK) == (col // GLA_CHUNK)
    lower = jnp.logical_and(same_chunk, col <= row).astype(BF16)
    upper = jnp.logical_and(same_chunk, col >= row).astype(BF16)
    for b in range(t // cb):
        rb_rows = slice(b * cb, (b + 1) * cb)
        dst = slice(r0 + b * cb, r0 + (b + 1) * cb)
        cf_scr[dst, :] = _dot(lower, laf_hi[rb_rows]) + _dot(lower, laf_lo[rb_rows])
        rb_scr[dst, :] = _dot(upper, lab_hi[rb_rows]) + _dot(upper, lab_lo[rb_rows])
    bg_pool = _sigmoid(_dot_nt(h, wt_ref[BGP_ROWS, :]))
    yield

    pos = tile_row0 + lax.broadcasted_iota(jnp.int32, (t, 1), 0)
    rb_ = POOL_ROW_BLOCK
    kb_ = POOL_ROW_BLOCK + 2 * HALO
    band_r = lax.broadcasted_iota(jnp.int32, (rb_, kb_), 0)
    band_c = lax.broadcasted_iota(jnp.int32, (rb_, kb_), 1)
    band_d = band_c - HALO - band_r
    a_parts = []
    for gi, w in enumerate(POOL_WINDOWS):
        hw = w // 2
        cs = slice(gi * POOL_GROUP_WIDTH, (gi + 1) * POOL_GROUP_WIDTH)
        ws = slice(gi * POOL_GROUP_WIDTH, (gi + 1) * POOL_GROUP_WIDTH)
        band = jnp.logical_and(band_d >= -hw, band_d < hw).astype(BF16)
        wsum = jnp.concatenate(
            [_dot(band, u_ext_bf[b * rb_:b * rb_ + kb_, cs]) for b in range(t // rb_)], axis=0)
        count = (jnp.minimum(pos + hw, seq_len) - jnp.maximum(pos - hw, 0)).astype(F32)
        pooled = wsum / count - u[:, cs]
        a_parts.append(_dot(pooled.astype(BF16), poolw_ref[ws, :]))
    a = jnp.concatenate(a_parts, axis=1) * pscale_ref[...]
    z_pool = (a * _silu(p_gate)).astype(BF16)
    gla_gate = _dot_nt(h, wt_ref[GGATE_ROWS, :])
    bg_gla = _sigmoid(_dot_nt(h, wt_ref[BGG_ROWS, :]))

    def put_heads(dst, rs, fwd, bwd):
        for hd in range(GLA_HEADS):
            ks = slice(hd * hk, (hd + 1) * hk)
            dst[rs, 2 * hd * hk:(2 * hd + 1) * hk] = fwd[:, ks].astype(BF16)
            dst[rs, (2 * hd + 1) * hk:(2 * hd + 2) * hk] = bwd[:, ks].astype(BF16)

    dec_f, dec_b = [], []
    for p in range(n_pairs):
        c0 = r0 + p * GLA_PAIR
        c1 = c0 + GLA_CHUNK
        ra = slice(c0, c1)
        rb = slice(c1, c1 + GLA_CHUNK)
        q0, q1 = q_scr[ra, :], q_scr[rb, :]
        k0, k1 = k_ref[ra, :], k_ref[rb, :]
        cum0, cum1 = cf_scr[ra, :], cf_scr[rb, :]
        tot0 = cf_scr[c1 - 1:c1, :]
        tot1 = cf_scr[c1 + GLA_CHUNK - 1:c1 + GLA_CHUNK, :]
        rc0, rc1 = rb_scr[ra, :], rb_scr[rb, :]
        tb0 = rb_scr[c0:c0 + 1, :]
        tb1 = rb_scr[c1:c1 + 1, :]
        e_tot0, e_tot1 = jnp.exp(tot0), jnp.exp(tot1)
        e_tb0, e_tb1 = jnp.exp(tb0), jnp.exp(tb1)

        a0 = cum0 - tot0
        ks_f0 = k0 * jnp.exp(-a0)
        qs_f1 = q1 * jnp.exp(cum1)
        qs_b0 = q0 * jnp.exp(rc0)
        put_heads(qs_scr, ra, q0 * jnp.exp(a0), qs_b0)
        put_heads(qd_scr, ra, q0 * jnp.exp(cum0), qs_b0 * e_tb1)
        kif_scr[ra, :] = ks_f0.astype(BF16)
        kif_scr[rb, :] = (k1 * jnp.exp(-cum1)).astype(BF16)
        a1 = rc1 - tb1
        ks_b1 = k1 * jnp.exp(-a1)
        put_heads(qs_scr, rb, qs_f1, q1 * jnp.exp(a1))
        put_heads(qd_scr, rb, qs_f1 * e_tot0, q1 * jnp.exp(rc1))
        kib_scr[ra, :] = (k0 * jnp.exp(-rc0)).astype(BF16)
        kib_scr[rb, :] = ks_b1.astype(BF16)
        put_heads(ke_scr, ra, ks_f0 * e_tot1, k0 * jnp.exp(tb0 - rc0))
        put_heads(ke_scr, rb, k1 * jnp.exp(tot1 - cum1), ks_b1 * e_tb0)
        dec_f.append(e_tot0 * e_tot1)
        dec_b.append(e_tb0 * e_tb1)
    yield

    merged = bg_pool * _dot(z_pool, wpp_ref[...])
    for hd in range(GLA_HEADS):
        vs = slice(hd * hv, (hd + 1) * hv)
        cat = slice(2 * hd * hk, (2 * hd + 2) * hk)
        for p in range(n_pairs):
            rs = slice(r0 + p * GLA_PAIR, r0 + (p + 1) * GLA_PAIR)
            kv_scr[sub, hd, p] = _dot_tn(v_ref[rs, vs], ke_scr[rs, cat])
    yield

    prow = lax.broadcasted_iota(jnp.int32, (GLA_PAIR, GLA_PAIR), 0)
    pcol = lax.broadcasted_iota(jnp.int32, (GLA_PAIR, GLA_PAIR), 1)
    mask_f = pcol <= prow
    for p in range(n_pairs):
        rs = slice(r0 + p * GLA_PAIR, r0 + (p + 1) * GLA_PAIR)
        for hd in range(GLA_HEADS):
            ks = slice(hd * hk, (hd + 1) * hk)
            sc_f = _dot_nt(qs_scr[rs, 2 * hd * hk:(2 * hd + 1) * hk], kif_scr[rs, ks])
            sc_b = _dot_nt(qs_scr[rs, (2 * hd + 1) * hk:(2 * hd + 2) * hk], kib_scr[rs, ks])
            sc_scr[sub, p, hd] = jnp.where(mask_f, sc_f, sc_b).astype(BF16)
    for hd in range(GLA_HEADS):
        ks = slice(hd * hk, (hd + 1) * hk)
        sf = sf_scr[hd]
        for p in range(n_pairs):
            scat_scr[sub, p, hd, 0:hk, :] = sf.T.astype(BF16)
            sf = sf * dec_f[p][:, ks] + kv_scr[sub, hd, p, :, 0:hk]
        sf_scr[hd] = sf
        sb = sb_ref[sb_idx, hd]
        for p in reversed(range(n_pairs)):
            scat_scr[sub, p, hd, hk:2 * hk, :] = sb.T.astype(BF16)
            if p > 0:
                sb = sb * dec_b[p][:, ks] + kv_scr[sub, hd, p, :, hk:2 * hk]
    yield

    for p in range(n_pairs):
        rs = slice(r0 + p * GLA_PAIR, r0 + (p + 1) * GLA_PAIR)
        for hd in range(GLA_HEADS):
            vs = slice(hd * hv, (hd + 1) * hv)
            cat = slice(2 * hd * hk, (2 * hd + 2) * hk)
            og_scr[rs, vs] = (_dot(sc_scr[sub, p, hd], v_ref[rs, vs])
                              + _dot(qd_scr[rs, cat], scat_scr[sub, p, hd]))
    yield

    glag = glag_ref[...]
    z_parts = []
    for hd in range(GLA_HEADS):
        vs = slice(hd * hv, (hd + 1) * hv)
        oh = og_scr[rows, vs]
        ms = jnp.mean(oh * oh, axis=-1, keepdims=True)
        z_parts.append((oh * lax.rsqrt(ms + RMS_EPS)) * glag)
    o_n = jnp.concatenate(z_parts, axis=1)
    y_gla = _dot((o_n * _silu(gla_gate)).astype(BF16), wpgla_ref[...])
    merged = merged + bg_gla * y_gla
    yield

    out = _dot(merged.astype(BF16), wout_ref[...])
    yield

    ms = jnp.mean(out * out, axis=-1, keepdims=True)
    out_n = (out * lax.rsqrt(ms + RMS_EPS)) * gpost_ref[...]
    o_ref[rows, :] = x_ref[rows, :] + gate * out_n
    yield


def _main_kernel(seq_len,
                 x_ref, h_ref, hp_ref, hn_ref, mod_ref, gpost_ref, k_ref, v_ref, lr_ref,
                 wt_ref,
                 poolw_ref, pscale_ref, gkcat_ref, gbias_ref, glag_ref,
                 wpp_ref, wpgla_ref, wout_ref, sb_ref,
                 o_ref,
                 sf_scr, hext_scr, q_scr, cf_scr, rb_scr, qs_scr, qd_scr, ke_scr, kif_scr, kib_scr,
                 kv_scr, scat_scr, sc_scr, og_scr):
    tb = MAIN_SUBTILES * MAIN_TILE

    @pl.when(pl.program_id(0) == 0)
    def _():
        sf_scr[...] = jnp.zeros_like(sf_scr)

    hext_scr[0:HALO, :] = hp_ref[...]
    hext_scr[HALO:HALO + tb, :] = h_ref[...]
    hext_scr[HALO + tb:HALO + tb + HALO, :] = hn_ref[...]

    refs = (x_ref, mod_ref, gpost_ref, k_ref, v_ref, lr_ref,
            wt_ref,
            poolw_ref, pscale_ref, gkcat_ref, gbias_ref, glag_ref,
            wpp_ref, wpgla_ref, wout_ref, sb_ref, o_ref,
            sf_scr, hext_scr, q_scr, cf_scr, rb_scr, qs_scr, qd_scr, ke_scr, kif_scr, kib_scr,
            kv_scr, scat_scr, sc_scr, og_scr)
    gens = [_subtile_stages(seq_len, sub, refs) for sub in range(MAIN_SUBTILES)]
    _run_round_robin(gens)


def _main_call(x2, h_all, mod, g_post, k_all, v_all, lr_all, w_bf, small, wpool_bf, wpp_bf, wpgla_bf, wout_bf, sb):
    s = x2.shape[0]
    tb = MAIN_SUBTILES * MAIN_TILE
    nt = s // tb
    hb = tb // HALO
    n_hb = s // HALO
    n_pairs = MAIN_TILE // GLA_PAIR
    const2 = lambda i: (0, 0)
    tile = lambda i: (i, 0)
    one = pl.Buffered(1)
    pool_scale, gk_cat, gbias_cat, gla_norm_g = small

    def resident(arr):
        return pl.BlockSpec(arr.shape, const2, pipeline_mode=one)

    in_specs = [
        pl.BlockSpec((tb, D_MODEL), tile),
        pl.BlockSpec((tb, D_MODEL), tile),
        pl.BlockSpec((HALO, D_MODEL), lambda i: (jnp.maximum(i * hb - 1, 0), 0)),
        pl.BlockSpec((HALO, D_MODEL), lambda i: (jnp.minimum((i + 1) * hb, n_hb - 1), 0)),
        pl.BlockSpec((3, D_MODEL), const2),
        pl.BlockSpec((1, D_MODEL), const2),
        pl.BlockSpec((tb, GLA_KEY_WIDTH), tile),
        pl.BlockSpec((tb, D_MODEL), tile),
        pl.BlockSpec((tb, LR_WIDTH), tile),
        resident(w_bf),
        resident(wpool_bf), resident(pool_scale), resident(gk_cat), resident(gbias_cat), resident(gla_norm_g),
        resident(wpp_bf), resident(wpgla_bf), resident(wout_bf),
        pl.BlockSpec((tb // SUB_TILE, GLA_HEADS, GLA_HEAD_V, GLA_HEAD_K), lambda i: (i, 0, 0, 0)),
    ]
    scratch = [
        pltpu.VMEM((GLA_HEADS, GLA_HEAD_V, GLA_HEAD_K), F32),
        pltpu.VMEM((tb + 2 * HALO, D_MODEL), BF16),
        pltpu.VMEM((tb, GLA_KEY_WIDTH), F32),
        pltpu.VMEM((tb, GLA_KEY_WIDTH), F32),
        pltpu.VMEM((tb, GLA_KEY_WIDTH), F32),
        pltpu.VMEM((tb, 2 * GLA_KEY_WIDTH), BF16),
        pltpu.VMEM((tb, 2 * GLA_KEY_WIDTH), BF16),
        pltpu.VMEM((tb, 2 * GLA_KEY_WIDTH), BF16),
        pltpu.VMEM((tb, GLA_KEY_WIDTH), BF16),
        pltpu.VMEM((tb, GLA_KEY_WIDTH), BF16),
        pltpu.VMEM((MAIN_SUBTILES, GLA_HEADS, n_pairs, GLA_HEAD_V, 2 * GLA_HEAD_K), F32),
        pltpu.VMEM((MAIN_SUBTILES, n_pairs, GLA_HEADS, 2 * GLA_HEAD_K, GLA_HEAD_V), BF16),
        pltpu.VMEM((MAIN_SUBTILES, n_pairs, GLA_HEADS, GLA_PAIR, GLA_PAIR), BF16),
        pltpu.VMEM((tb, D_MODEL), F32),
    ]
    return pl.pallas_call(
        functools.partial(_main_kernel, s),
        grid=(nt,),
        in_specs=in_specs,
        out_specs=pl.BlockSpec((tb, D_MODEL), tile),
        out_shape=jax.ShapeDtypeStruct((s, D_MODEL), F32),
        scratch_shapes=scratch,
        compiler_params=pltpu.CompilerParams(dimension_semantics=("arbitrary",),
                                             vmem_limit_bytes=VMEM_LIMIT_BYTES),
        name="main_call",
    )(x2, h_all, h_all, h_all, mod, g_post, k_all, v_all, lr_all,
      w_bf,
      wpool_bf, pool_scale, gk_cat, gbias_cat, gla_norm_g, wpp_bf, wpgla_bf, wout_bf, sb)


def _layer(x2, c, w_ada, b_ada, g_pre, g_post, w_in, pool_w, pool_scale, gk_up_fwd, gk_bias_fwd,
           gk_up_bwd, gk_bias_bwd, gla_norm_g, w_proj_pool, w_proj_gla, w_out):
    ada = _ada_call(c.reshape(D_MODEL, 1), w_ada, b_ada.reshape(1, -1))
    mod = ada.reshape(3, D_MODEL)

    zeros = jnp.zeros((GLA_GATE_RANK, GLA_KEY_WIDTH), F32)
    gk_cat = jnp.concatenate([jnp.concatenate([gk_up_fwd, zeros], axis=1),
                              jnp.concatenate([zeros, gk_up_bwd], axis=1)], axis=0).astype(BF16)
    gbias_cat = jnp.concatenate([gk_bias_fwd, gk_bias_bwd]).reshape(1, -1)
    gk_b = gk_cat[:, GLA_KEY_WIDTH:]
    gbias_b = gk_bias_bwd.reshape(1, -1)

    pool_w2 = pool_w.reshape(POOL_GROUPS * POOL_GROUP_WIDTH, POOL_GROUP_WIDTH)
    sb, k_all, v_all, h_all, lr_all, w_bf, wpool_bf, wpp_bf, wpgla_bf, wout_bf = _prep_call(
        x2, mod, g_pre.reshape(1, -1), gk_b, gbias_b, w_in.T, pool_w2, w_proj_pool, w_proj_gla, w_out)

    small = (pool_scale.reshape(1, -1), gk_cat, gbias_cat, gla_norm_g.reshape(1, -1))
    return _main_call(x2, h_all, mod, g_post.reshape(1, -1), k_all, v_all, lr_all, w_bf, small,
                      wpool_bf, wpp_bf, wpgla_bf, wout_bf, sb)


def kernel(x, c, w_ada, b_ada, g_pre, g_post, w_in, pool_w, pool_scale, gk_up_fwd, gk_bias_fwd, gk_up_bwd, gk_bias_bwd, gla_norm_g, w_proj_pool, w_proj_gla, w_out):
    b, s, d = x.shape
    depth = w_in.shape[0]
    xf = x.reshape(b * s, d)
    outs = []
    for bi in range(b):
        xb = xf if b == 1 else lax.slice_in_dim(xf, bi * s, (bi + 1) * s, axis=0)
        for l in range(depth):
            xb = _layer(xb, c[bi:bi + 1], w_ada[l], b_ada[l], g_pre[l], g_post[l], w_in[l], pool_w[l],
                        pool_scale[l], gk_up_fwd[l], gk_bias_fwd[l], gk_up_bwd[l], gk_bias_bwd[l],
                        gla_norm_g[l], w_proj_pool[l], w_proj_gla[l], w_out[l])
        outs.append(xb)
    out = outs[0] if b == 1 else jnp.concatenate(outs, axis=0)
    return out.reshape(b, s, d)
```

```python
import functools
import math

import jax
import jax.numpy as jnp
from jax import lax
from jax.experimental import pallas as pl
from jax.experimental.pallas import tpu as pltpu

D_MODEL = 1024
POOL_GROUPS = 4
POOL_GROUP_WIDTH = 256
POOL_WINDOWS = (2, 4, 8, 16)
GLA_HEADS = 4
GLA_KEY_WIDTH = 512
GLA_HEAD_K = 128
GLA_HEAD_V = 256
GLA_GATE_RANK = 16
GLA_GATE_NORMALIZER = 16.0
GATE_SCALE = 1.0 / GLA_GATE_NORMALIZER
assert math.frexp(GLA_GATE_NORMALIZER)[0] == 0.5
GLA_CHUNK = 64
GLA_PAIR = 2 * GLA_CHUNK
RMS_EPS = 1e-6
IN_SPLITS = (1024, 1024, 512, 512, 1024, 1024, 16, 16, 1024, 1024)
IN_WIDTH = sum(IN_SPLITS)

LANES = 128
SUB_TILE = 256
PREP_SUBTILES = 4
MAIN_TILE = 512
MAIN_SUBTILES = 1
CUMSUM_BLOCK = 256
HALO = 16
POOL_ROW_BLOCK = 128
VMEM_LIMIT_BYTES = 60 * 1024 * 1024

_OFFS = [sum(IN_SPLITS[:j]) for j in range(len(IN_SPLITS) + 1)]
(PIN_ROWS, PGATE_ROWS, Q_ROWS, K_ROWS, V_ROWS, GGATE_ROWS) = (slice(_OFFS[j], _OFFS[j + 1]) for j in range(6))
LR_ROWS = slice(_OFFS[6], _OFFS[8])
BGP_ROWS = slice(_OFFS[8], _OFFS[9])
BGG_ROWS = slice(_OFFS[9], _OFFS[10])
LR_WIDTH = 2 * GLA_GATE_RANK
WT_SLAB = 480

F32 = jnp.float32
BF16 = jnp.bfloat16


def _dot(a, b):
    return jnp.dot(a, b, preferred_element_type=F32)


def _dot_nt(a, b):
    return lax.dot_general(a, b, (((1,), (1,)), ((), ())), preferred_element_type=F32)


def _dot_tn(a, b):
    return lax.dot_general(a, b, (((0,), (0,)), ((), ())), preferred_element_type=F32)


def _sigmoid(x):
    return 0.5 * jnp.tanh(0.5 * x) + 0.5


def _silu(x):
    return x * _sigmoid(x)


def _log_sigmoid(x):
    return jnp.minimum(x, 0.0) - jnp.log(1.0 + jnp.exp(-jnp.abs(x)))


def _norm_mod(xv, g_pre, scale, shift):
    ms = jnp.mean(xv * xv, axis=-1, keepdims=True)
    hn = (xv * lax.rsqrt(ms + RMS_EPS)) * g_pre
    return hn * (1.0 + scale) + shift


def _split_hi_lo(a):
    hi = a.astype(BF16)
    lo = (a - hi.astype(F32)).astype(BF16)
    return hi, lo


def _run_round_robin(gens):
    live = list(gens)
    while live:
        for g in list(live):
            try:
                next(g)
            except StopIteration:
                live.remove(g)


def _ada_kernel(c_ref, w_ref, b_ref, o_ref):
    s = _silu(c_ref[...])
    o_ref[...] = jnp.sum(w_ref[...] * s, axis=0, keepdims=True) + b_ref[...]


def _ada_call(c_col, w_ada, b_ada):
    blk = 8 * LANES
    n_blk = (3 * D_MODEL) // blk
    return pl.pallas_call(
        _ada_kernel,
        grid=(n_blk,),
        in_specs=[
            pl.BlockSpec((D_MODEL, 1), lambda j: (0, 0)),
            pl.BlockSpec((D_MODEL, blk), lambda j: (0, j)),
            pl.BlockSpec((1, blk), lambda j: (0, j)),
        ],
        out_specs=pl.BlockSpec((1, blk), lambda j: (0, j)),
        out_shape=jax.ShapeDtypeStruct((1, 3 * D_MODEL), F32),
        compiler_params=pltpu.CompilerParams(dimension_semantics=("arbitrary",)),
        name="ada_call",
    )(c_col, w_ada, b_ada)


def _prep_stages(sub, refs):
    (gkb_ref, gbias_ref, sb_ref, k_ref, v_ref, lr_ref, st_scr) = refs
    t = SUB_TILE
    rows = slice(sub * t, (sub + 1) * t)
    k = k_ref[rows, :]
    v_bf = v_ref[rows, :]

    la = _log_sigmoid(_dot(lr_ref[rows, :], gkb_ref[...]) + gbias_ref[...])
    la_hi, la_lo = _split_hi_lo(la)
    yield

    row = lax.broadcasted_iota(jnp.int32, (t, t), 0)
    col = lax.broadcasted_iota(jnp.int32, (t, t), 1)
    upper = jnp.where(col >= row, GATE_SCALE, 0.0).astype(BF16)
    rc = _dot(upper, la_hi) + _dot(upper, la_lo)
    yield

    tot = rc[0:1, :]
    k_end = (k * jnp.exp(tot - rc)).astype(BF16)
    decay = jnp.exp(tot)
    kvs = []
    for hd in range(GLA_HEADS):
        ks = slice(hd * GLA_HEAD_K, (hd + 1) * GLA_HEAD_K)
        vs = slice(hd * GLA_HEAD_V, (hd + 1) * GLA_HEAD_V)
        kvs.append(_dot_tn(v_bf[:, vs], k_end[:, ks]))
    yield

    for hd in range(GLA_HEADS):
        ks = slice(hd * GLA_HEAD_K, (hd + 1) * GLA_HEAD_K)
        st = st_scr[hd]
        sb_ref[sub, hd] = st
        st_scr[hd] = st * decay[:, ks] + kvs[hd]
    yield


def _prep_kernel(x_ref, mod_ref, gpre_ref, gkb_ref, gbias_ref, wk_ref, wv_ref, wlr_ref,
                 win_ref, wpool_ref, wpp_ref, wpgla_ref, wout_ref,
                 sb_ref, k_ref, v_ref, h_ref, lr_ref, winbf_ref, wpoolbf_ref, wppbf_ref, wpglabf_ref, woutbf_ref,
                 st_scr, wk_scr, wv_scr, wlr_scr):
    @pl.when(pl.program_id(0) == 0)
    def _():
        st_scr[...] = jnp.zeros_like(st_scr)
        wk_scr[...] = wk_ref[...].astype(BF16)
        wv_scr[...] = wv_ref[...].astype(BF16)
        wlr_scr[...] = wlr_ref[...].astype(BF16)

    winbf_ref[...] = win_ref[...].astype(BF16)
    wpoolbf_ref[...] = wpool_ref[...].astype(BF16)
    wppbf_ref[...] = wpp_ref[...].astype(BF16)
    wpglabf_ref[...] = wpgla_ref[...].astype(BF16)
    woutbf_ref[...] = wout_ref[...].astype(BF16)

    h = _norm_mod(x_ref[...], gpre_ref[...], mod_ref[1:2, :], mod_ref[0:1, :]).astype(BF16)
    h_ref[...] = h
    k_ref[...] = _dot_nt(h, wk_scr[...])
    lr_ref[...] = _dot_nt(h, wlr_scr[...]).astype(BF16)
    v_ref[...] = _dot_nt(h, wv_scr[...]).astype(BF16)

    refs = (gkb_ref, gbias_ref, sb_ref, k_ref, v_ref, lr_ref, st_scr)
    gens = [_prep_stages(sub, refs) for sub in reversed(range(PREP_SUBTILES))]
    _run_round_robin(gens)


def _prep_call(x2, mod, g_pre, gk_b, gbias_b, w_in_t, pool_w2, w_proj_pool, w_proj_gla, w_out):
    s = x2.shape[0]
    tb = PREP_SUBTILES * SUB_TILE
    nb = s // tb
    slab = D_MODEL // nb
    n_wt = IN_WIDTH // WT_SLAB
    const = lambda i: (0, 0)
    rev = lambda i: (nb - 1 - i, 0)
    fwd = lambda i: (i, 0)
    wt_slab = lambda i: (jnp.minimum(i, n_wt - 1), 0)
    one = pl.Buffered(1)

    def rows_of(rs):
        n = rs.stop - rs.start
        return pl.BlockSpec((n, D_MODEL), lambda i: (rs.start // n, 0), pipeline_mode=one)

    return pl.pallas_call(
        _prep_kernel,
        grid=(nb,),
        in_specs=[
            pl.BlockSpec((tb, D_MODEL), rev),
            pl.BlockSpec((3, D_MODEL), const),
            pl.BlockSpec((1, D_MODEL), const),
            pl.BlockSpec(gk_b.shape, const),
            pl.BlockSpec(gbias_b.shape, const),
            rows_of(K_ROWS),
            rows_of(V_ROWS),
            rows_of(LR_ROWS),
            pl.BlockSpec((WT_SLAB, D_MODEL), wt_slab),
            pl.BlockSpec((slab, POOL_GROUP_WIDTH), fwd),
            pl.BlockSpec((slab, D_MODEL), fwd),
            pl.BlockSpec((slab, D_MODEL), fwd),
            pl.BlockSpec((slab, D_MODEL), fwd),
        ],
        out_specs=[
            pl.BlockSpec((PREP_SUBTILES, GLA_HEADS, GLA_HEAD_V, GLA_HEAD_K), lambda i: (nb - 1 - i, 0, 0, 0)),
            pl.BlockSpec((tb, GLA_KEY_WIDTH), rev),
            pl.BlockSpec((tb, D_MODEL), rev),
            pl.BlockSpec((tb, D_MODEL), rev),
            pl.BlockSpec((tb, LR_WIDTH), rev),
            pl.BlockSpec((WT_SLAB, D_MODEL), wt_slab),
            pl.BlockSpec((slab, POOL_GROUP_WIDTH), fwd),
            pl.BlockSpec((slab, D_MODEL), fwd),
            pl.BlockSpec((slab, D_MODEL), fwd),
            pl.BlockSpec((slab, D_MODEL), fwd),
        ],
        out_shape=[
            jax.ShapeDtypeStruct((s // SUB_TILE, GLA_HEADS, GLA_HEAD_V, GLA_HEAD_K), F32),
            jax.ShapeDtypeStruct((s, GLA_KEY_WIDTH), F32),
            jax.ShapeDtypeStruct((s, D_MODEL), BF16),
            jax.ShapeDtypeStruct((s, D_MODEL), BF16),
            jax.ShapeDtypeStruct((s, LR_WIDTH), BF16),
            jax.ShapeDtypeStruct((IN_WIDTH, D_MODEL), BF16),
            jax.ShapeDtypeStruct(pool_w2.shape, BF16),
            jax.ShapeDtypeStruct((D_MODEL, D_MODEL), BF16),
            jax.ShapeDtypeStruct((D_MODEL, D_MODEL), BF16),
            jax.ShapeDtypeStruct((D_MODEL, D_MODEL), BF16),
        ],
        scratch_shapes=[
            pltpu.VMEM((GLA_HEADS, GLA_HEAD_V, GLA_HEAD_K), F32),
            pltpu.VMEM((GLA_KEY_WIDTH, D_MODEL), BF16),
            pltpu.VMEM((D_MODEL, D_MODEL), BF16),
            pltpu.VMEM((LR_WIDTH, D_MODEL), BF16),
        ],
        compiler_params=pltpu.CompilerParams(dimension_semantics=("arbitrary",),
                                             vmem_limit_bytes=VMEM_LIMIT_BYTES),
        name="prep_call",
    )(x2, mod, g_pre, gk_b, gbias_b, w_in_t, w_in_t, w_in_t, w_in_t, pool_w2, w_proj_pool, w_proj_gla, w_out)


def _subtile_stages(seq_len, sub, refs):
    (x_ref, mod_ref, gpost_ref, k_ref, v_ref, lr_ref,
     wt_ref,
     poolw_ref, pscale_ref, gkcat_ref, gbias_ref, glag_ref,
     wpp_ref, wpgla_ref, wout_ref, sb_ref, o_ref,
     sf_scr, hext_scr, q_scr, cf_scr, rb_scr, qs_scr, qd_scr, ke_scr, kif_scr, kib_scr,
     kv_scr, scat_scr, sc_scr, og_scr) = refs
    t = MAIN_TILE
    r0 = sub * t
    rows = slice(r0, r0 + t)
    tile_row0 = pl.program_id(0) * (MAIN_SUBTILES * t) + r0
    sb_idx = (sub + 1) * (MAIN_TILE // SUB_TILE) - 1
    n_pairs = t // GLA_PAIR
    hk, hv = GLA_HEAD_K, GLA_HEAD_V
    gate = mod_ref[2:3, :]

    h_ext = hext_scr[r0:r0 + t + 2 * HALO, :]
    h = hext_scr[r0 + HALO:r0 + HALO + t, :]

    q_scr[rows, :] = _dot_nt(h, wt_ref[Q_ROWS, :]) * (GLA_HEAD_K ** -0.5)
    la_pre = _dot(lr_ref[rows, :], gkcat_ref[...]) + gbias_ref[...]
    te = t + 2 * HALO
    g_row = tile_row0 - HALO + lax.broadcasted_iota(jnp.int32, (te, 1), 0)
    valid = jnp.logical_and(g_row >= 0, g_row < seq_len)
    u_ext = jnp.where(valid, _dot_nt(h_ext, wt_ref[PIN_ROWS, :]), 0.0)
    yield

    la = _log_sigmoid(la_pre)
    p_gate = _dot_nt(h, wt_ref[PGATE_ROWS, :])
    u_ext_bf = u_ext.astype(BF16)
    u = u_ext[HALO:HALO + t, :]
    laf_hi, laf_lo = _split_hi_lo(la[:, :GLA_KEY_WIDTH])
    lab_hi, lab_lo = _split_hi_lo(la[:, GLA_KEY_WIDTH:])
    yield

    cb = CUMSUM_BLOCK
    row = lax.broadcasted_iota(jnp.int32, (cb, cb), 0)
    col = lax.broadcasted_iota(jnp.int32, (cb, cb), 1)
    same_chunk = (row // GLA_CHUNK) == (col // GLA_CHUNK)
    lower = jnp.where(jnp.logical_and(same_chunk, col <= row), GATE_SCALE, 0.0).astype(BF16)
    upper = jnp.where(jnp.logical_and(same_chunk, col >= row), GATE_SCALE, 0.0).astype(BF16)
    for b in range(t // cb):
        rb_rows = slice(b * cb, (b + 1) * cb)
        dst = slice(r0 + b * cb, r0 + (b + 1) * cb)
        cf_scr[dst, :] = _dot(lower, laf_hi[rb_rows]) + _dot(lower, laf_lo[rb_rows])
        rb_scr[dst, :] = _dot(upper, lab_hi[rb_rows]) + _dot(upper, lab_lo[rb_rows])
    bg_pool = _sigmoid(_dot_nt(h, wt_ref[BGP_ROWS, :]))
    yield

    pos = tile_row0 + lax.broadcasted_iota(jnp.int32, (t, 1), 0)
    rb_ = POOL_ROW_BLOCK
    kb_ = POOL_ROW_BLOCK + 2 * HALO
    band_r = lax.broadcasted_iota(jnp.int32, (rb_, kb_), 0)
    band_c = lax.broadcasted_iota(jnp.int32, (rb_, kb_), 1)
    band_d = band_c - HALO - band_r
    a_parts = []
    for gi, w in enumerate(POOL_WINDOWS):
        hw = w // 2
        cs = slice(gi * POOL_GROUP_WIDTH, (gi + 1) * POOL_GROUP_WIDTH)
        ws = slice(gi * POOL_GROUP_WIDTH, (gi + 1) * POOL_GROUP_WIDTH)
        band = jnp.logical_and(band_d >= -hw, band_d < hw).astype(BF16)
        wsum = jnp.concatenate(
            [_dot(band, u_ext_bf[b * rb_:b * rb_ + kb_, cs]) for b in range(t // rb_)], axis=0)
        count = (jnp.minimum(pos + hw, seq_len) - jnp.maximum(pos - hw, 0)).astype(F32)
        pooled = wsum / count - u[:, cs]
        a_parts.append(_dot(pooled.astype(BF16), poolw_ref[ws, :]))
    a = jnp.concatenate(a_parts, axis=1) * pscale_ref[...]
    z_pool = (a * _silu(p_gate)).astype(BF16)
    gla_gate = _dot_nt(h, wt_ref[GGATE_ROWS, :])
    bg_gla = _sigmoid(_dot_nt(h, wt_ref[BGG_ROWS, :]))

    def put_heads(dst, rs, fwd, bwd):
        for hd in range(GLA_HEADS):
            ks = slice(hd * hk, (hd + 1) * hk)
            dst[rs, 2 * hd * hk:(2 * hd + 1) * hk] = fwd[:, ks].astype(BF16)
            dst[rs, (2 * hd + 1) * hk:(2 * hd + 2) * hk] = bwd[:, ks].astype(BF16)

    dec_f, dec_b = [], []
    for p in range(n_pairs):
        c0 = r0 + p * GLA_PAIR
        c1 = c0 + GLA_CHUNK
        ra = slice(c0, c1)
        rb = slice(c1, c1 + GLA_CHUNK)
        q0, q1 = q_scr[ra, :], q_scr[rb, :]
        k0, k1 = k_ref[ra, :], k_ref[rb, :]
        cum0, cum1 = cf_scr[ra, :], cf_scr[rb, :]
        tot0 = cf_scr[c1 - 1:c1, :]
        tot1 = cf_scr[c1 + GLA_CHUNK - 1:c1 + GLA_CHUNK, :]
        rc0, rc1 = rb_scr[ra, :], rb_scr[rb, :]
        tb0 = rb_scr[c0:c0 + 1, :]
        tb1 = rb_scr[c1:c1 + 1, :]
        e_tot0, e_tot1 = jnp.exp(tot0), jnp.exp(tot1)
        e_tb0, e_tb1 = jnp.exp(tb0), jnp.exp(tb1)

        a0 = cum0 - tot0
        ks_f0 = k0 * jnp.exp(-a0)
        qs_f1 = q1 * jnp.exp(cum1)
        qs_b0 = q0 * jnp.exp(rc0)
        put_heads(qs_scr, ra, q0 * jnp.exp(a0), qs_b0)
        put_heads(qd_scr, ra, q0 * jnp.exp(cum0), qs_b0 * e_tb1)
        kif_scr[ra, :] = ks_f0.astype(BF16)
        kif_scr[rb, :] = (k1 * jnp.exp(-cum1)).astype(BF16)
        a1 = rc1 - tb1
        ks_b1 = k1 * jnp.exp(-a1)
        put_heads(qs_scr, rb, qs_f1, q1 * jnp.exp(a1))
        put_heads(qd_scr, rb, qs_f1 * e_tot0, q1 * jnp.exp(rc1))
        kib_scr[ra, :] = (k0 * jnp.exp(-rc0)).astype(BF16)
        kib_scr[rb, :] = ks_b1.astype(BF16)
        put_heads(ke_scr, ra, ks_f0 * e_tot1, k0 * jnp.exp(tb0 - rc0))
        put_heads(ke_scr, rb, k1 * jnp.exp(tot1 - cum1), ks_b1 * e_tb0)
        dec_f.append(e_tot0 * e_tot1)
        dec_b.append(e_tb0 * e_tb1)
    yield

    merged = bg_pool * _dot(z_pool, wpp_ref[...])
    for hd in range(GLA_HEADS):
        vs = slice(hd * hv, (hd + 1) * hv)
        cat = slice(2 * hd * hk, (2 * hd + 2) * hk)
        for p in range(n_pairs):
            rs = slice(r0 + p * GLA_PAIR, r0 + (p + 1) * GLA_PAIR)
            kv_scr[sub, hd, p] = _dot_tn(v_ref[rs, vs], ke_scr[rs, cat])
    yield

    prow = lax.broadcasted_iota(jnp.int32, (GLA_PAIR, GLA_PAIR), 0)
    pcol = lax.broadcasted_iota(jnp.int32, (GLA_PAIR, GLA_PAIR), 1)
    mask_f = pcol <= prow
    for p in range(n_pairs):
        rs = slice(r0 + p * GLA_PAIR, r0 + (p + 1) * GLA_PAIR)
        for hd in range(GLA_HEADS):
            ks = slice(hd * hk, (hd + 1) * hk)
            sc_f = _dot_nt(qs_scr[rs, 2 * hd * hk:(2 * hd + 1) * hk], kif_scr[rs, ks])
            sc_b = _dot_nt(qs_scr[rs, (2 * hd + 1) * hk:(2 * hd + 2) * hk], kib_scr[rs, ks])
            sc_scr[sub, p, hd] = jnp.where(mask_f, sc_f, sc_b).astype(BF16)
    for hd in range(GLA_HEADS):
        ks = slice(hd * hk, (hd + 1) * hk)
        sf = sf_scr[hd]
        for p in range(n_pairs):
            scat_scr[sub, p, hd, 0:hk, :] = sf.T.astype(BF16)
            sf = sf * dec_f[p][:, ks] + kv_scr[sub, hd, p, :, 0:hk]
        sf_scr[hd] = sf
        sb = sb_ref[sb_idx, hd]
        for p in reversed(range(n_pairs)):
            scat_scr[sub, p, hd, hk:2 * hk, :] = sb.T.astype(BF16)
            if p > 0:
                sb = sb * dec_b[p][:, ks] + kv_scr[sub, hd, p, :, hk:2 * hk]
    yield

    for p in range(n_pairs):
        rs = slice(r0 + p * GLA_PAIR, r0 + (p + 1) * GLA_PAIR)
        for hd in range(GLA_HEADS):
            vs = slice(hd * hv, (hd + 1) * hv)
            cat = slice(2 * hd * hk, (2 * hd + 2) * hk)
            og_scr[rs, vs] = (_dot(sc_scr[sub, p, hd], v_ref[rs, vs])
                              + _dot(qd_scr[rs, cat], scat_scr[sub, p, hd]))
    yield

    glag = glag_ref[...]
    z_parts = []
    for hd in range(GLA_HEADS):
        vs = slice(hd * hv, (hd + 1) * hv)
        oh = og_scr[rows, vs]
        ms = jnp.mean(oh * oh, axis=-1, keepdims=True)
        z_parts.append((oh * lax.rsqrt(ms + RMS_EPS)) * glag)
    o_n = jnp.concatenate(z_parts, axis=1)
    y_gla = _dot((o_n * _silu(gla_gate)).astype(BF16), wpgla_ref[...])
    merged = merged + bg_gla * y_gla
    yield

    out = _dot(merged.astype(BF16), wout_ref[...])
    yield

    ms = jnp.mean(out * out, axis=-1, keepdims=True)
    out_n = (out * lax.rsqrt(ms + RMS_EPS)) * gpost_ref[...]
    o_ref[rows, :] = x_ref[rows, :] + gate * out_n
    yield


def _main_kernel(seq_len,
                 x_ref, h_ref, hp_ref, hn_ref, mod_ref, gpost_ref, k_ref, v_ref, lr_ref,
                 wt_ref,
                 poolw_ref, pscale_ref, gkcat_ref, gbias_ref, glag_ref,
                 wpp_ref, wpgla_ref, wout_ref, sb_ref,
                 o_ref,
                 sf_scr, hext_scr, q_scr, cf_scr, rb_scr, qs_scr, qd_scr, ke_scr, kif_scr, kib_scr,
                 kv_scr, scat_scr, sc_scr, og_scr):
    tb = MAIN_SUBTILES * MAIN_TILE

    @pl.when(pl.program_id(0) == 0)
    def _():
        sf_scr[...] = jnp.zeros_like(sf_scr)

    hext_scr[0:HALO, :] = hp_ref[...]
    hext_scr[HALO:HALO + tb, :] = h_ref[...]
    hext_scr[HALO + tb:HALO + tb + HALO, :] = hn_ref[...]

    refs = (x_ref, mod_ref, gpost_ref, k_ref, v_ref, lr_ref,
            wt_ref,
            poolw_ref, pscale_ref, gkcat_ref, gbias_ref, glag_ref,
            wpp_ref, wpgla_ref, wout_ref, sb_ref, o_ref,
            sf_scr, hext_scr, q_scr, cf_scr, rb_scr, qs_scr, qd_scr, ke_scr, kif_scr, kib_scr,
            kv_scr, scat_scr, sc_scr, og_scr)
    gens = [_subtile_stages(seq_len, sub, refs) for sub in range(MAIN_SUBTILES)]
    _run_round_robin(gens)


def _main_call(x2, h_all, mod, g_post, k_all, v_all, lr_all, w_bf, small, wpool_bf, wpp_bf, wpgla_bf, wout_bf, sb):
    s = x2.shape[0]
    tb = MAIN_SUBTILES * MAIN_TILE
    nt = s // tb
    hb = tb // HALO
    n_hb = s // HALO
    n_pairs = MAIN_TILE // GLA_PAIR
    const2 = lambda i: (0, 0)
    tile = lambda i: (i, 0)
    one = pl.Buffered(1)
    pool_scale, gk_cat, gbias_cat, gla_norm_g = small

    def resident(arr):
        return pl.BlockSpec(arr.shape, const2, pipeline_mode=one)

    in_specs = [
        pl.BlockSpec((tb, D_MODEL), tile),
        pl.BlockSpec((tb, D_MODEL), tile),
        pl.BlockSpec((HALO, D_MODEL), lambda i: (jnp.maximum(i * hb - 1, 0), 0)),
        pl.BlockSpec((HALO, D_MODEL), lambda i: (jnp.minimum((i + 1) * hb, n_hb - 1), 0)),
        pl.BlockSpec((3, D_MODEL), const2),
        pl.BlockSpec((1, D_MODEL), const2),
        pl.BlockSpec((tb, GLA_KEY_WIDTH), tile),
        pl.BlockSpec((tb, D_MODEL), tile),
        pl.BlockSpec((tb, LR_WIDTH), tile),
        resident(w_bf),
        resident(wpool_bf), resident(pool_scale), resident(gk_cat), resident(gbias_cat), resident(gla_norm_g),
        resident(wpp_bf), resident(wpgla_bf), resident(wout_bf),
        pl.BlockSpec((tb // SUB_TILE, GLA_HEADS, GLA_HEAD_V, GLA_HEAD_K), lambda i: (i, 0, 0, 0)),
    ]
    scratch = [
        pltpu.VMEM((GLA_HEADS, GLA_HEAD_V, GLA_HEAD_K), F32),
        pltpu.VMEM((tb + 2 * HALO, D_MODEL), BF16),
        pltpu.VMEM((tb, GLA_KEY_WIDTH), F32),
        pltpu.VMEM((tb, GLA_KEY_WIDTH), F32),
        pltpu.VMEM((tb, GLA_KEY_WIDTH), F32),
        pltpu.VMEM((tb, 2 * GLA_KEY_WIDTH), BF16),
        pltpu.VMEM((tb, 2 * GLA_KEY_WIDTH), BF16),
        pltpu.VMEM((tb, 2 * GLA_KEY_WIDTH), BF16),
        pltpu.VMEM((tb, GLA_KEY_WIDTH), BF16),
        pltpu.VMEM((tb, GLA_KEY_WIDTH), BF16),
        pltpu.VMEM((MAIN_SUBTILES, GLA_HEADS, n_pairs, GLA_HEAD_V, 2 * GLA_HEAD_K), F32),
        pltpu.VMEM((MAIN_SUBTILES, n_pairs, GLA_HEADS, 2 * GLA_HEAD_K, GLA_HEAD_V), BF16),
        pltpu.VMEM((MAIN_SUBTILES, n_pairs, GLA_HEADS, GLA_PAIR, GLA_PAIR), BF16),
        pltpu.VMEM((tb, D_MODEL), F32),
    ]
    return pl.pallas_call(
        functools.partial(_main_kernel, s),
        grid=(nt,),
        in_specs=in_specs,
        out_specs=pl.BlockSpec((tb, D_MODEL), tile),
        out_shape=jax.ShapeDtypeStruct((s, D_MODEL), F32),
        scratch_shapes=scratch,
        compiler_params=pltpu.CompilerParams(dimension_semantics=("arbitrary",),
                                             vmem_limit_bytes=VMEM_LIMIT_BYTES),
        name="main_call",
    )(x2, h_all, h_all, h_all, mod, g_post, k_all, v_all, lr_all,
      w_bf,
      wpool_bf, pool_scale, gk_cat, gbias_cat, gla_norm_g, wpp_bf, wpgla_bf, wout_bf, sb)


def _layer(x2, c, w_ada, b_ada, g_pre, g_post, w_in, pool_w, pool_scale, gk_up_fwd, gk_bias_fwd,
           gk_up_bwd, gk_bias_bwd, gla_norm_g, w_proj_pool, w_proj_gla, w_out):
    ada = _ada_call(c.reshape(D_MODEL, 1), w_ada, b_ada.reshape(1, -1))
    mod = ada.reshape(3, D_MODEL)

    zeros = jnp.zeros((GLA_GATE_RANK, GLA_KEY_WIDTH), F32)
    gk_cat = jnp.concatenate([jnp.concatenate([gk_up_fwd, zeros], axis=1),
                              jnp.concatenate([zeros, gk_up_bwd], axis=1)], axis=0).astype(BF16)
    gbias_cat = jnp.concatenate([gk_bias_fwd, gk_bias_bwd]).reshape(1, -1)
    gk_b = gk_cat[:, GLA_KEY_WIDTH:]
    gbias_b = gk_bias_bwd.reshape(1, -1)

    pool_w2 = pool_w.reshape(POOL_GROUPS * POOL_GROUP_WIDTH, POOL_GROUP_WIDTH)
    sb, k_all, v_all, h_all, lr_all, w_bf, wpool_bf, wpp_bf, wpgla_bf, wout_bf = _prep_call(
        x2, mod, g_pre.reshape(1, -1), gk_b, gbias_b, w_in.T, pool_w2, w_proj_pool, w_proj_gla, w_out)

    small = (pool_scale.reshape(1, -1), gk_cat, gbias_cat, gla_norm_g.reshape(1, -1))
    return _main_call(x2, h_all, mod, g_post.reshape(1, -1), k_all, v_all, lr_all, w_bf, small,
                      wpool_bf, wpp_bf, wpgla_bf, wout_bf, sb)


def kernel(x, c, w_ada, b_ada, g_pre, g_post, w_in, pool_w, pool_scale, gk_up_fwd, gk_bias_fwd, gk_up_bwd, gk_bias_bwd, gla_norm_g, w_proj_pool, w_proj_gla, w_out):
    b, s, d = x.shape
    depth = w_in.shape[0]
    xf = x.reshape(b * s, d)
    outs = []
    for bi in range(b):
        xb = xf if b == 1 else lax.slice_in_dim(xf, bi * s, (bi + 1) * s, axis=0)
        for l in range(depth):
            xb = _layer(xb, c[bi:bi + 1], w_ada[l], b_ada[l], g_pre[l], g_post[l], w_in[l], pool_w[l],
                        pool_scale[l], gk_up_fwd[l], gk_bias_fwd[l], gk_up_bwd[l], gk_bias_bwd[l],
                        gla_norm_g[l], w_proj_pool[l], w_proj_gla[l], w_out[l])
        outs.append(xb)
    out = outs[0] if b == 1 else jnp.concatenate(outs, axis=0)
    return out.reshape(b, s, d)
```

```python
import functools
import math

import jax
import jax.numpy as jnp
from jax import lax
from jax.experimental import pallas as pl
from jax.experimental.pallas import tpu as pltpu

D_MODEL = 1024
POOL_GROUPS = 4
POOL_GROUP_WIDTH = 256
POOL_WINDOWS = (2, 4, 8, 16)
GLA_HEADS = 4
GLA_KEY_WIDTH = 512
GLA_HEAD_K = 128
GLA_HEAD_V = 256
GLA_GATE_RANK = 16
GLA_GATE_NORMALIZER = 16.0
GATE_SCALE = 1.0 / GLA_GATE_NORMALIZER
assert math.frexp(GLA_GATE_NORMALIZER)[0] == 0.5
GLA_CHUNK = 64
GLA_PAIR = 2 * GLA_CHUNK
RMS_EPS = 1e-6
IN_SPLITS = (1024, 1024, 512, 512, 1024, 1024, 16, 16, 1024, 1024)
IN_WIDTH = sum(IN_SPLITS)

LANES = 128
SUB_TILE = 256
PREP_SUBTILES = 4
MAIN_TILE = 512
MAIN_SUBTILES = 1
CUMSUM_BLOCK = 256
HALO = 16
POOL_ROW_BLOCK = 128
VMEM_LIMIT_BYTES = 60 * 1024 * 1024

_OFFS = [sum(IN_SPLITS[:j]) for j in range(len(IN_SPLITS) + 1)]
(PIN_ROWS, PGATE_ROWS, Q_ROWS, K_ROWS, V_ROWS, GGATE_ROWS) = (slice(_OFFS[j], _OFFS[j + 1]) for j in range(6))
LR_ROWS = slice(_OFFS[6], _OFFS[8])
BGP_ROWS = slice(_OFFS[8], _OFFS[9])
BGG_ROWS = slice(_OFFS[9], _OFFS[10])
LR_WIDTH = 2 * GLA_GATE_RANK
WT_SLAB = 480

F32 = jnp.float32
BF16 = jnp.bfloat16


def _dot(a, b):
    return jnp.dot(a, b, preferred_element_type=F32)


def _dot_nt(a, b):
    return lax.dot_general(a, b, (((1,), (1,)), ((), ())), preferred_element_type=F32)


def _dot_tn(a, b):
    return lax.dot_general(a, b, (((0,), (0,)), ((), ())), preferred_element_type=F32)


def _sigmoid(x):
    return 0.5 * jnp.tanh(0.5 * x) + 0.5


def _silu(x):
    return x * _sigmoid(x)


def _log_sigmoid(x):
    return jnp.minimum(x, 0.0) - jnp.log(1.0 + jnp.exp(-jnp.abs(x)))


def _norm_mod(xv, g_pre, scale, shift):
    ms = jnp.mean(xv * xv, axis=-1, keepdims=True)
    hn = (xv * lax.rsqrt(ms + RMS_EPS)) * g_pre
    return hn * (1.0 + scale) + shift


def _split_hi_lo(a):
    hi = a.astype(BF16)
    lo = (a - hi.astype(F32)).astype(BF16)
    return hi, lo


def _run_round_robin(gens):
    live = list(gens)
    while live:
        for g in list(live):
            try:
                next(g)
            except StopIteration:
                live.remove(g)


def _ada_kernel(c_ref, w_ref, b_ref, o_ref):
    s = _silu(c_ref[...])
    o_ref[...] = jnp.sum(w_ref[...] * s, axis=0, keepdims=True) + b_ref[...]


def _ada_call(c_col, w_ada, b_ada):
    return pl.pallas_call(
        _ada_kernel,
        grid=(3,),
        in_specs=[
            pl.BlockSpec((D_MODEL, 1), lambda j: (0, 0)),
            pl.BlockSpec((D_MODEL, D_MODEL), lambda j: (0, j)),
            pl.BlockSpec((1, D_MODEL), lambda j: (0, j)),
        ],
        out_specs=pl.BlockSpec((None, 1, D_MODEL), lambda j: (j, 0, 0)),
        out_shape=jax.ShapeDtypeStruct((3, 1, D_MODEL), F32),
        compiler_params=pltpu.CompilerParams(dimension_semantics=("arbitrary",)),
        name="ada_call",
    )(c_col, w_ada, b_ada)


def _prep_stages(sub, refs):
    (gkcat_ref, gbias_ref, sb_ref, k_ref, v_ref, lr_ref, st_scr) = refs
    t = SUB_TILE
    rows = slice(sub * t, (sub + 1) * t)
    k = k_ref[rows, :]
    v_bf = v_ref[rows, :]

    gk_bwd = gkcat_ref[:, GLA_KEY_WIDTH:]
    la = _log_sigmoid(_dot(lr_ref[rows, :], gk_bwd) + gbias_ref[...])
    la_hi, la_lo = _split_hi_lo(la)
    yield

    row = lax.broadcasted_iota(jnp.int32, (t, t), 0)
    col = lax.broadcasted_iota(jnp.int32, (t, t), 1)
    upper = jnp.where(col >= row, GATE_SCALE, 0.0).astype(BF16)
    rc = _dot(upper, la_hi) + _dot(upper, la_lo)
    yield

    tot = rc[0:1, :]
    k_end = (k * jnp.exp(tot - rc)).astype(BF16)
    decay = jnp.exp(tot)
    kvs = []
    for hd in range(GLA_HEADS):
        ks = slice(hd * GLA_HEAD_K, (hd + 1) * GLA_HEAD_K)
        vs = slice(hd * GLA_HEAD_V, (hd + 1) * GLA_HEAD_V)
        kvs.append(_dot_tn(v_bf[:, vs], k_end[:, ks]))
    yield

    for hd in range(GLA_HEADS):
        ks = slice(hd * GLA_HEAD_K, (hd + 1) * GLA_HEAD_K)
        st = st_scr[hd]
        sb_ref[sub, hd] = st
        st_scr[hd] = st * decay[:, ks] + kvs[hd]
    yield


def _prep_kernel(x_ref, mod_ref, gpre_ref, gkf_ref, gkb_ref, gbias_ref, wk_ref, wv_ref, wlr_ref,
                 win_ref, wpool_ref, wpp_ref, wpgla_ref, wout_ref,
                 sb_ref, k_ref, v_ref, h_ref, lr_ref, gkcat_ref,
                 winbf_ref, wpoolbf_ref, wppbf_ref, wpglabf_ref, woutbf_ref,
                 st_scr, wk_scr, wv_scr, wlr_scr):
    @pl.when(pl.program_id(0) == 0)
    def _():
        st_scr[...] = jnp.zeros_like(st_scr)
        wk_scr[...] = wk_ref[...].astype(BF16)
        wv_scr[...] = wv_ref[...].astype(BF16)
        wlr_scr[...] = wlr_ref[...].astype(BF16)
        r, w = GLA_GATE_RANK, GLA_KEY_WIDTH
        gkcat_ref[...] = jnp.zeros(gkcat_ref.shape, BF16)
        gkcat_ref[0:r, 0:w] = gkf_ref[...].astype(BF16)
        gkcat_ref[r:2 * r, w:2 * w] = gkb_ref[...].astype(BF16)

    winbf_ref[...] = win_ref[...].astype(BF16)
    wpoolbf_ref[...] = wpool_ref[...].astype(BF16)
    wppbf_ref[...] = wpp_ref[...].astype(BF16)
    wpglabf_ref[...] = wpgla_ref[...].astype(BF16)
    woutbf_ref[...] = wout_ref[...].astype(BF16)

    h = _norm_mod(x_ref[...], gpre_ref[...], mod_ref[1], mod_ref[0]).astype(BF16)
    h_ref[...] = h
    k_ref[...] = _dot_nt(h, wk_scr[...])
    lr_ref[...] = _dot_nt(h, wlr_scr[...]).astype(BF16)
    v_ref[...] = _dot_nt(h, wv_scr[...]).astype(BF16)

    refs = (gkcat_ref, gbias_ref, sb_ref, k_ref, v_ref, lr_ref, st_scr)
    gens = [_prep_stages(sub, refs) for sub in reversed(range(PREP_SUBTILES))]
    _run_round_robin(gens)


def _prep_call(x2, mod, g_pre, gk_f, gk_b, gbias_b, w_in_t, pool_w2, w_proj_pool, w_proj_gla, w_out):
    s = x2.shape[0]
    tb = PREP_SUBTILES * SUB_TILE
    nb = s // tb
    slab = D_MODEL // nb
    n_wt = IN_WIDTH // WT_SLAB
    const = lambda i: (0, 0)
    rev = lambda i: (nb - 1 - i, 0)
    fwd = lambda i: (i, 0)
    wt_slab = lambda i: (jnp.minimum(i, n_wt - 1), 0)
    one = pl.Buffered(1)

    def rows_of(rs):
        n = rs.stop - rs.start
        return pl.BlockSpec((n, D_MODEL), lambda i: (rs.start // n, 0), pipeline_mode=one)

    return pl.pallas_call(
        _prep_kernel,
        grid=(nb,),
        in_specs=[
            pl.BlockSpec((tb, D_MODEL), rev),
            pl.BlockSpec((3, 1, D_MODEL), lambda i: (0, 0, 0)),
            pl.BlockSpec((1, D_MODEL), const),
            pl.BlockSpec(gk_f.shape, const),
            pl.BlockSpec(gk_b.shape, const),
            pl.BlockSpec(gbias_b.shape, const),
            rows_of(K_ROWS),
            rows_of(V_ROWS),
            rows_of(LR_ROWS),
            pl.BlockSpec((WT_SLAB, D_MODEL), wt_slab),
            pl.BlockSpec((slab, POOL_GROUP_WIDTH), fwd),
            pl.BlockSpec((slab, D_MODEL), fwd),
            pl.BlockSpec((slab, D_MODEL), fwd),
            pl.BlockSpec((slab, D_MODEL), fwd),
        ],
        out_specs=[
            pl.BlockSpec((PREP_SUBTILES, GLA_HEADS, GLA_HEAD_V, GLA_HEAD_K), lambda i: (nb - 1 - i, 0, 0, 0)),
            pl.BlockSpec((tb, GLA_KEY_WIDTH), rev),
            pl.BlockSpec((tb, D_MODEL), rev),
            pl.BlockSpec((tb, D_MODEL), rev),
            pl.BlockSpec((tb, LR_WIDTH), rev),
            pl.BlockSpec((LR_WIDTH, 2 * GLA_KEY_WIDTH), const),
            pl.BlockSpec((WT_SLAB, D_MODEL), wt_slab),
            pl.BlockSpec((slab, POOL_GROUP_WIDTH), fwd),
            pl.BlockSpec((slab, D_MODEL), fwd),
            pl.BlockSpec((slab, D_MODEL), fwd),
            pl.BlockSpec((slab, D_MODEL), fwd),
        ],
        out_shape=[
            jax.ShapeDtypeStruct((s // SUB_TILE, GLA_HEADS, GLA_HEAD_V, GLA_HEAD_K), F32),
            jax.ShapeDtypeStruct((s, GLA_KEY_WIDTH), F32),
            jax.ShapeDtypeStruct((s, D_MODEL), BF16),
            jax.ShapeDtypeStruct((s, D_MODEL), BF16),
            jax.ShapeDtypeStruct((s, LR_WIDTH), BF16),
            jax.ShapeDtypeStruct((LR_WIDTH, 2 * GLA_KEY_WIDTH), BF16),
            jax.ShapeDtypeStruct((IN_WIDTH, D_MODEL), BF16),
            jax.ShapeDtypeStruct(pool_w2.shape, BF16),
            jax.ShapeDtypeStruct((D_MODEL, D_MODEL), BF16),
            jax.ShapeDtypeStruct((D_MODEL, D_MODEL), BF16),
            jax.ShapeDtypeStruct((D_MODEL, D_MODEL), BF16),
        ],
        scratch_shapes=[
            pltpu.VMEM((GLA_HEADS, GLA_HEAD_V, GLA_HEAD_K), F32),
            pltpu.VMEM((GLA_KEY_WIDTH, D_MODEL), BF16),
            pltpu.VMEM((D_MODEL, D_MODEL), BF16),
            pltpu.VMEM((LR_WIDTH, D_MODEL), BF16),
        ],
        compiler_params=pltpu.CompilerParams(dimension_semantics=("arbitrary",),
                                             vmem_limit_bytes=VMEM_LIMIT_BYTES),
        name="prep_call",
    )(x2, mod, g_pre, gk_f, gk_b, gbias_b, w_in_t, w_in_t, w_in_t, w_in_t, pool_w2, w_proj_pool, w_proj_gla, w_out)


def _subtile_stages(seq_len, sub, refs):
    (x_ref, mod_ref, gpost_ref, k_ref, v_ref, lr_ref,
     wt_ref,
     poolw_ref, pscale_ref, gkcat_ref, gbf_ref, gbb_ref, glag_ref,
     wpp_ref, wpgla_ref, wout_ref, sb_ref, o_ref,
     sf_scr, hext_scr, q_scr, cf_scr, rb_scr, qs_scr, qd_scr, ke_scr, kif_scr, kib_scr,
     kv_scr, scat_scr, sc_scr, og_scr) = refs
    t = MAIN_TILE
    r0 = sub * t
    rows = slice(r0, r0 + t)
    tile_row0 = pl.program_id(0) * (MAIN_SUBTILES * t) + r0
    sb_idx = (sub + 1) * (MAIN_TILE // SUB_TILE) - 1
    n_pairs = t // GLA_PAIR
    hk, hv = GLA_HEAD_K, GLA_HEAD_V
    gate = mod_ref[2]

    h_ext = hext_scr[r0:r0 + t + 2 * HALO, :]
    h = hext_scr[r0 + HALO:r0 + HALO + t, :]

    q_scr[rows, :] = _dot_nt(h, wt_ref[Q_ROWS, :]) * (GLA_HEAD_K ** -0.5)
    gbias = jnp.concatenate([gbf_ref[...], gbb_ref[...]], axis=1)
    la_pre = _dot(lr_ref[rows, :], gkcat_ref[...]) + gbias
    te = t + 2 * HALO
    g_row = tile_row0 - HALO + lax.broadcasted_iota(jnp.int32, (te, 1), 0)
    valid = jnp.logical_and(g_row >= 0, g_row < seq_len)
    u_ext = jnp.where(valid, _dot_nt(h_ext, wt_ref[PIN_ROWS, :]), 0.0)
    yield

    la = _log_sigmoid(la_pre)
    p_gate = _dot_nt(h, wt_ref[PGATE_ROWS, :])
    u_ext_bf = u_ext.astype(BF16)
    u = u_ext[HALO:HALO + t, :]
    laf_hi, laf_lo = _split_hi_lo(la[:, :GLA_KEY_WIDTH])
    lab_hi, lab_lo = _split_hi_lo(la[:, GLA_KEY_WIDTH:])
    yield

    cb = CUMSUM_BLOCK
    row = lax.broadcasted_iota(jnp.int32, (cb, cb), 0)
    col = lax.broadcasted_iota(jnp.int32, (cb, cb), 1)
    same_chunk = (row // GLA_CHUNK) == (col // GLA_CHUNK)
    lower = jnp.where(jnp.logical_and(same_chunk, col <= row), GATE_SCALE, 0.0).astype(BF16)
    upper = jnp.where(jnp.logical_and(same_chunk, col >= row), GATE_SCALE, 0.0).astype(BF16)
    for b in range(t // cb):
        rb_rows = slice(b * cb, (b + 1) * cb)
        dst = slice(r0 + b * cb, r0 + (b + 1) * cb)
        cf_scr[dst, :] = _dot(lower, laf_hi[rb_rows]) + _dot(lower, laf_lo[rb_rows])
        rb_scr[dst, :] = _dot(upper, lab_hi[rb_rows]) + _dot(upper, lab_lo[rb_rows])
    bg_pool = _sigmoid(_dot_nt(h, wt_ref[BGP_ROWS, :]))
    yield

    pos = tile_row0 + lax.broadcasted_iota(jnp.int32, (t, 1), 0)
    rb_ = POOL_ROW_BLOCK
    kb_ = POOL_ROW_BLOCK + 2 * HALO
    band_r = lax.broadcasted_iota(jnp.int32, (rb_, kb_), 0)
    band_c = lax.broadcasted_iota(jnp.int32, (rb_, kb_), 1)
    band_d = band_c - HALO - band_r
    a_parts = []
    for gi, w in enumerate(POOL_WINDOWS):
        hw = w // 2
        cs = slice(gi * POOL_GROUP_WIDTH, (gi + 1) * POOL_GROUP_WIDTH)
        ws = slice(gi * POOL_GROUP_WIDTH, (gi + 1) * POOL_GROUP_WIDTH)
        band = jnp.logical_and(band_d >= -hw, band_d < hw).astype(BF16)
        wsum = jnp.concatenate(
            [_dot(band, u_ext_bf[b * rb_:b * rb_ + kb_, cs]) for b in range(t // rb_)], axis=0)
        count = (jnp.minimum(pos + hw, seq_len) - jnp.maximum(pos - hw, 0)).astype(F32)
        pooled = wsum / count - u[:, cs]
        a_parts.append(_dot(pooled.astype(BF16), poolw_ref[ws, :]))
    a = jnp.concatenate(a_parts, axis=1) * pscale_ref[...]
    z_pool = (a * _silu(p_gate)).astype(BF16)
    gla_gate = _dot_nt(h, wt_ref[GGATE_ROWS, :])
    bg_gla = _sigmoid(_dot_nt(h, wt_ref[BGG_ROWS, :]))

    def put_heads(dst, rs, fwd, bwd):
        for hd in range(GLA_HEADS):
            ks = slice(hd * hk, (hd + 1) * hk)
            dst[rs, 2 * hd * hk:(2 * hd + 1) * hk] = fwd[:, ks].astype(BF16)
            dst[rs, (2 * hd + 1) * hk:(2 * hd + 2) * hk] = bwd[:, ks].astype(BF16)

    dec_f, dec_b = [], []
    for p in range(n_pairs):
        c0 = r0 + p * GLA_PAIR
        c1 = c0 + GLA_CHUNK
        ra = slice(c0, c1)
        rb = slice(c1, c1 + GLA_CHUNK)
        q0, q1 = q_scr[ra, :], q_scr[rb, :]
        k0, k1 = k_ref[ra, :], k_ref[rb, :]
        cum0, cum1 = cf_scr[ra, :], cf_scr[rb, :]
        tot0 = cf_scr[c1 - 1:c1, :]
        tot1 = cf_scr[c1 + GLA_CHUNK - 1:c1 + GLA_CHUNK, :]
        rc0, rc1 = rb_scr[ra, :], rb_scr[rb, :]
        tb0 = rb_scr[c0:c0 + 1, :]
        tb1 = rb_scr[c1:c1 + 1, :]
        e_tot0, e_tot1 = jnp.exp(tot0), jnp.exp(tot1)
        e_tb0, e_tb1 = jnp.exp(tb0), jnp.exp(tb1)

        a0 = cum0 - tot0
        ks_f0 = k0 * jnp.exp(-a0)
        qs_f1 = q1 * jnp.exp(cum1)
        qs_b0 = q0 * jnp.exp(rc0)
        put_heads(qs_scr, ra, q0 * jnp.exp(a0), qs_b0)
        put_heads(qd_scr, ra, q0 * jnp.exp(cum0), qs_b0 * e_tb1)
        kif_scr[ra, :] = ks_f0.astype(BF16)
        kif_scr[rb, :] = (k1 * jnp.exp(-cum1)).astype(BF16)
        a1 = rc1 - tb1
        ks_b1 = k1 * jnp.exp(-a1)
        put_heads(qs_scr, rb, qs_f1, q1 * jnp.exp(a1))
        put_heads(qd_scr, rb, qs_f1 * e_tot0, q1 * jnp.exp(rc1))
        kib_scr[ra, :] = (k0 * jnp.exp(-rc0)).astype(BF16)
        kib_scr[rb, :] = ks_b1.astype(BF16)
        put_heads(ke_scr, ra, ks_f0 * e_tot1, k0 * jnp.exp(tb0 - rc0))
        put_heads(ke_scr, rb, k1 * jnp.exp(tot1 - cum1), ks_b1 * e_tb0)
        dec_f.append(e_tot0 * e_tot1)
        dec_b.append(e_tb0 * e_tb1)
    yield

    merged = bg_pool * _dot(z_pool, wpp_ref[...])
    for hd in range(GLA_HEADS):
        vs = slice(hd * hv, (hd + 1) * hv)
        cat = slice(2 * hd * hk, (2 * hd + 2) * hk)
        for p in range(n_pairs):
            rs = slice(r0 + p * GLA_PAIR, r0 + (p + 1) * GLA_PAIR)
            kv_scr[sub, hd, p] = _dot_tn(v_ref[rs, vs], ke_scr[rs, cat])
    yield

    prow = lax.broadcasted_iota(jnp.int32, (GLA_PAIR, GLA_PAIR), 0)
    pcol = lax.broadcasted_iota(jnp.int32, (GLA_PAIR, GLA_PAIR), 1)
    mask_f = pcol <= prow
    for p in range(n_pairs):
        rs = slice(r0 + p * GLA_PAIR, r0 + (p + 1) * GLA_PAIR)
        for hd in range(GLA_HEADS):
            ks = slice(hd * hk, (hd + 1) * hk)
            sc_f = _dot_nt(qs_scr[rs, 2 * hd * hk:(2 * hd + 1) * hk], kif_scr[rs, ks])
            sc_b = _dot_nt(qs_scr[rs, (2 * hd + 1) * hk:(2 * hd + 2) * hk], kib_scr[rs, ks])
            sc_scr[sub, p, hd] = jnp.where(mask_f, sc_f, sc_b).astype(BF16)
    for hd in range(GLA_HEADS):
        ks = slice(hd * hk, (hd + 1) * hk)
        sf = sf_scr[hd]
        for p in range(n_pairs):
            scat_scr[sub, p, hd, 0:hk, :] = sf.T.astype(BF16)
            sf = sf * dec_f[p][:, ks] + kv_scr[sub, hd, p, :, 0:hk]
        sf_scr[hd] = sf
        sb = sb_ref[sb_idx, hd]
        for p in reversed(range(n_pairs)):
            scat_scr[sub, p, hd, hk:2 * hk, :] = sb.T.astype(BF16)
            if p > 0:
                sb = sb * dec_b[p][:, ks] + kv_scr[sub, hd, p, :, hk:2 * hk]
    yield

    for p in range(n_pairs):
        rs = slice(r0 + p * GLA_PAIR, r0 + (p + 1) * GLA_PAIR)
        for hd in range(GLA_HEADS):
            vs = slice(hd * hv, (hd + 1) * hv)
            cat = slice(2 * hd * hk, (2 * hd + 2) * hk)
            og_scr[rs, vs] = (_dot(sc_scr[sub, p, hd], v_ref[rs, vs])
                              + _dot(qd_scr[rs, cat], scat_scr[sub, p, hd]))
    yield

    glag = glag_ref[...]
    z_parts = []
    for hd in range(GLA_HEADS):
        vs = slice(hd * hv, (hd + 1) * hv)
        oh = og_scr[rows, vs]
        ms = jnp.mean(oh * oh, axis=-1, keepdims=True)
        z_parts.append((oh * lax.rsqrt(ms + RMS_EPS)) * glag)
    o_n = jnp.concatenate(z_parts, axis=1)
    y_gla = _dot((o_n * _silu(gla_gate)).astype(BF16), wpgla_ref[...])
    merged = merged + bg_gla * y_gla
    yield

    out = _dot(merged.astype(BF16), wout_ref[...])
    yield

    ms = jnp.mean(out * out, axis=-1, keepdims=True)
    out_n = (out * lax.rsqrt(ms + RMS_EPS)) * gpost_ref[...]
    o_ref[rows, :] = x_ref[rows, :] + gate * out_n
    yield


def _main_kernel(seq_len,
                 x_ref, h_ref, hp_ref, hn_ref, mod_ref, gpost_ref, k_ref, v_ref, lr_ref,
                 wt_ref,
                 poolw_ref, pscale_ref, gkcat_ref, gbf_ref, gbb_ref, glag_ref,
                 wpp_ref, wpgla_ref, wout_ref, sb_ref,
                 o_ref,
                 sf_scr, hext_scr, q_scr, cf_scr, rb_scr, qs_scr, qd_scr, ke_scr, kif_scr, kib_scr,
                 kv_scr, scat_scr, sc_scr, og_scr):
    tb = MAIN_SUBTILES * MAIN_TILE

    @pl.when(pl.program_id(0) == 0)
    def _():
        sf_scr[...] = jnp.zeros_like(sf_scr)

    hext_scr[0:HALO, :] = hp_ref[...]
    hext_scr[HALO:HALO + tb, :] = h_ref[...]
    hext_scr[HALO + tb:HALO + tb + HALO, :] = hn_ref[...]

    refs = (x_ref, mod_ref, gpost_ref, k_ref, v_ref, lr_ref,
            wt_ref,
            poolw_ref, pscale_ref, gkcat_ref, gbf_ref, gbb_ref, glag_ref,
            wpp_ref, wpgla_ref, wout_ref, sb_ref, o_ref,
            sf_scr, hext_scr, q_scr, cf_scr, rb_scr, qs_scr, qd_scr, ke_scr, kif_scr, kib_scr,
            kv_scr, scat_scr, sc_scr, og_scr)
    gens = [_subtile_stages(seq_len, sub, refs) for sub in range(MAIN_SUBTILES)]
    _run_round_robin(gens)


def _main_call(x2, h_all, mod, g_post, k_all, v_all, lr_all, w_bf, small, wpool_bf, wpp_bf, wpgla_bf, wout_bf, sb):
    s = x2.shape[0]
    tb = MAIN_SUBTILES * MAIN_TILE
    nt = s // tb
    hb = tb // HALO
    n_hb = s // HALO
    n_pairs = MAIN_TILE // GLA_PAIR
    const2 = lambda i: (0, 0)
    tile = lambda i: (i, 0)
    one = pl.Buffered(1)
    pool_scale, gk_cat, gbias_f, gbias_b, gla_norm_g = small

    def resident(arr):
        return pl.BlockSpec(arr.shape, const2, pipeline_mode=one)

    in_specs = [
        pl.BlockSpec((tb, D_MODEL), tile),
        pl.BlockSpec((tb, D_MODEL), tile),
        pl.BlockSpec((HALO, D_MODEL), lambda i: (jnp.maximum(i * hb - 1, 0), 0)),
        pl.BlockSpec((HALO, D_MODEL), lambda i: (jnp.minimum((i + 1) * hb, n_hb - 1), 0)),
        pl.BlockSpec((3, 1, D_MODEL), lambda i: (0, 0, 0)),
        pl.BlockSpec((1, D_MODEL), const2),
        pl.BlockSpec((tb, GLA_KEY_WIDTH), tile),
        pl.BlockSpec((tb, D_MODEL), tile),
        pl.BlockSpec((tb, LR_WIDTH), tile),
        resident(w_bf),
        resident(wpool_bf), resident(pool_scale), resident(gk_cat), resident(gbias_f), resident(gbias_b),
        resident(gla_norm_g),
        resident(wpp_bf), resident(wpgla_bf), resident(wout_bf),
        pl.BlockSpec((tb // SUB_TILE, GLA_HEADS, GLA_HEAD_V, GLA_HEAD_K), lambda i: (i, 0, 0, 0)),
    ]
    scratch = [
        pltpu.VMEM((GLA_HEADS, GLA_HEAD_V, GLA_HEAD_K), F32),
        pltpu.VMEM((tb + 2 * HALO, D_MODEL), BF16),
        pltpu.VMEM((tb, GLA_KEY_WIDTH), F32),
        pltpu.VMEM((tb, GLA_KEY_WIDTH), F32),
        pltpu.VMEM((tb, GLA_KEY_WIDTH), F32),
        pltpu.VMEM((tb, 2 * GLA_KEY_WIDTH), BF16),
        pltpu.VMEM((tb, 2 * GLA_KEY_WIDTH), BF16),
        pltpu.VMEM((tb, 2 * GLA_KEY_WIDTH), BF16),
        pltpu.VMEM((tb, GLA_KEY_WIDTH), BF16),
        pltpu.VMEM((tb, GLA_KEY_WIDTH), BF16),
        pltpu.VMEM((MAIN_SUBTILES, GLA_HEADS, n_pairs, GLA_HEAD_V, 2 * GLA_HEAD_K), F32),
        pltpu.VMEM((MAIN_SUBTILES, n_pairs, GLA_HEADS, 2 * GLA_HEAD_K, GLA_HEAD_V), BF16),
        pltpu.VMEM((MAIN_SUBTILES, n_pairs, GLA_HEADS, GLA_PAIR, GLA_PAIR), BF16),
        pltpu.VMEM((tb, D_MODEL), F32),
    ]
    return pl.pallas_call(
        functools.partial(_main_kernel, s),
        grid=(nt,),
        in_specs=in_specs,
        out_specs=pl.BlockSpec((tb, D_MODEL), tile),
        out_shape=jax.ShapeDtypeStruct((s, D_MODEL), F32),
        scratch_shapes=scratch,
        compiler_params=pltpu.CompilerParams(dimension_semantics=("arbitrary",),
                                             vmem_limit_bytes=VMEM_LIMIT_BYTES),
        name="main_call",
    )(x2, h_all, h_all, h_all, mod, g_post, k_all, v_all, lr_all,
      w_bf,
      wpool_bf, pool_scale, gk_cat, gbias_f, gbias_b, gla_norm_g, wpp_bf, wpgla_bf, wout_bf, sb)


def _layer(x2, c, w_ada, b_ada, g_pre, g_post, w_in, pool_w, pool_scale, gk_up_fwd, gk_bias_fwd,
           gk_up_bwd, gk_bias_bwd, gla_norm_g, w_proj_pool, w_proj_gla, w_out):
    mod = _ada_call(c.reshape(D_MODEL, 1), w_ada, b_ada.reshape(1, -1))
    gbias_f = gk_bias_fwd.reshape(1, -1)
    gbias_b = gk_bias_bwd.reshape(1, -1)

    pool_w2 = pool_w.reshape(POOL_GROUPS * POOL_GROUP_WIDTH, POOL_GROUP_WIDTH)
    sb, k_all, v_all, h_all, lr_all, gk_cat, w_bf, wpool_bf, wpp_bf, wpgla_bf, wout_bf = _prep_call(
        x2, mod, g_pre.reshape(1, -1), gk_up_fwd, gk_up_bwd, gbias_b, w_in.T, pool_w2, w_proj_pool, w_proj_gla,
        w_out)

    small = (pool_scale.reshape(1, -1), gk_cat, gbias_f, gbias_b, gla_norm_g.reshape(1, -1))
    return _main_call(x2, h_all, mod, g_post.reshape(1, -1), k_all, v_all, lr_all, w_bf, small,
                      wpool_bf, wpp_bf, wpgla_bf, wout_bf, sb)


def kernel(x, c, w_ada, b_ada, g_pre, g_post, w_in, pool_w, pool_scale, gk_up_fwd, gk_bias_fwd, gk_up_bwd, gk_bias_bwd, gla_norm_g, w_proj_pool, w_proj_gla, w_out):
    b, s, d = x.shape
    depth = w_in.shape[0]
    xf = x.reshape(b * s, d)
    outs = []
    for bi in range(b):
        xb = xf if b == 1 else lax.slice_in_dim(xf, bi * s, (bi + 1) * s, axis=0)
        for l in range(depth):
            xb = _layer(xb, c[bi:bi + 1], w_ada[l], b_ada[l], g_pre[l], g_post[l], w_in[l], pool_w[l],
                        pool_scale[l], gk_up_fwd[l], gk_bias_fwd[l], gk_up_bwd[l], gk_bias_bwd[l],
                        gla_norm_g[l], w_proj_pool[l], w_proj_gla[l], w_out[l])
        outs.append(xb)
    out = outs[0] if b == 1 else jnp.concatenate(outs, axis=0)
    return out.reshape(b, s, d)
```

```python
import functools
import math

import jax
import jax.numpy as jnp
from jax import lax
from jax.experimental import pallas as pl
from jax.experimental.pallas import tpu as pltpu

D_MODEL = 1024
POOL_GROUPS = 4
POOL_GROUP_WIDTH = 256
POOL_WINDOWS = (2, 4, 8, 16)
GLA_HEADS = 4
GLA_KEY_WIDTH = 512
GLA_HEAD_K = 128
GLA_HEAD_V = 256
GLA_GATE_RANK = 16
GLA_GATE_NORMALIZER = 16.0
GATE_SCALE = 1.0 / GLA_GATE_NORMALIZER
assert math.frexp(GLA_GATE_NORMALIZER)[0] == 0.5
GLA_CHUNK = 64
GLA_PAIR = 2 * GLA_CHUNK
RMS_EPS = 1e-6
IN_SPLITS = (1024, 1024, 512, 512, 1024, 1024, 16, 16, 1024, 1024)
IN_WIDTH = sum(IN_SPLITS)

LANES = 128
SUB_TILE = 256
PREP_SUBTILES = 4
MAIN_TILE = 512
MAIN_SUBTILES = 1
CUMSUM_BLOCK = 256
HALO = 16
POOL_ROW_BLOCK = 128
VMEM_LIMIT_BYTES = 60 * 1024 * 1024

_OFFS = [sum(IN_SPLITS[:j]) for j in range(len(IN_SPLITS) + 1)]
(PIN_ROWS, PGATE_ROWS, Q_ROWS, K_ROWS, V_ROWS, GGATE_ROWS) = (slice(_OFFS[j], _OFFS[j + 1]) for j in range(6))
LR_ROWS = slice(_OFFS[6], _OFFS[8])
BGP_ROWS = slice(_OFFS[8], _OFFS[9])
BGG_ROWS = slice(_OFFS[9], _OFFS[10])
LR_WIDTH = 2 * GLA_GATE_RANK
WT_SLAB = 480

F32 = jnp.float32
BF16 = jnp.bfloat16


def _dot(a, b):
    return jnp.dot(a, b, preferred_element_type=F32)


def _dot_nt(a, b):
    return lax.dot_general(a, b, (((1,), (1,)), ((), ())), preferred_element_type=F32)


def _dot_tn(a, b):
    return lax.dot_general(a, b, (((0,), (0,)), ((), ())), preferred_element_type=F32)


def _sigmoid(x):
    return 0.5 * jnp.tanh(0.5 * x) + 0.5


def _silu(x):
    return x * _sigmoid(x)


def _log_sigmoid(x):
    return jnp.minimum(x, 0.0) - jnp.log(1.0 + jnp.exp(-jnp.abs(x)))


def _norm_mod(xv, g_pre, scale, shift):
    ms = jnp.mean(xv * xv, axis=-1, keepdims=True)
    hn = (xv * lax.rsqrt(ms + RMS_EPS)) * g_pre
    return hn * (1.0 + scale) + shift


def _split_hi_lo(a):
    hi = a.astype(BF16)
    lo = (a - hi.astype(F32)).astype(BF16)
    return hi, lo


def _run_round_robin(gens):
    live = list(gens)
    while live:
        for g in list(live):
            try:
                next(g)
            except StopIteration:
                live.remove(g)


def _ada_kernel(c_ref, w_ref, b_ref, o_ref):
    s = _silu(c_ref[...])
    o_ref[...] = jnp.sum(w_ref[...] * s, axis=0, keepdims=True) + b_ref[...]


def _ada_call(c_col, w_ada, b_ada):
    return pl.pallas_call(
        _ada_kernel,
        grid=(3,),
        in_specs=[
            pl.BlockSpec((D_MODEL, 1), lambda j: (0, 0)),
            pl.BlockSpec((D_MODEL, D_MODEL), lambda j: (0, j)),
            pl.BlockSpec((1, D_MODEL), lambda j: (0, j)),
        ],
        out_specs=pl.BlockSpec((None, 1, D_MODEL), lambda j: (j, 0, 0)),
        out_shape=jax.ShapeDtypeStruct((3, 1, D_MODEL), F32),
        compiler_params=pltpu.CompilerParams(dimension_semantics=("arbitrary",)),
        name="ada_call",
    )(c_col, w_ada, b_ada)


def _prep_stages(sub, refs):
    (gkcat_ref, gbias_ref, sb_ref, k_ref, v_ref, lr_ref, st_scr) = refs
    t = SUB_TILE
    rows = slice(sub * t, (sub + 1) * t)
    k = k_ref[rows, :]
    v_bf = v_ref[rows, :]

    gk_bwd = gkcat_ref[:, GLA_KEY_WIDTH:]
    la = _log_sigmoid(_dot(lr_ref[rows, :], gk_bwd) + gbias_ref[...])
    la_hi, la_lo = _split_hi_lo(la)
    yield

    row = lax.broadcasted_iota(jnp.int32, (t, t), 0)
    col = lax.broadcasted_iota(jnp.int32, (t, t), 1)
    upper = jnp.where(col >= row, GATE_SCALE, 0.0).astype(BF16)
    rc = _dot(upper, la_hi) + _dot(upper, la_lo)
    yield

    tot = rc[0:1, :]
    k_end = (k * jnp.exp(tot - rc)).astype(BF16)
    decay = jnp.exp(tot)
    kvs = []
    for hd in range(GLA_HEADS):
        ks = slice(hd * GLA_HEAD_K, (hd + 1) * GLA_HEAD_K)
        vs = slice(hd * GLA_HEAD_V, (hd + 1) * GLA_HEAD_V)
        kvs.append(_dot_tn(v_bf[:, vs], k_end[:, ks]))
    yield

    for hd in range(GLA_HEADS):
        ks = slice(hd * GLA_HEAD_K, (hd + 1) * GLA_HEAD_K)
        st = st_scr[hd]
        sb_ref[sub, hd] = st
        st_scr[hd] = st * decay[:, ks] + kvs[hd]
    yield


def _prep_kernel(x_ref, mod_ref, gpre_ref, gkf_ref, gkb_ref, gbias_ref, wk_ref, wv_ref, wlr_ref,
                 win_ref, wpool_ref, wpp_ref, wpgla_ref, wout_ref,
                 sb_ref, k_ref, v_ref, h_ref, lr_ref, gkcat_ref,
                 winbf_ref, wpoolbf_ref, wppbf_ref, wpglabf_ref, woutbf_ref,
                 st_scr, wk_scr, wv_scr, wlr_scr):
    @pl.when(pl.program_id(0) == 0)
    def _():
        st_scr[...] = jnp.zeros_like(st_scr)
        wk_scr[...] = wk_ref[...].astype(BF16)
        wv_scr[...] = wv_ref[...].astype(BF16)
        wlr_scr[...] = wlr_ref[...].astype(BF16)
        r, w = GLA_GATE_RANK, GLA_KEY_WIDTH
        gkcat_ref[...] = jnp.zeros(gkcat_ref.shape, BF16)
        gkcat_ref[0:r, 0:w] = gkf_ref[...].astype(BF16)
        gkcat_ref[r:2 * r, w:2 * w] = gkb_ref[...].astype(BF16)

    winbf_ref[...] = win_ref[...].astype(BF16)
    wpoolbf_ref[...] = wpool_ref[...].astype(BF16)
    wppbf_ref[...] = wpp_ref[...].astype(BF16)
    wpglabf_ref[...] = wpgla_ref[...].astype(BF16)
    woutbf_ref[...] = wout_ref[...].astype(BF16)

    h = _norm_mod(x_ref[...], gpre_ref[...], mod_ref[1], mod_ref[0]).astype(BF16)
    h_ref[...] = h
    k_ref[...] = _dot_nt(h, wk_scr[...])
    lr_ref[...] = _dot_nt(h, wlr_scr[...]).astype(BF16)
    v_ref[...] = _dot_nt(h, wv_scr[...]).astype(BF16)

    refs = (gkcat_ref, gbias_ref, sb_ref, k_ref, v_ref, lr_ref, st_scr)
    gens = [_prep_stages(sub, refs) for sub in reversed(range(PREP_SUBTILES))]
    _run_round_robin(gens)


def _prep_call(x2, mod, g_pre, gk_f, gk_b, gbias_b, w_in_t, pool_w2, w_proj_pool, w_proj_gla, w_out):
    s = x2.shape[0]
    tb = PREP_SUBTILES * SUB_TILE
    nb = s // tb
    slab = D_MODEL // nb
    n_wt = IN_WIDTH // WT_SLAB
    const = lambda i: (0, 0)
    rev = lambda i: (nb - 1 - i, 0)
    fwd = lambda i: (i, 0)
    wt_slab = lambda i: (jnp.minimum(i, n_wt - 1), 0)
    one = pl.Buffered(1)

    def rows_of(rs):
        n = rs.stop - rs.start
        return pl.BlockSpec((n, D_MODEL), lambda i: (rs.start // n, 0), pipeline_mode=one)

    return pl.pallas_call(
        _prep_kernel,
        grid=(nb,),
        in_specs=[
            pl.BlockSpec((tb, D_MODEL), rev),
            pl.BlockSpec((3, 1, D_MODEL), lambda i: (0, 0, 0)),
            pl.BlockSpec((1, D_MODEL), const),
            pl.BlockSpec(gk_f.shape, const),
            pl.BlockSpec(gk_b.shape, const),
            pl.BlockSpec(gbias_b.shape, const),
            rows_of(K_ROWS),
            rows_of(V_ROWS),
            rows_of(LR_ROWS),
            pl.BlockSpec((WT_SLAB, D_MODEL), wt_slab),
            pl.BlockSpec((slab, POOL_GROUP_WIDTH), fwd),
            pl.BlockSpec((slab, D_MODEL), fwd),
            pl.BlockSpec((slab, D_MODEL), fwd),
            pl.BlockSpec((slab, D_MODEL), fwd),
        ],
        out_specs=[
            pl.BlockSpec((PREP_SUBTILES, GLA_HEADS, GLA_HEAD_V, GLA_HEAD_K), lambda i: (nb - 1 - i, 0, 0, 0)),
            pl.BlockSpec((tb, GLA_KEY_WIDTH), rev),
            pl.BlockSpec((tb, D_MODEL), rev),
            pl.BlockSpec((tb, D_MODEL), rev),
            pl.BlockSpec((tb, LR_WIDTH), rev),
            pl.BlockSpec((LR_WIDTH, 2 * GLA_KEY_WIDTH), const),
            pl.BlockSpec((WT_SLAB, D_MODEL), wt_slab),
            pl.BlockSpec((slab, POOL_GROUP_WIDTH), fwd),
            pl.BlockSpec((slab, D_MODEL), fwd),
            pl.BlockSpec((slab, D_MODEL), fwd),
            pl.BlockSpec((slab, D_MODEL), fwd),
        ],
        out_shape=[
            jax.ShapeDtypeStruct((s // SUB_TILE, GLA_HEADS, GLA_HEAD_V, GLA_HEAD_K), F32),
            jax.ShapeDtypeStruct((s, GLA_KEY_WIDTH), F32),
            jax.ShapeDtypeStruct((s, D_MODEL), BF16),
            jax.ShapeDtypeStruct((s, D_MODEL), BF16),
            jax.ShapeDtypeStruct((s, LR_WIDTH), BF16),
            jax.ShapeDtypeStruct((LR_WIDTH, 2 * GLA_KEY_WIDTH), BF16),
            jax.ShapeDtypeStruct((IN_WIDTH, D_MODEL), BF16),
            jax.ShapeDtypeStruct(pool_w2.shape, BF16),
            jax.ShapeDtypeStruct((D_MODEL, D_MODEL), BF16),
            jax.ShapeDtypeStruct((D_MODEL, D_MODEL), BF16),
            jax.ShapeDtypeStruct((D_MODEL, D_MODEL), BF16),
        ],
        scratch_shapes=[
            pltpu.VMEM((GLA_HEADS, GLA_HEAD_V, GLA_HEAD_K), F32),
            pltpu.VMEM((GLA_KEY_WIDTH, D_MODEL), BF16),
            pltpu.VMEM((D_MODEL, D_MODEL), BF16),
            pltpu.VMEM((LR_WIDTH, D_MODEL), BF16),
        ],
        compiler_params=pltpu.CompilerParams(dimension_semantics=("arbitrary",),
                                             vmem_limit_bytes=VMEM_LIMIT_BYTES),
        name="prep_call",
    )(x2, mod, g_pre, gk_f, gk_b, gbias_b, w_in_t, w_in_t, w_in_t, w_in_t, pool_w2, w_proj_pool, w_proj_gla, w_out)


def _subtile_stages(seq_len, sub, refs):
    (x_ref, mod_ref, gpost_ref, k_ref, v_ref, lr_ref,
     wt_ref,
     poolw_ref, pscale_ref, gkcat_ref, gbf_ref, gbb_ref, glag_ref,
     wpp_ref, wpgla_ref, wout_ref, sb_ref, o_ref,
     sf_scr, hext_scr, q_scr, cf_scr, rb_scr, qs_scr, qd_scr, ke_scr, kif_scr, kib_scr,
     kv_scr, scat_scr, sc_scr, og_scr) = refs
    t = MAIN_TILE
    r0 = sub * t
    rows = slice(r0, r0 + t)
    tile_row0 = pl.program_id(0) * (MAIN_SUBTILES * t) + r0
    sb_idx = (sub + 1) * (MAIN_TILE // SUB_TILE) - 1
    n_pairs = t // GLA_PAIR
    hk, hv = GLA_HEAD_K, GLA_HEAD_V
    gate = mod_ref[2]

    h_ext = hext_scr[r0:r0 + t + 2 * HALO, :]
    h = hext_scr[r0 + HALO:r0 + HALO + t, :]

    gbias = jnp.concatenate([gbf_ref[...], gbb_ref[...]], axis=1)
    la_pre = _dot(lr_ref[rows, :], gkcat_ref[...]) + gbias
    q_scr[rows, :] = _dot_nt(h, wt_ref[Q_ROWS, :]) * (GLA_HEAD_K ** -0.5)
    te = t + 2 * HALO
    g_row = tile_row0 - HALO + lax.broadcasted_iota(jnp.int32, (te, 1), 0)
    valid = jnp.logical_and(g_row >= 0, g_row < seq_len)
    u_ext = jnp.where(valid, _dot_nt(h_ext, wt_ref[PIN_ROWS, :]), 0.0)
    yield

    la = _log_sigmoid(la_pre)
    p_gate = _dot_nt(h, wt_ref[PGATE_ROWS, :])
    gla_gate = _dot_nt(h, wt_ref[GGATE_ROWS, :])
    u_ext_bf = u_ext.astype(BF16)
    u = u_ext[HALO:HALO + t, :]
    laf_hi, laf_lo = _split_hi_lo(la[:, :GLA_KEY_WIDTH])
    lab_hi, lab_lo = _split_hi_lo(la[:, GLA_KEY_WIDTH:])
    yield

    cb = CUMSUM_BLOCK
    row = lax.broadcasted_iota(jnp.int32, (cb, cb), 0)
    col = lax.broadcasted_iota(jnp.int32, (cb, cb), 1)
    same_chunk = (row // GLA_CHUNK) == (col // GLA_CHUNK)
    lower = jnp.where(jnp.logical_and(same_chunk, col <= row), GATE_SCALE, 0.0).astype(BF16)
    upper = jnp.where(jnp.logical_and(same_chunk, col >= row), GATE_SCALE, 0.0).astype(BF16)
    for b in range(t // cb):
        rb_rows = slice(b * cb, (b + 1) * cb)
        dst = slice(r0 + b * cb, r0 + (b + 1) * cb)
        cf_scr[dst, :] = _dot(lower, laf_hi[rb_rows]) + _dot(lower, laf_lo[rb_rows])
        rb_scr[dst, :] = _dot(upper, lab_hi[rb_rows]) + _dot(upper, lab_lo[rb_rows])
    bg_pool = _sigmoid(_dot_nt(h, wt_ref[BGP_ROWS, :]))
    bg_gla = _sigmoid(_dot_nt(h, wt_ref[BGG_ROWS, :]))
    yield

    pos = tile_row0 + lax.broadcasted_iota(jnp.int32, (t, 1), 0)
    rb_ = POOL_ROW_BLOCK
    kb_ = POOL_ROW_BLOCK + 2 * HALO
    band_r = lax.broadcasted_iota(jnp.int32, (rb_, kb_), 0)
    band_c = lax.broadcasted_iota(jnp.int32, (rb_, kb_), 1)
    band_d = band_c - HALO - band_r
    a_parts = []
    for gi, w in enumerate(POOL_WINDOWS):
        hw = w // 2
        cs = slice(gi * POOL_GROUP_WIDTH, (gi + 1) * POOL_GROUP_WIDTH)
        ws = slice(gi * POOL_GROUP_WIDTH, (gi + 1) * POOL_GROUP_WIDTH)
        band = jnp.logical_and(band_d >= -hw, band_d < hw).astype(BF16)
        wsum = jnp.concatenate(
            [_dot(band, u_ext_bf[b * rb_:b * rb_ + kb_, cs]) for b in range(t // rb_)], axis=0)
        count = (jnp.minimum(pos + hw, seq_len) - jnp.maximum(pos - hw, 0)).astype(F32)
        pooled = wsum / count - u[:, cs]
        a_parts.append(_dot(pooled.astype(BF16), poolw_ref[ws, :]))
    a = jnp.concatenate(a_parts, axis=1) * pscale_ref[...]
    z_pool = (a * _silu(p_gate)).astype(BF16)

    def put_heads(dst, rs, fwd, bwd):
        for hd in range(GLA_HEADS):
            ks = slice(hd * hk, (hd + 1) * hk)
            dst[rs, 2 * hd * hk:(2 * hd + 1) * hk] = fwd[:, ks].astype(BF16)
            dst[rs, (2 * hd + 1) * hk:(2 * hd + 2) * hk] = bwd[:, ks].astype(BF16)

    dec_f, dec_b = [], []
    for p in range(n_pairs):
        c0 = r0 + p * GLA_PAIR
        c1 = c0 + GLA_CHUNK
        ra = slice(c0, c1)
        rb = slice(c1, c1 + GLA_CHUNK)
        q0, q1 = q_scr[ra, :], q_scr[rb, :]
        k0, k1 = k_ref[ra, :], k_ref[rb, :]
        cum0, cum1 = cf_scr[ra, :], cf_scr[rb, :]
        tot0 = cf_scr[c1 - 1:c1, :]
        tot1 = cf_scr[c1 + GLA_CHUNK - 1:c1 + GLA_CHUNK, :]
        rc0, rc1 = rb_scr[ra, :], rb_scr[rb, :]
        tb0 = rb_scr[c0:c0 + 1, :]
        tb1 = rb_scr[c1:c1 + 1, :]
        e_tot0, e_tot1 = jnp.exp(tot0), jnp.exp(tot1)
        e_tb0, e_tb1 = jnp.exp(tb0), jnp.exp(tb1)

        a0 = cum0 - tot0
        ks_f0 = k0 * jnp.exp(-a0)
        qs_f1 = q1 * jnp.exp(cum1)
        qs_b0 = q0 * jnp.exp(rc0)
        put_heads(qs_scr, ra, q0 * jnp.exp(a0), qs_b0)
        put_heads(qd_scr, ra, q0 * jnp.exp(cum0), qs_b0 * e_tb1)
        kif_scr[ra, :] = ks_f0.astype(BF16)
        kif_scr[rb, :] = (k1 * jnp.exp(-cum1)).astype(BF16)
        a1 = rc1 - tb1
        ks_b1 = k1 * jnp.exp(-a1)
        put_heads(qs_scr, rb, qs_f1, q1 * jnp.exp(a1))
        put_heads(qd_scr, rb, qs_f1 * e_tot0, q1 * jnp.exp(rc1))
        kib_scr[ra, :] = (k0 * jnp.exp(-rc0)).astype(BF16)
        kib_scr[rb, :] = ks_b1.astype(BF16)
        put_heads(ke_scr, ra, ks_f0 * e_tot1, k0 * jnp.exp(tb0 - rc0))
        put_heads(ke_scr, rb, k1 * jnp.exp(tot1 - cum1), ks_b1 * e_tb0)
        dec_f.append(e_tot0 * e_tot1)
        dec_b.append(e_tb0 * e_tb1)
    yield

    merged = bg_pool * _dot(z_pool, wpp_ref[...])
    for hd in range(GLA_HEADS):
        vs = slice(hd * hv, (hd + 1) * hv)
        cat = slice(2 * hd * hk, (2 * hd + 2) * hk)
        for p in range(n_pairs):
            rs = slice(r0 + p * GLA_PAIR, r0 + (p + 1) * GLA_PAIR)
            kv_scr[sub, hd, p] = _dot_tn(v_ref[rs, vs], ke_scr[rs, cat])
    yield

    prow = lax.broadcasted_iota(jnp.int32, (GLA_PAIR, GLA_PAIR), 0)
    pcol = lax.broadcasted_iota(jnp.int32, (GLA_PAIR, GLA_PAIR), 1)
    mask_f = pcol <= prow
    for p in range(n_pairs):
        rs = slice(r0 + p * GLA_PAIR, r0 + (p + 1) * GLA_PAIR)
        for hd in range(GLA_HEADS):
            ks = slice(hd * hk, (hd + 1) * hk)
            sc_f = _dot_nt(qs_scr[rs, 2 * hd * hk:(2 * hd + 1) * hk], kif_scr[rs, ks])
            sc_b = _dot_nt(qs_scr[rs, (2 * hd + 1) * hk:(2 * hd + 2) * hk], kib_scr[rs, ks])
            sc_scr[sub, p, hd] = jnp.where(mask_f, sc_f, sc_b).astype(BF16)
    for hd in range(GLA_HEADS):
        ks = slice(hd * hk, (hd + 1) * hk)
        sf = sf_scr[hd]
        for p in range(n_pairs):
            scat_scr[sub, p, hd, 0:hk, :] = sf.T.astype(BF16)
            sf = sf * dec_f[p][:, ks] + kv_scr[sub, hd, p, :, 0:hk]
        sf_scr[hd] = sf
        sb = sb_ref[sb_idx, hd]
        for p in reversed(range(n_pairs)):
            scat_scr[sub, p, hd, hk:2 * hk, :] = sb.T.astype(BF16)
            if p > 0:
                sb = sb * dec_b[p][:, ks] + kv_scr[sub, hd, p, :, hk:2 * hk]
    yield

    for p in range(n_pairs):
        rs = slice(r0 + p * GLA_PAIR, r0 + (p + 1) * GLA_PAIR)
        for hd in range(GLA_HEADS):
            vs = slice(hd * hv, (hd + 1) * hv)
            cat = slice(2 * hd * hk, (2 * hd + 2) * hk)
            og_scr[rs, vs] = (_dot(sc_scr[sub, p, hd], v_ref[rs, vs])
                              + _dot(qd_scr[rs, cat], scat_scr[sub, p, hd]))
    yield

    glag = glag_ref[...]
    z_parts = []
    for hd in range(GLA_HEADS):
        vs = slice(hd * hv, (hd + 1) * hv)
        oh = og_scr[rows, vs]
        ms = jnp.mean(oh * oh, axis=-1, keepdims=True)
        z_parts.append((oh * lax.rsqrt(ms + RMS_EPS)) * glag)
    o_n = jnp.concatenate(z_parts, axis=1)
    y_gla = _dot((o_n * _silu(gla_gate)).astype(BF16), wpgla_ref[...])
    merged = merged + bg_gla * y_gla
    yield

    out = _dot(merged.astype(BF16), wout_ref[...])
    yield

    ms = jnp.mean(out * out, axis=-1, keepdims=True)
    out_n = (out * lax.rsqrt(ms + RMS_EPS)) * gpost_ref[...]
    o_ref[rows, :] = x_ref[rows, :] + gate * out_n
    yield


def _main_kernel(seq_len,
                 x_ref, h_ref, hp_ref, hn_ref, mod_ref, gpost_ref, k_ref, v_ref, lr_ref,
                 wt_ref,
                 poolw_ref, pscale_ref, gkcat_ref, gbf_ref, gbb_ref, glag_ref,
                 wpp_ref, wpgla_ref, wout_ref, sb_ref,
                 o_ref,
                 sf_scr, hext_scr, q_scr, cf_scr, rb_scr, qs_scr, qd_scr, ke_scr, kif_scr, kib_scr,
                 kv_scr, scat_scr, sc_scr, og_scr):
    tb = MAIN_SUBTILES * MAIN_TILE

    @pl.when(pl.program_id(0) == 0)
    def _():
        sf_scr[...] = jnp.zeros_like(sf_scr)

    hext_scr[0:HALO, :] = hp_ref[...]
    hext_scr[HALO:HALO + tb, :] = h_ref[...]
    hext_scr[HALO + tb:HALO + tb + HALO, :] = hn_ref[...]

    refs = (x_ref, mod_ref, gpost_ref, k_ref, v_ref, lr_ref,
            wt_ref,
            poolw_ref, pscale_ref, gkcat_ref, gbf_ref, gbb_ref, glag_ref,
            wpp_ref, wpgla_ref, wout_ref, sb_ref, o_ref,
            sf_scr, hext_scr, q_scr, cf_scr, rb_scr, qs_scr, qd_scr, ke_scr, kif_scr, kib_scr,
            kv_scr, scat_scr, sc_scr, og_scr)
    gens = [_subtile_stages(seq_len, sub, refs) for sub in range(MAIN_SUBTILES)]
    _run_round_robin(gens)


def _main_call(x2, h_all, mod, g_post, k_all, v_all, lr_all, w_bf, small, wpool_bf, wpp_bf, wpgla_bf, wout_bf, sb):
    s = x2.shape[0]
    tb = MAIN_SUBTILES * MAIN_TILE
    nt = s // tb
    hb = tb // HALO
    n_hb = s // HALO
    n_pairs = MAIN_TILE // GLA_PAIR
    const2 = lambda i: (0, 0)
    tile = lambda i: (i, 0)
    one = pl.Buffered(1)
    pool_scale, gk_cat, gbias_f, gbias_b, gla_norm_g = small

    def resident(arr):
        return pl.BlockSpec(arr.shape, const2, pipeline_mode=one)

    in_specs = [
        pl.BlockSpec((tb, D_MODEL), tile),
        pl.BlockSpec((tb, D_MODEL), tile),
        pl.BlockSpec((HALO, D_MODEL), lambda i: (jnp.maximum(i * hb - 1, 0), 0)),
        pl.BlockSpec((HALO, D_MODEL), lambda i: (jnp.minimum((i + 1) * hb, n_hb - 1), 0)),
        pl.BlockSpec((3, 1, D_MODEL), lambda i: (0, 0, 0)),
        pl.BlockSpec((1, D_MODEL), const2),
        pl.BlockSpec((tb, GLA_KEY_WIDTH), tile),
        pl.BlockSpec((tb, D_MODEL), tile),
        pl.BlockSpec((tb, LR_WIDTH), tile),
        resident(w_bf),
        resident(wpool_bf), resident(pool_scale), resident(gk_cat), resident(gbias_f), resident(gbias_b),
        resident(gla_norm_g),
        resident(wpp_bf), resident(wpgla_bf), resident(wout_bf),
        pl.BlockSpec((tb // SUB_TILE, GLA_HEADS, GLA_HEAD_V, GLA_HEAD_K), lambda i: (i, 0, 0, 0)),
    ]
    scratch = [
        pltpu.VMEM((GLA_HEADS, GLA_HEAD_V, GLA_HEAD_K), F32),
        pltpu.VMEM((tb + 2 * HALO, D_MODEL), BF16),
        pltpu.VMEM((tb, GLA_KEY_WIDTH), F32),
        pltpu.VMEM((tb, GLA_KEY_WIDTH), F32),
        pltpu.VMEM((tb, GLA_KEY_WIDTH), F32),
        pltpu.VMEM((tb, 2 * GLA_KEY_WIDTH), BF16),
        pltpu.VMEM((tb, 2 * GLA_KEY_WIDTH), BF16),
        pltpu.VMEM((tb, 2 * GLA_KEY_WIDTH), BF16),
        pltpu.VMEM((tb, GLA_KEY_WIDTH), BF16),
        pltpu.VMEM((tb, GLA_KEY_WIDTH), BF16),
        pltpu.VMEM((MAIN_SUBTILES, GLA_HEADS, n_pairs, GLA_HEAD_V, 2 * GLA_HEAD_K), F32),
        pltpu.VMEM((MAIN_SUBTILES, n_pairs, GLA_HEADS, 2 * GLA_HEAD_K, GLA_HEAD_V), BF16),
        pltpu.VMEM((MAIN_SUBTILES, n_pairs, GLA_HEADS, GLA_PAIR, GLA_PAIR), BF16),
        pltpu.VMEM((tb, D_MODEL), F32),
    ]
    return pl.pallas_call(
        functools.partial(_main_kernel, s),
        grid=(nt,),
        in_specs=in_specs,
        out_specs=pl.BlockSpec((tb, D_MODEL), tile),
        out_shape=jax.ShapeDtypeStruct((s, D_MODEL), F32),
        scratch_shapes=scratch,
        compiler_params=pltpu.CompilerParams(dimension_semantics=("arbitrary",),
                                             vmem_limit_bytes=VMEM_LIMIT_BYTES),
        name="main_call",
    )(x2, h_all, h_all, h_all, mod, g_post, k_all, v_all, lr_all,
      w_bf,
      wpool_bf, pool_scale, gk_cat, gbias_f, gbias_b, gla_norm_g, wpp_bf, wpgla_bf, wout_bf, sb)


def _layer(x2, c, w_ada, b_ada, g_pre, g_post, w_in, pool_w, pool_scale, gk_up_fwd, gk_bias_fwd,
           gk_up_bwd, gk_bias_bwd, gla_norm_g, w_proj_pool, w_proj_gla, w_out):
    mod = _ada_call(c.reshape(D_MODEL, 1), w_ada, b_ada.reshape(1, -1))
    gbias_f = gk_bias_fwd.reshape(1, -1)
    gbias_b = gk_bias_bwd.reshape(1, -1)

    pool_w2 = pool_w.reshape(POOL_GROUPS * POOL_GROUP_WIDTH, POOL_GROUP_WIDTH)
    sb, k_all, v_all, h_all, lr_all, gk_cat, w_bf, wpool_bf, wpp_bf, wpgla_bf, wout_bf = _prep_call(
        x2, mod, g_pre.reshape(1, -1), gk_up_fwd, gk_up_bwd, gbias_b, w_in.T, pool_w2, w_proj_pool, w_proj_gla,
        w_out)

    small = (pool_scale.reshape(1, -1), gk_cat, gbias_f, gbias_b, gla_norm_g.reshape(1, -1))
    return _main_call(x2, h_all, mod, g_post.reshape(1, -1), k_all, v_all, lr_all, w_bf, small,
                      wpool_bf, wpp_bf, wpgla_bf, wout_bf, sb)


def kernel(x, c, w_ada, b_ada, g_pre, g_post, w_in, pool_w, pool_scale, gk_up_fwd, gk_bias_fwd, gk_up_bwd, gk_bias_bwd, gla_norm_g, w_proj_pool, w_proj_gla, w_out):
    b, s, d = x.shape
    depth = w_in.shape[0]
    xf = x.reshape(b * s, d)
    outs = []
    for bi in range(b):
        xb = xf if b == 1 else lax.slice_in_dim(xf, bi * s, (bi + 1) * s, axis=0)
        for l in range(depth):
            xb = _layer(xb, c[bi:bi + 1], w_ada[l], b_ada[l], g_pre[l], g_post[l], w_in[l], pool_w[l],
                        pool_scale[l], gk_up_fwd[l], gk_bias_fwd[l], gk_up_bwd[l], gk_bias_bwd[l],
                        gla_norm_g[l], w_proj_pool[l], w_proj_gla[l], w_out[l])
        outs.append(xb)
    out = outs[0] if b == 1 else jnp.concatenate(outs, axis=0)
    return out.reshape(b, s, d)
```

```python
import functools
import math

import jax
import jax.numpy as jnp
from jax import lax
from jax.experimental import pallas as pl
from jax.experimental.pallas import tpu as pltpu

D_MODEL = 1024
POOL_GROUPS = 4
POOL_GROUP_WIDTH = 256
POOL_WINDOWS = (2, 4, 8, 16)
GLA_HEADS = 4
GLA_KEY_WIDTH = 512
GLA_HEAD_K = 128
GLA_HEAD_V = 256
GLA_GATE_RANK = 16
GLA_GATE_NORMALIZER = 16.0
GATE_SCALE = 1.0 / GLA_GATE_NORMALIZER
assert math.frexp(GLA_GATE_NORMALIZER)[0] == 0.5
GLA_CHUNK = 64
GLA_PAIR = 2 * GLA_CHUNK
RMS_EPS = 1e-6
IN_SPLITS = (1024, 1024, 512, 512, 1024, 1024, 16, 16, 1024, 1024)
IN_WIDTH = sum(IN_SPLITS)

LANES = 128
SUB_TILE = 256
PREP_SUBTILES = 4
MAIN_TILE = 512
MAIN_SUBTILES = 1
CUMSUM_BLOCK = 256
HALO = 16
POOL_ROW_BLOCK = 128
VMEM_LIMIT_BYTES = 60 * 1024 * 1024

_OFFS = [sum(IN_SPLITS[:j]) for j in range(len(IN_SPLITS) + 1)]
(PIN_ROWS, PGATE_ROWS, Q_ROWS, K_ROWS, V_ROWS, GGATE_ROWS) = (slice(_OFFS[j], _OFFS[j + 1]) for j in range(6))
LR_ROWS = slice(_OFFS[6], _OFFS[8])
BGP_ROWS = slice(_OFFS[8], _OFFS[9])
BGG_ROWS = slice(_OFFS[9], _OFFS[10])
LR_WIDTH = 2 * GLA_GATE_RANK
WT_SLAB = 480

F32 = jnp.float32
BF16 = jnp.bfloat16


def _dot(a, b):
    return jnp.dot(a, b, preferred_element_type=F32)


def _dot_nt(a, b):
    return lax.dot_general(a, b, (((1,), (1,)), ((), ())), preferred_element_type=F32)


def _dot_tn(a, b):
    return lax.dot_general(a, b, (((0,), (0,)), ((), ())), preferred_element_type=F32)


def _sigmoid(x):
    return 0.5 * jnp.tanh(0.5 * x) + 0.5


def _silu(x):
    return x * _sigmoid(x)


def _log_sigmoid(x):
    return jnp.minimum(x, 0.0) - jnp.log(1.0 + jnp.exp(-jnp.abs(x)))


def _norm_mod(xv, g_pre, scale, shift):
    ms = jnp.mean(xv * xv, axis=-1, keepdims=True)
    gain = g_pre * (1.0 + scale)
    return (xv * lax.rsqrt(ms + RMS_EPS)) * gain + shift


def _split_hi_lo(a):
    hi = a.astype(BF16)
    lo = (a - hi.astype(F32)).astype(BF16)
    return hi, lo


def _run_round_robin(gens):
    live = list(gens)
    while live:
        for g in list(live):
            try:
                next(g)
            except StopIteration:
                live.remove(g)


def _ada_kernel(c_ref, w_ref, b_ref, o_ref):
    s = _silu(c_ref[...])
    o_ref[...] = jnp.sum(w_ref[...] * s, axis=0, keepdims=True) + b_ref[...]


def _ada_call(c_col, w_ada, b_ada):
    return pl.pallas_call(
        _ada_kernel,
        grid=(3,),
        in_specs=[
            pl.BlockSpec((D_MODEL, 1), lambda j: (0, 0)),
            pl.BlockSpec((D_MODEL, D_MODEL), lambda j: (0, j)),
            pl.BlockSpec((1, D_MODEL), lambda j: (0, j)),
        ],
        out_specs=pl.BlockSpec((None, 1, D_MODEL), lambda j: (j, 0, 0)),
        out_shape=jax.ShapeDtypeStruct((3, 1, D_MODEL), F32),
        compiler_params=pltpu.CompilerParams(dimension_semantics=("arbitrary",)),
        name="ada_call",
    )(c_col, w_ada, b_ada)


def _prep_stages(sub, refs):
    (gkcat_ref, gbias_ref, sb_ref, k_ref, v_ref, lr_ref, st_scr) = refs
    t = SUB_TILE
    rows = slice(sub * t, (sub + 1) * t)
    k = k_ref[rows, :]
    v_bf = v_ref[rows, :]

    gk_bwd = gkcat_ref[:, GLA_KEY_WIDTH:]
    la = _log_sigmoid(_dot(lr_ref[rows, :], gk_bwd) + gbias_ref[...])
    la_hi, la_lo = _split_hi_lo(la)
    yield

    row = lax.broadcasted_iota(jnp.int32, (t, t), 0)
    col = lax.broadcasted_iota(jnp.int32, (t, t), 1)
    upper = jnp.where(col >= row, GATE_SCALE, 0.0).astype(BF16)
    rc = _dot(upper, la_hi) + _dot(upper, la_lo)
    yield

    tot = rc[0:1, :]
    k_end = (k * jnp.exp(tot - rc)).astype(BF16)
    decay = jnp.exp(tot)
    kvs = []
    for hd in range(GLA_HEADS):
        ks = slice(hd * GLA_HEAD_K, (hd + 1) * GLA_HEAD_K)
        vs = slice(hd * GLA_HEAD_V, (hd + 1) * GLA_HEAD_V)
        kvs.append(_dot_tn(v_bf[:, vs], k_end[:, ks]))
    yield

    for hd in range(GLA_HEADS):
        ks = slice(hd * GLA_HEAD_K, (hd + 1) * GLA_HEAD_K)
        st = st_scr[hd]
        sb_ref[sub, hd] = st
        st_scr[hd] = st * decay[:, ks] + kvs[hd]
    yield


def _prep_kernel(x_ref, mod_ref, gpre_ref, gkf_ref, gkb_ref, gbias_ref, wk_ref, wv_ref, wlr_ref,
                 win_ref, wpool_ref, wpp_ref, wpgla_ref, wout_ref,
                 sb_ref, k_ref, v_ref, h_ref, lr_ref, gkcat_ref,
                 winbf_ref, wpoolbf_ref, wppbf_ref, wpglabf_ref, woutbf_ref,
                 st_scr, wk_scr, wv_scr, wlr_scr):
    @pl.when(pl.program_id(0) == 0)
    def _():
        st_scr[...] = jnp.zeros_like(st_scr)
        wk_scr[...] = wk_ref[...].astype(BF16)
        wv_scr[...] = wv_ref[...].astype(BF16)
        wlr_scr[...] = wlr_ref[...].astype(BF16)
        r, w = GLA_GATE_RANK, GLA_KEY_WIDTH
        gkcat_ref[...] = jnp.zeros(gkcat_ref.shape, BF16)
        gkcat_ref[0:r, 0:w] = gkf_ref[...].astype(BF16)
        gkcat_ref[r:2 * r, w:2 * w] = gkb_ref[...].astype(BF16)

    winbf_ref[...] = win_ref[...].astype(BF16)
    wpoolbf_ref[...] = wpool_ref[...].astype(BF16)
    wppbf_ref[...] = wpp_ref[...].astype(BF16)
    wpglabf_ref[...] = wpgla_ref[...].astype(BF16)
    woutbf_ref[...] = wout_ref[...].astype(BF16)

    h = _norm_mod(x_ref[...], gpre_ref[...], mod_ref[1], mod_ref[0]).astype(BF16)
    h_ref[...] = h
    k_ref[...] = _dot_nt(h, wk_scr[...])
    lr_ref[...] = _dot_nt(h, wlr_scr[...]).astype(BF16)
    v_ref[...] = _dot_nt(h, wv_scr[...]).astype(BF16)

    refs = (gkcat_ref, gbias_ref, sb_ref, k_ref, v_ref, lr_ref, st_scr)
    gens = [_prep_stages(sub, refs) for sub in reversed(range(PREP_SUBTILES))]
    _run_round_robin(gens)


def _prep_call(x2, mod, g_pre, gk_f, gk_b, gbias_b, w_in_t, pool_w2, w_proj_pool, w_proj_gla, w_out):
    s = x2.shape[0]
    tb = PREP_SUBTILES * SUB_TILE
    nb = s // tb
    slab = D_MODEL // nb
    n_wt = IN_WIDTH // WT_SLAB
    const = lambda i: (0, 0)
    rev = lambda i: (nb - 1 - i, 0)
    fwd = lambda i: (i, 0)
    wt_slab = lambda i: (jnp.minimum(i, n_wt - 1), 0)
    one = pl.Buffered(1)

    def rows_of(rs):
        n = rs.stop - rs.start
        return pl.BlockSpec((n, D_MODEL), lambda i: (rs.start // n, 0), pipeline_mode=one)

    return pl.pallas_call(
        _prep_kernel,
        grid=(nb,),
        in_specs=[
            pl.BlockSpec((tb, D_MODEL), rev),
            pl.BlockSpec((3, 1, D_MODEL), lambda i: (0, 0, 0)),
            pl.BlockSpec((1, D_MODEL), const),
            pl.BlockSpec(gk_f.shape, const),
            pl.BlockSpec(gk_b.shape, const),
            pl.BlockSpec(gbias_b.shape, const),
            rows_of(K_ROWS),
            rows_of(V_ROWS),
            rows_of(LR_ROWS),
            pl.BlockSpec((WT_SLAB, D_MODEL), wt_slab),
            pl.BlockSpec((slab, POOL_GROUP_WIDTH), fwd),
            pl.BlockSpec((slab, D_MODEL), fwd),
            pl.BlockSpec((slab, D_MODEL), fwd),
            pl.BlockSpec((slab, D_MODEL), fwd),
        ],
        out_specs=[
            pl.BlockSpec((PREP_SUBTILES, GLA_HEADS, GLA_HEAD_V, GLA_HEAD_K), lambda i: (nb - 1 - i, 0, 0, 0)),
            pl.BlockSpec((tb, GLA_KEY_WIDTH), rev),
            pl.BlockSpec((tb, D_MODEL), rev),
            pl.BlockSpec((tb, D_MODEL), rev),
            pl.BlockSpec((tb, LR_WIDTH), rev),
            pl.BlockSpec((LR_WIDTH, 2 * GLA_KEY_WIDTH), const),
            pl.BlockSpec((WT_SLAB, D_MODEL), wt_slab),
            pl.BlockSpec((slab, POOL_GROUP_WIDTH), fwd),
            pl.BlockSpec((slab, D_MODEL), fwd),
            pl.BlockSpec((slab, D_MODEL), fwd),
            pl.BlockSpec((slab, D_MODEL), fwd),
        ],
        out_shape=[
            jax.ShapeDtypeStruct((s // SUB_TILE, GLA_HEADS, GLA_HEAD_V, GLA_HEAD_K), F32),
            jax.ShapeDtypeStruct((s, GLA_KEY_WIDTH), F32),
            jax.ShapeDtypeStruct((s, D_MODEL), BF16),
            jax.ShapeDtypeStruct((s, D_MODEL), BF16),
            jax.ShapeDtypeStruct((s, LR_WIDTH), BF16),
            jax.ShapeDtypeStruct((LR_WIDTH, 2 * GLA_KEY_WIDTH), BF16),
            jax.ShapeDtypeStruct((IN_WIDTH, D_MODEL), BF16),
            jax.ShapeDtypeStruct(pool_w2.shape, BF16),
            jax.ShapeDtypeStruct((D_MODEL, D_MODEL), BF16),
            jax.ShapeDtypeStruct((D_MODEL, D_MODEL), BF16),
            jax.ShapeDtypeStruct((D_MODEL, D_MODEL), BF16),
        ],
        scratch_shapes=[
            pltpu.VMEM((GLA_HEADS, GLA_HEAD_V, GLA_HEAD_K), F32),
            pltpu.VMEM((GLA_KEY_WIDTH, D_MODEL), BF16),
            pltpu.VMEM((D_MODEL, D_MODEL), BF16),
            pltpu.VMEM((LR_WIDTH, D_MODEL), BF16),
        ],
        compiler_params=pltpu.CompilerParams(dimension_semantics=("arbitrary",),
                                             vmem_limit_bytes=VMEM_LIMIT_BYTES),
        name="prep_call",
    )(x2, mod, g_pre, gk_f, gk_b, gbias_b, w_in_t, w_in_t, w_in_t, w_in_t, pool_w2, w_proj_pool, w_proj_gla, w_out)


def _subtile_stages(seq_len, sub, refs):
    (x_ref, mod_ref, gpost_ref, k_ref, v_ref, lr_ref,
     wt_ref,
     poolw_ref, pscale_ref, gkcat_ref, gbf_ref, gbb_ref, glag_ref,
     wpp_ref, wpgla_ref, wout_ref, sb_ref, o_ref,
     sf_scr, hext_scr, q_scr, cf_scr, rb_scr, qs_scr, qd_scr, ke_scr, kif_scr, kib_scr,
     kv_scr, scat_scr, sc_scr, og_scr) = refs
    t = MAIN_TILE
    r0 = sub * t
    rows = slice(r0, r0 + t)
    tile_row0 = pl.program_id(0) * (MAIN_SUBTILES * t) + r0
    sb_idx = (sub + 1) * (MAIN_TILE // SUB_TILE) - 1
    n_pairs = t // GLA_PAIR
    hk, hv = GLA_HEAD_K, GLA_HEAD_V
    gate = mod_ref[2]

    h_ext = hext_scr[r0:r0 + t + 2 * HALO, :]
    h = hext_scr[r0 + HALO:r0 + HALO + t, :]

    gbias = jnp.concatenate([gbf_ref[...], gbb_ref[...]], axis=1)
    la_pre = _dot(lr_ref[rows, :], gkcat_ref[...]) + gbias
    q_scr[rows, :] = _dot_nt(h, wt_ref[Q_ROWS, :]) * (GLA_HEAD_K ** -0.5)
    te = t + 2 * HALO
    g_row = tile_row0 - HALO + lax.broadcasted_iota(jnp.int32, (te, 1), 0)
    valid = jnp.logical_and(g_row >= 0, g_row < seq_len)
    u_ext = jnp.where(valid, _dot_nt(h_ext, wt_ref[PIN_ROWS, :]), 0.0)
    yield

    la = _log_sigmoid(la_pre)
    p_gate = _dot_nt(h, wt_ref[PGATE_ROWS, :])
    gla_gate = _dot_nt(h, wt_ref[GGATE_ROWS, :])
    u_ext_bf = u_ext.astype(BF16)
    u = u_ext[HALO:HALO + t, :]
    laf_hi, laf_lo = _split_hi_lo(la[:, :GLA_KEY_WIDTH])
    lab_hi, lab_lo = _split_hi_lo(la[:, GLA_KEY_WIDTH:])
    yield

    cb = CUMSUM_BLOCK
    row = lax.broadcasted_iota(jnp.int32, (cb, cb), 0)
    col = lax.broadcasted_iota(jnp.int32, (cb, cb), 1)
    same_chunk = (row // GLA_CHUNK) == (col // GLA_CHUNK)
    lower = jnp.where(jnp.logical_and(same_chunk, col <= row), GATE_SCALE, 0.0).astype(BF16)
    upper = jnp.where(jnp.logical_and(same_chunk, col >= row), GATE_SCALE, 0.0).astype(BF16)
    for b in range(t // cb):
        rb_rows = slice(b * cb, (b + 1) * cb)
        dst = slice(r0 + b * cb, r0 + (b + 1) * cb)
        cf_scr[dst, :] = _dot(lower, laf_hi[rb_rows]) + _dot(lower, laf_lo[rb_rows])
        rb_scr[dst, :] = _dot(upper, lab_hi[rb_rows]) + _dot(upper, lab_lo[rb_rows])
    bg_pool = _sigmoid(_dot_nt(h, wt_ref[BGP_ROWS, :]))
    bg_gla = _sigmoid(_dot_nt(h, wt_ref[BGG_ROWS, :]))
    yield

    pos = tile_row0 + lax.broadcasted_iota(jnp.int32, (t, 1), 0)
    rb_ = POOL_ROW_BLOCK
    kb_ = POOL_ROW_BLOCK + 2 * HALO
    band_r = lax.broadcasted_iota(jnp.int32, (rb_, kb_), 0)
    band_c = lax.broadcasted_iota(jnp.int32, (rb_, kb_), 1)
    band_d = band_c - HALO - band_r
    a_parts = []
    for gi, w in enumerate(POOL_WINDOWS):
        hw = w // 2
        cs = slice(gi * POOL_GROUP_WIDTH, (gi + 1) * POOL_GROUP_WIDTH)
        ws = slice(gi * POOL_GROUP_WIDTH, (gi + 1) * POOL_GROUP_WIDTH)
        band = jnp.logical_and(band_d >= -hw, band_d < hw).astype(BF16)
        wsum = jnp.concatenate(
            [_dot(band, u_ext_bf[b * rb_:b * rb_ + kb_, cs]) for b in range(t // rb_)], axis=0)
        count = (jnp.minimum(pos + hw, seq_len) - jnp.maximum(pos - hw, 0)).astype(F32)
        pooled = wsum / count - u[:, cs]
        a_parts.append(_dot(pooled.astype(BF16), poolw_ref[ws, :]))
    a = jnp.concatenate(a_parts, axis=1) * pscale_ref[...]
    z_pool = (a * _silu(p_gate)).astype(BF16)

    def put_heads(dst, rs, fwd, bwd):
        for hd in range(GLA_HEADS):
            ks = slice(hd * hk, (hd + 1) * hk)
            dst[rs, 2 * hd * hk:(2 * hd + 1) * hk] = fwd[:, ks].astype(BF16)
            dst[rs, (2 * hd + 1) * hk:(2 * hd + 2) * hk] = bwd[:, ks].astype(BF16)

    dec_f, dec_b = [], []
    for p in range(n_pairs):
        c0 = r0 + p * GLA_PAIR
        c1 = c0 + GLA_CHUNK
        ra = slice(c0, c1)
        rb = slice(c1, c1 + GLA_CHUNK)
        q0, q1 = q_scr[ra, :], q_scr[rb, :]
        k0, k1 = k_ref[ra, :], k_ref[rb, :]
        cum0, cum1 = cf_scr[ra, :], cf_scr[rb, :]
        tot0 = cf_scr[c1 - 1:c1, :]
        tot1 = cf_scr[c1 + GLA_CHUNK - 1:c1 + GLA_CHUNK, :]
        rc0, rc1 = rb_scr[ra, :], rb_scr[rb, :]
        tb0 = rb_scr[c0:c0 + 1, :]
        tb1 = rb_scr[c1:c1 + 1, :]
        e_tot0, e_tot1 = jnp.exp(tot0), jnp.exp(tot1)
        e_tb0, e_tb1 = jnp.exp(tb0), jnp.exp(tb1)

        a0 = cum0 - tot0
        ks_f0 = k0 * jnp.exp(-a0)
        qs_f1 = q1 * jnp.exp(cum1)
        qs_b0 = q0 * jnp.exp(rc0)
        put_heads(qs_scr, ra, q0 * jnp.exp(a0), qs_b0)
        put_heads(qd_scr, ra, q0 * jnp.exp(cum0), qs_b0 * e_tb1)
        kif_scr[ra, :] = ks_f0.astype(BF16)
        kif_scr[rb, :] = (k1 * jnp.exp(-cum1)).astype(BF16)
        a1 = rc1 - tb1
        ks_b1 = k1 * jnp.exp(-a1)
        put_heads(qs_scr, rb, qs_f1, q1 * jnp.exp(a1))
        put_heads(qd_scr, rb, qs_f1 * e_tot0, q1 * jnp.exp(rc1))
        kib_scr[ra, :] = (k0 * jnp.exp(-rc0)).astype(BF16)
        kib_scr[rb, :] = ks_b1.astype(BF16)
        put_heads(ke_scr, ra, ks_f0 * e_tot1, k0 * jnp.exp(tb0 - rc0))
        put_heads(ke_scr, rb, k1 * jnp.exp(tot1 - cum1), ks_b1 * e_tb0)
        dec_f.append(e_tot0 * e_tot1)
        dec_b.append(e_tb0 * e_tb1)
    yield

    merged = bg_pool * _dot(z_pool, wpp_ref[...])
    for hd in range(GLA_HEADS):
        vs = slice(hd * hv, (hd + 1) * hv)
        cat = slice(2 * hd * hk, (2 * hd + 2) * hk)
        for p in range(n_pairs):
            rs = slice(r0 + p * GLA_PAIR, r0 + (p + 1) * GLA_PAIR)
            kv_scr[sub, hd, p] = _dot_tn(v_ref[rs, vs], ke_scr[rs, cat])
    yield

    prow = lax.broadcasted_iota(jnp.int32, (GLA_PAIR, GLA_PAIR), 0)
    pcol = lax.broadcasted_iota(jnp.int32, (GLA_PAIR, GLA_PAIR), 1)
    mask_f = pcol <= prow
    for p in range(n_pairs):
        rs = slice(r0 + p * GLA_PAIR, r0 + (p + 1) * GLA_PAIR)
        for hd in range(GLA_HEADS):
            ks = slice(hd * hk, (hd + 1) * hk)
            sc_f = _dot_nt(qs_scr[rs, 2 * hd * hk:(2 * hd + 1) * hk], kif_scr[rs, ks])
            sc_b = _dot_nt(qs_scr[rs, (2 * hd + 1) * hk:(2 * hd + 2) * hk], kib_scr[rs, ks])
            sc_scr[sub, p, hd] = jnp.where(mask_f, sc_f, sc_b).astype(BF16)
    for hd in range(GLA_HEADS):
        ks = slice(hd * hk, (hd + 1) * hk)
        sf = sf_scr[hd]
        for p in range(n_pairs):
            scat_scr[sub, p, hd, 0:hk, :] = sf.T.astype(BF16)
            sf = sf * dec_f[p][:, ks] + kv_scr[sub, hd, p, :, 0:hk]
        sf_scr[hd] = sf
        sb = sb_ref[sb_idx, hd]
        for p in reversed(range(n_pairs)):
            scat_scr[sub, p, hd, hk:2 * hk, :] = sb.T.astype(BF16)
            if p > 0:
                sb = sb * dec_b[p][:, ks] + kv_scr[sub, hd, p, :, hk:2 * hk]
    yield

    for p in range(n_pairs):
        rs = slice(r0 + p * GLA_PAIR, r0 + (p + 1) * GLA_PAIR)
        for hd in range(GLA_HEADS):
            vs = slice(hd * hv, (hd + 1) * hv)
            cat = slice(2 * hd * hk, (2 * hd + 2) * hk)
            og_scr[rs, vs] = (_dot(sc_scr[sub, p, hd], v_ref[rs, vs])
                              + _dot(qd_scr[rs, cat], scat_scr[sub, p, hd]))
    yield

    glag = glag_ref[...]
    z_parts = []
    for hd in range(GLA_HEADS):
        vs = slice(hd * hv, (hd + 1) * hv)
        oh = og_scr[rows, vs]
        ms = jnp.mean(oh * oh, axis=-1, keepdims=True)
        z_parts.append((oh * lax.rsqrt(ms + RMS_EPS)) * glag)
    o_n = jnp.concatenate(z_parts, axis=1)
    y_gla = _dot((o_n * _silu(gla_gate)).astype(BF16), wpgla_ref[...])
    merged = merged + bg_gla * y_gla
    yield

    out = _dot(merged.astype(BF16), wout_ref[...])
    yield

    ms = jnp.mean(out * out, axis=-1, keepdims=True)
    out_n = (out * lax.rsqrt(ms + RMS_EPS)) * gpost_ref[...]
    o_ref[rows, :] = x_ref[rows, :] + gate * out_n
    yield


def _main_kernel(seq_len,
                 x_ref, h_ref, hp_ref, hn_ref, mod_ref, gpost_ref, k_ref, v_ref, lr_ref,
                 wt_ref,
                 poolw_ref, pscale_ref, gkcat_ref, gbf_ref, gbb_ref, glag_ref,
                 wpp_ref, wpgla_ref, wout_ref, sb_ref,
                 o_ref,
                 sf_scr, hext_scr, q_scr, cf_scr, rb_scr, qs_scr, qd_scr, ke_scr, kif_scr, kib_scr,
                 kv_scr, scat_scr, sc_scr, og_scr):
    tb = MAIN_SUBTILES * MAIN_TILE

    @pl.when(pl.program_id(0) == 0)
    def _():
        sf_scr[...] = jnp.zeros_like(sf_scr)

    hext_scr[0:HALO, :] = hp_ref[...]
    hext_scr[HALO:HALO + tb, :] = h_ref[...]
    hext_scr[HALO + tb:HALO + tb + HALO, :] = hn_ref[...]

    refs = (x_ref, mod_ref, gpost_ref, k_ref, v_ref, lr_ref,
            wt_ref,
            poolw_ref, pscale_ref, gkcat_ref, gbf_ref, gbb_ref, glag_ref,
            wpp_ref, wpgla_ref, wout_ref, sb_ref, o_ref,
            sf_scr, hext_scr, q_scr, cf_scr, rb_scr, qs_scr, qd_scr, ke_scr, kif_scr, kib_scr,
            kv_scr, scat_scr, sc_scr, og_scr)
    gens = [_subtile_stages(seq_len, sub, refs) for sub in range(MAIN_SUBTILES)]
    _run_round_robin(gens)


def _main_call(x2, h_all, mod, g_post, k_all, v_all, lr_all, w_bf, small, wpool_bf, wpp_bf, wpgla_bf, wout_bf, sb):
    s = x2.shape[0]
    tb = MAIN_SUBTILES * MAIN_TILE
    nt = s // tb
    hb = tb // HALO
    n_hb = s // HALO
    n_pairs = MAIN_TILE // GLA_PAIR
    const2 = lambda i: (0, 0)
    tile = lambda i: (i, 0)
    one = pl.Buffered(1)
    pool_scale, gk_cat, gbias_f, gbias_b, gla_norm_g = small

    def resident(arr):
        return pl.BlockSpec(arr.shape, const2, pipeline_mode=one)

    in_specs = [
        pl.BlockSpec((tb, D_MODEL), tile),
        pl.BlockSpec((tb, D_MODEL), tile),
        pl.BlockSpec((HALO, D_MODEL), lambda i: (jnp.maximum(i * hb - 1, 0), 0)),
        pl.BlockSpec((HALO, D_MODEL), lambda i: (jnp.minimum((i + 1) * hb, n_hb - 1), 0)),
        pl.BlockSpec((3, 1, D_MODEL), lambda i: (0, 0, 0)),
        pl.BlockSpec((1, D_MODEL), const2),
        pl.BlockSpec((tb, GLA_KEY_WIDTH), tile),
        pl.BlockSpec((tb, D_MODEL), tile),
        pl.BlockSpec((tb, LR_WIDTH), tile),
        resident(w_bf),
        resident(wpool_bf), resident(pool_scale), resident(gk_cat), resident(gbias_f), resident(gbias_b),
        resident(gla_norm_g),
        resident(wpp_bf), resident(wpgla_bf), resident(wout_bf),
        pl.BlockSpec((tb // SUB_TILE, GLA_HEADS, GLA_HEAD_V, GLA_HEAD_K), lambda i: (i, 0, 0, 0)),
    ]
    scratch = [
        pltpu.VMEM((GLA_HEADS, GLA_HEAD_V, GLA_HEAD_K), F32),
        pltpu.VMEM((tb + 2 * HALO, D_MODEL), BF16),
        pltpu.VMEM((tb, GLA_KEY_WIDTH), F32),
        pltpu.VMEM((tb, GLA_KEY_WIDTH), F32),
        pltpu.VMEM((tb, GLA_KEY_WIDTH), F32),
        pltpu.VMEM((tb, 2 * GLA_KEY_WIDTH), BF16),
        pltpu.VMEM((tb, 2 * GLA_KEY_WIDTH), BF16),
        pltpu.VMEM((tb, 2 * GLA_KEY_WIDTH), BF16),
        pltpu.VMEM((tb, GLA_KEY_WIDTH), BF16),
        pltpu.VMEM((tb, GLA_KEY_WIDTH), BF16),
        pltpu.VMEM((MAIN_SUBTILES, GLA_HEADS, n_pairs, GLA_HEAD_V, 2 * GLA_HEAD_K), F32),
        pltpu.VMEM((MAIN_SUBTILES, n_pairs, GLA_HEADS, 2 * GLA_HEAD_K, GLA_HEAD_V), BF16),
        pltpu.VMEM((MAIN_SUBTILES, n_pairs, GLA_HEADS, GLA_PAIR, GLA_PAIR), BF16),
        pltpu.VMEM((tb, D_MODEL), F32),
    ]
    return pl.pallas_call(
        functools.partial(_main_kernel, s),
        grid=(nt,),
        in_specs=in_specs,
        out_specs=pl.BlockSpec((tb, D_MODEL), tile),
        out_shape=jax.ShapeDtypeStruct((s, D_MODEL), F32),
        scratch_shapes=scratch,
        compiler_params=pltpu.CompilerParams(dimension_semantics=("arbitrary",),
                                             vmem_limit_bytes=VMEM_LIMIT_BYTES),
        name="main_call",
    )(x2, h_all, h_all, h_all, mod, g_post, k_all, v_all, lr_all,
      w_bf,
      wpool_bf, pool_scale, gk_cat, gbias_f, gbias_b, gla_norm_g, wpp_bf, wpgla_bf, wout_bf, sb)


def _layer(x2, c, w_ada, b_ada, g_pre, g_post, w_in, pool_w, pool_scale, gk_up_fwd, gk_bias_fwd,
           gk_up_bwd, gk_bias_bwd, gla_norm_g, w_proj_pool, w_proj_gla, w_out):
    mod = _ada_call(c.reshape(D_MODEL, 1), w_ada, b_ada.reshape(1, -1))
    gbias_f = gk_bias_fwd.reshape(1, -1)
    gbias_b = gk_bias_bwd.reshape(1, -1)

    pool_w2 = pool_w.reshape(POOL_GROUPS * POOL_GROUP_WIDTH, POOL_GROUP_WIDTH)
    sb, k_all, v_all, h_all, lr_all, gk_cat, w_bf, wpool_bf, wpp_bf, wpgla_bf, wout_bf = _prep_call(
        x2, mod, g_pre.reshape(1, -1), gk_up_fwd, gk_up_bwd, gbias_b, w_in.T, pool_w2, w_proj_pool, w_proj_gla,
        w_out)

    small = (pool_scale.reshape(1, -1), gk_cat, gbias_f, gbias_b, gla_norm_g.reshape(1, -1))
    return _main_call(x2, h_all, mod, g_post.reshape(1, -1), k_all, v_all, lr_all, w_bf, small,
                      wpool_bf, wpp_bf, wpgla_bf, wout_bf, sb)


def kernel(x, c, w_ada, b_ada, g_pre, g_post, w_in, pool_w, pool_scale, gk_up_fwd, gk_bias_fwd, gk_up_bwd, gk_bias_bwd, gla_norm_g, w_proj_pool, w_proj_gla, w_out):
    b, s, d = x.shape
    depth = w_in.shape[0]
    xf = x.reshape(b * s, d)
    outs = []
    for bi in range(b):
        xb = xf if b == 1 else lax.slice_in_dim(xf, bi * s, (bi + 1) * s, axis=0)
        for l in range(depth):
            xb = _layer(xb, c[bi:bi + 1], w_ada[l], b_ada[l], g_pre[l], g_post[l], w_in[l], pool_w[l],
                        pool_scale[l], gk_up_fwd[l], gk_bias_fwd[l], gk_up_bwd[l], gk_bias_bwd[l],
                        gla_norm_g[l], w_proj_pool[l], w_proj_gla[l], w_out[l])
        outs.append(xb)
    out = outs[0] if b == 1 else jnp.concatenate(outs, axis=0)
    return out.reshape(b, s, d)
```

```python
import functools
import math

import jax
import jax.numpy as jnp
from jax import lax
from jax.experimental import pallas as pl
from jax.experimental.pallas import tpu as pltpu

D_MODEL = 1024
POOL_GROUPS = 4
POOL_GROUP_WIDTH = 256
POOL_WINDOWS = (2, 4, 8, 16)
GLA_HEADS = 4
GLA_KEY_WIDTH = 512
GLA_HEAD_K = 128
GLA_HEAD_V = 256
GLA_GATE_RANK = 16
GLA_GATE_NORMALIZER = 16.0
GATE_SCALE = 1.0 / GLA_GATE_NORMALIZER
assert math.frexp(GLA_GATE_NORMALIZER)[0] == 0.5
GLA_CHUNK = 64
GLA_PAIR = 2 * GLA_CHUNK
RMS_EPS = 1e-6
IN_SPLITS = (1024, 1024, 512, 512, 1024, 1024, 16, 16, 1024, 1024)
IN_WIDTH = sum(IN_SPLITS)

LANES = 128
SUB_TILE = 256
PREP_SUBTILES = 4
MAIN_TILE = 512
MAIN_SUBTILES = 1
CUMSUM_BLOCK = 256
HALO = 16
POOL_ROW_BLOCK = 128
VMEM_LIMIT_BYTES = 60 * 1024 * 1024

_OFFS = [sum(IN_SPLITS[:j]) for j in range(len(IN_SPLITS) + 1)]
(PIN_ROWS, PGATE_ROWS, Q_ROWS, K_ROWS, V_ROWS, GGATE_ROWS) = (slice(_OFFS[j], _OFFS[j + 1]) for j in range(6))
LR_ROWS = slice(_OFFS[6], _OFFS[8])
BGP_ROWS = slice(_OFFS[8], _OFFS[9])
BGG_ROWS = slice(_OFFS[9], _OFFS[10])
LR_WIDTH = 2 * GLA_GATE_RANK
WT_SLAB = 480

F32 = jnp.float32
BF16 = jnp.bfloat16


def _dot(a, b):
    return jnp.dot(a, b, preferred_element_type=F32)


def _dot_nt(a, b):
    return lax.dot_general(a, b, (((1,), (1,)), ((), ())), preferred_element_type=F32)


def _dot_tn(a, b):
    return lax.dot_general(a, b, (((0,), (0,)), ((), ())), preferred_element_type=F32)


def _sigmoid(x):
    return 0.5 * jnp.tanh(0.5 * x) + 0.5


def _silu(x):
    return x * _sigmoid(x)


def _log_sigmoid(x):
    return jnp.minimum(x, 0.0) - jnp.log(1.0 + jnp.exp(-jnp.abs(x)))


def _norm_mod(xv, g_pre, scale, shift):
    ms = jnp.mean(xv * xv, axis=-1, keepdims=True)
    gain = g_pre * (1.0 + scale)
    return (xv * lax.rsqrt(ms + RMS_EPS)) * gain + shift


def _split_hi_lo(a):
    hi = a.astype(BF16)
    lo = (a - hi.astype(F32)).astype(BF16)
    return hi, lo


def _run_round_robin(gens):
    live = list(gens)
    while live:
        for g in list(live):
            try:
                next(g)
            except StopIteration:
                live.remove(g)


def _ada_kernel(c_ref, w_ref, b_ref, o_ref):
    s = _silu(c_ref[...])
    o_ref[...] = jnp.sum(w_ref[...] * s, axis=0, keepdims=True) + b_ref[...]


def _ada_call(c_col, w_ada, b_ada):
    return pl.pallas_call(
        _ada_kernel,
        grid=(3,),
        in_specs=[
            pl.BlockSpec((D_MODEL, 1), lambda j: (0, 0)),
            pl.BlockSpec((D_MODEL, D_MODEL), lambda j: (0, j)),
            pl.BlockSpec((1, D_MODEL), lambda j: (0, j)),
        ],
        out_specs=pl.BlockSpec((None, 1, D_MODEL), lambda j: (j, 0, 0)),
        out_shape=jax.ShapeDtypeStruct((3, 1, D_MODEL), F32),
        compiler_params=pltpu.CompilerParams(dimension_semantics=("arbitrary",)),
        name="ada_call",
    )(c_col, w_ada, b_ada)


def _prep_stages(sub, refs):
    (gkcat_ref, gbias_ref, sb_ref, k_ref, v_ref, lr_ref, labhi_ref, lablo_ref, st_scr) = refs
    t = SUB_TILE
    rows = slice(sub * t, (sub + 1) * t)
    k = k_ref[rows, :]
    v_bf = v_ref[rows, :]

    gk_bwd = gkcat_ref[:, GLA_KEY_WIDTH:]
    la = _log_sigmoid(_dot(lr_ref[rows, :], gk_bwd) + gbias_ref[...])
    la_hi, la_lo = _split_hi_lo(la)
    labhi_ref[rows, :] = la_hi
    lablo_ref[rows, :] = la_lo
    yield

    row = lax.broadcasted_iota(jnp.int32, (t, t), 0)
    col = lax.broadcasted_iota(jnp.int32, (t, t), 1)
    upper = jnp.where(col >= row, GATE_SCALE, 0.0).astype(BF16)
    rc = _dot(upper, la_hi) + _dot(upper, la_lo)
    yield

    tot = rc[0:1, :]
    k_end = (k * jnp.exp(tot - rc)).astype(BF16)
    decay = jnp.exp(tot)
    kvs = []
    for hd in range(GLA_HEADS):
        ks = slice(hd * GLA_HEAD_K, (hd + 1) * GLA_HEAD_K)
        vs = slice(hd * GLA_HEAD_V, (hd + 1) * GLA_HEAD_V)
        kvs.append(_dot_tn(v_bf[:, vs], k_end[:, ks]))
    yield

    for hd in range(GLA_HEADS):
        ks = slice(hd * GLA_HEAD_K, (hd + 1) * GLA_HEAD_K)
        st = st_scr[hd]
        sb_ref[sub, hd] = st
        st_scr[hd] = st * decay[:, ks] + kvs[hd]
    yield


def _prep_kernel(x_ref, mod_ref, gpre_ref, gkf_ref, gkb_ref, gbias_ref, wk_ref, wv_ref, wlr_ref,
                 win_ref, wpool_ref, wpp_ref, wpgla_ref, wout_ref,
                 sb_ref, k_ref, v_ref, h_ref, lr_ref, labhi_ref, lablo_ref, gkcat_ref,
                 winbf_ref, wpoolbf_ref, wppbf_ref, wpglabf_ref, woutbf_ref,
                 st_scr, wk_scr, wv_scr, wlr_scr):
    @pl.when(pl.program_id(0) == 0)
    def _():
        st_scr[...] = jnp.zeros_like(st_scr)
        wk_scr[...] = wk_ref[...].astype(BF16)
        wv_scr[...] = wv_ref[...].astype(BF16)
        wlr_scr[...] = wlr_ref[...].astype(BF16)
        r, w = GLA_GATE_RANK, GLA_KEY_WIDTH
        gkcat_ref[...] = jnp.zeros(gkcat_ref.shape, BF16)
        gkcat_ref[0:r, 0:w] = gkf_ref[...].astype(BF16)
        gkcat_ref[r:2 * r, w:2 * w] = gkb_ref[...].astype(BF16)

    winbf_ref[...] = win_ref[...].astype(BF16)
    wpoolbf_ref[...] = wpool_ref[...].astype(BF16)
    wppbf_ref[...] = wpp_ref[...].astype(BF16)
    wpglabf_ref[...] = wpgla_ref[...].astype(BF16)
    woutbf_ref[...] = wout_ref[...].astype(BF16)

    h = _norm_mod(x_ref[...], gpre_ref[...], mod_ref[1], mod_ref[0]).astype(BF16)
    h_ref[...] = h
    k_ref[...] = _dot_nt(h, wk_scr[...])
    lr_ref[...] = _dot_nt(h, wlr_scr[...]).astype(BF16)
    v_ref[...] = _dot_nt(h, wv_scr[...]).astype(BF16)

    refs = (gkcat_ref, gbias_ref, sb_ref, k_ref, v_ref, lr_ref, labhi_ref, lablo_ref, st_scr)
    gens = [_prep_stages(sub, refs) for sub in reversed(range(PREP_SUBTILES))]
    _run_round_robin(gens)


def _prep_call(x2, mod, g_pre, gk_f, gk_b, gbias_b, w_in_t, pool_w2, w_proj_pool, w_proj_gla, w_out):
    s = x2.shape[0]
    tb = PREP_SUBTILES * SUB_TILE
    nb = s // tb
    slab = D_MODEL // nb
    n_wt = IN_WIDTH // WT_SLAB
    const = lambda i: (0, 0)
    rev = lambda i: (nb - 1 - i, 0)
    fwd = lambda i: (i, 0)
    wt_slab = lambda i: (jnp.minimum(i, n_wt - 1), 0)
    one = pl.Buffered(1)

    def rows_of(rs):
        n = rs.stop - rs.start
        return pl.BlockSpec((n, D_MODEL), lambda i: (rs.start // n, 0), pipeline_mode=one)

    return pl.pallas_call(
        _prep_kernel,
        grid=(nb,),
        in_specs=[
            pl.BlockSpec((tb, D_MODEL), rev),
            pl.BlockSpec((3, 1, D_MODEL), lambda i: (0, 0, 0)),
            pl.BlockSpec((1, D_MODEL), const),
            pl.BlockSpec(gk_f.shape, const),
            pl.BlockSpec(gk_b.shape, const),
            pl.BlockSpec(gbias_b.shape, const),
            rows_of(K_ROWS),
            rows_of(V_ROWS),
            rows_of(LR_ROWS),
            pl.BlockSpec((WT_SLAB, D_MODEL), wt_slab),
            pl.BlockSpec((slab, POOL_GROUP_WIDTH), fwd),
            pl.BlockSpec((slab, D_MODEL), fwd),
            pl.BlockSpec((slab, D_MODEL), fwd),
            pl.BlockSpec((slab, D_MODEL), fwd),
        ],
        out_specs=[
            pl.BlockSpec((PREP_SUBTILES, GLA_HEADS, GLA_HEAD_V, GLA_HEAD_K), lambda i: (nb - 1 - i, 0, 0, 0)),
            pl.BlockSpec((tb, GLA_KEY_WIDTH), rev),
            pl.BlockSpec((tb, D_MODEL), rev),
            pl.BlockSpec((tb, D_MODEL), rev),
            pl.BlockSpec((tb, LR_WIDTH), rev),
            pl.BlockSpec((tb, GLA_KEY_WIDTH), rev),
            pl.BlockSpec((tb, GLA_KEY_WIDTH), rev),
            pl.BlockSpec((LR_WIDTH, 2 * GLA_KEY_WIDTH), const),
            pl.BlockSpec((WT_SLAB, D_MODEL), wt_slab),
            pl.BlockSpec((slab, POOL_GROUP_WIDTH), fwd),
            pl.BlockSpec((slab, D_MODEL), fwd),
            pl.BlockSpec((slab, D_MODEL), fwd),
            pl.BlockSpec((slab, D_MODEL), fwd),
        ],
        out_shape=[
            jax.ShapeDtypeStruct((s // SUB_TILE, GLA_HEADS, GLA_HEAD_V, GLA_HEAD_K), F32),
            jax.ShapeDtypeStruct((s, GLA_KEY_WIDTH), F32),
            jax.ShapeDtypeStruct((s, D_MODEL), BF16),
            jax.ShapeDtypeStruct((s, D_MODEL), BF16),
            jax.ShapeDtypeStruct((s, LR_WIDTH), BF16),
            jax.ShapeDtypeStruct((s, GLA_KEY_WIDTH), BF16),
            jax.ShapeDtypeStruct((s, GLA_KEY_WIDTH), BF16),
            jax.ShapeDtypeStruct((LR_WIDTH, 2 * GLA_KEY_WIDTH), BF16),
            jax.ShapeDtypeStruct((IN_WIDTH, D_MODEL), BF16),
            jax.ShapeDtypeStruct(pool_w2.shape, BF16),
            jax.ShapeDtypeStruct((D_MODEL, D_MODEL), BF16),
            jax.ShapeDtypeStruct((D_MODEL, D_MODEL), BF16),
            jax.ShapeDtypeStruct((D_MODEL, D_MODEL), BF16),
        ],
        scratch_shapes=[
            pltpu.VMEM((GLA_HEADS, GLA_HEAD_V, GLA_HEAD_K), F32),
            pltpu.VMEM((GLA_KEY_WIDTH, D_MODEL), BF16),
            pltpu.VMEM((D_MODEL, D_MODEL), BF16),
            pltpu.VMEM((LR_WIDTH, D_MODEL), BF16),
        ],
        compiler_params=pltpu.CompilerParams(dimension_semantics=("arbitrary",),
                                             vmem_limit_bytes=VMEM_LIMIT_BYTES),
        name="prep_call",
    )(x2, mod, g_pre, gk_f, gk_b, gbias_b, w_in_t, w_in_t, w_in_t, w_in_t, pool_w2, w_proj_pool, w_proj_gla, w_out)


def _subtile_stages(seq_len, sub, refs):
    (x_ref, mod_ref, gpost_ref, k_ref, v_ref, lr_ref, labhi_ref, lablo_ref,
     wt_ref,
     poolw_ref, pscale_ref, gkcat_ref, gbf_ref, glag_ref,
     wpp_ref, wpgla_ref, wout_ref, sb_ref, o_ref,
     sf_scr, hext_scr, q_scr, cf_scr, rb_scr, qs_scr, qd_scr, ke_scr, kif_scr, kib_scr,
     kv_scr, scat_scr, sc_scr, og_scr) = refs
    t = MAIN_TILE
    r0 = sub * t
    rows = slice(r0, r0 + t)
    tile_row0 = pl.program_id(0) * (MAIN_SUBTILES * t) + r0
    sb_idx = (sub + 1) * (MAIN_TILE // SUB_TILE) - 1
    n_pairs = t // GLA_PAIR
    hk, hv = GLA_HEAD_K, GLA_HEAD_V
    gate = mod_ref[2]

    h_ext = hext_scr[r0:r0 + t + 2 * HALO, :]
    h = hext_scr[r0 + HALO:r0 + HALO + t, :]

    la_pre = _dot(lr_ref[rows, :], gkcat_ref[:, 0:GLA_KEY_WIDTH]) + gbf_ref[...]
    q_scr[rows, :] = _dot_nt(h, wt_ref[Q_ROWS, :]) * (GLA_HEAD_K ** -0.5)
    te = t + 2 * HALO
    g_row = tile_row0 - HALO + lax.broadcasted_iota(jnp.int32, (te, 1), 0)
    valid = jnp.logical_and(g_row >= 0, g_row < seq_len)
    u_ext = jnp.where(valid, _dot_nt(h_ext, wt_ref[PIN_ROWS, :]), 0.0)
    yield

    la = _log_sigmoid(la_pre)
    p_gate = _dot_nt(h, wt_ref[PGATE_ROWS, :])
    gla_gate = _dot_nt(h, wt_ref[GGATE_ROWS, :])
    u_ext_bf = u_ext.astype(BF16)
    u = u_ext[HALO:HALO + t, :]
    laf_hi, laf_lo = _split_hi_lo(la)
    lab_hi, lab_lo = labhi_ref[rows, :], lablo_ref[rows, :]
    yield

    cb = CUMSUM_BLOCK
    row = lax.broadcasted_iota(jnp.int32, (cb, cb), 0)
    col = lax.broadcasted_iota(jnp.int32, (cb, cb), 1)
    same_chunk = (row // GLA_CHUNK) == (col // GLA_CHUNK)
    lower = jnp.where(jnp.logical_and(same_chunk, col <= row), GATE_SCALE, 0.0).astype(BF16)
    upper = jnp.where(jnp.logical_and(same_chunk, col >= row), GATE_SCALE, 0.0).astype(BF16)
    for b in range(t // cb):
        rb_rows = slice(b * cb, (b + 1) * cb)
        dst = slice(r0 + b * cb, r0 + (b + 1) * cb)
        cf_scr[dst, :] = _dot(lower, laf_hi[rb_rows]) + _dot(lower, laf_lo[rb_rows])
        rb_scr[dst, :] = _dot(upper, lab_hi[rb_rows]) + _dot(upper, lab_lo[rb_rows])
    bg_pool = _sigmoid(_dot_nt(h, wt_ref[BGP_ROWS, :]))
    bg_gla = _sigmoid(_dot_nt(h, wt_ref[BGG_ROWS, :]))
    yield

    pos = tile_row0 + lax.broadcasted_iota(jnp.int32, (t, 1), 0)
    rb_ = POOL_ROW_BLOCK
    kb_ = POOL_ROW_BLOCK + 2 * HALO
    band_r = lax.broadcasted_iota(jnp.int32, (rb_, kb_), 0)
    band_c = lax.broadcasted_iota(jnp.int32, (rb_, kb_), 1)
    band_d = band_c - HALO - band_r
    a_parts = []
    for gi, w in enumerate(POOL_WINDOWS):
        hw = w // 2
        cs = slice(gi * POOL_GROUP_WIDTH, (gi + 1) * POOL_GROUP_WIDTH)
        ws = slice(gi * POOL_GROUP_WIDTH, (gi + 1) * POOL_GROUP_WIDTH)
        band = jnp.logical_and(band_d >= -hw, band_d < hw).astype(BF16)
        wsum = jnp.concatenate(
            [_dot(band, u_ext_bf[b * rb_:b * rb_ + kb_, cs]) for b in range(t // rb_)], axis=0)
        count = (jnp.minimum(pos + hw, seq_len) - jnp.maximum(pos - hw, 0)).astype(F32)
        pooled = wsum / count - u[:, cs]
        a_parts.append(_dot(pooled.astype(BF16), poolw_ref[ws, :]))
    a = jnp.concatenate(a_parts, axis=1) * pscale_ref[...]
    z_pool = (a * _silu(p_gate)).astype(BF16)

    def put_heads(dst, rs, fwd, bwd):
        for hd in range(GLA_HEADS):
            ks = slice(hd * hk, (hd + 1) * hk)
            dst[rs, 2 * hd * hk:(2 * hd + 1) * hk] = fwd[:, ks].astype(BF16)
            dst[rs, (2 * hd + 1) * hk:(2 * hd + 2) * hk] = bwd[:, ks].astype(BF16)

    dec_f, dec_b = [], []
    for p in range(n_pairs):
        c0 = r0 + p * GLA_PAIR
        c1 = c0 + GLA_CHUNK
        ra = slice(c0, c1)
        rb = slice(c1, c1 + GLA_CHUNK)
        q0, q1 = q_scr[ra, :], q_scr[rb, :]
        k0, k1 = k_ref[ra, :], k_ref[rb, :]
        cum0, cum1 = cf_scr[ra, :], cf_scr[rb, :]
        tot0 = cf_scr[c1 - 1:c1, :]
        tot1 = cf_scr[c1 + GLA_CHUNK - 1:c1 + GLA_CHUNK, :]
        rc0, rc1 = rb_scr[ra, :], rb_scr[rb, :]
        tb0 = rb_scr[c0:c0 + 1, :]
        tb1 = rb_scr[c1:c1 + 1, :]
        e_tot0, e_tot1 = jnp.exp(tot0), jnp.exp(tot1)
        e_tb0, e_tb1 = jnp.exp(tb0), jnp.exp(tb1)

        a0 = cum0 - tot0
        ks_f0 = k0 * jnp.exp(-a0)
        qs_f1 = q1 * jnp.exp(cum1)
        qs_b0 = q0 * jnp.exp(rc0)
        put_heads(qs_scr, ra, q0 * jnp.exp(a0), qs_b0)
        put_heads(qd_scr, ra, q0 * jnp.exp(cum0), qs_b0 * e_tb1)
        kif_scr[ra, :] = ks_f0.astype(BF16)
        kif_scr[rb, :] = (k1 * jnp.exp(-cum1)).astype(BF16)
        a1 = rc1 - tb1
        ks_b1 = k1 * jnp.exp(-a1)
        put_heads(qs_scr, rb, qs_f1, q1 * jnp.exp(a1))
        put_heads(qd_scr, rb, qs_f1 * e_tot0, q1 * jnp.exp(rc1))
        kib_scr[ra, :] = (k0 * jnp.exp(-rc0)).astype(BF16)
        kib_scr[rb, :] = ks_b1.astype(BF16)
        put_heads(ke_scr, ra, ks_f0 * e_tot1, k0 * jnp.exp(tb0 - rc0))
        put_heads(ke_scr, rb, k1 * jnp.exp(tot1 - cum1), ks_b1 * e_tb0)
        dec_f.append(e_tot0 * e_tot1)
        dec_b.append(e_tb0 * e_tb1)
    yield

    merged = bg_pool * _dot(z_pool, wpp_ref[...])
    for hd in range(GLA_HEADS):
        vs = slice(hd * hv, (hd + 1) * hv)
        cat = slice(2 * hd * hk, (2 * hd + 2) * hk)
        for p in range(n_pairs):
            rs = slice(r0 + p * GLA_PAIR, r0 + (p + 1) * GLA_PAIR)
            kv_scr[sub, hd, p] = _dot_tn(v_ref[rs, vs], ke_scr[rs, cat])
    yield

    prow = lax.broadcasted_iota(jnp.int32, (GLA_PAIR, GLA_PAIR), 0)
    pcol = lax.broadcasted_iota(jnp.int32, (GLA_PAIR, GLA_PAIR), 1)
    mask_f = pcol <= prow
    for p in range(n_pairs):
        rs = slice(r0 + p * GLA_PAIR, r0 + (p + 1) * GLA_PAIR)
        for hd in range(GLA_HEADS):
            ks = slice(hd * hk, (hd + 1) * hk)
            sc_f = _dot_nt(qs_scr[rs, 2 * hd * hk:(2 * hd + 1) * hk], kif_scr[rs, ks])
            sc_b = _dot_nt(qs_scr[rs, (2 * hd + 1) * hk:(2 * hd + 2) * hk], kib_scr[rs, ks])
            sc_scr[sub, p, hd] = jnp.where(mask_f, sc_f, sc_b).astype(BF16)
    for hd in range(GLA_HEADS):
        ks = slice(hd * hk, (hd + 1) * hk)
        sf = sf_scr[hd]
        for p in range(n_pairs):
            scat_scr[sub, p, hd, 0:hk, :] = sf.T.astype(BF16)
            sf = sf * dec_f[p][:, ks] + kv_scr[sub, hd, p, :, 0:hk]
        sf_scr[hd] = sf
        sb = sb_ref[sb_idx, hd]
        for p in reversed(range(n_pairs)):
            scat_scr[sub, p, hd, hk:2 * hk, :] = sb.T.astype(BF16)
            if p > 0:
                sb = sb * dec_b[p][:, ks] + kv_scr[sub, hd, p, :, hk:2 * hk]
    yield

    for p in range(n_pairs):
        rs = slice(r0 + p * GLA_PAIR, r0 + (p + 1) * GLA_PAIR)
        for hd in range(GLA_HEADS):
            vs = slice(hd * hv, (hd + 1) * hv)
            cat = slice(2 * hd * hk, (2 * hd + 2) * hk)
            og_scr[rs, vs] = (_dot(sc_scr[sub, p, hd], v_ref[rs, vs])
                              + _dot(qd_scr[rs, cat], scat_scr[sub, p, hd]))
    yield

    glag = glag_ref[...]
    z_parts = []
    for hd in range(GLA_HEADS):
        vs = slice(hd * hv, (hd + 1) * hv)
        oh = og_scr[rows, vs]
        ms = jnp.mean(oh * oh, axis=-1, keepdims=True)
        z_parts.append((oh * lax.rsqrt(ms + RMS_EPS)) * glag)
    o_n = jnp.concatenate(z_parts, axis=1)
    y_gla = _dot((o_n * _silu(gla_gate)).astype(BF16), wpgla_ref[...])
    merged = merged + bg_gla * y_gla
    yield

    out = _dot(merged.astype(BF16), wout_ref[...])
    yield

    ms = jnp.mean(out * out, axis=-1, keepdims=True)
    out_n = (out * lax.rsqrt(ms + RMS_EPS)) * gpost_ref[...]
    o_ref[rows, :] = x_ref[rows, :] + gate * out_n
    yield


def _main_kernel(seq_len,
                 x_ref, h_ref, hp_ref, hn_ref, mod_ref, gpost_ref, k_ref, v_ref, lr_ref, labhi_ref, lablo_ref,
                 wt_ref,
                 poolw_ref, pscale_ref, gkcat_ref, gbf_ref, glag_ref,
                 wpp_ref, wpgla_ref, wout_ref, sb_ref,
                 o_ref,
                 sf_scr, hext_scr, q_scr, cf_scr, rb_scr, qs_scr, qd_scr, ke_scr, kif_scr, kib_scr,
                 kv_scr, scat_scr, sc_scr, og_scr):
    tb = MAIN_SUBTILES * MAIN_TILE

    @pl.when(pl.program_id(0) == 0)
    def _():
        sf_scr[...] = jnp.zeros_like(sf_scr)

    hext_scr[0:HALO, :] = hp_ref[...]
    hext_scr[HALO:HALO + tb, :] = h_ref[...]
    hext_scr[HALO + tb:HALO + tb + HALO, :] = hn_ref[...]

    refs = (x_ref, mod_ref, gpost_ref, k_ref, v_ref, lr_ref, labhi_ref, lablo_ref,
            wt_ref,
            poolw_ref, pscale_ref, gkcat_ref, gbf_ref, glag_ref,
            wpp_ref, wpgla_ref, wout_ref, sb_ref, o_ref,
            sf_scr, hext_scr, q_scr, cf_scr, rb_scr, qs_scr, qd_scr, ke_scr, kif_scr, kib_scr,
            kv_scr, scat_scr, sc_scr, og_scr)
    gens = [_subtile_stages(seq_len, sub, refs) for sub in range(MAIN_SUBTILES)]
    _run_round_robin(gens)


def _main_call(x2, h_all, mod, g_post, k_all, v_all, lr_all, labhi_all, lablo_all, w_bf, small,
               wpool_bf, wpp_bf, wpgla_bf, wout_bf, sb):
    s = x2.shape[0]
    tb = MAIN_SUBTILES * MAIN_TILE
    nt = s // tb
    hb = tb // HALO
    n_hb = s // HALO
    n_pairs = MAIN_TILE // GLA_PAIR
    const2 = lambda i: (0, 0)
    tile = lambda i: (i, 0)
    one = pl.Buffered(1)
    pool_scale, gk_cat, gbias_f, gla_norm_g = small

    def resident(arr):
        return pl.BlockSpec(arr.shape, const2, pipeline_mode=one)

    in_specs = [
        pl.BlockSpec((tb, D_MODEL), tile),
        pl.BlockSpec((tb, D_MODEL), tile),
        pl.BlockSpec((HALO, D_MODEL), lambda i: (jnp.maximum(i * hb - 1, 0), 0)),
        pl.BlockSpec((HALO, D_MODEL), lambda i: (jnp.minimum((i + 1) * hb, n_hb - 1), 0)),
        pl.BlockSpec((3, 1, D_MODEL), lambda i: (0, 0, 0)),
        pl.BlockSpec((1, D_MODEL), const2),
        pl.BlockSpec((tb, GLA_KEY_WIDTH), tile),
        pl.BlockSpec((tb, D_MODEL), tile),
        pl.BlockSpec((tb, LR_WIDTH), tile),
        pl.BlockSpec((tb, GLA_KEY_WIDTH), tile),
        pl.BlockSpec((tb, GLA_KEY_WIDTH), tile),
        resident(w_bf),
        resident(wpool_bf), resident(pool_scale), resident(gk_cat), resident(gbias_f),
        resident(gla_norm_g),
        resident(wpp_bf), resident(wpgla_bf), resident(wout_bf),
        pl.BlockSpec((tb // SUB_TILE, GLA_HEADS, GLA_HEAD_V, GLA_HEAD_K), lambda i: (i, 0, 0, 0)),
    ]
    scratch = [
        pltpu.VMEM((GLA_HEADS, GLA_HEAD_V, GLA_HEAD_K), F32),
        pltpu.VMEM((tb + 2 * HALO, D_MODEL), BF16),
        pltpu.VMEM((tb, GLA_KEY_WIDTH), F32),
        pltpu.VMEM((tb, GLA_KEY_WIDTH), F32),
        pltpu.VMEM((tb, GLA_KEY_WIDTH), F32),
        pltpu.VMEM((tb, 2 * GLA_KEY_WIDTH), BF16),
        pltpu.VMEM((tb, 2 * GLA_KEY_WIDTH), BF16),
        pltpu.VMEM((tb, 2 * GLA_KEY_WIDTH), BF16),
        pltpu.VMEM((tb, GLA_KEY_WIDTH), BF16),
        pltpu.VMEM((tb, GLA_KEY_WIDTH), BF16),
        pltpu.VMEM((MAIN_SUBTILES, GLA_HEADS, n_pairs, GLA_HEAD_V, 2 * GLA_HEAD_K), F32),
        pltpu.VMEM((MAIN_SUBTILES, n_pairs, GLA_HEADS, 2 * GLA_HEAD_K, GLA_HEAD_V), BF16),
        pltpu.VMEM((MAIN_SUBTILES, n_pairs, GLA_HEADS, GLA_PAIR, GLA_PAIR), BF16),
        pltpu.VMEM((tb, D_MODEL), F32),
    ]
    return pl.pallas_call(
        functools.partial(_main_kernel, s),
        grid=(nt,),
        in_specs=in_specs,
        out_specs=pl.BlockSpec((tb, D_MODEL), tile),
        out_shape=jax.ShapeDtypeStruct((s, D_MODEL), F32),
        scratch_shapes=scratch,
        compiler_params=pltpu.CompilerParams(dimension_semantics=("arbitrary",),
                                             vmem_limit_bytes=VMEM_LIMIT_BYTES),
        name="main_call",
    )(x2, h_all, h_all, h_all, mod, g_post, k_all, v_all, lr_all, labhi_all, lablo_all,
      w_bf,
      wpool_bf, pool_scale, gk_cat, gbias_f, gla_norm_g, wpp_bf, wpgla_bf, wout_bf, sb)


def _layer(x2, c, w_ada, b_ada, g_pre, g_post, w_in, pool_w, pool_scale, gk_up_fwd, gk_bias_fwd,
           gk_up_bwd, gk_bias_bwd, gla_norm_g, w_proj_pool, w_proj_gla, w_out):
    mod = _ada_call(c.reshape(D_MODEL, 1), w_ada, b_ada.reshape(1, -1))
    gbias_f = gk_bias_fwd.reshape(1, -1)
    gbias_b = gk_bias_bwd.reshape(1, -1)

    pool_w2 = pool_w.reshape(POOL_GROUPS * POOL_GROUP_WIDTH, POOL_GROUP_WIDTH)
    (sb, k_all, v_all, h_all, lr_all, labhi_all, lablo_all, gk_cat,
     w_bf, wpool_bf, wpp_bf, wpgla_bf, wout_bf) = _prep_call(
        x2, mod, g_pre.reshape(1, -1), gk_up_fwd, gk_up_bwd, gbias_b, w_in.T, pool_w2, w_proj_pool, w_proj_gla,
        w_out)

    small = (pool_scale.reshape(1, -1), gk_cat, gbias_f, gla_norm_g.reshape(1, -1))
    return _main_call(x2, h_all, mod, g_post.reshape(1, -1), k_all, v_all, lr_all, labhi_all, lablo_all, w_bf,
                      small, wpool_bf, wpp_bf, wpgla_bf, wout_bf, sb)


def kernel(x, c, w_ada, b_ada, g_pre, g_post, w_in, pool_w, pool_scale, gk_up_fwd, gk_bias_fwd, gk_up_bwd, gk_bias_bwd, gla_norm_g, w_proj_pool, w_proj_gla, w_out):
    b, s, d = x.shape
    depth = w_in.shape[0]
    xf = x.reshape(b * s, d)
    outs = []
    for bi in range(b):
        xb = xf if b == 1 else lax.slice_in_dim(xf, bi * s, (bi + 1) * s, axis=0)
        for l in range(depth):
            xb = _layer(xb, c[bi:bi + 1], w_ada[l], b_ada[l], g_pre[l], g_post[l], w_in[l], pool_w[l],
                        pool_scale[l], gk_up_fwd[l], gk_bias_fwd[l], gk_up_bwd[l], gk_bias_bwd[l],
                        gla_norm_g[l], w_proj_pool[l], w_proj_gla[l], w_out[l])
        outs.append(xb)
    out = outs[0] if b == 1 else jnp.concatenate(outs, axis=0)
    return out.reshape(b, s, d)
```

```python
import functools
import math

import jax
import jax.numpy as jnp
from jax import lax
from jax.experimental import pallas as pl
from jax.experimental.pallas import tpu as pltpu

D_MODEL = 1024
POOL_GROUPS = 4
POOL_GROUP_WIDTH = 256
POOL_WINDOWS = (2, 4, 8, 16)
GLA_HEADS = 4
GLA_KEY_WIDTH = 512
GLA_HEAD_K = 128
GLA_HEAD_V = 256
GLA_GATE_RANK = 16
GLA_GATE_NORMALIZER = 16.0
GATE_SCALE = 1.0 / GLA_GATE_NORMALIZER
assert math.frexp(GLA_GATE_NORMALIZER)[0] == 0.5
GLA_CHUNK = 64
GLA_PAIR = 2 * GLA_CHUNK
RMS_EPS = 1e-6
IN_SPLITS = (1024, 1024, 512, 512, 1024, 1024, 16, 16, 1024, 1024)
IN_WIDTH = sum(IN_SPLITS)

LANES = 128
SUB_TILE = 256
PREP_SUBTILES = 4
MAIN_TILE = 512
MAIN_SUBTILES = 1
CUMSUM_BLOCK = 256
HALO = 16
POOL_ROW_BLOCK = 128
VMEM_LIMIT_BYTES = 60 * 1024 * 1024

_OFFS = [sum(IN_SPLITS[:j]) for j in range(len(IN_SPLITS) + 1)]
(PIN_ROWS, PGATE_ROWS, Q_ROWS, K_ROWS, V_ROWS, GGATE_ROWS) = (slice(_OFFS[j], _OFFS[j + 1]) for j in range(6))
LR_ROWS = slice(_OFFS[6], _OFFS[8])
BGP_ROWS = slice(_OFFS[8], _OFFS[9])
BGG_ROWS = slice(_OFFS[9], _OFFS[10])
LR_WIDTH = 2 * GLA_GATE_RANK
WT_SLAB = 480

F32 = jnp.float32
BF16 = jnp.bfloat16


def _dot(a, b):
    return jnp.dot(a, b, preferred_element_type=F32)


def _dot_nt(a, b):
    return lax.dot_general(a, b, (((1,), (1,)), ((), ())), preferred_element_type=F32)


def _dot_tn(a, b):
    return lax.dot_general(a, b, (((0,), (0,)), ((), ())), preferred_element_type=F32)


def _sigmoid(x):
    return 0.5 * jnp.tanh(0.5 * x) + 0.5


def _silu(x):
    return x * _sigmoid(x)


def _log_sigmoid(x):
    return jnp.minimum(x, 0.0) - jnp.log(1.0 + jnp.exp(-jnp.abs(x)))


def _norm_mod(xv, g_pre, scale, shift):
    ms = jnp.mean(xv * xv, axis=-1, keepdims=True)
    gain = g_pre * (1.0 + scale)
    return (xv * lax.rsqrt(ms + RMS_EPS)) * gain + shift


def _split_hi_lo(a):
    hi = a.astype(BF16)
    lo = (a - hi.astype(F32)).astype(BF16)
    return hi, lo


def _run_round_robin(gens):
    live = list(gens)
    while live:
        for g in list(live):
            try:
                next(g)
            except StopIteration:
                live.remove(g)


def _ada_kernel(c_ref, w_ref, b_ref, o_ref):
    s = _silu(c_ref[...])
    o_ref[...] = jnp.sum(w_ref[...] * s, axis=0, keepdims=True) + b_ref[...]


def _ada_call(c_col, w_ada, b_ada):
    return pl.pallas_call(
        _ada_kernel,
        grid=(3,),
        in_specs=[
            pl.BlockSpec((D_MODEL, 1), lambda j: (0, 0)),
            pl.BlockSpec((D_MODEL, D_MODEL), lambda j: (0, j)),
            pl.BlockSpec((1, D_MODEL), lambda j: (0, j)),
        ],
        out_specs=pl.BlockSpec((None, 1, D_MODEL), lambda j: (j, 0, 0)),
        out_shape=jax.ShapeDtypeStruct((3, 1, D_MODEL), F32),
        compiler_params=pltpu.CompilerParams(dimension_semantics=("arbitrary",)),
        name="ada_call",
    )(c_col, w_ada, b_ada)


def _prep_stages(sub, refs):
    (gkcat_ref, gbias_ref, sb_ref, k_ref, v_ref, lr_ref, st_scr) = refs
    t = SUB_TILE
    rows = slice(sub * t, (sub + 1) * t)
    k = k_ref[rows, :]
    v_bf = v_ref[rows, :]

    gk_bwd = gkcat_ref[:, GLA_KEY_WIDTH:]
    la = _log_sigmoid(_dot(lr_ref[rows, :], gk_bwd) + gbias_ref[...])
    la_hi, la_lo = _split_hi_lo(la)
    yield

    row = lax.broadcasted_iota(jnp.int32, (t, t), 0)
    col = lax.broadcasted_iota(jnp.int32, (t, t), 1)
    upper = jnp.where(col >= row, GATE_SCALE, 0.0).astype(BF16)
    rc = _dot(upper, la_hi) + _dot(upper, la_lo)
    yield

    tot = rc[0:1, :]
    k_end = (k * jnp.exp(tot - rc)).astype(BF16)
    decay = jnp.exp(tot)
    kvs = []
    for hd in range(GLA_HEADS):
        ks = slice(hd * GLA_HEAD_K, (hd + 1) * GLA_HEAD_K)
        vs = slice(hd * GLA_HEAD_V, (hd + 1) * GLA_HEAD_V)
        kvs.append(_dot_tn(v_bf[:, vs], k_end[:, ks]))
    yield

    per_main = MAIN_TILE // SUB_TILE
    for hd in range(GLA_HEADS):
        ks = slice(hd * GLA_HEAD_K, (hd + 1) * GLA_HEAD_K)
        st = st_scr[hd]
        if (sub + 1) % per_main == 0:
            sb_ref[sub // per_main, hd] = st
        st_scr[hd] = st * decay[:, ks] + kvs[hd]
    yield


def _prep_kernel(x_ref, mod_ref, gpre_ref, gkf_ref, gkb_ref, gbias_ref, wk_ref, wv_ref, wlr_ref,
                 win_ref, wpool_ref, wpp_ref, wpgla_ref, wout_ref,
                 sb_ref, k_ref, v_ref, h_ref, lr_ref, gkcat_ref,
                 winbf_ref, wpoolbf_ref, wppbf_ref, wpglabf_ref, woutbf_ref,
                 st_scr, wk_scr, wv_scr, wlr_scr):
    @pl.when(pl.program_id(0) == 0)
    def _():
        st_scr[...] = jnp.zeros_like(st_scr)
        wk_scr[...] = wk_ref[...].astype(BF16)
        wv_scr[...] = wv_ref[...].astype(BF16)
        wlr_scr[...] = wlr_ref[...].astype(BF16)
        r, w = GLA_GATE_RANK, GLA_KEY_WIDTH
        gkcat_ref[...] = jnp.zeros(gkcat_ref.shape, BF16)
        gkcat_ref[0:r, 0:w] = gkf_ref[...].astype(BF16)
        gkcat_ref[r:2 * r, w:2 * w] = gkb_ref[...].astype(BF16)

    winbf_ref[...] = win_ref[...].astype(BF16)
    wpoolbf_ref[...] = wpool_ref[...].astype(BF16)
    wppbf_ref[...] = wpp_ref[...].astype(BF16)
    wpglabf_ref[...] = wpgla_ref[...].astype(BF16)
    woutbf_ref[...] = wout_ref[...].astype(BF16)

    h = _norm_mod(x_ref[...], gpre_ref[...], mod_ref[1], mod_ref[0]).astype(BF16)
    h_ref[...] = h
    k_ref[...] = _dot_nt(h, wk_scr[...])
    lr_ref[...] = _dot_nt(h, wlr_scr[...]).astype(BF16)
    v_ref[...] = _dot_nt(h, wv_scr[...]).astype(BF16)

    refs = (gkcat_ref, gbias_ref, sb_ref, k_ref, v_ref, lr_ref, st_scr)
    gens = [_prep_stages(sub, refs) for sub in reversed(range(PREP_SUBTILES))]
    _run_round_robin(gens)


def _prep_call(x2, mod, g_pre, gk_f, gk_b, gbias_b, w_in_t, pool_w2, w_proj_pool, w_proj_gla, w_out):
    s = x2.shape[0]
    tb = PREP_SUBTILES * SUB_TILE
    nb = s // tb
    slab = D_MODEL // nb
    n_wt = IN_WIDTH // WT_SLAB
    const = lambda i: (0, 0)
    rev = lambda i: (nb - 1 - i, 0)
    fwd = lambda i: (i, 0)
    wt_slab = lambda i: (jnp.minimum(i, n_wt - 1), 0)
    one = pl.Buffered(1)

    def rows_of(rs):
        n = rs.stop - rs.start
        return pl.BlockSpec((n, D_MODEL), lambda i: (rs.start // n, 0), pipeline_mode=one)

    return pl.pallas_call(
        _prep_kernel,
        grid=(nb,),
        in_specs=[
            pl.BlockSpec((tb, D_MODEL), rev),
            pl.BlockSpec((3, 1, D_MODEL), lambda i: (0, 0, 0)),
            pl.BlockSpec((1, D_MODEL), const),
            pl.BlockSpec(gk_f.shape, const),
            pl.BlockSpec(gk_b.shape, const),
            pl.BlockSpec(gbias_b.shape, const),
            rows_of(K_ROWS),
            rows_of(V_ROWS),
            rows_of(LR_ROWS),
            pl.BlockSpec((WT_SLAB, D_MODEL), wt_slab),
            pl.BlockSpec((slab, POOL_GROUP_WIDTH), fwd),
            pl.BlockSpec((slab, D_MODEL), fwd),
            pl.BlockSpec((slab, D_MODEL), fwd),
            pl.BlockSpec((slab, D_MODEL), fwd),
        ],
        out_specs=[
            pl.BlockSpec((tb // MAIN_TILE, GLA_HEADS, GLA_HEAD_V, GLA_HEAD_K), lambda i: (nb - 1 - i, 0, 0, 0)),
            pl.BlockSpec((tb, GLA_KEY_WIDTH), rev),
            pl.BlockSpec((tb, D_MODEL), rev),
            pl.BlockSpec((tb, D_MODEL), rev),
            pl.BlockSpec((tb, LR_WIDTH), rev),
            pl.BlockSpec((LR_WIDTH, 2 * GLA_KEY_WIDTH), const),
            pl.BlockSpec((WT_SLAB, D_MODEL), wt_slab),
            pl.BlockSpec((slab, POOL_GROUP_WIDTH), fwd),
            pl.BlockSpec((slab, D_MODEL), fwd),
            pl.BlockSpec((slab, D_MODEL), fwd),
            pl.BlockSpec((slab, D_MODEL), fwd),
        ],
        out_shape=[
            jax.ShapeDtypeStruct((s // MAIN_TILE, GLA_HEADS, GLA_HEAD_V, GLA_HEAD_K), F32),
            jax.ShapeDtypeStruct((s, GLA_KEY_WIDTH), F32),
            jax.ShapeDtypeStruct((s, D_MODEL), BF16),
            jax.ShapeDtypeStruct((s, D_MODEL), BF16),
            jax.ShapeDtypeStruct((s, LR_WIDTH), BF16),
            jax.ShapeDtypeStruct((LR_WIDTH, 2 * GLA_KEY_WIDTH), BF16),
            jax.ShapeDtypeStruct((IN_WIDTH, D_MODEL), BF16),
            jax.ShapeDtypeStruct(pool_w2.shape, BF16),
            jax.ShapeDtypeStruct((D_MODEL, D_MODEL), BF16),
            jax.ShapeDtypeStruct((D_MODEL, D_MODEL), BF16),
            jax.ShapeDtypeStruct((D_MODEL, D_MODEL), BF16),
        ],
        scratch_shapes=[
            pltpu.VMEM((GLA_HEADS, GLA_HEAD_V, GLA_HEAD_K), F32),
            pltpu.VMEM((GLA_KEY_WIDTH, D_MODEL), BF16),
            pltpu.VMEM((D_MODEL, D_MODEL), BF16),
            pltpu.VMEM((LR_WIDTH, D_MODEL), BF16),
        ],
        compiler_params=pltpu.CompilerParams(dimension_semantics=("arbitrary",),
                                             vmem_limit_bytes=VMEM_LIMIT_BYTES),
        name="prep_call",
    )(x2, mod, g_pre, gk_f, gk_b, gbias_b, w_in_t, w_in_t, w_in_t, w_in_t, pool_w2, w_proj_pool, w_proj_gla, w_out)


def _subtile_stages(seq_len, sub, refs):
    (x_ref, mod_ref, gpost_ref, k_ref, v_ref, lr_ref,
     wt_ref,
     poolw_ref, pscale_ref, gkcat_ref, gbf_ref, gbb_ref, glag_ref,
     wpp_ref, wpgla_ref, wout_ref, sb_ref, o_ref,
     sf_scr, hext_scr, q_scr, cf_scr, rb_scr, qs_scr, qd_scr, ke_scr, kif_scr, kib_scr,
     kv_scr, scat_scr, sc_scr, og_scr) = refs
    t = MAIN_TILE
    r0 = sub * t
    rows = slice(r0, r0 + t)
    tile_row0 = pl.program_id(0) * (MAIN_SUBTILES * t) + r0
    sb_idx = sub
    n_pairs = t // GLA_PAIR
    hk, hv = GLA_HEAD_K, GLA_HEAD_V
    gate = mod_ref[2]

    h_ext = hext_scr[r0:r0 + t + 2 * HALO, :]
    h = hext_scr[r0 + HALO:r0 + HALO + t, :]

    gbias = jnp.concatenate([gbf_ref[...], gbb_ref[...]], axis=1)
    la_pre = _dot(lr_ref[rows, :], gkcat_ref[...]) + gbias
    q_scr[rows, :] = _dot_nt(h, wt_ref[Q_ROWS, :]) * (GLA_HEAD_K ** -0.5)
    te = t + 2 * HALO
    g_row = tile_row0 - HALO + lax.broadcasted_iota(jnp.int32, (te, 1), 0)
    valid = jnp.logical_and(g_row >= 0, g_row < seq_len)
    u_ext = jnp.where(valid, _dot_nt(h_ext, wt_ref[PIN_ROWS, :]), 0.0)
    yield

    la = _log_sigmoid(la_pre)
    p_gate = _dot_nt(h, wt_ref[PGATE_ROWS, :])
    gla_gate = _dot_nt(h, wt_ref[GGATE_ROWS, :])
    u_ext_bf = u_ext.astype(BF16)
    u = u_ext[HALO:HALO + t, :]
    laf_hi, laf_lo = _split_hi_lo(la[:, :GLA_KEY_WIDTH])
    lab_hi, lab_lo = _split_hi_lo(la[:, GLA_KEY_WIDTH:])
    yield

    cb = CUMSUM_BLOCK
    row = lax.broadcasted_iota(jnp.int32, (cb, cb), 0)
    col = lax.broadcasted_iota(jnp.int32, (cb, cb), 1)
    same_chunk = (row // GLA_CHUNK) == (col // GLA_CHUNK)
    lower = jnp.where(jnp.logical_and(same_chunk, col <= row), GATE_SCALE, 0.0).astype(BF16)
    upper = jnp.where(jnp.logical_and(same_chunk, col >= row), GATE_SCALE, 0.0).astype(BF16)
    for b in range(t // cb):
        rb_rows = slice(b * cb, (b + 1) * cb)
        dst = slice(r0 + b * cb, r0 + (b + 1) * cb)
        cf_scr[dst, :] = _dot(lower, laf_hi[rb_rows]) + _dot(lower, laf_lo[rb_rows])
        rb_scr[dst, :] = _dot(upper, lab_hi[rb_rows]) + _dot(upper, lab_lo[rb_rows])
    bg_pool = _sigmoid(_dot_nt(h, wt_ref[BGP_ROWS, :]))
    bg_gla = _sigmoid(_dot_nt(h, wt_ref[BGG_ROWS, :]))
    yield

    pos = tile_row0 + lax.broadcasted_iota(jnp.int32, (t, 1), 0)
    rb_ = POOL_ROW_BLOCK
    kb_ = POOL_ROW_BLOCK + 2 * HALO
    band_r = lax.broadcasted_iota(jnp.int32, (rb_, kb_), 0)
    band_c = lax.broadcasted_iota(jnp.int32, (rb_, kb_), 1)
    band_d = band_c - HALO - band_r
    a_parts = []
    for gi, w in enumerate(POOL_WINDOWS):
        hw = w // 2
        cs = slice(gi * POOL_GROUP_WIDTH, (gi + 1) * POOL_GROUP_WIDTH)
        ws = slice(gi * POOL_GROUP_WIDTH, (gi + 1) * POOL_GROUP_WIDTH)
        band = jnp.logical_and(band_d >= -hw, band_d < hw).astype(BF16)
        wsum = jnp.concatenate(
            [_dot(band, u_ext_bf[b * rb_:b * rb_ + kb_, cs]) for b in range(t // rb_)], axis=0)
        count = (jnp.minimum(pos + hw, seq_len) - jnp.maximum(pos - hw, 0)).astype(F32)
        pooled = wsum / count - u[:, cs]
        a_parts.append(_dot(pooled.astype(BF16), poolw_ref[ws, :]))
    a = jnp.concatenate(a_parts, axis=1) * pscale_ref[...]
    z_pool = (a * _silu(p_gate)).astype(BF16)

    def put_heads(dst, rs, fwd, bwd):
        for hd in range(GLA_HEADS):
            ks = slice(hd * hk, (hd + 1) * hk)
            dst[rs, 2 * hd * hk:(2 * hd + 1) * hk] = fwd[:, ks].astype(BF16)
            dst[rs, (2 * hd + 1) * hk:(2 * hd + 2) * hk] = bwd[:, ks].astype(BF16)

    dec_f, dec_b = [], []
    for p in range(n_pairs):
        c0 = r0 + p * GLA_PAIR
        c1 = c0 + GLA_CHUNK
        ra = slice(c0, c1)
        rb = slice(c1, c1 + GLA_CHUNK)
        q0, q1 = q_scr[ra, :], q_scr[rb, :]
        k0, k1 = k_ref[ra, :], k_ref[rb, :]
        cum0, cum1 = cf_scr[ra, :], cf_scr[rb, :]
        tot0 = cf_scr[c1 - 1:c1, :]
        tot1 = cf_scr[c1 + GLA_CHUNK - 1:c1 + GLA_CHUNK, :]
        rc0, rc1 = rb_scr[ra, :], rb_scr[rb, :]
        tb0 = rb_scr[c0:c0 + 1, :]
        tb1 = rb_scr[c1:c1 + 1, :]
        e_tot0, e_tot1 = jnp.exp(tot0), jnp.exp(tot1)
        e_tb0, e_tb1 = jnp.exp(tb0), jnp.exp(tb1)

        a0 = cum0 - tot0
        ks_f0 = k0 * jnp.exp(-a0)
        qs_f1 = q1 * jnp.exp(cum1)
        qs_b0 = q0 * jnp.exp(rc0)
        put_heads(qs_scr, ra, q0 * jnp.exp(a0), qs_b0)
        put_heads(qd_scr, ra, q0 * jnp.exp(cum0), qs_b0 * e_tb1)
        kif_scr[ra, :] = ks_f0.astype(BF16)
        kif_scr[rb, :] = (k1 * jnp.exp(-cum1)).astype(BF16)
        a1 = rc1 - tb1
        ks_b1 = k1 * jnp.exp(-a1)
        put_heads(qs_scr, rb, qs_f1, q1 * jnp.exp(a1))
        put_heads(qd_scr, rb, qs_f1 * e_tot0, q1 * jnp.exp(rc1))
        kib_scr[ra, :] = (k0 * jnp.exp(-rc0)).astype(BF16)
        kib_scr[rb, :] = ks_b1.astype(BF16)
        put_heads(ke_scr, ra, ks_f0 * e_tot1, k0 * jnp.exp(tb0 - rc0))
        put_heads(ke_scr, rb, k1 * jnp.exp(tot1 - cum1), ks_b1 * e_tb0)
        dec_f.append(e_tot0 * e_tot1)
        dec_b.append(e_tb0 * e_tb1)
    yield

    merged = bg_pool * _dot(z_pool, wpp_ref[...])
    for hd in range(GLA_HEADS):
        vs = slice(hd * hv, (hd + 1) * hv)
        cat = slice(2 * hd * hk, (2 * hd + 2) * hk)
        for p in range(n_pairs):
            rs = slice(r0 + p * GLA_PAIR, r0 + (p + 1) * GLA_PAIR)
            kv_scr[sub, hd, p] = _dot_tn(v_ref[rs, vs], ke_scr[rs, cat])
    yield

    prow = lax.broadcasted_iota(jnp.int32, (GLA_PAIR, GLA_PAIR), 0)
    pcol = lax.broadcasted_iota(jnp.int32, (GLA_PAIR, GLA_PAIR), 1)
    mask_f = pcol <= prow
    for p in range(n_pairs):
        rs = slice(r0 + p * GLA_PAIR, r0 + (p + 1) * GLA_PAIR)
        for hd in range(GLA_HEADS):
            ks = slice(hd * hk, (hd + 1) * hk)
            sc_f = _dot_nt(qs_scr[rs, 2 * hd * hk:(2 * hd + 1) * hk], kif_scr[rs, ks])
            sc_b = _dot_nt(qs_scr[rs, (2 * hd + 1) * hk:(2 * hd + 2) * hk], kib_scr[rs, ks])
            sc_scr[sub, p, hd] = jnp.where(mask_f, sc_f, sc_b).astype(BF16)
    for hd in range(GLA_HEADS):
        ks = slice(hd * hk, (hd + 1) * hk)
        sf = sf_scr[hd]
        for p in range(n_pairs):
            scat_scr[sub, p, hd, 0:hk, :] = sf.T.astype(BF16)
            sf = sf * dec_f[p][:, ks] + kv_scr[sub, hd, p, :, 0:hk]
        sf_scr[hd] = sf
        sb = sb_ref[sb_idx, hd]
        for p in reversed(range(n_pairs)):
            scat_scr[sub, p, hd, hk:2 * hk, :] = sb.T.astype(BF16)
            if p > 0:
                sb = sb * dec_b[p][:, ks] + kv_scr[sub, hd, p, :, hk:2 * hk]
    yield

    for p in range(n_pairs):
        rs = slice(r0 + p * GLA_PAIR, r0 + (p + 1) * GLA_PAIR)
        for hd in range(GLA_HEADS):
            vs = slice(hd * hv, (hd + 1) * hv)
            cat = slice(2 * hd * hk, (2 * hd + 2) * hk)
            og_scr[rs, vs] = (_dot(sc_scr[sub, p, hd], v_ref[rs, vs])
                              + _dot(qd_scr[rs, cat], scat_scr[sub, p, hd]))
    yield

    glag = glag_ref[...]
    z_parts = []
    for hd in range(GLA_HEADS):
        vs = slice(hd * hv, (hd + 1) * hv)
        oh = og_scr[rows, vs]
        ms = jnp.mean(oh * oh, axis=-1, keepdims=True)
        z_parts.append((oh * lax.rsqrt(ms + RMS_EPS)) * glag)
    o_n = jnp.concatenate(z_parts, axis=1)
    y_gla = _dot((o_n * _silu(gla_gate)).astype(BF16), wpgla_ref[...])
    merged = merged + bg_gla * y_gla
    yield

    out = _dot(merged.astype(BF16), wout_ref[...])
    yield

    ms = jnp.mean(out * out, axis=-1, keepdims=True)
    out_n = (out * lax.rsqrt(ms + RMS_EPS)) * gpost_ref[...]
    o_ref[rows, :] = x_ref[rows, :] + gate * out_n
    yield


def _main_kernel(seq_len,
                 x_ref, h_ref, hp_ref, hn_ref, mod_ref, gpost_ref, k_ref, v_ref, lr_ref,
                 wt_ref,
                 poolw_ref, pscale_ref, gkcat_ref, gbf_ref, gbb_ref, glag_ref,
                 wpp_ref, wpgla_ref, wout_ref, sb_ref,
                 o_ref,
                 sf_scr, hext_scr, q_scr, cf_scr, rb_scr, qs_scr, qd_scr, ke_scr, kif_scr, kib_scr,
                 kv_scr, scat_scr, sc_scr, og_scr):
    tb = MAIN_SUBTILES * MAIN_TILE

    @pl.when(pl.program_id(0) == 0)
    def _():
        sf_scr[...] = jnp.zeros_like(sf_scr)

    hext_scr[0:HALO, :] = hp_ref[...]
    hext_scr[HALO:HALO + tb, :] = h_ref[...]
    hext_scr[HALO + tb:HALO + tb + HALO, :] = hn_ref[...]

    refs = (x_ref, mod_ref, gpost_ref, k_ref, v_ref, lr_ref,
            wt_ref,
            poolw_ref, pscale_ref, gkcat_ref, gbf_ref, gbb_ref, glag_ref,
            wpp_ref, wpgla_ref, wout_ref, sb_ref, o_ref,
            sf_scr, hext_scr, q_scr, cf_scr, rb_scr, qs_scr, qd_scr, ke_scr, kif_scr, kib_scr,
            kv_scr, scat_scr, sc_scr, og_scr)
    gens = [_subtile_stages(seq_len, sub, refs) for sub in range(MAIN_SUBTILES)]
    _run_round_robin(gens)


def _main_call(x2, h_all, mod, g_post, k_all, v_all, lr_all, w_bf, small, wpool_bf, wpp_bf, wpgla_bf, wout_bf, sb):
    s = x2.shape[0]
    tb = MAIN_SUBTILES * MAIN_TILE
    nt = s // tb
    hb = tb // HALO
    n_hb = s // HALO
    n_pairs = MAIN_TILE // GLA_PAIR
    const2 = lambda i: (0, 0)
    tile = lambda i: (i, 0)
    one = pl.Buffered(1)
    pool_scale, gk_cat, gbias_f, gbias_b, gla_norm_g = small

    def resident(arr):
        return pl.BlockSpec(arr.shape, const2, pipeline_mode=one)

    in_specs = [
        pl.BlockSpec((tb, D_MODEL), tile),
        pl.BlockSpec((tb, D_MODEL), tile),
        pl.BlockSpec((HALO, D_MODEL), lambda i: (jnp.maximum(i * hb - 1, 0), 0)),
        pl.BlockSpec((HALO, D_MODEL), lambda i: (jnp.minimum((i + 1) * hb, n_hb - 1), 0)),
        pl.BlockSpec((3, 1, D_MODEL), lambda i: (0, 0, 0)),
        pl.BlockSpec((1, D_MODEL), const2),
        pl.BlockSpec((tb, GLA_KEY_WIDTH), tile),
        pl.BlockSpec((tb, D_MODEL), tile),
        pl.BlockSpec((tb, LR_WIDTH), tile),
        resident(w_bf),
        resident(wpool_bf), resident(pool_scale), resident(gk_cat), resident(gbias_f), resident(gbias_b),
        resident(gla_norm_g),
        resident(wpp_bf), resident(wpgla_bf), resident(wout_bf),
        pl.BlockSpec((MAIN_SUBTILES, GLA_HEADS, GLA_HEAD_V, GLA_HEAD_K), lambda i: (i, 0, 0, 0)),
    ]
    scratch = [
        pltpu.VMEM((GLA_HEADS, GLA_HEAD_V, GLA_HEAD_K), F32),
        pltpu.VMEM((tb + 2 * HALO, D_MODEL), BF16),
        pltpu.VMEM((tb, GLA_KEY_WIDTH), F32),
        pltpu.VMEM((tb, GLA_KEY_WIDTH), F32),
        pltpu.VMEM((tb, GLA_KEY_WIDTH), F32),
        pltpu.VMEM((tb, 2 * GLA_KEY_WIDTH), BF16),
        pltpu.VMEM((tb, 2 * GLA_KEY_WIDTH), BF16),
        pltpu.VMEM((tb, 2 * GLA_KEY_WIDTH), BF16),
        pltpu.VMEM((tb, GLA_KEY_WIDTH), BF16),
        pltpu.VMEM((tb, GLA_KEY_WIDTH), BF16),
        pltpu.VMEM((MAIN_SUBTILES, GLA_HEADS, n_pairs, GLA_HEAD_V, 2 * GLA_HEAD_K), F32),
        pltpu.VMEM((MAIN_SUBTILES, n_pairs, GLA_HEADS, 2 * GLA_HEAD_K, GLA_HEAD_V), BF16),
        pltpu.VMEM((MAIN_SUBTILES, n_pairs, GLA_HEADS, GLA_PAIR, GLA_PAIR), BF16),
        pltpu.VMEM((tb, D_MODEL), F32),
    ]
    return pl.pallas_call(
        functools.partial(_main_kernel, s),
        grid=(nt,),
        in_specs=in_specs,
        out_specs=pl.BlockSpec((tb, D_MODEL), tile),
        out_shape=jax.ShapeDtypeStruct((s, D_MODEL), F32),
        scratch_shapes=scratch,
        compiler_params=pltpu.CompilerParams(dimension_semantics=("arbitrary",),
                                             vmem_limit_bytes=VMEM_LIMIT_BYTES),
        name="main_call",
    )(x2, h_all, h_all, h_all, mod, g_post, k_all, v_all, lr_all,
      w_bf,
      wpool_bf, pool_scale, gk_cat, gbias_f, gbias_b, gla_norm_g, wpp_bf, wpgla_bf, wout_bf, sb)


def _layer(x2, c, w_ada, b_ada, g_pre, g_post, w_in, pool_w, pool_scale, gk_up_fwd, gk_bias_fwd,
           gk_up_bwd, gk_bias_bwd, gla_norm_g, w_proj_pool, w_proj_gla, w_out):
    mod = _ada_call(c.reshape(D_MODEL, 1), w_ada, b_ada.reshape(1, -1))
    gbias_f = gk_bias_fwd.reshape(1, -1)
    gbias_b = gk_bias_bwd.reshape(1, -1)

    pool_w2 = pool_w.reshape(POOL_GROUPS * POOL_GROUP_WIDTH, POOL_GROUP_WIDTH)
    sb, k_all, v_all, h_all, lr_all, gk_cat, w_bf, wpool_bf, wpp_bf, wpgla_bf, wout_bf = _prep_call(
        x2, mod, g_pre.reshape(1, -1), gk_up_fwd, gk_up_bwd, gbias_b, w_in.T, pool_w2, w_proj_pool, w_proj_gla,
        w_out)

    small = (pool_scale.reshape(1, -1), gk_cat, gbias_f, gbias_b, gla_norm_g.reshape(1, -1))
    return _main_call(x2, h_all, mod, g_post.reshape(1, -1), k_all, v_all, lr_all, w_bf, small,
                      wpool_bf, wpp_bf, wpgla_bf, wout_bf, sb)


def kernel(x, c, w_ada, b_ada, g_pre, g_post, w_in, pool_w, pool_scale, gk_up_fwd, gk_bias_fwd, gk_up_bwd, gk_bias_bwd, gla_norm_g, w_proj_pool, w_proj_gla, w_out):
    b, s, d = x.shape
    depth = w_in.shape[0]
    xf = x.reshape(b * s, d)
    outs = []
    for bi in range(b):
        xb = xf if b == 1 else lax.slice_in_dim(xf, bi * s, (bi + 1) * s, axis=0)
        for l in range(depth):
            xb = _layer(xb, c[bi:bi + 1], w_ada[l], b_ada[l], g_pre[l], g_post[l], w_in[l], pool_w[l],
                        pool_scale[l], gk_up_fwd[l], gk_bias_fwd[l], gk_up_bwd[l], gk_bias_bwd[l],
                        gla_norm_g[l], w_proj_pool[l], w_proj_gla[l], w_out[l])
        outs.append(xb)
    out = outs[0] if b == 1 else jnp.concatenate(outs, axis=0)
    return out.reshape(b, s, d)
```

```python
import functools
import math

import jax
import jax.numpy as jnp
from jax import lax
from jax.experimental import pallas as pl
from jax.experimental.pallas import tpu as pltpu

D_MODEL = 1024
POOL_GROUPS = 4
POOL_GROUP_WIDTH = 256
POOL_WINDOWS = (2, 4, 8, 16)
GLA_HEADS = 4
GLA_KEY_WIDTH = 512
GLA_HEAD_K = 128
GLA_HEAD_V = 256
GLA_GATE_RANK = 16
GLA_GATE_NORMALIZER = 16.0
GATE_SCALE = 1.0 / GLA_GATE_NORMALIZER
assert math.frexp(GLA_GATE_NORMALIZER)[0] == 0.5
GLA_CHUNK = 64
GLA_PAIR = 2 * GLA_CHUNK
RMS_EPS = 1e-6
IN_SPLITS = (1024, 1024, 512, 512, 1024, 1024, 16, 16, 1024, 1024)
IN_WIDTH = sum(IN_SPLITS)

LANES = 128
SUB_TILE = 256
PREP_SUBTILES = 4
MAIN_TILE = 512
MAIN_SUBTILES = 1
CUMSUM_BLOCK = 256
HALO = 16
POOL_ROW_BLOCK = 128
VMEM_LIMIT_BYTES = 60 * 1024 * 1024

_OFFS = [sum(IN_SPLITS[:j]) for j in range(len(IN_SPLITS) + 1)]
(PIN_ROWS, PGATE_ROWS, Q_ROWS, K_ROWS, V_ROWS, GGATE_ROWS) = (slice(_OFFS[j], _OFFS[j + 1]) for j in range(6))
LR_ROWS = slice(_OFFS[6], _OFFS[8])
BGP_ROWS = slice(_OFFS[8], _OFFS[9])
BGG_ROWS = slice(_OFFS[9], _OFFS[10])
LR_WIDTH = 2 * GLA_GATE_RANK
WT_SLAB = 480

F32 = jnp.float32
BF16 = jnp.bfloat16


def _dot(a, b):
    return jnp.dot(a, b, preferred_element_type=F32)


def _dot_nt(a, b):
    return lax.dot_general(a, b, (((1,), (1,)), ((), ())), preferred_element_type=F32)


def _dot_tn(a, b):
    return lax.dot_general(a, b, (((0,), (0,)), ((), ())), preferred_element_type=F32)


def _sigmoid(x):
    return 0.5 * jnp.tanh(0.5 * x) + 0.5


def _silu(x):
    return x * _sigmoid(x)


def _log_sigmoid(x):
    return jnp.minimum(x, 0.0) - jnp.log(1.0 + jnp.exp(-jnp.abs(x)))


def _norm_mod(xv, g_pre, scale, shift):
    ms = jnp.mean(xv * xv, axis=-1, keepdims=True)
    gain = g_pre * (1.0 + scale)
    return (xv * lax.rsqrt(ms + RMS_EPS)) * gain + shift


def _split_hi_lo(a):
    hi = a.astype(BF16)
    lo = (a - hi.astype(F32)).astype(BF16)
    return hi, lo


def _run_round_robin(gens):
    live = list(gens)
    while live:
        for g in list(live):
            try:
                next(g)
            except StopIteration:
                live.remove(g)


def _ada_kernel(c_ref, w_ref, b_ref, o_ref):
    s = _silu(c_ref[...])
    o_ref[...] = jnp.sum(w_ref[...] * s, axis=0, keepdims=True) + b_ref[...]


def _ada_call(c_col, w_ada, b_ada):
    return pl.pallas_call(
        _ada_kernel,
        grid=(3,),
        in_specs=[
            pl.BlockSpec((D_MODEL, 1), lambda j: (0, 0)),
            pl.BlockSpec((D_MODEL, D_MODEL), lambda j: (0, j)),
            pl.BlockSpec((1, D_MODEL), lambda j: (0, j)),
        ],
        out_specs=pl.BlockSpec((None, 1, D_MODEL), lambda j: (j, 0, 0)),
        out_shape=jax.ShapeDtypeStruct((3, 1, D_MODEL), F32),
        compiler_params=pltpu.CompilerParams(dimension_semantics=("arbitrary",)),
        name="ada_call",
    )(c_col, w_ada, b_ada)


def _prep_stages(sub, refs):
    (gkcat_ref, gbias_ref, sb_ref, k_ref, v_ref, lr_ref, st_scr) = refs
    t = SUB_TILE
    rows = slice(sub * t, (sub + 1) * t)
    k = k_ref[rows, :].astype(F32)
    v_bf = v_ref[rows, :]

    gk_bwd = gkcat_ref[:, GLA_KEY_WIDTH:]
    la = _log_sigmoid(_dot(lr_ref[rows, :], gk_bwd) + gbias_ref[...])
    la_hi, la_lo = _split_hi_lo(la)
    yield

    row = lax.broadcasted_iota(jnp.int32, (t, t), 0)
    col = lax.broadcasted_iota(jnp.int32, (t, t), 1)
    upper = jnp.where(col >= row, GATE_SCALE, 0.0).astype(BF16)
    rc = _dot(upper, la_hi) + _dot(upper, la_lo)
    yield

    tot = rc[0:1, :]
    k_end = (k * jnp.exp(tot - rc)).astype(BF16)
    decay = jnp.exp(tot)
    kvs = []
    for hd in range(GLA_HEADS):
        ks = slice(hd * GLA_HEAD_K, (hd + 1) * GLA_HEAD_K)
        vs = slice(hd * GLA_HEAD_V, (hd + 1) * GLA_HEAD_V)
        kvs.append(_dot_tn(v_bf[:, vs], k_end[:, ks]))
    yield

    per_main = MAIN_TILE // SUB_TILE
    for hd in range(GLA_HEADS):
        ks = slice(hd * GLA_HEAD_K, (hd + 1) * GLA_HEAD_K)
        st = st_scr[hd]
        if (sub + 1) % per_main == 0:
            sb_ref[sub // per_main, hd] = st
        st_scr[hd] = st * decay[:, ks] + kvs[hd]
    yield


def _prep_kernel(x_ref, mod_ref, gpre_ref, gkf_ref, gkb_ref, gbias_ref, wk_ref, wv_ref, wlr_ref,
                 win_ref, wpool_ref, wpp_ref, wpgla_ref, wout_ref,
                 sb_ref, k_ref, v_ref, h_ref, lr_ref, gkcat_ref,
                 winbf_ref, wpoolbf_ref, wppbf_ref, wpglabf_ref, woutbf_ref,
                 st_scr, wk_scr, wv_scr, wlr_scr):
    @pl.when(pl.program_id(0) == 0)
    def _():
        st_scr[...] = jnp.zeros_like(st_scr)
        wk_scr[...] = wk_ref[...].astype(BF16)
        wv_scr[...] = wv_ref[...].astype(BF16)
        wlr_scr[...] = wlr_ref[...].astype(BF16)
        r, w = GLA_GATE_RANK, GLA_KEY_WIDTH
        gkcat_ref[...] = jnp.zeros(gkcat_ref.shape, BF16)
        gkcat_ref[0:r, 0:w] = gkf_ref[...].astype(BF16)
        gkcat_ref[r:2 * r, w:2 * w] = gkb_ref[...].astype(BF16)

    winbf_ref[...] = win_ref[...].astype(BF16)
    wpoolbf_ref[...] = wpool_ref[...].astype(BF16)
    wppbf_ref[...] = wpp_ref[...].astype(BF16)
    wpglabf_ref[...] = wpgla_ref[...].astype(BF16)
    woutbf_ref[...] = wout_ref[...].astype(BF16)

    h = _norm_mod(x_ref[...], gpre_ref[...], mod_ref[1], mod_ref[0]).astype(BF16)
    h_ref[...] = h
    k_ref[...] = _dot_nt(h, wk_scr[...]).astype(BF16)
    lr_ref[...] = _dot_nt(h, wlr_scr[...]).astype(BF16)
    v_ref[...] = _dot_nt(h, wv_scr[...]).astype(BF16)

    refs = (gkcat_ref, gbias_ref, sb_ref, k_ref, v_ref, lr_ref, st_scr)
    gens = [_prep_stages(sub, refs) for sub in reversed(range(PREP_SUBTILES))]
    _run_round_robin(gens)


def _prep_call(x2, mod, g_pre, gk_f, gk_b, gbias_b, w_in_t, pool_w2, w_proj_pool, w_proj_gla, w_out):
    s = x2.shape[0]
    tb = PREP_SUBTILES * SUB_TILE
    nb = s // tb
    slab = D_MODEL // nb
    n_wt = IN_WIDTH // WT_SLAB
    const = lambda i: (0, 0)
    rev = lambda i: (nb - 1 - i, 0)
    fwd = lambda i: (i, 0)
    wt_slab = lambda i: (jnp.minimum(i, n_wt - 1), 0)
    one = pl.Buffered(1)

    def rows_of(rs):
        n = rs.stop - rs.start
        return pl.BlockSpec((n, D_MODEL), lambda i: (rs.start // n, 0), pipeline_mode=one)

    return pl.pallas_call(
        _prep_kernel,
        grid=(nb,),
        in_specs=[
            pl.BlockSpec((tb, D_MODEL), rev),
            pl.BlockSpec((3, 1, D_MODEL), lambda i: (0, 0, 0)),
            pl.BlockSpec((1, D_MODEL), const),
            pl.BlockSpec(gk_f.shape, const),
            pl.BlockSpec(gk_b.shape, const),
            pl.BlockSpec(gbias_b.shape, const),
            rows_of(K_ROWS),
            rows_of(V_ROWS),
            rows_of(LR_ROWS),
            pl.BlockSpec((WT_SLAB, D_MODEL), wt_slab),
            pl.BlockSpec((slab, POOL_GROUP_WIDTH), fwd),
            pl.BlockSpec((slab, D_MODEL), fwd),
            pl.BlockSpec((slab, D_MODEL), fwd),
            pl.BlockSpec((slab, D_MODEL), fwd),
        ],
        out_specs=[
            pl.BlockSpec((tb // MAIN_TILE, GLA_HEADS, GLA_HEAD_V, GLA_HEAD_K), lambda i: (nb - 1 - i, 0, 0, 0)),
            pl.BlockSpec((tb, GLA_KEY_WIDTH), rev),
            pl.BlockSpec((tb, D_MODEL), rev),
            pl.BlockSpec((tb, D_MODEL), rev),
            pl.BlockSpec((tb, LR_WIDTH), rev),
            pl.BlockSpec((LR_WIDTH, 2 * GLA_KEY_WIDTH), const),
            pl.BlockSpec((WT_SLAB, D_MODEL), wt_slab),
            pl.BlockSpec((slab, POOL_GROUP_WIDTH), fwd),
            pl.BlockSpec((slab, D_MODEL), fwd),
            pl.BlockSpec((slab, D_MODEL), fwd),
            pl.BlockSpec((slab, D_MODEL), fwd),
        ],
        out_shape=[
            jax.ShapeDtypeStruct((s // MAIN_TILE, GLA_HEADS, GLA_HEAD_V, GLA_HEAD_K), F32),
            jax.ShapeDtypeStruct((s, GLA_KEY_WIDTH), BF16),
            jax.ShapeDtypeStruct((s, D_MODEL), BF16),
            jax.ShapeDtypeStruct((s, D_MODEL), BF16),
            jax.ShapeDtypeStruct((s, LR_WIDTH), BF16),
            jax.ShapeDtypeStruct((LR_WIDTH, 2 * GLA_KEY_WIDTH), BF16),
            jax.ShapeDtypeStruct((IN_WIDTH, D_MODEL), BF16),
            jax.ShapeDtypeStruct(pool_w2.shape, BF16),
            jax.ShapeDtypeStruct((D_MODEL, D_MODEL), BF16),
            jax.ShapeDtypeStruct((D_MODEL, D_MODEL), BF16),
            jax.ShapeDtypeStruct((D_MODEL, D_MODEL), BF16),
        ],
        scratch_shapes=[
            pltpu.VMEM((GLA_HEADS, GLA_HEAD_V, GLA_HEAD_K), F32),
            pltpu.VMEM((GLA_KEY_WIDTH, D_MODEL), BF16),
            pltpu.VMEM((D_MODEL, D_MODEL), BF16),
            pltpu.VMEM((LR_WIDTH, D_MODEL), BF16),
        ],
        compiler_params=pltpu.CompilerParams(dimension_semantics=("arbitrary",),
                                             vmem_limit_bytes=VMEM_LIMIT_BYTES),
        name="prep_call",
    )(x2, mod, g_pre, gk_f, gk_b, gbias_b, w_in_t, w_in_t, w_in_t, w_in_t, pool_w2, w_proj_pool, w_proj_gla, w_out)


def _subtile_stages(seq_len, sub, refs):
    (x_ref, mod_ref, gpost_ref, k_ref, v_ref, lr_ref,
     wt_ref,
     poolw_ref, pscale_ref, gkcat_ref, gbf_ref, gbb_ref, glag_ref,
     wpp_ref, wpgla_ref, wout_ref, sb_ref, o_ref,
     sf_scr, hext_scr, q_scr, cf_scr, rb_scr, qs_scr, qd_scr, ke_scr, kif_scr, kib_scr,
     kv_scr, scat_scr, sc_scr, og_scr) = refs
    t = MAIN_TILE
    r0 = sub * t
    rows = slice(r0, r0 + t)
    tile_row0 = pl.program_id(0) * (MAIN_SUBTILES * t) + r0
    sb_idx = sub
    n_pairs = t // GLA_PAIR
    hk, hv = GLA_HEAD_K, GLA_HEAD_V
    gate = mod_ref[2]

    h_ext = hext_scr[r0:r0 + t + 2 * HALO, :]
    h = hext_scr[r0 + HALO:r0 + HALO + t, :]

    gbias = jnp.concatenate([gbf_ref[...], gbb_ref[...]], axis=1)
    la_pre = _dot(lr_ref[rows, :], gkcat_ref[...]) + gbias
    q_scr[rows, :] = _dot_nt(h, wt_ref[Q_ROWS, :]) * (GLA_HEAD_K ** -0.5)
    te = t + 2 * HALO
    g_row = tile_row0 - HALO + lax.broadcasted_iota(jnp.int32, (te, 1), 0)
    valid = jnp.logical_and(g_row >= 0, g_row < seq_len)
    u_ext = jnp.where(valid, _dot_nt(h_ext, wt_ref[PIN_ROWS, :]), 0.0)
    yield

    la = _log_sigmoid(la_pre)
    p_gate = _dot_nt(h, wt_ref[PGATE_ROWS, :])
    gla_gate = _dot_nt(h, wt_ref[GGATE_ROWS, :])
    u_ext_bf = u_ext.astype(BF16)
    u = u_ext[HALO:HALO + t, :]
    laf_hi, laf_lo = _split_hi_lo(la[:, :GLA_KEY_WIDTH])
    lab_hi, lab_lo = _split_hi_lo(la[:, GLA_KEY_WIDTH:])
    yield

    cb = CUMSUM_BLOCK
    row = lax.broadcasted_iota(jnp.int32, (cb, cb), 0)
    col = lax.broadcasted_iota(jnp.int32, (cb, cb), 1)
    same_chunk = (row // GLA_CHUNK) == (col // GLA_CHUNK)
    lower = jnp.where(jnp.logical_and(same_chunk, col <= row), GATE_SCALE, 0.0).astype(BF16)
    upper = jnp.where(jnp.logical_and(same_chunk, col >= row), GATE_SCALE, 0.0).astype(BF16)
    for b in range(t // cb):
        rb_rows = slice(b * cb, (b + 1) * cb)
        dst = slice(r0 + b * cb, r0 + (b + 1) * cb)
        cf_scr[dst, :] = _dot(lower, laf_hi[rb_rows]) + _dot(lower, laf_lo[rb_rows])
        rb_scr[dst, :] = _dot(upper, lab_hi[rb_rows]) + _dot(upper, lab_lo[rb_rows])
    bg_pool = _sigmoid(_dot_nt(h, wt_ref[BGP_ROWS, :]))
    bg_gla = _sigmoid(_dot_nt(h, wt_ref[BGG_ROWS, :]))
    yield

    pos = tile_row0 + lax.broadcasted_iota(jnp.int32, (t, 1), 0)
    rb_ = POOL_ROW_BLOCK
    kb_ = POOL_ROW_BLOCK + 2 * HALO
    band_r = lax.broadcasted_iota(jnp.int32, (rb_, kb_), 0)
    band_c = lax.broadcasted_iota(jnp.int32, (rb_, kb_), 1)
    band_d = band_c - HALO - band_r
    a_parts = []
    for gi, w in enumerate(POOL_WINDOWS):
        hw = w // 2
        cs = slice(gi * POOL_GROUP_WIDTH, (gi + 1) * POOL_GROUP_WIDTH)
        ws = slice(gi * POOL_GROUP_WIDTH, (gi + 1) * POOL_GROUP_WIDTH)
        band = jnp.logical_and(band_d >= -hw, band_d < hw).astype(BF16)
        wsum = jnp.concatenate(
            [_dot(band, u_ext_bf[b * rb_:b * rb_ + kb_, cs]) for b in range(t // rb_)], axis=0)
        count = (jnp.minimum(pos + hw, seq_len) - jnp.maximum(pos - hw, 0)).astype(F32)
        pooled = wsum / count - u[:, cs]
        a_parts.append(_dot(pooled.astype(BF16), poolw_ref[ws, :]))
    a = jnp.concatenate(a_parts, axis=1) * pscale_ref[...]
    z_pool = (a * _silu(p_gate)).astype(BF16)

    def put_heads(dst, rs, fwd, bwd):
        for hd in range(GLA_HEADS):
            ks = slice(hd * hk, (hd + 1) * hk)
            dst[rs, 2 * hd * hk:(2 * hd + 1) * hk] = fwd[:, ks].astype(BF16)
            dst[rs, (2 * hd + 1) * hk:(2 * hd + 2) * hk] = bwd[:, ks].astype(BF16)

    dec_f, dec_b = [], []
    for p in range(n_pairs):
        c0 = r0 + p * GLA_PAIR
        c1 = c0 + GLA_CHUNK
        ra = slice(c0, c1)
        rb = slice(c1, c1 + GLA_CHUNK)
        q0, q1 = q_scr[ra, :], q_scr[rb, :]
        k0, k1 = k_ref[ra, :].astype(F32), k_ref[rb, :].astype(F32)
        cum0, cum1 = cf_scr[ra, :], cf_scr[rb, :]
        tot0 = cf_scr[c1 - 1:c1, :]
        tot1 = cf_scr[c1 + GLA_CHUNK - 1:c1 + GLA_CHUNK, :]
        rc0, rc1 = rb_scr[ra, :], rb_scr[rb, :]
        tb0 = rb_scr[c0:c0 + 1, :]
        tb1 = rb_scr[c1:c1 + 1, :]
        e_tot0, e_tot1 = jnp.exp(tot0), jnp.exp(tot1)
        e_tb0, e_tb1 = jnp.exp(tb0), jnp.exp(tb1)

        a0 = cum0 - tot0
        ks_f0 = k0 * jnp.exp(-a0)
        qs_f1 = q1 * jnp.exp(cum1)
        qs_b0 = q0 * jnp.exp(rc0)
        put_heads(qs_scr, ra, q0 * jnp.exp(a0), qs_b0)
        put_heads(qd_scr, ra, q0 * jnp.exp(cum0), qs_b0 * e_tb1)
        kif_scr[ra, :] = ks_f0.astype(BF16)
        kif_scr[rb, :] = (k1 * jnp.exp(-cum1)).astype(BF16)
        a1 = rc1 - tb1
        ks_b1 = k1 * jnp.exp(-a1)
        put_heads(qs_scr, rb, qs_f1, q1 * jnp.exp(a1))
        put_heads(qd_scr, rb, qs_f1 * e_tot0, q1 * jnp.exp(rc1))
        kib_scr[ra, :] = (k0 * jnp.exp(-rc0)).astype(BF16)
        kib_scr[rb, :] = ks_b1.astype(BF16)
        put_heads(ke_scr, ra, ks_f0 * e_tot1, k0 * jnp.exp(tb0 - rc0))
        put_heads(ke_scr, rb, k1 * jnp.exp(tot1 - cum1), ks_b1 * e_tb0)
        dec_f.append(e_tot0 * e_tot1)
        dec_b.append(e_tb0 * e_tb1)
    yield

    merged = bg_pool * _dot(z_pool, wpp_ref[...])
    for hd in range(GLA_HEADS):
        vs = slice(hd * hv, (hd + 1) * hv)
        cat = slice(2 * hd * hk, (2 * hd + 2) * hk)
        for p in range(n_pairs):
            rs = slice(r0 + p * GLA_PAIR, r0 + (p + 1) * GLA_PAIR)
            kv_scr[sub, hd, p] = _dot_tn(v_ref[rs, vs], ke_scr[rs, cat])
    yield

    prow = lax.broadcasted_iota(jnp.int32, (GLA_PAIR, GLA_PAIR), 0)
    pcol = lax.broadcasted_iota(jnp.int32, (GLA_PAIR, GLA_PAIR), 1)
    mask_f = pcol <= prow
    for p in range(n_pairs):
        rs = slice(r0 + p * GLA_PAIR, r0 + (p + 1) * GLA_PAIR)
        for hd in range(GLA_HEADS):
            ks = slice(hd * hk, (hd + 1) * hk)
            sc_f = _dot_nt(qs_scr[rs, 2 * hd * hk:(2 * hd + 1) * hk], kif_scr[rs, ks])
            sc_b = _dot_nt(qs_scr[rs, (2 * hd + 1) * hk:(2 * hd + 2) * hk], kib_scr[rs, ks])
            sc_scr[sub, p, hd] = jnp.where(mask_f, sc_f, sc_b).astype(BF16)
    for hd in range(GLA_HEADS):
        ks = slice(hd * hk, (hd + 1) * hk)
        sf = sf_scr[hd]
        for p in range(n_pairs):
            scat_scr[sub, p, hd, 0:hk, :] = sf.T.astype(BF16)
            sf = sf * dec_f[p][:, ks] + kv_scr[sub, hd, p, :, 0:hk]
        sf_scr[hd] = sf
        sb = sb_ref[sb_idx, hd]
        for p in reversed(range(n_pairs)):
            scat_scr[sub, p, hd, hk:2 * hk, :] = sb.T.astype(BF16)
            if p > 0:
                sb = sb * dec_b[p][:, ks] + kv_scr[sub, hd, p, :, hk:2 * hk]
    yield

    for p in range(n_pairs):
        rs = slice(r0 + p * GLA_PAIR, r0 + (p + 1) * GLA_PAIR)
        for hd in range(GLA_HEADS):
            vs = slice(hd * hv, (hd + 1) * hv)
            cat = slice(2 * hd * hk, (2 * hd + 2) * hk)
            og_scr[rs, vs] = (_dot(sc_scr[sub, p, hd], v_ref[rs, vs])
                              + _dot(qd_scr[rs, cat], scat_scr[sub, p, hd]))
    yield

    glag = glag_ref[...]
    z_parts = []
    for hd in range(GLA_HEADS):
        vs = slice(hd * hv, (hd + 1) * hv)
        oh = og_scr[rows, vs]
        ms = jnp.mean(oh * oh, axis=-1, keepdims=True)
        z_parts.append((oh * lax.rsqrt(ms + RMS_EPS)) * glag)
    o_n = jnp.concatenate(z_parts, axis=1)
    y_gla = _dot((o_n * _silu(gla_gate)).astype(BF16), wpgla_ref[...])
    merged = merged + bg_gla * y_gla
    yield

    out = _dot(merged.astype(BF16), wout_ref[...])
    yield

    ms = jnp.mean(out * out, axis=-1, keepdims=True)
    out_n = (out * lax.rsqrt(ms + RMS_EPS)) * gpost_ref[...]
    o_ref[rows, :] = x_ref[rows, :] + gate * out_n
    yield


def _main_kernel(seq_len,
                 x_ref, h_ref, hp_ref, hn_ref, mod_ref, gpost_ref, k_ref, v_ref, lr_ref,
                 wt_ref,
                 poolw_ref, pscale_ref, gkcat_ref, gbf_ref, gbb_ref, glag_ref,
                 wpp_ref, wpgla_ref, wout_ref, sb_ref,
                 o_ref,
                 sf_scr, hext_scr, q_scr, cf_scr, rb_scr, qs_scr, qd_scr, ke_scr, kif_scr, kib_scr,
                 kv_scr, scat_scr, sc_scr, og_scr):
    tb = MAIN_SUBTILES * MAIN_TILE

    @pl.when(pl.program_id(0) == 0)
    def _():
        sf_scr[...] = jnp.zeros_like(sf_scr)

    hext_scr[0:HALO, :] = hp_ref[...]
    hext_scr[HALO:HALO + tb, :] = h_ref[...]
    hext_scr[HALO + tb:HALO + tb + HALO, :] = hn_ref[...]

    refs = (x_ref, mod_ref, gpost_ref, k_ref, v_ref, lr_ref,
            wt_ref,
            poolw_ref, pscale_ref, gkcat_ref, gbf_ref, gbb_ref, glag_ref,
            wpp_ref, wpgla_ref, wout_ref, sb_ref, o_ref,
            sf_scr, hext_scr, q_scr, cf_scr, rb_scr, qs_scr, qd_scr, ke_scr, kif_scr, kib_scr,
            kv_scr, scat_scr, sc_scr, og_scr)
    gens = [_subtile_stages(seq_len, sub, refs) for sub in range(MAIN_SUBTILES)]
    _run_round_robin(gens)


def _main_call(x2, h_all, mod, g_post, k_all, v_all, lr_all, w_bf, small, wpool_bf, wpp_bf, wpgla_bf, wout_bf, sb):
    s = x2.shape[0]
    tb = MAIN_SUBTILES * MAIN_TILE
    nt = s // tb
    hb = tb // HALO
    n_hb = s // HALO
    n_pairs = MAIN_TILE // GLA_PAIR
    const2 = lambda i: (0, 0)
    tile = lambda i: (i, 0)
    one = pl.Buffered(1)
    pool_scale, gk_cat, gbias_f, gbias_b, gla_norm_g = small

    def resident(arr):
        return pl.BlockSpec(arr.shape, const2, pipeline_mode=one)

    in_specs = [
        pl.BlockSpec((tb, D_MODEL), tile),
        pl.BlockSpec((tb, D_MODEL), tile),
        pl.BlockSpec((HALO, D_MODEL), lambda i: (jnp.maximum(i * hb - 1, 0), 0)),
        pl.BlockSpec((HALO, D_MODEL), lambda i: (jnp.minimum((i + 1) * hb, n_hb - 1), 0)),
        pl.BlockSpec((3, 1, D_MODEL), lambda i: (0, 0, 0)),
        pl.BlockSpec((1, D_MODEL), const2),
        pl.BlockSpec((tb, GLA_KEY_WIDTH), tile),
        pl.BlockSpec((tb, D_MODEL), tile),
        pl.BlockSpec((tb, LR_WIDTH), tile),
        resident(w_bf),
        resident(wpool_bf), resident(pool_scale), resident(gk_cat), resident(gbias_f), resident(gbias_b),
        resident(gla_norm_g),
        resident(wpp_bf), resident(wpgla_bf), resident(wout_bf),
        pl.BlockSpec((MAIN_SUBTILES, GLA_HEADS, GLA_HEAD_V, GLA_HEAD_K), lambda i: (i, 0, 0, 0)),
    ]
    scratch = [
        pltpu.VMEM((GLA_HEADS, GLA_HEAD_V, GLA_HEAD_K), F32),
        pltpu.VMEM((tb + 2 * HALO, D_MODEL), BF16),
        pltpu.VMEM((tb, GLA_KEY_WIDTH), F32),
        pltpu.VMEM((tb, GLA_KEY_WIDTH), F32),
        pltpu.VMEM((tb, GLA_KEY_WIDTH), F32),
        pltpu.VMEM((tb, 2 * GLA_KEY_WIDTH), BF16),
        pltpu.VMEM((tb, 2 * GLA_KEY_WIDTH), BF16),
        pltpu.VMEM((tb, 2 * GLA_KEY_WIDTH), BF16),
        pltpu.VMEM((tb, GLA_KEY_WIDTH), BF16),
        pltpu.VMEM((tb, GLA_KEY_WIDTH), BF16),
        pltpu.VMEM((MAIN_SUBTILES, GLA_HEADS, n_pairs, GLA_HEAD_V, 2 * GLA_HEAD_K), F32),
        pltpu.VMEM((MAIN_SUBTILES, n_pairs, GLA_HEADS, 2 * GLA_HEAD_K, GLA_HEAD_V), BF16),
        pltpu.VMEM((MAIN_SUBTILES, n_pairs, GLA_HEADS, GLA_PAIR, GLA_PAIR), BF16),
        pltpu.VMEM((tb, D_MODEL), F32),
    ]
    return pl.pallas_call(
        functools.partial(_main_kernel, s),
        grid=(nt,),
        in_specs=in_specs,
        out_specs=pl.BlockSpec((tb, D_MODEL), tile),
        out_shape=jax.ShapeDtypeStruct((s, D_MODEL), F32),
        scratch_shapes=scratch,
        compiler_params=pltpu.CompilerParams(dimension_semantics=("arbitrary",),
                                             vmem_limit_bytes=VMEM_LIMIT_BYTES),
        name="main_call",
    )(x2, h_all, h_all, h_all, mod, g_post, k_all, v_all, lr_all,
      w_bf,
      wpool_bf, pool_scale, gk_cat, gbias_f, gbias_b, gla_norm_g, wpp_bf, wpgla_bf, wout_bf, sb)


def _layer(x2, c, w_ada, b_ada, g_pre, g_post, w_in, pool_w, pool_scale, gk_up_fwd, gk_bias_fwd,
           gk_up_bwd, gk_bias_bwd, gla_norm_g, w_proj_pool, w_proj_gla, w_out):
    mod = _ada_call(c.reshape(D_MODEL, 1), w_ada, b_ada.reshape(1, -1))
    gbias_f = gk_bias_fwd.reshape(1, -1)
    gbias_b = gk_bias_bwd.reshape(1, -1)

    pool_w2 = pool_w.reshape(POOL_GROUPS * POOL_GROUP_WIDTH, POOL_GROUP_WIDTH)
    sb, k_all, v_all, h_all, lr_all, gk_cat, w_bf, wpool_bf, wpp_bf, wpgla_bf, wout_bf = _prep_call(
        x2, mod, g_pre.reshape(1, -1), gk_up_fwd, gk_up_bwd, gbias_b, w_in.T, pool_w2, w_proj_pool, w_proj_gla,
        w_out)

    small = (pool_scale.reshape(1, -1), gk_cat, gbias_f, gbias_b, gla_norm_g.reshape(1, -1))
    return _main_call(x2, h_all, mod, g_post.reshape(1, -1), k_all, v_all, lr_all, w_bf, small,
                      wpool_bf, wpp_bf, wpgla_bf, wout_bf, sb)


def kernel(x, c, w_ada, b_ada, g_pre, g_post, w_in, pool_w, pool_scale, gk_up_fwd, gk_bias_fwd, gk_up_bwd, gk_bias_bwd, gla_norm_g, w_proj_pool, w_proj_gla, w_out):
    b, s, d = x.shape
    depth = w_in.shape[0]
    xf = x.reshape(b * s, d)
    outs = []
    for bi in range(b):
        xb = xf if b == 1 else lax.slice_in_dim(xf, bi * s, (bi + 1) * s, axis=0)
        for l in range(depth):
            xb = _layer(xb, c[bi:bi + 1], w_ada[l], b_ada[l], g_pre[l], g_post[l], w_in[l], pool_w[l],
                        pool_scale[l], gk_up_fwd[l], gk_bias_fwd[l], gk_up_bwd[l], gk_bias_bwd[l],
                        gla_norm_g[l], w_proj_pool[l], w_proj_gla[l], w_out[l])
        outs.append(xb)
    out = outs[0] if b == 1 else jnp.concatenate(outs, axis=0)
    return out.reshape(b, s, d)
```

```python
import functools
import math

import jax
import jax.numpy as jnp
from jax import lax
from jax.experimental import pallas as pl
from jax.experimental.pallas import tpu as pltpu

D_MODEL = 1024
POOL_GROUPS = 4
POOL_GROUP_WIDTH = 256
POOL_WINDOWS = (2, 4, 8, 16)
GLA_HEADS = 4
GLA_KEY_WIDTH = 512
GLA_HEAD_K = 128
GLA_HEAD_V = 256
GLA_GATE_RANK = 16
GLA_GATE_NORMALIZER = 16.0
GATE_SCALE = 1.0 / GLA_GATE_NORMALIZER
assert math.frexp(GLA_GATE_NORMALIZER)[0] == 0.5
GLA_CHUNK = 64
GLA_PAIR = 2 * GLA_CHUNK
RMS_EPS = 1e-6
IN_SPLITS = (1024, 1024, 512, 512, 1024, 1024, 16, 16, 1024, 1024)
IN_WIDTH = sum(IN_SPLITS)

LANES = 128
SUB_TILE = 256
PREP_SUBTILES = 4
MAIN_TILE = 512
MAIN_SUBTILES = 1
CUMSUM_BLOCK = 256
HALO = 16
POOL_ROW_BLOCK = 128
VMEM_LIMIT_BYTES = 60 * 1024 * 1024

_OFFS = [sum(IN_SPLITS[:j]) for j in range(len(IN_SPLITS) + 1)]
(PIN_ROWS, PGATE_ROWS, Q_ROWS, K_ROWS, V_ROWS, GGATE_ROWS) = (slice(_OFFS[j], _OFFS[j + 1]) for j in range(6))
LR_ROWS = slice(_OFFS[6], _OFFS[8])
BGP_ROWS = slice(_OFFS[8], _OFFS[9])
BGG_ROWS = slice(_OFFS[9], _OFFS[10])
LR_WIDTH = 2 * GLA_GATE_RANK
WT_SLAB = 480

F32 = jnp.float32
BF16 = jnp.bfloat16


def _dot(a, b):
    return jnp.dot(a, b, preferred_element_type=F32)


def _dot_nt(a, b):
    return lax.dot_general(a, b, (((1,), (1,)), ((), ())), preferred_element_type=F32)


def _dot_tn(a, b):
    return lax.dot_general(a, b, (((0,), (0,)), ((), ())), preferred_element_type=F32)


def _sigmoid(x):
    return 0.5 * jnp.tanh(0.5 * x) + 0.5


def _silu(x):
    return x * _sigmoid(x)


def _log_sigmoid(x):
    return jnp.minimum(x, 0.0) - jnp.log(1.0 + jnp.exp(-jnp.abs(x)))


def _norm_mod(xv, g_pre, scale, shift):
    ms = jnp.mean(xv * xv, axis=-1, keepdims=True)
    gain = g_pre * (1.0 + scale)
    return (xv * lax.rsqrt(ms + RMS_EPS)) * gain + shift


def _split_hi_lo(a):
    hi = a.astype(BF16)
    lo = (a - hi.astype(F32)).astype(BF16)
    return hi, lo


def _run_round_robin(gens):
    live = list(gens)
    while live:
        for g in list(live):
            try:
                next(g)
            except StopIteration:
                live.remove(g)


def _ada_kernel(c_ref, w_ref, b_ref, o_ref):
    s = _silu(c_ref[...])
    for j in range(3):
        cols = slice(j * D_MODEL, (j + 1) * D_MODEL)
        o_ref[j] = jnp.sum(w_ref[:, cols] * s, axis=0, keepdims=True) + b_ref[:, cols]


def _ada_call(c_col, w_ada, b_ada):
    return pl.pallas_call(
        _ada_kernel,
        grid=(1,),
        in_specs=[
            pl.BlockSpec((D_MODEL, 1), lambda j: (0, 0)),
            pl.BlockSpec((D_MODEL, 3 * D_MODEL), lambda j: (0, 0), pipeline_mode=pl.Buffered(1)),
            pl.BlockSpec((1, 3 * D_MODEL), lambda j: (0, 0)),
        ],
        out_specs=pl.BlockSpec((3, 1, D_MODEL), lambda j: (0, 0, 0)),
        out_shape=jax.ShapeDtypeStruct((3, 1, D_MODEL), F32),
        compiler_params=pltpu.CompilerParams(dimension_semantics=("arbitrary",)),
        name="ada_call",
    )(c_col, w_ada, b_ada)


def _prep_stages(sub, refs):
    (gkcat_ref, gbias_ref, sb_ref, k_ref, v_ref, lr_ref, st_scr) = refs
    t = SUB_TILE
    rows = slice(sub * t, (sub + 1) * t)
    k = k_ref[rows, :]
    v_bf = v_ref[rows, :]

    gk_bwd = gkcat_ref[:, GLA_KEY_WIDTH:]
    la = _log_sigmoid(_dot(lr_ref[rows, :], gk_bwd) + gbias_ref[...])
    la_hi, la_lo = _split_hi_lo(la)
    yield

    row = lax.broadcasted_iota(jnp.int32, (t, t), 0)
    col = lax.broadcasted_iota(jnp.int32, (t, t), 1)
    upper = jnp.where(col >= row, GATE_SCALE, 0.0).astype(BF16)
    rc = _dot(upper, la_hi) + _dot(upper, la_lo)
    yield

    tot = rc[0:1, :]
    k_end = (k * jnp.exp(tot - rc)).astype(BF16)
    decay = jnp.exp(tot)
    kvs = []
    for hd in range(GLA_HEADS):
        ks = slice(hd * GLA_HEAD_K, (hd + 1) * GLA_HEAD_K)
        vs = slice(hd * GLA_HEAD_V, (hd + 1) * GLA_HEAD_V)
        kvs.append(_dot_tn(v_bf[:, vs], k_end[:, ks]))
    yield

    per_main = MAIN_TILE // SUB_TILE
    for hd in range(GLA_HEADS):
        ks = slice(hd * GLA_HEAD_K, (hd + 1) * GLA_HEAD_K)
        st = st_scr[hd]
        if (sub + 1) % per_main == 0:
            sb_ref[sub // per_main, hd] = st
        st_scr[hd] = st * decay[:, ks] + kvs[hd]
    yield


def _prep_kernel(x_ref, mod_ref, gpre_ref, gkf_ref, gkb_ref, gbias_ref, wk_ref, wv_ref, wlr_ref,
                 win_ref, wpool_ref, wpp_ref, wpgla_ref, wout_ref,
                 sb_ref, k_ref, v_ref, h_ref, lr_ref, gkcat_ref,
                 winbf_ref, wpoolbf_ref, wppbf_ref, wpglabf_ref, woutbf_ref,
                 st_scr, wk_scr, wv_scr, wlr_scr):
    @pl.when(pl.program_id(0) == 0)
    def _():
        st_scr[...] = jnp.zeros_like(st_scr)
        wk_scr[...] = wk_ref[...].astype(BF16)
        wv_scr[...] = wv_ref[...].astype(BF16)
        wlr_scr[...] = wlr_ref[...].astype(BF16)
        r, w = GLA_GATE_RANK, GLA_KEY_WIDTH
        gkcat_ref[...] = jnp.zeros(gkcat_ref.shape, BF16)
        gkcat_ref[0:r, 0:w] = gkf_ref[...].astype(BF16)
        gkcat_ref[r:2 * r, w:2 * w] = gkb_ref[...].astype(BF16)

    winbf_ref[...] = win_ref[...].astype(BF16)
    wpoolbf_ref[...] = wpool_ref[...].astype(BF16)
    wppbf_ref[...] = wpp_ref[...].astype(BF16)
    wpglabf_ref[...] = wpgla_ref[...].astype(BF16)
    woutbf_ref[...] = wout_ref[...].astype(BF16)

    h = _norm_mod(x_ref[...], gpre_ref[...], mod_ref[1], mod_ref[0]).astype(BF16)
    h_ref[...] = h
    k_ref[...] = _dot_nt(h, wk_scr[...])
    lr_ref[...] = _dot_nt(h, wlr_scr[...]).astype(BF16)
    v_ref[...] = _dot_nt(h, wv_scr[...]).astype(BF16)

    refs = (gkcat_ref, gbias_ref, sb_ref, k_ref, v_ref, lr_ref, st_scr)
    gens = [_prep_stages(sub, refs) for sub in reversed(range(PREP_SUBTILES))]
    _run_round_robin(gens)


def _prep_call(x2, mod, g_pre, gk_f, gk_b, gbias_b, w_in_t, pool_w2, w_proj_pool, w_proj_gla, w_out):
    s = x2.shape[0]
    tb = PREP_SUBTILES * SUB_TILE
    nb = s // tb
    slab = D_MODEL // nb
    n_wt = IN_WIDTH // WT_SLAB
    const = lambda i: (0, 0)
    rev = lambda i: (nb - 1 - i, 0)
    fwd = lambda i: (i, 0)
    wt_slab = lambda i: (jnp.minimum(i, n_wt - 1), 0)
    one = pl.Buffered(1)

    def rows_of(rs):
        n = rs.stop - rs.start
        return pl.BlockSpec((n, D_MODEL), lambda i: (rs.start // n, 0), pipeline_mode=one)

    return pl.pallas_call(
        _prep_kernel,
        grid=(nb,),
        in_specs=[
            pl.BlockSpec((tb, D_MODEL), rev),
            pl.BlockSpec((3, 1, D_MODEL), lambda i: (0, 0, 0)),
            pl.BlockSpec((1, D_MODEL), const),
            pl.BlockSpec(gk_f.shape, const),
            pl.BlockSpec(gk_b.shape, const),
            pl.BlockSpec(gbias_b.shape, const),
            rows_of(K_ROWS),
            rows_of(V_ROWS),
            rows_of(LR_ROWS),
            pl.BlockSpec((WT_SLAB, D_MODEL), wt_slab),
            pl.BlockSpec((slab, POOL_GROUP_WIDTH), fwd),
            pl.BlockSpec((slab, D_MODEL), fwd),
            pl.BlockSpec((slab, D_MODEL), fwd),
            pl.BlockSpec((slab, D_MODEL), fwd),
        ],
        out_specs=[
            pl.BlockSpec((tb // MAIN_TILE, GLA_HEADS, GLA_HEAD_V, GLA_HEAD_K), lambda i: (nb - 1 - i, 0, 0, 0)),
            pl.BlockSpec((tb, GLA_KEY_WIDTH), rev),
            pl.BlockSpec((tb, D_MODEL), rev),
            pl.BlockSpec((tb, D_MODEL), rev),
            pl.BlockSpec((tb, LR_WIDTH), rev),
            pl.BlockSpec((LR_WIDTH, 2 * GLA_KEY_WIDTH), const),
            pl.BlockSpec((WT_SLAB, D_MODEL), wt_slab),
            pl.BlockSpec((slab, POOL_GROUP_WIDTH), fwd),
            pl.BlockSpec((slab, D_MODEL), fwd),
            pl.BlockSpec((slab, D_MODEL), fwd),
            pl.BlockSpec((slab, D_MODEL), fwd),
        ],
        out_shape=[
            jax.ShapeDtypeStruct((s // MAIN_TILE, GLA_HEADS, GLA_HEAD_V, GLA_HEAD_K), F32),
            jax.ShapeDtypeStruct((s, GLA_KEY_WIDTH), F32),
            jax.ShapeDtypeStruct((s, D_MODEL), BF16),
            jax.ShapeDtypeStruct((s, D_MODEL), BF16),
            jax.ShapeDtypeStruct((s, LR_WIDTH), BF16),
            jax.ShapeDtypeStruct((LR_WIDTH, 2 * GLA_KEY_WIDTH), BF16),
            jax.ShapeDtypeStruct((IN_WIDTH, D_MODEL), BF16),
            jax.ShapeDtypeStruct(pool_w2.shape, BF16),
            jax.ShapeDtypeStruct((D_MODEL, D_MODEL), BF16),
            jax.ShapeDtypeStruct((D_MODEL, D_MODEL), BF16),
            jax.ShapeDtypeStruct((D_MODEL, D_MODEL), BF16),
        ],
        scratch_shapes=[
            pltpu.VMEM((GLA_HEADS, GLA_HEAD_V, GLA_HEAD_K), F32),
            pltpu.VMEM((GLA_KEY_WIDTH, D_MODEL), BF16),
            pltpu.VMEM((D_MODEL, D_MODEL), BF16),
            pltpu.VMEM((LR_WIDTH, D_MODEL), BF16),
        ],
        compiler_params=pltpu.CompilerParams(dimension_semantics=("arbitrary",),
                                             vmem_limit_bytes=VMEM_LIMIT_BYTES),
        name="prep_call",
    )(x2, mod, g_pre, gk_f, gk_b, gbias_b, w_in_t, w_in_t, w_in_t, w_in_t, pool_w2, w_proj_pool, w_proj_gla, w_out)


def _subtile_stages(seq_len, sub, refs):
    (x_ref, mod_ref, gpost_ref, k_ref, v_ref, lr_ref,
     wt_ref,
     poolw_ref, pscale_ref, gkcat_ref, gbf_ref, gbb_ref, glag_ref,
     wpp_ref, wpgla_ref, wout_ref, sb_ref, o_ref,
     sf_scr, hext_scr, q_scr, cf_scr, rb_scr, qs_scr, qd_scr, ke_scr, kif_scr, kib_scr,
     kv_scr, scat_scr, sc_scr, og_scr) = refs
    t = MAIN_TILE
    r0 = sub * t
    rows = slice(r0, r0 + t)
    tile_row0 = pl.program_id(0) * (MAIN_SUBTILES * t) + r0
    sb_idx = sub
    n_pairs = t // GLA_PAIR
    hk, hv = GLA_HEAD_K, GLA_HEAD_V
    gate = mod_ref[2]

    h_ext = hext_scr[r0:r0 + t + 2 * HALO, :]
    h = hext_scr[r0 + HALO:r0 + HALO + t, :]

    gbias = jnp.concatenate([gbf_ref[...], gbb_ref[...]], axis=1)
    la_pre = _dot(lr_ref[rows, :], gkcat_ref[...]) + gbias
    q_scr[rows, :] = _dot_nt(h, wt_ref[Q_ROWS, :]) * (GLA_HEAD_K ** -0.5)
    te = t + 2 * HALO
    g_row = tile_row0 - HALO + lax.broadcasted_iota(jnp.int32, (te, 1), 0)
    valid = jnp.logical_and(g_row >= 0, g_row < seq_len)
    u_ext = jnp.where(valid, _dot_nt(h_ext, wt_ref[PIN_ROWS, :]), 0.0)
    yield

    la = _log_sigmoid(la_pre)
    p_gate = _dot_nt(h, wt_ref[PGATE_ROWS, :])
    gla_gate = _dot_nt(h, wt_ref[GGATE_ROWS, :])
    u_ext_bf = u_ext.astype(BF16)
    u = u_ext[HALO:HALO + t, :]
    laf_hi, laf_lo = _split_hi_lo(la[:, :GLA_KEY_WIDTH])
    lab_hi, lab_lo = _split_hi_lo(la[:, GLA_KEY_WIDTH:])
    yield

    cb = CUMSUM_BLOCK
    row = lax.broadcasted_iota(jnp.int32, (cb, cb), 0)
    col = lax.broadcasted_iota(jnp.int32, (cb, cb), 1)
    same_chunk = (row // GLA_CHUNK) == (col // GLA_CHUNK)
    lower = jnp.where(jnp.logical_and(same_chunk, col <= row), GATE_SCALE, 0.0).astype(BF16)
    upper = jnp.where(jnp.logical_and(same_chunk, col >= row), GATE_SCALE, 0.0).astype(BF16)
    for b in range(t // cb):
        rb_rows = slice(b * cb, (b + 1) * cb)
        dst = slice(r0 + b * cb, r0 + (b + 1) * cb)
        cf_scr[dst, :] = _dot(lower, laf_hi[rb_rows]) + _dot(lower, laf_lo[rb_rows])
        rb_scr[dst, :] = _dot(upper, lab_hi[rb_rows]) + _dot(upper, lab_lo[rb_rows])
    bg_pool = _sigmoid(_dot_nt(h, wt_ref[BGP_ROWS, :]))
    bg_gla = _sigmoid(_dot_nt(h, wt_ref[BGG_ROWS, :]))
    yield

    pos = tile_row0 + lax.broadcasted_iota(jnp.int32, (t, 1), 0)
    rb_ = POOL_ROW_BLOCK
    kb_ = POOL_ROW_BLOCK + 2 * HALO
    band_r = lax.broadcasted_iota(jnp.int32, (rb_, kb_), 0)
    band_c = lax.broadcasted_iota(jnp.int32, (rb_, kb_), 1)
    band_d = band_c - HALO - band_r
    a_parts = []
    for gi, w in enumerate(POOL_WINDOWS):
        hw = w // 2
        cs = slice(gi * POOL_GROUP_WIDTH, (gi + 1) * POOL_GROUP_WIDTH)
        ws = slice(gi * POOL_GROUP_WIDTH, (gi + 1) * POOL_GROUP_WIDTH)
        band = jnp.logical_and(band_d >= -hw, band_d < hw).astype(BF16)
        wsum = jnp.concatenate(
            [_dot(band, u_ext_bf[b * rb_:b * rb_ + kb_, cs]) for b in range(t // rb_)], axis=0)
        count = (jnp.minimum(pos + hw, seq_len) - jnp.maximum(pos - hw, 0)).astype(F32)
        pooled = wsum / count - u[:, cs]
        a_parts.append(_dot(pooled.astype(BF16), poolw_ref[ws, :]))
    a = jnp.concatenate(a_parts, axis=1) * pscale_ref[...]
    z_pool = (a * _silu(p_gate)).astype(BF16)

    def put_heads(dst, rs, fwd, bwd):
        for hd in range(GLA_HEADS):
            ks = slice(hd * hk, (hd + 1) * hk)
            dst[rs, 2 * hd * hk:(2 * hd + 1) * hk] = fwd[:, ks].astype(BF16)
            dst[rs, (2 * hd + 1) * hk:(2 * hd + 2) * hk] = bwd[:, ks].astype(BF16)

    dec_f, dec_b = [], []
    for p in range(n_pairs):
        c0 = r0 + p * GLA_PAIR
        c1 = c0 + GLA_CHUNK
        ra = slice(c0, c1)
        rb = slice(c1, c1 + GLA_CHUNK)
        q0, q1 = q_scr[ra, :], q_scr[rb, :]
        k0, k1 = k_ref[ra, :], k_ref[rb, :]
        cum0, cum1 = cf_scr[ra, :], cf_scr[rb, :]
        tot0 = cf_scr[c1 - 1:c1, :]
        tot1 = cf_scr[c1 + GLA_CHUNK - 1:c1 + GLA_CHUNK, :]
        rc0, rc1 = rb_scr[ra, :], rb_scr[rb, :]
        tb0 = rb_scr[c0:c0 + 1, :]
        tb1 = rb_scr[c1:c1 + 1, :]
        e_tot0, e_tot1 = jnp.exp(tot0), jnp.exp(tot1)
        e_tb0, e_tb1 = jnp.exp(tb0), jnp.exp(tb1)

        a0 = cum0 - tot0
        ks_f0 = k0 * jnp.exp(-a0)
        qs_f1 = q1 * jnp.exp(cum1)
        qs_b0 = q0 * jnp.exp(rc0)
        put_heads(qs_scr, ra, q0 * jnp.exp(a0), qs_b0)
        put_heads(qd_scr, ra, q0 * jnp.exp(cum0), qs_b0 * e_tb1)
        kif_scr[ra, :] = ks_f0.astype(BF16)
        kif_scr[rb, :] = (k1 * jnp.exp(-cum1)).astype(BF16)
        a1 = rc1 - tb1
        ks_b1 = k1 * jnp.exp(-a1)
        put_heads(qs_scr, rb, qs_f1, q1 * jnp.exp(a1))
        put_heads(qd_scr, rb, qs_f1 * e_tot0, q1 * jnp.exp(rc1))
        kib_scr[ra, :] = (k0 * jnp.exp(-rc0)).astype(BF16)
        kib_scr[rb, :] = ks_b1.astype(BF16)
        put_heads(ke_scr, ra, ks_f0 * e_tot1, k0 * jnp.exp(tb0 - rc0))
        put_heads(ke_scr, rb, k1 * jnp.exp(tot1 - cum1), ks_b1 * e_tb0)
        dec_f.append(e_tot0 * e_tot1)
        dec_b.append(e_tb0 * e_tb1)
    yield

    merged = bg_pool * _dot(z_pool, wpp_ref[...])
    for hd in range(GLA_HEADS):
        vs = slice(hd * hv, (hd + 1) * hv)
        cat = slice(2 * hd * hk, (2 * hd + 2) * hk)
        for p in range(n_pairs):
            rs = slice(r0 + p * GLA_PAIR, r0 + (p + 1) * GLA_PAIR)
            kv_scr[sub, hd, p] = _dot_tn(v_ref[rs, vs], ke_scr[rs, cat])
    yield

    prow = lax.broadcasted_iota(jnp.int32, (GLA_PAIR, GLA_PAIR), 0)
    pcol = lax.broadcasted_iota(jnp.int32, (GLA_PAIR, GLA_PAIR), 1)
    mask_f = pcol <= prow
    for p in range(n_pairs):
        rs = slice(r0 + p * GLA_PAIR, r0 + (p + 1) * GLA_PAIR)
        for hd in range(GLA_HEADS):
            ks = slice(hd * hk, (hd + 1) * hk)
            sc_f = _dot_nt(qs_scr[rs, 2 * hd * hk:(2 * hd + 1) * hk], kif_scr[rs, ks])
            sc_b = _dot_nt(qs_scr[rs, (2 * hd + 1) * hk:(2 * hd + 2) * hk], kib_scr[rs, ks])
            sc_scr[sub, p, hd] = jnp.where(mask_f, sc_f, sc_b).astype(BF16)
    for hd in range(GLA_HEADS):
        ks = slice(hd * hk, (hd + 1) * hk)
        sf = sf_scr[hd]
        for p in range(n_pairs):
            scat_scr[sub, p, hd, 0:hk, :] = sf.T.astype(BF16)
            sf = sf * dec_f[p][:, ks] + kv_scr[sub, hd, p, :, 0:hk]
        sf_scr[hd] = sf
        sb = sb_ref[sb_idx, hd]
        for p in reversed(range(n_pairs)):
            scat_scr[sub, p, hd, hk:2 * hk, :] = sb.T.astype(BF16)
            if p > 0:
                sb = sb * dec_b[p][:, ks] + kv_scr[sub, hd, p, :, hk:2 * hk]
    yield

    for p in range(n_pairs):
        rs = slice(r0 + p * GLA_PAIR, r0 + (p + 1) * GLA_PAIR)
        for hd in range(GLA_HEADS):
            vs = slice(hd * hv, (hd + 1) * hv)
            cat = slice(2 * hd * hk, (2 * hd + 2) * hk)
            og_scr[rs, vs] = (_dot(sc_scr[sub, p, hd], v_ref[rs, vs])
                              + _dot(qd_scr[rs, cat], scat_scr[sub, p, hd]))
    yield

    glag = glag_ref[...]
    z_parts = []
    for hd in range(GLA_HEADS):
        vs = slice(hd * hv, (hd + 1) * hv)
        oh = og_scr[rows, vs]
        ms = jnp.mean(oh * oh, axis=-1, keepdims=True)
        z_parts.append((oh * lax.rsqrt(ms + RMS_EPS)) * glag)
    o_n = jnp.concatenate(z_parts, axis=1)
    y_gla = _dot((o_n * _silu(gla_gate)).astype(BF16), wpgla_ref[...])
    merged = merged + bg_gla * y_gla
    yield

    out = _dot(merged.astype(BF16), wout_ref[...])
    yield

    ms = jnp.mean(out * out, axis=-1, keepdims=True)
    out_n = (out * lax.rsqrt(ms + RMS_EPS)) * gpost_ref[...]
    o_ref[rows, :] = x_ref[rows, :] + gate * out_n
    yield


def _main_kernel(seq_len,
                 x_ref, h_ref, hp_ref, hn_ref, mod_ref, gpost_ref, k_ref, v_ref, lr_ref,
                 wt_ref,
                 poolw_ref, pscale_ref, gkcat_ref, gbf_ref, gbb_ref, glag_ref,
                 wpp_ref, wpgla_ref, wout_ref, sb_ref,
                 o_ref,
                 sf_scr, hext_scr, q_scr, cf_scr, rb_scr, qs_scr, qd_scr, ke_scr, kif_scr, kib_scr,
                 kv_scr, scat_scr, sc_scr, og_scr):
    tb = MAIN_SUBTILES * MAIN_TILE

    @pl.when(pl.program_id(0) == 0)
    def _():
        sf_scr[...] = jnp.zeros_like(sf_scr)

    hext_scr[0:HALO, :] = hp_ref[...]
    hext_scr[HALO:HALO + tb, :] = h_ref[...]
    hext_scr[HALO + tb:HALO + tb + HALO, :] = hn_ref[...]

    refs = (x_ref, mod_ref, gpost_ref, k_ref, v_ref, lr_ref,
            wt_ref,
            poolw_ref, pscale_ref, gkcat_ref, gbf_ref, gbb_ref, glag_ref,
            wpp_ref, wpgla_ref, wout_ref, sb_ref, o_ref,
            sf_scr, hext_scr, q_scr, cf_scr, rb_scr, qs_scr, qd_scr, ke_scr, kif_scr, kib_scr,
            kv_scr, scat_scr, sc_scr, og_scr)
    gens = [_subtile_stages(seq_len, sub, refs) for sub in range(MAIN_SUBTILES)]
    _run_round_robin(gens)


def _main_call(x2, h_all, mod, g_post, k_all, v_all, lr_all, w_bf, small, wpool_bf, wpp_bf, wpgla_bf, wout_bf, sb):
    s = x2.shape[0]
    tb = MAIN_SUBTILES * MAIN_TILE
    nt = s // tb
    hb = tb // HALO
    n_hb = s // HALO
    n_pairs = MAIN_TILE // GLA_PAIR
    const2 = lambda i: (0, 0)
    tile = lambda i: (i, 0)
    one = pl.Buffered(1)
    pool_scale, gk_cat, gbias_f, gbias_b, gla_norm_g = small

    def resident(arr):
        return pl.BlockSpec(arr.shape, const2, pipeline_mode=one)

    in_specs = [
        pl.BlockSpec((tb, D_MODEL), tile),
        pl.BlockSpec((tb, D_MODEL), tile),
        pl.BlockSpec((HALO, D_MODEL), lambda i: (jnp.maximum(i * hb - 1, 0), 0)),
        pl.BlockSpec((HALO, D_MODEL), lambda i: (jnp.minimum((i + 1) * hb, n_hb - 1), 0)),
        pl.BlockSpec((3, 1, D_MODEL), lambda i: (0, 0, 0)),
        pl.BlockSpec((1, D_MODEL), const2),
        pl.BlockSpec((tb, GLA_KEY_WIDTH), tile),
        pl.BlockSpec((tb, D_MODEL), tile),
        pl.BlockSpec((tb, LR_WIDTH), tile),
        resident(w_bf),
        resident(wpool_bf), resident(pool_scale), resident(gk_cat), resident(gbias_f), resident(gbias_b),
        resident(gla_norm_g),
        resident(wpp_bf), resident(wpgla_bf), resident(wout_bf),
        pl.BlockSpec((MAIN_SUBTILES, GLA_HEADS, GLA_HEAD_V, GLA_HEAD_K), lambda i: (i, 0, 0, 0)),
    ]
    scratch = [
        pltpu.VMEM((GLA_HEADS, GLA_HEAD_V, GLA_HEAD_K), F32),
        pltpu.VMEM((tb + 2 * HALO, D_MODEL), BF16),
        pltpu.VMEM((tb, GLA_KEY_WIDTH), F32),
        pltpu.VMEM((tb, GLA_KEY_WIDTH), F32),
        pltpu.VMEM((tb, GLA_KEY_WIDTH), F32),
        pltpu.VMEM((tb, 2 * GLA_KEY_WIDTH), BF16),
        pltpu.VMEM((tb, 2 * GLA_KEY_WIDTH), BF16),
        pltpu.VMEM((tb, 2 * GLA_KEY_WIDTH), BF16),
        pltpu.VMEM((tb, GLA_KEY_WIDTH), BF16),
        pltpu.VMEM((tb, GLA_KEY_WIDTH), BF16),
        pltpu.VMEM((MAIN_SUBTILES, GLA_HEADS, n_pairs, GLA_HEAD_V, 2 * GLA_HEAD_K), F32),
        pltpu.VMEM((MAIN_SUBTILES, n_pairs, GLA_HEADS, 2 * GLA_HEAD_K, GLA_HEAD_V), BF16),
        pltpu.VMEM((MAIN_SUBTILES, n_pairs, GLA_HEADS, GLA_PAIR, GLA_PAIR), BF16),
        pltpu.VMEM((tb, D_MODEL), F32),
    ]
    return pl.pallas_call(
        functools.partial(_main_kernel, s),
        grid=(nt,),
        in_specs=in_specs,
        out_specs=pl.BlockSpec((tb, D_MODEL), tile),
        out_shape=jax.ShapeDtypeStruct((s, D_MODEL), F32),
        scratch_shapes=scratch,
        compiler_params=pltpu.CompilerParams(dimension_semantics=("arbitrary",),
                                             vmem_limit_bytes=VMEM_LIMIT_BYTES),
        name="main_call",
    )(x2, h_all, h_all, h_all, mod, g_post, k_all, v_all, lr_all,
      w_bf,
      wpool_bf, pool_scale, gk_cat, gbias_f, gbias_b, gla_norm_g, wpp_bf, wpgla_bf, wout_bf, sb)


def _layer(x2, c, w_ada, b_ada, g_pre, g_post, w_in, pool_w, pool_scale, gk_up_fwd, gk_bias_fwd,
           gk_up_bwd, gk_bias_bwd, gla_norm_g, w_proj_pool, w_proj_gla, w_out):
    mod = _ada_call(c.reshape(D_MODEL, 1), w_ada, b_ada.reshape(1, -1))
    gbias_f = gk_bias_fwd.reshape(1, -1)
    gbias_b = gk_bias_bwd.reshape(1, -1)

    pool_w2 = pool_w.reshape(POOL_GROUPS * POOL_GROUP_WIDTH, POOL_GROUP_WIDTH)
    sb, k_all, v_all, h_all, lr_all, gk_cat, w_bf, wpool_bf, wpp_bf, wpgla_bf, wout_bf = _prep_call(
        x2, mod, g_pre.reshape(1, -1), gk_up_fwd, gk_up_bwd, gbias_b, w_in.T, pool_w2, w_proj_pool, w_proj_gla,
        w_out)

    small = (pool_scale.reshape(1, -1), gk_cat, gbias_f, gbias_b, gla_norm_g.reshape(1, -1))
    return _main_call(x2, h_all, mod, g_post.reshape(1, -1), k_all, v_all, lr_all, w_bf, small,
                      wpool_bf, wpp_bf, wpgla_bf, wout_bf, sb)


def kernel(x, c, w_ada, b_ada, g_pre, g_post, w_in, pool_w, pool_scale, gk_up_fwd, gk_bias_fwd, gk_up_bwd, gk_bias_bwd, gla_norm_g, w_proj_pool, w_proj_gla, w_out):
    b, s, d = x.shape
    depth = w_in.shape[0]
    xf = x.reshape(b * s, d)
    outs = []
    for bi in range(b):
        xb = xf if b == 1 else lax.slice_in_dim(xf, bi * s, (bi + 1) * s, axis=0)
        for l in range(depth):
            xb = _layer(xb, c[bi:bi + 1], w_ada[l], b_ada[l], g_pre[l], g_post[l], w_in[l], pool_w[l],
                        pool_scale[l], gk_up_fwd[l], gk_bias_fwd[l], gk_up_bwd[l], gk_bias_bwd[l],
                        gla_norm_g[l], w_proj_pool[l], w_proj_gla[l], w_out[l])
        outs.append(xb)
    out = outs[0] if b == 1 else jnp.concatenate(outs, axis=0)
    return out.reshape(b, s, d)
```

```python
import functools
import math

import jax
import jax.numpy as jnp
from jax import lax
from jax.experimental import pallas as pl
from jax.experimental.pallas import tpu as pltpu

D_MODEL = 1024
POOL_GROUPS = 4
POOL_GROUP_WIDTH = 256
POOL_WINDOWS = (2, 4, 8, 16)
GLA_HEADS = 4
GLA_KEY_WIDTH = 512
GLA_HEAD_K = 128
GLA_HEAD_V = 256
GLA_GATE_RANK = 16
GLA_GATE_NORMALIZER = 16.0
GATE_SCALE = 1.0 / GLA_GATE_NORMALIZER
assert math.frexp(GLA_GATE_NORMALIZER)[0] == 0.5
GLA_CHUNK = 64
GLA_PAIR = 2 * GLA_CHUNK
RMS_EPS = 1e-6
IN_SPLITS = (1024, 1024, 512, 512, 1024, 1024, 16, 16, 1024, 1024)
IN_WIDTH = sum(IN_SPLITS)

LANES = 128
SUB_TILE = 256
PREP_SUBTILES = 4
MAIN_TILE = 512
MAIN_SUBTILES = 1
CUMSUM_BLOCK = 256
HALO = 16
POOL_ROW_BLOCK = 128
VMEM_LIMIT_BYTES = 60 * 1024 * 1024

_OFFS = [sum(IN_SPLITS[:j]) for j in range(len(IN_SPLITS) + 1)]
(PIN_ROWS, PGATE_ROWS, Q_ROWS, K_ROWS, V_ROWS, GGATE_ROWS) = (slice(_OFFS[j], _OFFS[j + 1]) for j in range(6))
LR_ROWS = slice(_OFFS[6], _OFFS[8])
BGP_ROWS = slice(_OFFS[8], _OFFS[9])
BGG_ROWS = slice(_OFFS[9], _OFFS[10])
LR_WIDTH = 2 * GLA_GATE_RANK
WT_SLAB = 480

F32 = jnp.float32
BF16 = jnp.bfloat16


def _dot(a, b):
    return jnp.dot(a, b, preferred_element_type=F32)


def _dot_nt(a, b):
    return lax.dot_general(a, b, (((1,), (1,)), ((), ())), preferred_element_type=F32)


def _dot_tn(a, b):
    return lax.dot_general(a, b, (((0,), (0,)), ((), ())), preferred_element_type=F32)


def _sigmoid(x):
    return 0.5 * jnp.tanh(0.5 * x) + 0.5


def _silu(x):
    return x * _sigmoid(x)


def _log_sigmoid(x):
    return jnp.minimum(x, 0.0) - jnp.log(1.0 + jnp.exp(-jnp.abs(x)))


def _norm_mod(xv, g_pre, scale, shift):
    ms = jnp.mean(xv * xv, axis=-1, keepdims=True)
    gain = g_pre * (1.0 + scale)
    return (xv * lax.rsqrt(ms + RMS_EPS)) * gain + shift


def _split_hi_lo(a):
    hi = a.astype(BF16)
    lo = (a - hi.astype(F32)).astype(BF16)
    return hi, lo


def _run_round_robin(gens):
    live = list(gens)
    while live:
        for g in list(live):
            try:
                next(g)
            except StopIteration:
                live.remove(g)


def _ada_kernel(c_ref, w_ref, b_ref, o_ref):
    s = _silu(c_ref[...])
    for j in range(3):
        cols = slice(j * D_MODEL, (j + 1) * D_MODEL)
        o_ref[j] = jnp.sum(w_ref[:, cols] * s, axis=0, keepdims=True) + b_ref[:, cols]


def _ada_call(c_col, w_ada, b_ada):
    return pl.pallas_call(
        _ada_kernel,
        grid=(1,),
        in_specs=[
            pl.BlockSpec((D_MODEL, 1), lambda j: (0, 0)),
            pl.BlockSpec((D_MODEL, 3 * D_MODEL), lambda j: (0, 0), pipeline_mode=pl.Buffered(1)),
            pl.BlockSpec((1, 3 * D_MODEL), lambda j: (0, 0)),
        ],
        out_specs=pl.BlockSpec((3, 1, D_MODEL), lambda j: (0, 0, 0)),
        out_shape=jax.ShapeDtypeStruct((3, 1, D_MODEL), F32),
        compiler_params=pltpu.CompilerParams(dimension_semantics=("arbitrary",)),
        name="ada_call",
    )(c_col, w_ada, b_ada)


def _prep_stages(sub, refs):
    (gkcat_ref, gbias_ref, sb_ref, k_ref, v_ref, lr_ref, st_scr) = refs
    t = SUB_TILE
    rows = slice(sub * t, (sub + 1) * t)
    k = k_ref[rows, :]
    v_bf = v_ref[rows, :]

    gk_bwd = gkcat_ref[:, GLA_KEY_WIDTH:]
    la = _log_sigmoid(_dot(lr_ref[rows, :], gk_bwd) + gbias_ref[...])
    la_hi, la_lo = _split_hi_lo(la)
    yield

    row = lax.broadcasted_iota(jnp.int32, (t, t), 0)
    col = lax.broadcasted_iota(jnp.int32, (t, t), 1)
    upper = jnp.where(col >= row, GATE_SCALE, 0.0).astype(BF16)
    rc = _dot(upper, la_hi) + _dot(upper, la_lo)
    yield

    tot = rc[0:1, :]
    k_end = (k * jnp.exp(tot - rc)).astype(BF16)
    decay = jnp.exp(tot)
    kvs = []
    for hd in range(GLA_HEADS):
        ks = slice(hd * GLA_HEAD_K, (hd + 1) * GLA_HEAD_K)
        vs = slice(hd * GLA_HEAD_V, (hd + 1) * GLA_HEAD_V)
        kvs.append(_dot_tn(v_bf[:, vs], k_end[:, ks]))
    yield

    per_main = MAIN_TILE // SUB_TILE
    for hd in range(GLA_HEADS):
        ks = slice(hd * GLA_HEAD_K, (hd + 1) * GLA_HEAD_K)
        st = st_scr[hd]
        if (sub + 1) % per_main == 0:
            sb_ref[sub // per_main, hd] = st
        st_scr[hd] = st * decay[:, ks] + kvs[hd]
    yield


def _prep_kernel(x_ref, mod_ref, gpre_ref, gkf_ref, gkb_ref, gbias_ref, wk_ref, wv_ref, wlr_ref,
                 win_ref, wpool_ref, wpp_ref, wpgla_ref, wout_ref,
                 sb_ref, k_ref, v_ref, h_ref, lr_ref, gkcat_ref,
                 winbf_ref, wpoolbf_ref, wppbf_ref, wpglabf_ref, woutbf_ref,
                 st_scr, wk_scr, wv_scr, wlr_scr):
    @pl.when(pl.program_id(0) == 0)
    def _():
        st_scr[...] = jnp.zeros_like(st_scr)
        wk_scr[...] = wk_ref[...].astype(BF16)
        wv_scr[...] = wv_ref[...].astype(BF16)
        wlr_scr[...] = wlr_ref[...].astype(BF16)
        r, w = GLA_GATE_RANK, GLA_KEY_WIDTH
        gkcat_ref[...] = jnp.zeros(gkcat_ref.shape, BF16)
        gkcat_ref[0:r, 0:w] = gkf_ref[...].astype(BF16)
        gkcat_ref[r:2 * r, w:2 * w] = gkb_ref[...].astype(BF16)

    winbf_ref[...] = win_ref[...].astype(BF16)
    wpoolbf_ref[...] = wpool_ref[...].astype(BF16)
    wppbf_ref[...] = wpp_ref[...].astype(BF16)
    wpglabf_ref[...] = wpgla_ref[...].astype(BF16)
    woutbf_ref[...] = wout_ref[...].astype(BF16)

    h = _norm_mod(x_ref[...], gpre_ref[...], mod_ref[1], mod_ref[0]).astype(BF16)
    h_ref[...] = h
    k_ref[...] = _dot_nt(h, wk_scr[...])
    lr_ref[...] = _dot_nt(h, wlr_scr[...]).astype(BF16)
    v_ref[...] = _dot_nt(h, wv_scr[...]).astype(BF16)

    refs = (gkcat_ref, gbias_ref, sb_ref, k_ref, v_ref, lr_ref, st_scr)
    gens = [_prep_stages(sub, refs) for sub in reversed(range(PREP_SUBTILES))]
    _run_round_robin(gens)


def _prep_call(x2, mod, g_pre, gk_f, gk_b, gbias_b, w_in_t, pool_w2, w_proj_pool, w_proj_gla, w_out):
    s = x2.shape[0]
    tb = PREP_SUBTILES * SUB_TILE
    nb = s // tb
    slab = D_MODEL // nb
    n_wt = IN_WIDTH // WT_SLAB
    const = lambda i: (0, 0)
    rev = lambda i: (nb - 1 - i, 0)
    fwd = lambda i: (i, 0)
    wt_slab = lambda i: (jnp.minimum(i, n_wt - 1), 0)
    one = pl.Buffered(1)

    def rows_of(rs):
        n = rs.stop - rs.start
        return pl.BlockSpec((n, D_MODEL), lambda i: (rs.start // n, 0), pipeline_mode=one)

    return pl.pallas_call(
        _prep_kernel,
        grid=(nb,),
        in_specs=[
            pl.BlockSpec((tb, D_MODEL), rev),
            pl.BlockSpec((3, 1, D_MODEL), lambda i: (0, 0, 0)),
            pl.BlockSpec((1, D_MODEL), const),
            pl.BlockSpec(gk_f.shape, const),
            pl.BlockSpec(gk_b.shape, const),
            pl.BlockSpec(gbias_b.shape, const),
            rows_of(K_ROWS),
            rows_of(V_ROWS),
            rows_of(LR_ROWS),
            pl.BlockSpec((WT_SLAB, D_MODEL), wt_slab),
            pl.BlockSpec((slab, POOL_GROUP_WIDTH), fwd),
            pl.BlockSpec((slab, D_MODEL), fwd),
            pl.BlockSpec((slab, D_MODEL), fwd),
            pl.BlockSpec((slab, D_MODEL), fwd),
        ],
        out_specs=[
            pl.BlockSpec((tb // MAIN_TILE, GLA_HEADS, GLA_HEAD_V, GLA_HEAD_K), lambda i: (nb - 1 - i, 0, 0, 0)),
            pl.BlockSpec((tb, GLA_KEY_WIDTH), rev),
            pl.BlockSpec((tb, D_MODEL), rev),
            pl.BlockSpec((tb, D_MODEL), rev),
            pl.BlockSpec((tb, LR_WIDTH), rev),
            pl.BlockSpec((LR_WIDTH, 2 * GLA_KEY_WIDTH), const),
            pl.BlockSpec((WT_SLAB, D_MODEL), wt_slab),
            pl.BlockSpec((slab, POOL_GROUP_WIDTH), fwd),
            pl.BlockSpec((slab, D_MODEL), fwd),
            pl.BlockSpec((slab, D_MODEL), fwd),
            pl.BlockSpec((slab, D_MODEL), fwd),
        ],
        out_shape=[
            jax.ShapeDtypeStruct((s // MAIN_TILE, GLA_HEADS, GLA_HEAD_V, GLA_HEAD_K), F32),
            jax.ShapeDtypeStruct((s, GLA_KEY_WIDTH), F32),
            jax.ShapeDtypeStruct((s, D_MODEL), BF16),
            jax.ShapeDtypeStruct((s, D_MODEL), BF16),
            jax.ShapeDtypeStruct((s, LR_WIDTH), BF16),
            jax.ShapeDtypeStruct((LR_WIDTH, 2 * GLA_KEY_WIDTH), BF16),
            jax.ShapeDtypeStruct((IN_WIDTH, D_MODEL), BF16),
            jax.ShapeDtypeStruct(pool_w2.shape, BF16),
            jax.ShapeDtypeStruct((D_MODEL, D_MODEL), BF16),
            jax.ShapeDtypeStruct((D_MODEL, D_MODEL), BF16),
            jax.ShapeDtypeStruct((D_MODEL, D_MODEL), BF16),
        ],
        scratch_shapes=[
            pltpu.VMEM((GLA_HEADS, GLA_HEAD_V, GLA_HEAD_K), F32),
            pltpu.VMEM((GLA_KEY_WIDTH, D_MODEL), BF16),
            pltpu.VMEM((D_MODEL, D_MODEL), BF16),
            pltpu.VMEM((LR_WIDTH, D_MODEL), BF16),
        ],
        compiler_params=pltpu.CompilerParams(dimension_semantics=("arbitrary",),
                                             vmem_limit_bytes=VMEM_LIMIT_BYTES),
        name="prep_call",
    )(x2, mod, g_pre, gk_f, gk_b, gbias_b, w_in_t, w_in_t, w_in_t, w_in_t, pool_w2, w_proj_pool, w_proj_gla, w_out)


def _subtile_stages(seq_len, sub, refs):
    (x_ref, mod_ref, gpost_ref, k_ref, v_ref, lr_ref,
     wt_ref,
     poolw_ref, pscale_ref, gkcat_ref, gbf_ref, gbb_ref, glag_ref,
     wpp_ref, wpgla_ref, wout_ref, sb_ref, o_ref,
     sf_scr, hext_scr, q_scr, cf_scr, rb_scr, qs_scr, qd_scr, ke_scr, kif_scr, kib_scr,
     kv_scr, scat_scr, sc_scr, og_scr) = refs
    t = MAIN_TILE
    r0 = sub * t
    rows = slice(r0, r0 + t)
    tile_row0 = pl.program_id(0) * (MAIN_SUBTILES * t) + r0
    sb_idx = sub
    n_pairs = t // GLA_PAIR
    hk, hv = GLA_HEAD_K, GLA_HEAD_V
    gate = mod_ref[2]

    h_ext = hext_scr[r0:r0 + t + 2 * HALO, :]
    h = hext_scr[r0 + HALO:r0 + HALO + t, :]

    gbias = jnp.concatenate([gbf_ref[...], gbb_ref[...]], axis=1)
    la_pre = _dot(lr_ref[rows, :], gkcat_ref[...]) + gbias
    q_scr[rows, :] = _dot_nt(h, wt_ref[Q_ROWS, :]) * (GLA_HEAD_K ** -0.5)
    te = t + 2 * HALO
    g_row = tile_row0 - HALO + lax.broadcasted_iota(jnp.int32, (te, 1), 0)
    valid = jnp.logical_and(g_row >= 0, g_row < seq_len)
    u_ext = jnp.where(valid, _dot_nt(h_ext, wt_ref[PIN_ROWS, :]), 0.0)
    yield

    la = _log_sigmoid(la_pre)
    p_gate = _dot_nt(h, wt_ref[PGATE_ROWS, :])
    gla_gate = _dot_nt(h, wt_ref[GGATE_ROWS, :])
    u_ext_bf = u_ext.astype(BF16)
    u = u_ext[HALO:HALO + t, :]
    laf_hi, laf_lo = _split_hi_lo(la[:, :GLA_KEY_WIDTH])
    lab_hi, lab_lo = _split_hi_lo(la[:, GLA_KEY_WIDTH:])
    yield

    cb = CUMSUM_BLOCK
    row = lax.broadcasted_iota(jnp.int32, (cb, cb), 0)
    col = lax.broadcasted_iota(jnp.int32, (cb, cb), 1)
    same_chunk = (row // GLA_CHUNK) == (col // GLA_CHUNK)
    lower = jnp.where(jnp.logical_and(same_chunk, col <= row), GATE_SCALE, 0.0).astype(BF16)
    upper = jnp.where(jnp.logical_and(same_chunk, col >= row), GATE_SCALE, 0.0).astype(BF16)
    for b in range(t // cb):
        rb_rows = slice(b * cb, (b + 1) * cb)
        dst = slice(r0 + b * cb, r0 + (b + 1) * cb)
        cf_scr[dst, :] = _dot(lower, laf_hi[rb_rows]) + _dot(lower, laf_lo[rb_rows])
        rb_scr[dst, :] = _dot(upper, lab_hi[rb_rows]) + _dot(upper, lab_lo[rb_rows])
    bg_pool = _sigmoid(_dot_nt(h, wt_ref[BGP_ROWS, :]))
    bg_gla = _sigmoid(_dot_nt(h, wt_ref[BGG_ROWS, :]))
    yield

    pos = tile_row0 + lax.broadcasted_iota(jnp.int32, (t, 1), 0)
    rb_ = POOL_ROW_BLOCK
    kb_ = POOL_ROW_BLOCK + 2 * HALO
    band_r = lax.broadcasted_iota(jnp.int32, (rb_, kb_), 0)
    band_c = lax.broadcasted_iota(jnp.int32, (rb_, kb_), 1)
    band_d = band_c - HALO - band_r
    a_parts = []
    for gi, w in enumerate(POOL_WINDOWS):
        hw = w // 2
        cs = slice(gi * POOL_GROUP_WIDTH, (gi + 1) * POOL_GROUP_WIDTH)
        ws = slice(gi * POOL_GROUP_WIDTH, (gi + 1) * POOL_GROUP_WIDTH)
        band = jnp.logical_and(band_d >= -hw, band_d < hw).astype(BF16)
        wsum = jnp.concatenate(
            [_dot(band, u_ext_bf[b * rb_:b * rb_ + kb_, cs]) for b in range(t // rb_)], axis=0)
        count = (jnp.minimum(pos + hw, seq_len) - jnp.maximum(pos - hw, 0)).astype(F32)
        pooled = wsum / count - u[:, cs]
        a_parts.append(_dot(pooled.astype(BF16), poolw_ref[ws, :]))
    a = jnp.concatenate(a_parts, axis=1) * pscale_ref[...]
    z_pool = (a * _silu(p_gate)).astype(BF16)

    def put_heads(dst, rs, fwd, bwd):
        for hd in range(GLA_HEADS):
            ks = slice(hd * hk, (hd + 1) * hk)
            dst[rs, 2 * hd * hk:(2 * hd + 1) * hk] = fwd[:, ks].astype(BF16)
            dst[rs, (2 * hd + 1) * hk:(2 * hd + 2) * hk] = bwd[:, ks].astype(BF16)

    dec_f, dec_b = [], []
    for p in range(n_pairs):
        c0 = r0 + p * GLA_PAIR
        c1 = c0 + GLA_CHUNK
        ra = slice(c0, c1)
        rb = slice(c1, c1 + GLA_CHUNK)
        q0, q1 = q_scr[ra, :], q_scr[rb, :]
        k0, k1 = k_ref[ra, :], k_ref[rb, :]
        cum0, cum1 = cf_scr[ra, :], cf_scr[rb, :]
        tot0 = cf_scr[c1 - 1:c1, :]
        tot1 = cf_scr[c1 + GLA_CHUNK - 1:c1 + GLA_CHUNK, :]
        rc0, rc1 = rb_scr[ra, :], rb_scr[rb, :]
        tb0 = rb_scr[c0:c0 + 1, :]
        tb1 = rb_scr[c1:c1 + 1, :]
        e_tot0, e_tot1 = jnp.exp(tot0), jnp.exp(tot1)
        e_tb0, e_tb1 = jnp.exp(tb0), jnp.exp(tb1)

        a0 = cum0 - tot0
        ks_f0 = k0 * jnp.exp(-a0)
        qs_f1 = q1 * jnp.exp(cum1)
        qs_b0 = q0 * jnp.exp(rc0)
        put_heads(qs_scr, ra, q0 * jnp.exp(a0), qs_b0)
        put_heads(qd_scr, ra, q0 * jnp.exp(cum0), qs_b0 * e_tb1)
        kif_scr[ra, :] = ks_f0.astype(BF16)
        kif_scr[rb, :] = (k1 * jnp.exp(-cum1)).astype(BF16)
        a1 = rc1 - tb1
        ks_b1 = k1 * jnp.exp(-a1)
        put_heads(qs_scr, rb, qs_f1, q1 * jnp.exp(a1))
        put_heads(qd_scr, rb, qs_f1 * e_tot0, q1 * jnp.exp(rc1))
        kib_scr[ra, :] = (k0 * jnp.exp(-rc0)).astype(BF16)
        kib_scr[rb, :] = ks_b1.astype(BF16)
        put_heads(ke_scr, ra, ks_f0 * e_tot1, k0 * jnp.exp(tb0 - rc0))
        put_heads(ke_scr, rb, k1 * jnp.exp(tot1 - cum1), ks_b1 * e_tb0)
        dec_f.append(e_tot0 * e_tot1)
        dec_b.append(e_tb0 * e_tb1)
    yield

    merged = bg_pool * _dot(z_pool, wpp_ref[...])
    for hd in range(GLA_HEADS):
        vs = slice(hd * hv, (hd + 1) * hv)
        cat = slice(2 * hd * hk, (2 * hd + 2) * hk)
        for p in range(n_pairs):
            rs = slice(r0 + p * GLA_PAIR, r0 + (p + 1) * GLA_PAIR)
            kv_scr[sub, hd, p] = _dot_tn(v_ref[rs, vs], ke_scr[rs, cat])
    yield

    prow = lax.broadcasted_iota(jnp.int32, (GLA_PAIR, GLA_PAIR), 0)
    pcol = lax.broadcasted_iota(jnp.int32, (GLA_PAIR, GLA_PAIR), 1)
    mask_f = pcol <= prow
    for p in range(n_pairs):
        rs = slice(r0 + p * GLA_PAIR, r0 + (p + 1) * GLA_PAIR)
        for hd in range(GLA_HEADS):
            ks = slice(hd * hk, (hd + 1) * hk)
            sc_f = _dot_nt(qs_scr[rs, 2 * hd * hk:(2 * hd + 1) * hk], kif_scr[rs, ks])
            sc_b = _dot_nt(qs_scr[rs, (2 * hd + 1) * hk:(2 * hd + 2) * hk], kib_scr[rs, ks])
            sc_scr[sub, p, hd] = jnp.where(mask_f, sc_f, sc_b).astype(BF16)
    for hd in range(GLA_HEADS):
        ks = slice(hd * hk, (hd + 1) * hk)
        sf = sf_scr[hd]
        for p in range(n_pairs):
            scat_scr[sub, p, hd, 0:hk, :] = sf.T.astype(BF16)
            sf = sf * dec_f[p][:, ks] + kv_scr[sub, hd, p, :, 0:hk]
        sf_scr[hd] = sf
        sb = sb_ref[sb_idx, hd]
        for p in reversed(range(n_pairs)):
            scat_scr[sub, p, hd, hk:2 * hk, :] = sb.T.astype(BF16)
            if p > 0:
                sb = sb * dec_b[p][:, ks] + kv_scr[sub, hd, p, :, hk:2 * hk]
    yield

    for p in range(n_pairs):
        rs = slice(r0 + p * GLA_PAIR, r0 + (p + 1) * GLA_PAIR)
        for hd in range(GLA_HEADS):
            vs = slice(hd * hv, (hd + 1) * hv)
            cat = slice(2 * hd * hk, (2 * hd + 2) * hk)
            og_scr[rs, vs] = (_dot(sc_scr[sub, p, hd], v_ref[rs, vs])
                              + _dot(qd_scr[rs, cat], scat_scr[sub, p, hd]))
    yield

    glag = glag_ref[...]
    z_parts = []
    for hd in range(GLA_HEADS):
        vs = slice(hd * hv, (hd + 1) * hv)
        oh = og_scr[rows, vs]
        ms = jnp.mean(oh * oh, axis=-1, keepdims=True)
        z_parts.append((oh * lax.rsqrt(ms + RMS_EPS)) * glag)
    o_n = jnp.concatenate(z_parts, axis=1)
    y_gla = _dot((o_n * _silu(gla_gate)).astype(BF16), wpgla_ref[...])
    merged = merged + bg_gla * y_gla
    yield

    out = _dot(merged.astype(BF16), wout_ref[...])
    yield

    ms = jnp.mean(out * out, axis=-1, keepdims=True)
    gain = gate * gpost_ref[...]
    o_ref[rows, :] = x_ref[rows, :] + (out * lax.rsqrt(ms + RMS_EPS)) * gain
    yield


def _main_kernel(seq_len,
                 x_ref, h_ref, hp_ref, hn_ref, mod_ref, gpost_ref, k_ref, v_ref, lr_ref,
                 wt_ref,
                 poolw_ref, pscale_ref, gkcat_ref, gbf_ref, gbb_ref, glag_ref,
                 wpp_ref, wpgla_ref, wout_ref, sb_ref,
                 o_ref,
                 sf_scr, hext_scr, q_scr, cf_scr, rb_scr, qs_scr, qd_scr, ke_scr, kif_scr, kib_scr,
                 kv_scr, scat_scr, sc_scr, og_scr):
    tb = MAIN_SUBTILES * MAIN_TILE

    @pl.when(pl.program_id(0) == 0)
    def _():
        sf_scr[...] = jnp.zeros_like(sf_scr)

    hext_scr[0:HALO, :] = hp_ref[...]
    hext_scr[HALO:HALO + tb, :] = h_ref[...]
    hext_scr[HALO + tb:HALO + tb + HALO, :] = hn_ref[...]

    refs = (x_ref, mod_ref, gpost_ref, k_ref, v_ref, lr_ref,
            wt_ref,
            poolw_ref, pscale_ref, gkcat_ref, gbf_ref, gbb_ref, glag_ref,
            wpp_ref, wpgla_ref, wout_ref, sb_ref, o_ref,
            sf_scr, hext_scr, q_scr, cf_scr, rb_scr, qs_scr, qd_scr, ke_scr, kif_scr, kib_scr,
            kv_scr, scat_scr, sc_scr, og_scr)
    gens = [_subtile_stages(seq_len, sub, refs) for sub in range(MAIN_SUBTILES)]
    _run_round_robin(gens)


def _main_call(x2, h_all, mod, g_post, k_all, v_all, lr_all, w_bf, small, wpool_bf, wpp_bf, wpgla_bf, wout_bf, sb):
    s = x2.shape[0]
    tb = MAIN_SUBTILES * MAIN_TILE
    nt = s // tb
    hb = tb // HALO
    n_hb = s // HALO
    n_pairs = MAIN_TILE // GLA_PAIR
    const2 = lambda i: (0, 0)
    tile = lambda i: (i, 0)
    one = pl.Buffered(1)
    pool_scale, gk_cat, gbias_f, gbias_b, gla_norm_g = small

    def resident(arr):
        return pl.BlockSpec(arr.shape, const2, pipeline_mode=one)

    in_specs = [
        pl.BlockSpec((tb, D_MODEL), tile),
        pl.BlockSpec((tb, D_MODEL), tile),
        pl.BlockSpec((HALO, D_MODEL), lambda i: (jnp.maximum(i * hb - 1, 0), 0)),
        pl.BlockSpec((HALO, D_MODEL), lambda i: (jnp.minimum((i + 1) * hb, n_hb - 1), 0)),
        pl.BlockSpec((3, 1, D_MODEL), lambda i: (0, 0, 0)),
        pl.BlockSpec((1, D_MODEL), const2),
        pl.BlockSpec((tb, GLA_KEY_WIDTH), tile),
        pl.BlockSpec((tb, D_MODEL), tile),
        pl.BlockSpec((tb, LR_WIDTH), tile),
        resident(w_bf),
        resident(wpool_bf), resident(pool_scale), resident(gk_cat), resident(gbias_f), resident(gbias_b),
        resident(gla_norm_g),
        resident(wpp_bf), resident(wpgla_bf), resident(wout_bf),
        pl.BlockSpec((MAIN_SUBTILES, GLA_HEADS, GLA_HEAD_V, GLA_HEAD_K), lambda i: (i, 0, 0, 0)),
    ]
    scratch = [
        pltpu.VMEM((GLA_HEADS, GLA_HEAD_V, GLA_HEAD_K), F32),
        pltpu.VMEM((tb + 2 * HALO, D_MODEL), BF16),
        pltpu.VMEM((tb, GLA_KEY_WIDTH), F32),
        pltpu.VMEM((tb, GLA_KEY_WIDTH), F32),
        pltpu.VMEM((tb, GLA_KEY_WIDTH), F32),
        pltpu.VMEM((tb, 2 * GLA_KEY_WIDTH), BF16),
        pltpu.VMEM((tb, 2 * GLA_KEY_WIDTH), BF16),
        pltpu.VMEM((tb, 2 * GLA_KEY_WIDTH), BF16),
        pltpu.VMEM((tb, GLA_KEY_WIDTH), BF16),
        pltpu.VMEM((tb, GLA_KEY_WIDTH), BF16),
        pltpu.VMEM((MAIN_SUBTILES, GLA_HEADS, n_pairs, GLA_HEAD_V, 2 * GLA_HEAD_K), F32),
        pltpu.VMEM((MAIN_SUBTILES, n_pairs, GLA_HEADS, 2 * GLA_HEAD_K, GLA_HEAD_V), BF16),
        pltpu.VMEM((MAIN_SUBTILES, n_pairs, GLA_HEADS, GLA_PAIR, GLA_PAIR), BF16),
        pltpu.VMEM((tb, D_MODEL), F32),
    ]
    return pl.pallas_call(
        functools.partial(_main_kernel, s),
        grid=(nt,),
        in_specs=in_specs,
        out_specs=pl.BlockSpec((tb, D_MODEL), tile),
        out_shape=jax.ShapeDtypeStruct((s, D_MODEL), F32),
        scratch_shapes=scratch,
        compiler_params=pltpu.CompilerParams(dimension_semantics=("arbitrary",),
                                             vmem_limit_bytes=VMEM_LIMIT_BYTES),
        name="main_call",
    )(x2, h_all, h_all, h_all, mod, g_post, k_all, v_all, lr_all,
      w_bf,
      wpool_bf, pool_scale, gk_cat, gbias_f, gbias_b, gla_norm_g, wpp_bf, wpgla_bf, wout_bf, sb)


def _layer(x2, c, w_ada, b_ada, g_pre, g_post, w_in, pool_w, pool_scale, gk_up_fwd, gk_bias_fwd,
           gk_up_bwd, gk_bias_bwd, gla_norm_g, w_proj_pool, w_proj_gla, w_out):
    mod = _ada_call(c.reshape(D_MODEL, 1), w_ada, b_ada.reshape(1, -1))
    gbias_f = gk_bias_fwd.reshape(1, -1)
    gbias_b = gk_bias_bwd.reshape(1, -1)

    pool_w2 = pool_w.reshape(POOL_GROUPS * POOL_GROUP_WIDTH, POOL_GROUP_WIDTH)
    sb, k_all, v_all, h_all, lr_all, gk_cat, w_bf, wpool_bf, wpp_bf, wpgla_bf, wout_bf = _prep_call(
        x2, mod, g_pre.reshape(1, -1), gk_up_fwd, gk_up_bwd, gbias_b, w_in.T, pool_w2, w_proj_pool, w_proj_gla,
        w_out)

    small = (pool_scale.reshape(1, -1), gk_cat, gbias_f, gbias_b, gla_norm_g.reshape(1, -1))
    return _main_call(x2, h_all, mod, g_post.reshape(1, -1), k_all, v_all, lr_all, w_bf, small,
                      wpool_bf, wpp_bf, wpgla_bf, wout_bf, sb)


def kernel(x, c, w_ada, b_ada, g_pre, g_post, w_in, pool_w, pool_scale, gk_up_fwd, gk_bias_fwd, gk_up_bwd, gk_bias_bwd, gla_norm_g, w_proj_pool, w_proj_gla, w_out):
    b, s, d = x.shape
    depth = w_in.shape[0]
    xf = x.reshape(b * s, d)
    outs = []
    for bi in range(b):
        xb = xf if b == 1 else lax.slice_in_dim(xf, bi * s, (bi + 1) * s, axis=0)
        for l in range(depth):
            xb = _layer(xb, c[bi:bi + 1], w_ada[l], b_ada[l], g_pre[l], g_post[l], w_in[l], pool_w[l],
                        pool_scale[l], gk_up_fwd[l], gk_bias_fwd[l], gk_up_bwd[l], gk_bias_bwd[l],
                        gla_norm_g[l], w_proj_pool[l], w_proj_gla[l], w_out[l])
        outs.append(xb)
    out = outs[0] if b == 1 else jnp.concatenate(outs, axis=0)
    return out.reshape(b, s, d)
```

```python
import functools
import math

import jax
import jax.numpy as jnp
from jax import lax
from jax.experimental import pallas as pl
from jax.experimental.pallas import tpu as pltpu

D_MODEL = 1024
POOL_GROUPS = 4
POOL_GROUP_WIDTH = 256
POOL_WINDOWS = (2, 4, 8, 16)
GLA_HEADS = 4
GLA_KEY_WIDTH = 512
GLA_HEAD_K = 128
GLA_HEAD_V = 256
GLA_GATE_RANK = 16
GLA_GATE_NORMALIZER = 16.0
GATE_SCALE = 1.0 / GLA_GATE_NORMALIZER
assert math.frexp(GLA_GATE_NORMALIZER)[0] == 0.5
GLA_CHUNK = 64
GLA_PAIR = 2 * GLA_CHUNK
RMS_EPS = 1e-6
IN_SPLITS = (1024, 1024, 512, 512, 1024, 1024, 16, 16, 1024, 1024)
IN_WIDTH = sum(IN_SPLITS)

LANES = 128
SUBLANES = 8
SUB_TILE = 256
PREP_SUBTILES = 4
MAIN_TILE = 512
MAIN_SUBTILES = 1
CUMSUM_BLOCK = 256
HALO = 16
POOL_ROW_BLOCK = 128
VMEM_LIMIT_BYTES = 60 * 1024 * 1024

_OFFS = [sum(IN_SPLITS[:j]) for j in range(len(IN_SPLITS) + 1)]
(PIN_ROWS, PGATE_ROWS, Q_ROWS, K_ROWS, V_ROWS, GGATE_ROWS) = (slice(_OFFS[j], _OFFS[j + 1]) for j in range(6))
LR_ROWS = slice(_OFFS[6], _OFFS[8])
BGP_ROWS = slice(_OFFS[8], _OFFS[9])
BGG_ROWS = slice(_OFFS[9], _OFFS[10])
LR_WIDTH = 2 * GLA_GATE_RANK
WT_SLAB = 480

F32 = jnp.float32
BF16 = jnp.bfloat16


def _dot(a, b):
    return jnp.dot(a, b, preferred_element_type=F32)


def _dot_nt(a, b):
    return lax.dot_general(a, b, (((1,), (1,)), ((), ())), preferred_element_type=F32)


def _dot_tn(a, b):
    return lax.dot_general(a, b, (((0,), (0,)), ((), ())), preferred_element_type=F32)


def _sigmoid(x):
    return 0.5 * jnp.tanh(0.5 * x) + 0.5


def _silu(x):
    return x * _sigmoid(x)


def _log_sigmoid(x):
    return jnp.minimum(x, 0.0) - jnp.log(1.0 + jnp.exp(-jnp.abs(x)))


def _norm_mod(xv, g_pre, scale, shift):
    ms = jnp.mean(xv * xv, axis=-1, keepdims=True)
    gain = g_pre * (1.0 + scale)
    return (xv * lax.rsqrt(ms + RMS_EPS)) * gain + shift


def _split_hi_lo(a):
    hi = a.astype(BF16)
    lo = (a - hi.astype(F32)).astype(BF16)
    return hi, lo


def _run_round_robin(gens):
    live = list(gens)
    while live:
        for g in list(live):
            try:
                next(g)
            except StopIteration:
                live.remove(g)


def _ada_kernel(c_ref, w_ref, b_ref, o_ref):
    c_rows = jnp.broadcast_to(_silu(c_ref[...]), (SUBLANES, D_MODEL))
    s = c_rows.T[:, 0:1]
    for j in range(3):
        cols = slice(j * D_MODEL, (j + 1) * D_MODEL)
        o_ref[j] = jnp.sum(w_ref[:, cols] * s, axis=0, keepdims=True) + b_ref[:, cols]


def _ada_call(c_row, w_ada, b_ada):
    return pl.pallas_call(
        _ada_kernel,
        grid=(1,),
        in_specs=[
            pl.BlockSpec((1, D_MODEL), lambda j: (0, 0)),
            pl.BlockSpec((D_MODEL, 3 * D_MODEL), lambda j: (0, 0), pipeline_mode=pl.Buffered(1)),
            pl.BlockSpec((1, 3 * D_MODEL), lambda j: (0, 0)),
        ],
        out_specs=pl.BlockSpec((3, 1, D_MODEL), lambda j: (0, 0, 0)),
        out_shape=jax.ShapeDtypeStruct((3, 1, D_MODEL), F32),
        compiler_params=pltpu.CompilerParams(dimension_semantics=("arbitrary",)),
        name="ada_call",
    )(c_row, w_ada, b_ada)


def _prep_stages(sub, refs):
    (gkcat_ref, gbias_ref, sb_ref, k_ref, v_ref, lr_ref, st_scr) = refs
    t = SUB_TILE
    rows = slice(sub * t, (sub + 1) * t)
    k = k_ref[rows, :]
    v_bf = v_ref[rows, :]

    gk_bwd = gkcat_ref[:, GLA_KEY_WIDTH:]
    la = _log_sigmoid(_dot(lr_ref[rows, :], gk_bwd) + gbias_ref[...])
    la_hi, la_lo = _split_hi_lo(la)
    yield

    row = lax.broadcasted_iota(jnp.int32, (t, t), 0)
    col = lax.broadcasted_iota(jnp.int32, (t, t), 1)
    upper = jnp.where(col >= row, GATE_SCALE, 0.0).astype(BF16)
    rc = _dot(upper, la_hi) + _dot(upper, la_lo)
    yield

    tot = rc[0:1, :]
    k_end = (k * jnp.exp(tot - rc)).astype(BF16)
    decay = jnp.exp(tot)
    kvs = []
    for hd in range(GLA_HEADS):
        ks = slice(hd * GLA_HEAD_K, (hd + 1) * GLA_HEAD_K)
        vs = slice(hd * GLA_HEAD_V, (hd + 1) * GLA_HEAD_V)
        kvs.append(_dot_tn(v_bf[:, vs], k_end[:, ks]))
    yield

    per_main = MAIN_TILE // SUB_TILE
    for hd in range(GLA_HEADS):
        ks = slice(hd * GLA_HEAD_K, (hd + 1) * GLA_HEAD_K)
        st = st_scr[hd]
        if (sub + 1) % per_main == 0:
            sb_ref[sub // per_main, hd] = st
        st_scr[hd] = st * decay[:, ks] + kvs[hd]
    yield


def _prep_kernel(x_ref, mod_ref, gpre_ref, gkf_ref, gkb_ref, gbias_ref, wk_ref, wv_ref, wlr_ref,
                 win_ref, wpool_ref, wpp_ref, wpgla_ref, wout_ref,
                 sb_ref, k_ref, v_ref, h_ref, lr_ref, gkcat_ref,
                 winbf_ref, wpoolbf_ref, wppbf_ref, wpglabf_ref, woutbf_ref,
                 st_scr, wk_scr, wv_scr, wlr_scr):
    @pl.when(pl.program_id(0) == 0)
    def _():
        st_scr[...] = jnp.zeros_like(st_scr)
        wk_scr[...] = wk_ref[...].astype(BF16)
        wv_scr[...] = wv_ref[...].astype(BF16)
        wlr_scr[...] = wlr_ref[...].astype(BF16)
        r, w = GLA_GATE_RANK, GLA_KEY_WIDTH
        gkcat_ref[...] = jnp.zeros(gkcat_ref.shape, BF16)
        gkcat_ref[0:r, 0:w] = gkf_ref[...].astype(BF16)
        gkcat_ref[r:2 * r, w:2 * w] = gkb_ref[...].astype(BF16)

    winbf_ref[...] = win_ref[...].astype(BF16)
    wpoolbf_ref[...] = wpool_ref[...].astype(BF16)
    wppbf_ref[...] = wpp_ref[...].astype(BF16)
    wpglabf_ref[...] = wpgla_ref[...].astype(BF16)
    woutbf_ref[...] = wout_ref[...].astype(BF16)

    h = _norm_mod(x_ref[...], gpre_ref[...], mod_ref[1], mod_ref[0]).astype(BF16)
    h_ref[...] = h
    k_ref[...] = _dot_nt(h, wk_scr[...])
    lr_ref[...] = _dot_nt(h, wlr_scr[...]).astype(BF16)
    v_ref[...] = _dot_nt(h, wv_scr[...]).astype(BF16)

    refs = (gkcat_ref, gbias_ref, sb_ref, k_ref, v_ref, lr_ref, st_scr)
    gens = [_prep_stages(sub, refs) for sub in reversed(range(PREP_SUBTILES))]
    _run_round_robin(gens)


def _prep_call(x2, mod, g_pre, gk_f, gk_b, gbias_b, w_in_t, pool_w2, w_proj_pool, w_proj_gla, w_out):
    s = x2.shape[0]
    tb = PREP_SUBTILES * SUB_TILE
    nb = s // tb
    slab = D_MODEL // nb
    n_wt = IN_WIDTH // WT_SLAB
    const = lambda i: (0, 0)
    rev = lambda i: (nb - 1 - i, 0)
    fwd = lambda i: (i, 0)
    wt_slab = lambda i: (jnp.minimum(i, n_wt - 1), 0)
    one = pl.Buffered(1)

    def rows_of(rs):
        n = rs.stop - rs.start
        return pl.BlockSpec((n, D_MODEL), lambda i: (rs.start // n, 0), pipeline_mode=one)

    return pl.pallas_call(
        _prep_kernel,
        grid=(nb,),
        in_specs=[
            pl.BlockSpec((tb, D_MODEL), rev),
            pl.BlockSpec((3, 1, D_MODEL), lambda i: (0, 0, 0)),
            pl.BlockSpec((1, D_MODEL), const),
            pl.BlockSpec(gk_f.shape, const),
            pl.BlockSpec(gk_b.shape, const),
            pl.BlockSpec(gbias_b.shape, const),
            rows_of(K_ROWS),
            rows_of(V_ROWS),
            rows_of(LR_ROWS),
            pl.BlockSpec((WT_SLAB, D_MODEL), wt_slab),
            pl.BlockSpec((slab, POOL_GROUP_WIDTH), fwd),
            pl.BlockSpec((slab, D_MODEL), fwd),
            pl.BlockSpec((slab, D_MODEL), fwd),
            pl.BlockSpec((slab, D_MODEL), fwd),
        ],
        out_specs=[
            pl.BlockSpec((tb // MAIN_TILE, GLA_HEADS, GLA_HEAD_V, GLA_HEAD_K), lambda i: (nb - 1 - i, 0, 0, 0)),
            pl.BlockSpec((tb, GLA_KEY_WIDTH), rev),
            pl.BlockSpec((tb, D_MODEL), rev),
            pl.BlockSpec((tb, D_MODEL), rev),
            pl.BlockSpec((tb, LR_WIDTH), rev),
            pl.BlockSpec((LR_WIDTH, 2 * GLA_KEY_WIDTH), const),
            pl.BlockSpec((WT_SLAB, D_MODEL), wt_slab),
            pl.BlockSpec((slab, POOL_GROUP_WIDTH), fwd),
            pl.BlockSpec((slab, D_MODEL), fwd),
            pl.BlockSpec((slab, D_MODEL), fwd),
            pl.BlockSpec((slab, D_MODEL), fwd),
        ],
        out_shape=[
            jax.ShapeDtypeStruct((s // MAIN_TILE, GLA_HEADS, GLA_HEAD_V, GLA_HEAD_K), F32),
            jax.ShapeDtypeStruct((s, GLA_KEY_WIDTH), F32),
            jax.ShapeDtypeStruct((s, D_MODEL), BF16),
            jax.ShapeDtypeStruct((s, D_MODEL), BF16),
            jax.ShapeDtypeStruct((s, LR_WIDTH), BF16),
            jax.ShapeDtypeStruct((LR_WIDTH, 2 * GLA_KEY_WIDTH), BF16),
            jax.ShapeDtypeStruct((IN_WIDTH, D_MODEL), BF16),
            jax.ShapeDtypeStruct(pool_w2.shape, BF16),
            jax.ShapeDtypeStruct((D_MODEL, D_MODEL), BF16),
            jax.ShapeDtypeStruct((D_MODEL, D_MODEL), BF16),
            jax.ShapeDtypeStruct((D_MODEL, D_MODEL), BF16),
        ],
        scratch_shapes=[
            pltpu.VMEM((GLA_HEADS, GLA_HEAD_V, GLA_HEAD_K), F32),
            pltpu.VMEM((GLA_KEY_WIDTH, D_MODEL), BF16),
            pltpu.VMEM((D_MODEL, D_MODEL), BF16),
            pltpu.VMEM((LR_WIDTH, D_MODEL), BF16),
        ],
        compiler_params=pltpu.CompilerParams(dimension_semantics=("arbitrary",),
                                             vmem_limit_bytes=VMEM_LIMIT_BYTES),
        name="prep_call",
    )(x2, mod, g_pre, gk_f, gk_b, gbias_b, w_in_t, w_in_t, w_in_t, w_in_t, pool_w2, w_proj_pool, w_proj_gla, w_out)


def _subtile_stages(seq_len, sub, refs):
    (x_ref, mod_ref, gpost_ref, k_ref, v_ref, lr_ref,
     wt_ref,
     poolw_ref, pscale_ref, gkcat_ref, gbf_ref, gbb_ref, glag_ref,
     wpp_ref, wpgla_ref, wout_ref, sb_ref, o_ref,
     sf_scr, hext_scr, q_scr, cf_scr, rb_scr, qs_scr, qd_scr, ke_scr, kif_scr, kib_scr,
     kv_scr, scat_scr, sc_scr, og_scr) = refs
    t = MAIN_TILE
    r0 = sub * t
    rows = slice(r0, r0 + t)
    tile_row0 = pl.program_id(0) * (MAIN_SUBTILES * t) + r0
    sb_idx = sub
    n_pairs = t // GLA_PAIR
    hk, hv = GLA_HEAD_K, GLA_HEAD_V
    gate = mod_ref[2]

    h_ext = hext_scr[r0:r0 + t + 2 * HALO, :]
    h = hext_scr[r0 + HALO:r0 + HALO + t, :]

    gbias = jnp.concatenate([gbf_ref[...], gbb_ref[...]], axis=1)
    la_pre = _dot(lr_ref[rows, :], gkcat_ref[...]) + gbias
    q_scr[rows, :] = _dot_nt(h, wt_ref[Q_ROWS, :]) * (GLA_HEAD_K ** -0.5)
    te = t + 2 * HALO
    g_row = tile_row0 - HALO + lax.broadcasted_iota(jnp.int32, (te, 1), 0)
    valid = jnp.logical_and(g_row >= 0, g_row < seq_len)
    u_ext = jnp.where(valid, _dot_nt(h_ext, wt_ref[PIN_ROWS, :]), 0.0)
    yield

    la = _log_sigmoid(la_pre)
    p_gate = _dot_nt(h, wt_ref[PGATE_ROWS, :])
    gla_gate = _dot_nt(h, wt_ref[GGATE_ROWS, :])
    u_ext_bf = u_ext.astype(BF16)
    u = u_ext[HALO:HALO + t, :]
    laf_hi, laf_lo = _split_hi_lo(la[:, :GLA_KEY_WIDTH])
    lab_hi, lab_lo = _split_hi_lo(la[:, GLA_KEY_WIDTH:])
    yield

    cb = CUMSUM_BLOCK
    row = lax.broadcasted_iota(jnp.int32, (cb, cb), 0)
    col = lax.broadcasted_iota(jnp.int32, (cb, cb), 1)
    same_chunk = (row // GLA_CHUNK) == (col // GLA_CHUNK)
    lower = jnp.where(jnp.logical_and(same_chunk, col <= row), GATE_SCALE, 0.0).astype(BF16)
    upper = jnp.where(jnp.logical_and(same_chunk, col >= row), GATE_SCALE, 0.0).astype(BF16)
    for b in range(t // cb):
        rb_rows = slice(b * cb, (b + 1) * cb)
        dst = slice(r0 + b * cb, r0 + (b + 1) * cb)
        cf_scr[dst, :] = _dot(lower, laf_hi[rb_rows]) + _dot(lower, laf_lo[rb_rows])
        rb_scr[dst, :] = _dot(upper, lab_hi[rb_rows]) + _dot(upper, lab_lo[rb_rows])
    bg_pool = _sigmoid(_dot_nt(h, wt_ref[BGP_ROWS, :]))
    bg_gla = _sigmoid(_dot_nt(h, wt_ref[BGG_ROWS, :]))
    yield

    pos = tile_row0 + lax.broadcasted_iota(jnp.int32, (t, 1), 0)
    rb_ = POOL_ROW_BLOCK
    kb_ = POOL_ROW_BLOCK + 2 * HALO
    band_r = lax.broadcasted_iota(jnp.int32, (rb_, kb_), 0)
    band_c = lax.broadcasted_iota(jnp.int32, (rb_, kb_), 1)
    band_d = band_c - HALO - band_r
    a_parts = []
    for gi, w in enumerate(POOL_WINDOWS):
        hw = w // 2
        cs = slice(gi * POOL_GROUP_WIDTH, (gi + 1) * POOL_GROUP_WIDTH)
        ws = slice(gi * POOL_GROUP_WIDTH, (gi + 1) * POOL_GROUP_WIDTH)
        band = jnp.logical_and(band_d >= -hw, band_d < hw).astype(BF16)
        wsum = jnp.concatenate(
            [_dot(band, u_ext_bf[b * rb_:b * rb_ + kb_, cs]) for b in range(t // rb_)], axis=0)
        count = (jnp.minimum(pos + hw, seq_len) - jnp.maximum(pos - hw, 0)).astype(F32)
        pooled = wsum / count - u[:, cs]
        a_parts.append(_dot(pooled.astype(BF16), poolw_ref[ws, :]))
    a = jnp.concatenate(a_parts, axis=1) * pscale_ref[...]
    z_pool = (a * _silu(p_gate)).astype(BF16)

    def put_heads(dst, rs, fwd, bwd):
        for hd in range(GLA_HEADS):
            ks = slice(hd * hk, (hd + 1) * hk)
            dst[rs, 2 * hd * hk:(2 * hd + 1) * hk] = fwd[:, ks].astype(BF16)
            dst[rs, (2 * hd + 1) * hk:(2 * hd + 2) * hk] = bwd[:, ks].astype(BF16)

    dec_f, dec_b = [], []
    for p in range(n_pairs):
        c0 = r0 + p * GLA_PAIR
        c1 = c0 + GLA_CHUNK
        ra = slice(c0, c1)
        rb = slice(c1, c1 + GLA_CHUNK)
        q0, q1 = q_scr[ra, :], q_scr[rb, :]
        k0, k1 = k_ref[ra, :], k_ref[rb, :]
        cum0, cum1 = cf_scr[ra, :], cf_scr[rb, :]
        tot0 = cf_scr[c1 - 1:c1, :]
        tot1 = cf_scr[c1 + GLA_CHUNK - 1:c1 + GLA_CHUNK, :]
        rc0, rc1 = rb_scr[ra, :], rb_scr[rb, :]
        tb0 = rb_scr[c0:c0 + 1, :]
        tb1 = rb_scr[c1:c1 + 1, :]
        e_tot0, e_tot1 = jnp.exp(tot0), jnp.exp(tot1)
        e_tb0, e_tb1 = jnp.exp(tb0), jnp.exp(tb1)

        a0 = cum0 - tot0
        ks_f0 = k0 * jnp.exp(-a0)
        qs_f1 = q1 * jnp.exp(cum1)
        qs_b0 = q0 * jnp.exp(rc0)
        put_heads(qs_scr, ra, q0 * jnp.exp(a0), qs_b0)
        put_heads(qd_scr, ra, q0 * jnp.exp(cum0), qs_b0 * e_tb1)
        kif_scr[ra, :] = ks_f0.astype(BF16)
        kif_scr[rb, :] = (k1 * jnp.exp(-cum1)).astype(BF16)
        a1 = rc1 - tb1
        ks_b1 = k1 * jnp.exp(-a1)
        put_heads(qs_scr, rb, qs_f1, q1 * jnp.exp(a1))
        put_heads(qd_scr, rb, qs_f1 * e_tot0, q1 * jnp.exp(rc1))
        kib_scr[ra, :] = (k0 * jnp.exp(-rc0)).astype(BF16)
        kib_scr[rb, :] = ks_b1.astype(BF16)
        put_heads(ke_scr, ra, ks_f0 * e_tot1, k0 * jnp.exp(tb0 - rc0))
        put_heads(ke_scr, rb, k1 * jnp.exp(tot1 - cum1), ks_b1 * e_tb0)
        dec_f.append(e_tot0 * e_tot1)
        dec_b.append(e_tb0 * e_tb1)
    yield

    merged = bg_pool * _dot(z_pool, wpp_ref[...])
    for hd in range(GLA_HEADS):
        vs = slice(hd * hv, (hd + 1) * hv)
        cat = slice(2 * hd * hk, (2 * hd + 2) * hk)
        for p in range(n_pairs):
            rs = slice(r0 + p * GLA_PAIR, r0 + (p + 1) * GLA_PAIR)
            kv_scr[sub, hd, p] = _dot_tn(v_ref[rs, vs], ke_scr[rs, cat])
    yield

    prow = lax.broadcasted_iota(jnp.int32, (GLA_PAIR, GLA_PAIR), 0)
    pcol = lax.broadcasted_iota(jnp.int32, (GLA_PAIR, GLA_PAIR), 1)
    mask_f = pcol <= prow
    for p in range(n_pairs):
        rs = slice(r0 + p * GLA_PAIR, r0 + (p + 1) * GLA_PAIR)
        for hd in range(GLA_HEADS):
            ks = slice(hd * hk, (hd + 1) * hk)
            sc_f = _dot_nt(qs_scr[rs, 2 * hd * hk:(2 * hd + 1) * hk], kif_scr[rs, ks])
            sc_b = _dot_nt(qs_scr[rs, (2 * hd + 1) * hk:(2 * hd + 2) * hk], kib_scr[rs, ks])
            sc_scr[sub, p, hd] = jnp.where(mask_f, sc_f, sc_b).astype(BF16)
    for hd in range(GLA_HEADS):
        ks = slice(hd * hk, (hd + 1) * hk)
        sf = sf_scr[hd]
        for p in range(n_pairs):
            scat_scr[sub, p, hd, 0:hk, :] = sf.T.astype(BF16)
            sf = sf * dec_f[p][:, ks] + kv_scr[sub, hd, p, :, 0:hk]
        sf_scr[hd] = sf
        sb = sb_ref[sb_idx, hd]
        for p in reversed(range(n_pairs)):
            scat_scr[sub, p, hd, hk:2 * hk, :] = sb.T.astype(BF16)
            if p > 0:
                sb = sb * dec_b[p][:, ks] + kv_scr[sub, hd, p, :, hk:2 * hk]
    yield

    for p in range(n_pairs):
        rs = slice(r0 + p * GLA_PAIR, r0 + (p + 1) * GLA_PAIR)
        for hd in range(GLA_HEADS):
            vs = slice(hd * hv, (hd + 1) * hv)
            cat = slice(2 * hd * hk, (2 * hd + 2) * hk)
            og_scr[rs, vs] = (_dot(sc_scr[sub, p, hd], v_ref[rs, vs])
                              + _dot(qd_scr[rs, cat], scat_scr[sub, p, hd]))
    yield

    glag = glag_ref[...]
    z_parts = []
    for hd in range(GLA_HEADS):
        vs = slice(hd * hv, (hd + 1) * hv)
        oh = og_scr[rows, vs]
        ms = jnp.mean(oh * oh, axis=-1, keepdims=True)
        z_parts.append((oh * lax.rsqrt(ms + RMS_EPS)) * glag)
    o_n = jnp.concatenate(z_parts, axis=1)
    y_gla = _dot((o_n * _silu(gla_gate)).astype(BF16), wpgla_ref[...])
    merged = merged + bg_gla * y_gla
    yield

    out = _dot(merged.astype(BF16), wout_ref[...])
    yield

    ms = jnp.mean(out * out, axis=-1, keepdims=True)
    gain = gate * gpost_ref[...]
    o_ref[rows, :] = x_ref[rows, :] + (out * lax.rsqrt(ms + RMS_EPS)) * gain
    yield


def _main_kernel(seq_len,
                 x_ref, h_ref, hp_ref, hn_ref, mod_ref, gpost_ref, k_ref, v_ref, lr_ref,
                 wt_ref,
                 poolw_ref, pscale_ref, gkcat_ref, gbf_ref, gbb_ref, glag_ref,
                 wpp_ref, wpgla_ref, wout_ref, sb_ref,
                 o_ref,
                 sf_scr, hext_scr, q_scr, cf_scr, rb_scr, qs_scr, qd_scr, ke_scr, kif_scr, kib_scr,
                 kv_scr, scat_scr, sc_scr, og_scr):
    tb = MAIN_SUBTILES * MAIN_TILE

    @pl.when(pl.program_id(0) == 0)
    def _():
        sf_scr[...] = jnp.zeros_like(sf_scr)

    hext_scr[0:HALO, :] = hp_ref[...]
    hext_scr[HALO:HALO + tb, :] = h_ref[...]
    hext_scr[HALO + tb:HALO + tb + HALO, :] = hn_ref[...]

    refs = (x_ref, mod_ref, gpost_ref, k_ref, v_ref, lr_ref,
            wt_ref,
            poolw_ref, pscale_ref, gkcat_ref, gbf_ref, gbb_ref, glag_ref,
            wpp_ref, wpgla_ref, wout_ref, sb_ref, o_ref,
            sf_scr, hext_scr, q_scr, cf_scr, rb_scr, qs_scr, qd_scr, ke_scr, kif_scr, kib_scr,
            kv_scr, scat_scr, sc_scr, og_scr)
    gens = [_subtile_stages(seq_len, sub, refs) for sub in range(MAIN_SUBTILES)]
    _run_round_robin(gens)


def _main_call(x2, h_all, mod, g_post, k_all, v_all, lr_all, w_bf, small, wpool_bf, wpp_bf, wpgla_bf, wout_bf, sb):
    s = x2.shape[0]
    tb = MAIN_SUBTILES * MAIN_TILE
    nt = s // tb
    hb = tb // HALO
    n_hb = s // HALO
    n_pairs = MAIN_TILE // GLA_PAIR
    const2 = lambda i: (0, 0)
    tile = lambda i: (i, 0)
    one = pl.Buffered(1)
    pool_scale, gk_cat, gbias_f, gbias_b, gla_norm_g = small

    def resident(arr):
        return pl.BlockSpec(arr.shape, const2, pipeline_mode=one)

    in_specs = [
        pl.BlockSpec((tb, D_MODEL), tile),
        pl.BlockSpec((tb, D_MODEL), tile),
        pl.BlockSpec((HALO, D_MODEL), lambda i: (jnp.maximum(i * hb - 1, 0), 0)),
        pl.BlockSpec((HALO, D_MODEL), lambda i: (jnp.minimum((i + 1) * hb, n_hb - 1), 0)),
        pl.BlockSpec((3, 1, D_MODEL), lambda i: (0, 0, 0)),
        pl.BlockSpec((1, D_MODEL), const2),
        pl.BlockSpec((tb, GLA_KEY_WIDTH), tile),
        pl.BlockSpec((tb, D_MODEL), tile),
        pl.BlockSpec((tb, LR_WIDTH), tile),
        resident(w_bf),
        resident(wpool_bf), resident(pool_scale), resident(gk_cat), resident(gbias_f), resident(gbias_b),
        resident(gla_norm_g),
        resident(wpp_bf), resident(wpgla_bf), resident(wout_bf),
        pl.BlockSpec((MAIN_SUBTILES, GLA_HEADS, GLA_HEAD_V, GLA_HEAD_K), lambda i: (i, 0, 0, 0)),
    ]
    scratch = [
        pltpu.VMEM((GLA_HEADS, GLA_HEAD_V, GLA_HEAD_K), F32),
        pltpu.VMEM((tb + 2 * HALO, D_MODEL), BF16),
        pltpu.VMEM((tb, GLA_KEY_WIDTH), F32),
        pltpu.VMEM((tb, GLA_KEY_WIDTH), F32),
        pltpu.VMEM((tb, GLA_KEY_WIDTH), F32),
        pltpu.VMEM((tb, 2 * GLA_KEY_WIDTH), BF16),
        pltpu.VMEM((tb, 2 * GLA_KEY_WIDTH), BF16),
        pltpu.VMEM((tb, 2 * GLA_KEY_WIDTH), BF16),
        pltpu.VMEM((tb, GLA_KEY_WIDTH), BF16),
        pltpu.VMEM((tb, GLA_KEY_WIDTH), BF16),
        pltpu.VMEM((MAIN_SUBTILES, GLA_HEADS, n_pairs, GLA_HEAD_V, 2 * GLA_HEAD_K), F32),
        pltpu.VMEM((MAIN_SUBTILES, n_pairs, GLA_HEADS, 2 * GLA_HEAD_K, GLA_HEAD_V), BF16),
        pltpu.VMEM((MAIN_SUBTILES, n_pairs, GLA_HEADS, GLA_PAIR, GLA_PAIR), BF16),
        pltpu.VMEM((tb, D_MODEL), F32),
    ]
    return pl.pallas_call(
        functools.partial(_main_kernel, s),
        grid=(nt,),
        in_specs=in_specs,
        out_specs=pl.BlockSpec((tb, D_MODEL), tile),
        out_shape=jax.ShapeDtypeStruct((s, D_MODEL), F32),
        scratch_shapes=scratch,
        compiler_params=pltpu.CompilerParams(dimension_semantics=("arbitrary",),
                                             vmem_limit_bytes=VMEM_LIMIT_BYTES),
        name="main_call",
    )(x2, h_all, h_all, h_all, mod, g_post, k_all, v_all, lr_all,
      w_bf,
      wpool_bf, pool_scale, gk_cat, gbias_f, gbias_b, gla_norm_g, wpp_bf, wpgla_bf, wout_bf, sb)


def _layer(x2, c, w_ada, b_ada, g_pre, g_post, w_in, pool_w, pool_scale, gk_up_fwd, gk_bias_fwd,
           gk_up_bwd, gk_bias_bwd, gla_norm_g, w_proj_pool, w_proj_gla, w_out):
    mod = _ada_call(c.reshape(1, D_MODEL), w_ada, b_ada.reshape(1, -1))
    gbias_f = gk_bias_fwd.reshape(1, -1)
    gbias_b = gk_bias_bwd.reshape(1, -1)

    pool_w2 = pool_w.reshape(POOL_GROUPS * POOL_GROUP_WIDTH, POOL_GROUP_WIDTH)
    sb, k_all, v_all, h_all, lr_all, gk_cat, w_bf, wpool_bf, wpp_bf, wpgla_bf, wout_bf = _prep_call(
        x2, mod, g_pre.reshape(1, -1), gk_up_fwd, gk_up_bwd, gbias_b, w_in.T, pool_w2, w_proj_pool, w_proj_gla,
        w_out)

    small = (pool_scale.reshape(1, -1), gk_cat, gbias_f, gbias_b, gla_norm_g.reshape(1, -1))
    return _main_call(x2, h_all, mod, g_post.reshape(1, -1), k_all, v_all, lr_all, w_bf, small,
                      wpool_bf, wpp_bf, wpgla_bf, wout_bf, sb)


def kernel(x, c, w_ada, b_ada, g_pre, g_post, w_in, pool_w, pool_scale, gk_up_fwd, gk_bias_fwd, gk_up_bwd, gk_bias_bwd, gla_norm_g, w_proj_pool, w_proj_gla, w_out):
    b, s, d = x.shape
    depth = w_in.shape[0]
    xf = x.reshape(b * s, d)
    outs = []
    for bi in range(b):
        xb = xf if b == 1 else lax.slice_in_dim(xf, bi * s, (bi + 1) * s, axis=0)
        for l in range(depth):
            xb = _layer(xb, c[bi:bi + 1], w_ada[l], b_ada[l], g_pre[l], g_post[l], w_in[l], pool_w[l],
                        pool_scale[l], gk_up_fwd[l], gk_bias_fwd[l], gk_up_bwd[l], gk_bias_bwd[l],
                        gla_norm_g[l], w_proj_pool[l], w_proj_gla[l], w_out[l])
        outs.append(xb)
    out = outs[0] if b == 1 else jnp.concatenate(outs, axis=0)
    return out.reshape(b, s, d)
```

```python
import functools
import math

import jax
import jax.numpy as jnp
from jax import lax
from jax.experimental import pallas as pl
from jax.experimental.pallas import tpu as pltpu

D_MODEL = 1024
POOL_GROUPS = 4
POOL_GROUP_WIDTH = 256
POOL_WINDOWS = (2, 4, 8, 16)
GLA_HEADS = 4
GLA_KEY_WIDTH = 512
GLA_HEAD_K = 128
GLA_HEAD_V = 256
GLA_GATE_RANK = 16
GLA_GATE_NORMALIZER = 16.0
GATE_SCALE = 1.0 / GLA_GATE_NORMALIZER
assert math.frexp(GLA_GATE_NORMALIZER)[0] == 0.5
GLA_CHUNK = 64
GLA_PAIR = 2 * GLA_CHUNK
RMS_EPS = 1e-6
IN_SPLITS = (1024, 1024, 512, 512, 1024, 1024, 16, 16, 1024, 1024)
IN_WIDTH = sum(IN_SPLITS)

LANES = 128
SUBLANES = 8
SUB_TILE = 256
PREP_SUBTILES = 4
MAIN_TILE = 512
MAIN_SUBTILES = 1
CUMSUM_BLOCK = 256
HALO = 16
POOL_ROW_BLOCK = 128
VMEM_LIMIT_BYTES = 62 * 1024 * 1024

_OFFS = [sum(IN_SPLITS[:j]) for j in range(len(IN_SPLITS) + 1)]
(PIN_ROWS, PGATE_ROWS, Q_ROWS, K_ROWS, V_ROWS, GGATE_ROWS) = (slice(_OFFS[j], _OFFS[j + 1]) for j in range(6))
LR_ROWS = slice(_OFFS[6], _OFFS[8])
BGP_ROWS = slice(_OFFS[8], _OFFS[9])
BGG_ROWS = slice(_OFFS[9], _OFFS[10])
LR_WIDTH = 2 * GLA_GATE_RANK
WT_SLAB = 480

F32 = jnp.float32
BF16 = jnp.bfloat16


def _dot(a, b):
    return jnp.dot(a, b, preferred_element_type=F32)


def _dot_nt(a, b):
    return lax.dot_general(a, b, (((1,), (1,)), ((), ())), preferred_element_type=F32)


def _dot_tn(a, b):
    return lax.dot_general(a, b, (((0,), (0,)), ((), ())), preferred_element_type=F32)


def _sigmoid(x):
    return 0.5 * jnp.tanh(0.5 * x) + 0.5


def _silu(x):
    return x * _sigmoid(x)


def _log_sigmoid(x):
    return jnp.minimum(x, 0.0) - jnp.log(1.0 + jnp.exp(-jnp.abs(x)))


def _norm_mod(xv, g_pre, scale, shift):
    ms = jnp.mean(xv * xv, axis=-1, keepdims=True)
    gain = g_pre * (1.0 + scale)
    return (xv * lax.rsqrt(ms + RMS_EPS)) * gain + shift


def _split_hi_lo(a):
    hi = a.astype(BF16)
    lo = (a - hi.astype(F32)).astype(BF16)
    return hi, lo


def _run_round_robin(gens):
    live = list(gens)
    while live:
        for g in list(live):
            try:
                next(g)
            except StopIteration:
                live.remove(g)


def _ada_kernel(c_ref, w_ref, b_ref, o_ref):
    c_rows = jnp.broadcast_to(_silu(c_ref[...]), (SUBLANES, D_MODEL))
    s = c_rows.T[:, 0:1]
    for j in range(3):
        cols = slice(j * D_MODEL, (j + 1) * D_MODEL)
        o_ref[j] = jnp.sum(w_ref[:, cols] * s, axis=0, keepdims=True) + b_ref[:, cols]


def _ada_call(c_row, w_ada, b_ada):
    return pl.pallas_call(
        _ada_kernel,
        grid=(1,),
        in_specs=[
            pl.BlockSpec((1, D_MODEL), lambda j: (0, 0)),
            pl.BlockSpec((D_MODEL, 3 * D_MODEL), lambda j: (0, 0), pipeline_mode=pl.Buffered(1)),
            pl.BlockSpec((1, 3 * D_MODEL), lambda j: (0, 0)),
        ],
        out_specs=pl.BlockSpec((3, 1, D_MODEL), lambda j: (0, 0, 0)),
        out_shape=jax.ShapeDtypeStruct((3, 1, D_MODEL), F32),
        compiler_params=pltpu.CompilerParams(dimension_semantics=("arbitrary",)),
        name="ada_call",
    )(c_row, w_ada, b_ada)


def _prep_stages(sub, refs):
    (gkcat_ref, gbias_ref, sb_ref, k_ref, v_ref, lr_ref, st_scr) = refs
    t = SUB_TILE
    rows = slice(sub * t, (sub + 1) * t)
    k = k_ref[rows, :]
    v_bf = v_ref[rows, :]

    gk_bwd = gkcat_ref[:, GLA_KEY_WIDTH:]
    la = _log_sigmoid(_dot(lr_ref[rows, :], gk_bwd) + gbias_ref[...])
    la_hi, la_lo = _split_hi_lo(la)
    yield

    row = lax.broadcasted_iota(jnp.int32, (t, t), 0)
    col = lax.broadcasted_iota(jnp.int32, (t, t), 1)
    upper = jnp.where(col >= row, GATE_SCALE, 0.0).astype(BF16)
    rc = _dot(upper, la_hi) + _dot(upper, la_lo)
    yield

    tot = rc[0:1, :]
    k_end = (k * jnp.exp(tot - rc)).astype(BF16)
    decay = jnp.exp(tot)
    kvs = []
    for hd in range(GLA_HEADS):
        ks = slice(hd * GLA_HEAD_K, (hd + 1) * GLA_HEAD_K)
        vs = slice(hd * GLA_HEAD_V, (hd + 1) * GLA_HEAD_V)
        kvs.append(_dot_tn(v_bf[:, vs], k_end[:, ks]))
    yield

    per_main = MAIN_TILE // SUB_TILE
    for hd in range(GLA_HEADS):
        ks = slice(hd * GLA_HEAD_K, (hd + 1) * GLA_HEAD_K)
        st = st_scr[hd]
        if (sub + 1) % per_main == 0:
            sb_ref[sub // per_main, hd] = st
        st_scr[hd] = st * decay[:, ks] + kvs[hd]
    yield


def _prep_kernel(x_ref, mod_ref, gpre_ref, gkf_ref, gkb_ref, gbias_ref, wk_ref, wv_ref, wlr_ref,
                 win_ref, wpool_ref, wpp_ref, wpgla_ref, wout_ref,
                 sb_ref, k_ref, v_ref, h_ref, lr_ref, gkcat_ref,
                 winbf_ref, wpoolbf_ref, wppbf_ref, wpglabf_ref, woutbf_ref,
                 st_scr, wk_scr, wv_scr, wlr_scr):
    @pl.when(pl.program_id(0) == 0)
    def _():
        st_scr[...] = jnp.zeros_like(st_scr)
        wk_scr[...] = wk_ref[...].astype(BF16)
        wv_scr[...] = wv_ref[...].astype(BF16)
        wlr_scr[...] = wlr_ref[...].astype(BF16)
        r, w = GLA_GATE_RANK, GLA_KEY_WIDTH
        gkcat_ref[...] = jnp.zeros(gkcat_ref.shape, BF16)
        gkcat_ref[0:r, 0:w] = gkf_ref[...].astype(BF16)
        gkcat_ref[r:2 * r, w:2 * w] = gkb_ref[...].astype(BF16)

    winbf_ref[...] = win_ref[...].astype(BF16)
    wpoolbf_ref[...] = wpool_ref[...].astype(BF16)
    wppbf_ref[...] = wpp_ref[...].astype(BF16)
    wpglabf_ref[...] = wpgla_ref[...].astype(BF16)
    woutbf_ref[...] = wout_ref[...].astype(BF16)

    h = _norm_mod(x_ref[...], gpre_ref[...], mod_ref[1], mod_ref[0]).astype(BF16)
    h_ref[...] = h
    k_ref[...] = _dot_nt(h, wk_scr[...])
    lr_ref[...] = _dot_nt(h, wlr_scr[...]).astype(BF16)
    v_ref[...] = _dot_nt(h, wv_scr[...]).astype(BF16)

    refs = (gkcat_ref, gbias_ref, sb_ref, k_ref, v_ref, lr_ref, st_scr)
    gens = [_prep_stages(sub, refs) for sub in reversed(range(PREP_SUBTILES))]
    _run_round_robin(gens)


def _prep_call(x2, mod, g_pre, gk_f, gk_b, gbias_b, w_in_t, pool_w2, w_proj_pool, w_proj_gla, w_out):
    s = x2.shape[0]
    tb = PREP_SUBTILES * SUB_TILE
    nb = s // tb
    slab = D_MODEL // nb
    n_wt = IN_WIDTH // WT_SLAB
    const = lambda i: (0, 0)
    rev = lambda i: (nb - 1 - i, 0)
    fwd = lambda i: (i, 0)
    wt_slab = lambda i: (jnp.minimum(i, n_wt - 1), 0)
    one = pl.Buffered(1)

    def rows_of(rs):
        n = rs.stop - rs.start
        return pl.BlockSpec((n, D_MODEL), lambda i: (rs.start // n, 0), pipeline_mode=one)

    return pl.pallas_call(
        _prep_kernel,
        grid=(nb,),
        in_specs=[
            pl.BlockSpec((tb, D_MODEL), rev),
            pl.BlockSpec((3, 1, D_MODEL), lambda i: (0, 0, 0)),
            pl.BlockSpec((1, D_MODEL), const),
            pl.BlockSpec(gk_f.shape, const),
            pl.BlockSpec(gk_b.shape, const),
            pl.BlockSpec(gbias_b.shape, const),
            rows_of(K_ROWS),
            rows_of(V_ROWS),
            rows_of(LR_ROWS),
            pl.BlockSpec((WT_SLAB, D_MODEL), wt_slab),
            pl.BlockSpec((slab, POOL_GROUP_WIDTH), fwd),
            pl.BlockSpec((slab, D_MODEL), fwd),
            pl.BlockSpec((slab, D_MODEL), fwd),
            pl.BlockSpec((slab, D_MODEL), fwd),
        ],
        out_specs=[
            pl.BlockSpec((tb // MAIN_TILE, GLA_HEADS, GLA_HEAD_V, GLA_HEAD_K), lambda i: (nb - 1 - i, 0, 0, 0)),
            pl.BlockSpec((tb, GLA_KEY_WIDTH), rev),
            pl.BlockSpec((tb, D_MODEL), rev),
            pl.BlockSpec((tb, D_MODEL), rev),
            pl.BlockSpec((tb, LR_WIDTH), rev),
            pl.BlockSpec((LR_WIDTH, 2 * GLA_KEY_WIDTH), const),
            pl.BlockSpec((WT_SLAB, D_MODEL), wt_slab),
            pl.BlockSpec((slab, POOL_GROUP_WIDTH), fwd),
            pl.BlockSpec((slab, D_MODEL), fwd),
            pl.BlockSpec((slab, D_MODEL), fwd),
            pl.BlockSpec((slab, D_MODEL), fwd),
        ],
        out_shape=[
            jax.ShapeDtypeStruct((s // MAIN_TILE, GLA_HEADS, GLA_HEAD_V, GLA_HEAD_K), F32),
            jax.ShapeDtypeStruct((s, GLA_KEY_WIDTH), F32),
            jax.ShapeDtypeStruct((s, D_MODEL), BF16),
            jax.ShapeDtypeStruct((s, D_MODEL), BF16),
            jax.ShapeDtypeStruct((s, LR_WIDTH), BF16),
            jax.ShapeDtypeStruct((LR_WIDTH, 2 * GLA_KEY_WIDTH), BF16),
            jax.ShapeDtypeStruct((IN_WIDTH, D_MODEL), BF16),
            jax.ShapeDtypeStruct(pool_w2.shape, BF16),
            jax.ShapeDtypeStruct((D_MODEL, D_MODEL), BF16),
            jax.ShapeDtypeStruct((D_MODEL, D_MODEL), BF16),
            jax.ShapeDtypeStruct((D_MODEL, D_MODEL), BF16),
        ],
        scratch_shapes=[
            pltpu.VMEM((GLA_HEADS, GLA_HEAD_V, GLA_HEAD_K), F32),
            pltpu.VMEM((GLA_KEY_WIDTH, D_MODEL), BF16),
            pltpu.VMEM((D_MODEL, D_MODEL), BF16),
            pltpu.VMEM((LR_WIDTH, D_MODEL), BF16),
        ],
        compiler_params=pltpu.CompilerParams(dimension_semantics=("arbitrary",),
                                             vmem_limit_bytes=VMEM_LIMIT_BYTES),
        name="prep_call",
    )(x2, mod, g_pre, gk_f, gk_b, gbias_b, w_in_t, w_in_t, w_in_t, w_in_t, pool_w2, w_proj_pool, w_proj_gla, w_out)


def _subtile_stages(seq_len, sub, refs):
    (x_ref, mod_ref, gpost_ref, k_ref, v_ref, lr_ref,
     wt_ref,
     poolw_ref, pscale_ref, gkcat_ref, gbf_ref, gbb_ref, glag_ref,
     wpp_ref, wpgla_ref, wout_ref, sb_ref, o_ref,
     sf_scr, hext_scr, q_scr, cf_scr, rb_scr, qs_scr, qd_scr, ke_scr, kif_scr, kib_scr,
     kv_scr, scat_scr, sc_scr, og_scr, hnext_ref) = refs
    t = MAIN_TILE
    r0 = sub * t
    rows = slice(r0, r0 + t)
    tile_row0 = pl.program_id(0) * (MAIN_SUBTILES * t) + r0
    sb_idx = sub
    n_pairs = t // GLA_PAIR
    hk, hv = GLA_HEAD_K, GLA_HEAD_V
    gate = mod_ref[2]

    h_ext = hext_scr[r0:r0 + t + 2 * HALO, :]
    h = hext_scr[r0 + HALO:r0 + HALO + t, :]

    gbias = jnp.concatenate([gbf_ref[...], gbb_ref[...]], axis=1)
    la_pre = _dot(lr_ref[rows, :], gkcat_ref[...]) + gbias
    te = t + 2 * HALO
    g_row = tile_row0 - HALO + lax.broadcasted_iota(jnp.int32, (te, 1), 0)
    valid = jnp.logical_and(g_row >= 0, g_row < seq_len)
    u_ext = jnp.where(valid, _dot_nt(h_ext, wt_ref[PIN_ROWS, :]), 0.0)
    yield

    la = _log_sigmoid(la_pre)
    p_gate = _dot_nt(h, wt_ref[PGATE_ROWS, :])
    gla_gate = _dot_nt(h, wt_ref[GGATE_ROWS, :])
    u_ext_bf = u_ext.astype(BF16)
    u = u_ext[HALO:HALO + t, :]
    laf_hi, laf_lo = _split_hi_lo(la[:, :GLA_KEY_WIDTH])
    lab_hi, lab_lo = _split_hi_lo(la[:, GLA_KEY_WIDTH:])
    yield

    cb = CUMSUM_BLOCK
    row = lax.broadcasted_iota(jnp.int32, (cb, cb), 0)
    col = lax.broadcasted_iota(jnp.int32, (cb, cb), 1)
    same_chunk = (row // GLA_CHUNK) == (col // GLA_CHUNK)
    lower = jnp.where(jnp.logical_and(same_chunk, col <= row), GATE_SCALE, 0.0).astype(BF16)
    upper = jnp.where(jnp.logical_and(same_chunk, col >= row), GATE_SCALE, 0.0).astype(BF16)
    for b in range(t // cb):
        rb_rows = slice(b * cb, (b + 1) * cb)
        dst = slice(r0 + b * cb, r0 + (b + 1) * cb)
        cf_scr[dst, :] = _dot(lower, laf_hi[rb_rows]) + _dot(lower, laf_lo[rb_rows])
        rb_scr[dst, :] = _dot(upper, lab_hi[rb_rows]) + _dot(upper, lab_lo[rb_rows])
    bg_pool = _sigmoid(_dot_nt(h, wt_ref[BGP_ROWS, :]))
    bg_gla = _sigmoid(_dot_nt(h, wt_ref[BGG_ROWS, :]))
    yield

    pos = tile_row0 + lax.broadcasted_iota(jnp.int32, (t, 1), 0)
    rb_ = POOL_ROW_BLOCK
    kb_ = POOL_ROW_BLOCK + 2 * HALO
    band_r = lax.broadcasted_iota(jnp.int32, (rb_, kb_), 0)
    band_c = lax.broadcasted_iota(jnp.int32, (rb_, kb_), 1)
    band_d = band_c - HALO - band_r
    a_parts = []
    for gi, w in enumerate(POOL_WINDOWS):
        hw = w // 2
        cs = slice(gi * POOL_GROUP_WIDTH, (gi + 1) * POOL_GROUP_WIDTH)
        ws = slice(gi * POOL_GROUP_WIDTH, (gi + 1) * POOL_GROUP_WIDTH)
        band = jnp.logical_and(band_d >= -hw, band_d < hw).astype(BF16)
        wsum = jnp.concatenate(
            [_dot(band, u_ext_bf[b * rb_:b * rb_ + kb_, cs]) for b in range(t // rb_)], axis=0)
        count = (jnp.minimum(pos + hw, seq_len) - jnp.maximum(pos - hw, 0)).astype(F32)
        pooled = wsum / count - u[:, cs]
        a_parts.append(_dot(pooled.astype(BF16), poolw_ref[ws, :]))
    a = jnp.concatenate(a_parts, axis=1) * pscale_ref[...]
    z_pool = (a * _silu(p_gate)).astype(BF16)

    def put_heads(dst, rs, fwd, bwd):
        for hd in range(GLA_HEADS):
            ks = slice(hd * hk, (hd + 1) * hk)
            dst[rs, 2 * hd * hk:(2 * hd + 1) * hk] = fwd[:, ks].astype(BF16)
            dst[rs, (2 * hd + 1) * hk:(2 * hd + 2) * hk] = bwd[:, ks].astype(BF16)

    dec_f, dec_b = [], []
    for p in range(n_pairs):
        c0 = r0 + p * GLA_PAIR
        c1 = c0 + GLA_CHUNK
        ra = slice(c0, c1)
        rb = slice(c1, c1 + GLA_CHUNK)
        q0, q1 = q_scr[ra, :], q_scr[rb, :]
        k0, k1 = k_ref[ra, :], k_ref[rb, :]
        cum0, cum1 = cf_scr[ra, :], cf_scr[rb, :]
        tot0 = cf_scr[c1 - 1:c1, :]
        tot1 = cf_scr[c1 + GLA_CHUNK - 1:c1 + GLA_CHUNK, :]
        rc0, rc1 = rb_scr[ra, :], rb_scr[rb, :]
        tb0 = rb_scr[c0:c0 + 1, :]
        tb1 = rb_scr[c1:c1 + 1, :]
        e_tot0, e_tot1 = jnp.exp(tot0), jnp.exp(tot1)
        e_tb0, e_tb1 = jnp.exp(tb0), jnp.exp(tb1)

        a0 = cum0 - tot0
        ks_f0 = k0 * jnp.exp(-a0)
        qs_f1 = q1 * jnp.exp(cum1)
        qs_b0 = q0 * jnp.exp(rc0)
        put_heads(qs_scr, ra, q0 * jnp.exp(a0), qs_b0)
        put_heads(qd_scr, ra, q0 * jnp.exp(cum0), qs_b0 * e_tb1)
        kif_scr[ra, :] = ks_f0.astype(BF16)
        kif_scr[rb, :] = (k1 * jnp.exp(-cum1)).astype(BF16)
        a1 = rc1 - tb1
        ks_b1 = k1 * jnp.exp(-a1)
        put_heads(qs_scr, rb, qs_f1, q1 * jnp.exp(a1))
        put_heads(qd_scr, rb, qs_f1 * e_tot0, q1 * jnp.exp(rc1))
        kib_scr[ra, :] = (k0 * jnp.exp(-rc0)).astype(BF16)
        kib_scr[rb, :] = ks_b1.astype(BF16)
        put_heads(ke_scr, ra, ks_f0 * e_tot1, k0 * jnp.exp(tb0 - rc0))
        put_heads(ke_scr, rb, k1 * jnp.exp(tot1 - cum1), ks_b1 * e_tb0)
        dec_f.append(e_tot0 * e_tot1)
        dec_b.append(e_tb0 * e_tb1)
    yield

    merged = bg_pool * _dot(z_pool, wpp_ref[...])
    for hd in range(GLA_HEADS):
        vs = slice(hd * hv, (hd + 1) * hv)
        cat = slice(2 * hd * hk, (2 * hd + 2) * hk)
        for p in range(n_pairs):
            rs = slice(r0 + p * GLA_PAIR, r0 + (p + 1) * GLA_PAIR)
            kv_scr[sub, hd, p] = _dot_tn(v_ref[rs, vs], ke_scr[rs, cat])
    yield

    prow = lax.broadcasted_iota(jnp.int32, (GLA_PAIR, GLA_PAIR), 0)
    pcol = lax.broadcasted_iota(jnp.int32, (GLA_PAIR, GLA_PAIR), 1)
    mask_f = pcol <= prow
    for p in range(n_pairs):
        rs = slice(r0 + p * GLA_PAIR, r0 + (p + 1) * GLA_PAIR)
        for hd in range(GLA_HEADS):
            ks = slice(hd * hk, (hd + 1) * hk)
            sc_f = _dot_nt(qs_scr[rs, 2 * hd * hk:(2 * hd + 1) * hk], kif_scr[rs, ks])
            sc_b = _dot_nt(qs_scr[rs, (2 * hd + 1) * hk:(2 * hd + 2) * hk], kib_scr[rs, ks])
            sc_scr[sub, p, hd] = jnp.where(mask_f, sc_f, sc_b).astype(BF16)
    for hd in range(GLA_HEADS):
        ks = slice(hd * hk, (hd + 1) * hk)
        sf = sf_scr[hd]
        for p in range(n_pairs):
            scat_scr[sub, p, hd, 0:hk, :] = sf.T.astype(BF16)
            sf = sf * dec_f[p][:, ks] + kv_scr[sub, hd, p, :, 0:hk]
        sf_scr[hd] = sf
        sb = sb_ref[sb_idx, hd]
        for p in reversed(range(n_pairs)):
            scat_scr[sub, p, hd, hk:2 * hk, :] = sb.T.astype(BF16)
            if p > 0:
                sb = sb * dec_b[p][:, ks] + kv_scr[sub, hd, p, :, hk:2 * hk]
    yield

    for p in range(n_pairs):
        rs = slice(r0 + p * GLA_PAIR, r0 + (p + 1) * GLA_PAIR)
        for hd in range(GLA_HEADS):
            vs = slice(hd * hv, (hd + 1) * hv)
            cat = slice(2 * hd * hk, (2 * hd + 2) * hk)
            og_scr[rs, vs] = (_dot(sc_scr[sub, p, hd], v_ref[rs, vs])
                              + _dot(qd_scr[rs, cat], scat_scr[sub, p, hd]))
    yield

    glag = glag_ref[...]
    z_parts = []
    for hd in range(GLA_HEADS):
        vs = slice(hd * hv, (hd + 1) * hv)
        oh = og_scr[rows, vs]
        ms = jnp.mean(oh * oh, axis=-1, keepdims=True)
        z_parts.append((oh * lax.rsqrt(ms + RMS_EPS)) * glag)
    o_n = jnp.concatenate(z_parts, axis=1)
    y_gla = _dot((o_n * _silu(gla_gate)).astype(BF16), wpgla_ref[...])
    merged = merged + bg_gla * y_gla
    yield

    out = _dot(merged.astype(BF16), wout_ref[...])
    yield

    ms = jnp.mean(out * out, axis=-1, keepdims=True)
    gain = gate * gpost_ref[...]
    o_ref[rows, :] = x_ref[rows, :] + (out * lax.rsqrt(ms + RMS_EPS)) * gain
    q_scr[rows, :] = _dot_nt(hnext_ref[rows, :], wt_ref[Q_ROWS, :]) * (GLA_HEAD_K ** -0.5)
    yield


def _main_kernel(seq_len,
                 x_ref, h_ref, hp_ref, hn_ref, hnext_ref, mod_ref, gpost_ref, k_ref, v_ref, lr_ref,
                 wt_ref,
                 poolw_ref, pscale_ref, gkcat_ref, gbf_ref, gbb_ref, glag_ref,
                 wpp_ref, wpgla_ref, wout_ref, sb_ref,
                 o_ref,
                 sf_scr, hext_scr, q_scr, cf_scr, rb_scr, qs_scr, qd_scr, ke_scr, kif_scr, kib_scr,
                 kv_scr, scat_scr, sc_scr, og_scr):
    tb = MAIN_SUBTILES * MAIN_TILE

    @pl.when(pl.program_id(0) == 0)
    def _():
        sf_scr[...] = jnp.zeros_like(sf_scr)
        q_scr[...] = _dot_nt(h_ref[...], wt_ref[Q_ROWS, :]) * (GLA_HEAD_K ** -0.5)

    hext_scr[0:HALO, :] = hp_ref[...]
    hext_scr[HALO:HALO + tb, :] = h_ref[...]
    hext_scr[HALO + tb:HALO + tb + HALO, :] = hn_ref[...]

    refs = (x_ref, mod_ref, gpost_ref, k_ref, v_ref, lr_ref,
            wt_ref,
            poolw_ref, pscale_ref, gkcat_ref, gbf_ref, gbb_ref, glag_ref,
            wpp_ref, wpgla_ref, wout_ref, sb_ref, o_ref,
            sf_scr, hext_scr, q_scr, cf_scr, rb_scr, qs_scr, qd_scr, ke_scr, kif_scr, kib_scr,
            kv_scr, scat_scr, sc_scr, og_scr, hnext_ref)
    gens = [_subtile_stages(seq_len, sub, refs) for sub in range(MAIN_SUBTILES)]
    _run_round_robin(gens)


def _main_call(x2, h_all, mod, g_post, k_all, v_all, lr_all, w_bf, small, wpool_bf, wpp_bf, wpgla_bf, wout_bf, sb):
    s = x2.shape[0]
    tb = MAIN_SUBTILES * MAIN_TILE
    nt = s // tb
    hb = tb // HALO
    n_hb = s // HALO
    n_pairs = MAIN_TILE // GLA_PAIR
    const2 = lambda i: (0, 0)
    tile = lambda i: (i, 0)
    one = pl.Buffered(1)
    pool_scale, gk_cat, gbias_f, gbias_b, gla_norm_g = small

    def resident(arr):
        return pl.BlockSpec(arr.shape, const2, pipeline_mode=one)

    in_specs = [
        pl.BlockSpec((tb, D_MODEL), tile),
        pl.BlockSpec((tb, D_MODEL), tile),
        pl.BlockSpec((HALO, D_MODEL), lambda i: (jnp.maximum(i * hb - 1, 0), 0)),
        pl.BlockSpec((HALO, D_MODEL), lambda i: (jnp.minimum((i + 1) * hb, n_hb - 1), 0)),
        pl.BlockSpec((tb, D_MODEL), lambda i: (jnp.minimum(i + 1, nt - 1), 0)),
        pl.BlockSpec((3, 1, D_MODEL), lambda i: (0, 0, 0)),
        pl.BlockSpec((1, D_MODEL), const2),
        pl.BlockSpec((tb, GLA_KEY_WIDTH), tile),
        pl.BlockSpec((tb, D_MODEL), tile),
        pl.BlockSpec((tb, LR_WIDTH), tile),
        resident(w_bf),
        resident(wpool_bf), resident(pool_scale), resident(gk_cat), resident(gbias_f), resident(gbias_b),
        resident(gla_norm_g),
        resident(wpp_bf), resident(wpgla_bf), resident(wout_bf),
        pl.BlockSpec((MAIN_SUBTILES, GLA_HEADS, GLA_HEAD_V, GLA_HEAD_K), lambda i: (i, 0, 0, 0)),
    ]
    scratch = [
        pltpu.VMEM((GLA_HEADS, GLA_HEAD_V, GLA_HEAD_K), F32),
        pltpu.VMEM((tb + 2 * HALO, D_MODEL), BF16),
        pltpu.VMEM((tb, GLA_KEY_WIDTH), F32),
        pltpu.VMEM((tb, GLA_KEY_WIDTH), F32),
        pltpu.VMEM((tb, GLA_KEY_WIDTH), F32),
        pltpu.VMEM((tb, 2 * GLA_KEY_WIDTH), BF16),
        pltpu.VMEM((tb, 2 * GLA_KEY_WIDTH), BF16),
        pltpu.VMEM((tb, 2 * GLA_KEY_WIDTH), BF16),
        pltpu.VMEM((tb, GLA_KEY_WIDTH), BF16),
        pltpu.VMEM((tb, GLA_KEY_WIDTH), BF16),
        pltpu.VMEM((MAIN_SUBTILES, GLA_HEADS, n_pairs, GLA_HEAD_V, 2 * GLA_HEAD_K), F32),
        pltpu.VMEM((MAIN_SUBTILES, n_pairs, GLA_HEADS, 2 * GLA_HEAD_K, GLA_HEAD_V), BF16),
        pltpu.VMEM((MAIN_SUBTILES, n_pairs, GLA_HEADS, GLA_PAIR, GLA_PAIR), BF16),
        pltpu.VMEM((tb, D_MODEL), F32),
    ]
    return pl.pallas_call(
        functools.partial(_main_kernel, s),
        grid=(nt,),
        in_specs=in_specs,
        out_specs=pl.BlockSpec((tb, D_MODEL), tile),
        out_shape=jax.ShapeDtypeStruct((s, D_MODEL), F32),
        scratch_shapes=scratch,
        compiler_params=pltpu.CompilerParams(dimension_semantics=("arbitrary",),
                                             vmem_limit_bytes=VMEM_LIMIT_BYTES),
        name="main_call",
    )(x2, h_all, h_all, h_all, h_all, mod, g_post, k_all, v_all, lr_all,
      w_bf,
      wpool_bf, pool_scale, gk_cat, gbias_f, gbias_b, gla_norm_g, wpp_bf, wpgla_bf, wout_bf, sb)


def _layer(x2, c, w_ada, b_ada, g_pre, g_post, w_in, pool_w, pool_scale, gk_up_fwd, gk_bias_fwd,
           gk_up_bwd, gk_bias_bwd, gla_norm_g, w_proj_pool, w_proj_gla, w_out):
    mod = _ada_call(c.reshape(1, D_MODEL), w_ada, b_ada.reshape(1, -1))
    gbias_f = gk_bias_fwd.reshape(1, -1)
    gbias_b = gk_bias_bwd.reshape(1, -1)

    pool_w2 = pool_w.reshape(POOL_GROUPS * POOL_GROUP_WIDTH, POOL_GROUP_WIDTH)
    sb, k_all, v_all, h_all, lr_all, gk_cat, w_bf, wpool_bf, wpp_bf, wpgla_bf, wout_bf = _prep_call(
        x2, mod, g_pre.reshape(1, -1), gk_up_fwd, gk_up_bwd, gbias_b, w_in.T, pool_w2, w_proj_pool, w_proj_gla,
        w_out)

    small = (pool_scale.reshape(1, -1), gk_cat, gbias_f, gbias_b, gla_norm_g.reshape(1, -1))
    return _main_call(x2, h_all, mod, g_post.reshape(1, -1), k_all, v_all, lr_all, w_bf, small,
                      wpool_bf, wpp_bf, wpgla_bf, wout_bf, sb)


def kernel(x, c, w_ada, b_ada, g_pre, g_post, w_in, pool_w, pool_scale, gk_up_fwd, gk_bias_fwd, gk_up_bwd, gk_bias_bwd, gla_norm_g, w_proj_pool, w_proj_gla, w_out):
    b, s, d = x.shape
    depth = w_in.shape[0]
    xf = x.reshape(b * s, d)
    outs = []
    for bi in range(b):
        xb = xf if b == 1 else lax.slice_in_dim(xf, bi * s, (bi + 1) * s, axis=0)
        for l in range(depth):
            xb = _layer(xb, c[bi:bi + 1], w_ada[l], b_ada[l], g_pre[l], g_post[l], w_in[l], pool_w[l],
                        pool_scale[l], gk_up_fwd[l], gk_bias_fwd[l], gk_up_bwd[l], gk_bias_bwd[l],
                        gla_norm_g[l], w_proj_pool[l], w_proj_gla[l], w_out[l])
        outs.append(xb)
    out = outs[0] if b == 1 else jnp.concatenate(outs, axis=0)
    return out.reshape(b, s, d)
```

```python
import functools
import math

import jax
import jax.numpy as jnp
from jax import lax
from jax.experimental import pallas as pl
from jax.experimental.pallas import tpu as pltpu

D_MODEL = 1024
POOL_GROUPS = 4
POOL_GROUP_WIDTH = 256
POOL_WINDOWS = (2, 4, 8, 16)
GLA_HEADS = 4
GLA_KEY_WIDTH = 512
GLA_HEAD_K = 128
GLA_HEAD_V = 256
GLA_GATE_RANK = 16
GLA_GATE_NORMALIZER = 16.0
GATE_SCALE = 1.0 / GLA_GATE_NORMALIZER
assert math.frexp(GLA_GATE_NORMALIZER)[0] == 0.5
GLA_CHUNK = 64
GLA_PAIR = 2 * GLA_CHUNK
RMS_EPS = 1e-6
IN_SPLITS = (1024, 1024, 512, 512, 1024, 1024, 16, 16, 1024, 1024)
IN_WIDTH = sum(IN_SPLITS)

LANES = 128
SUBLANES = 8
SUB_TILE = 256
PREP_SUBTILES = 4
MAIN_TILE = 512
MAIN_SUBTILES = 1
CUMSUM_BLOCK = 256
HALO = 16
POOL_ROW_BLOCK = 128
VMEM_LIMIT_BYTES = 60 * 1024 * 1024

_OFFS = [sum(IN_SPLITS[:j]) for j in range(len(IN_SPLITS) + 1)]
(PIN_ROWS, PGATE_ROWS, Q_ROWS, K_ROWS, V_ROWS, GGATE_ROWS) = (slice(_OFFS[j], _OFFS[j + 1]) for j in range(6))
LR_ROWS = slice(_OFFS[6], _OFFS[8])
BGP_ROWS = slice(_OFFS[8], _OFFS[9])
BGG_ROWS = slice(_OFFS[9], _OFFS[10])
LR_WIDTH = 2 * GLA_GATE_RANK
WT_SLAB = 480

F32 = jnp.float32
BF16 = jnp.bfloat16


def _dot(a, b):
    return jnp.dot(a, b, preferred_element_type=F32)


def _dot_nt(a, b):
    return lax.dot_general(a, b, (((1,), (1,)), ((), ())), preferred_element_type=F32)


def _dot_tn(a, b):
    return lax.dot_general(a, b, (((0,), (0,)), ((), ())), preferred_element_type=F32)


def _sigmoid(x):
    return 0.5 * jnp.tanh(0.5 * x) + 0.5


def _silu(x):
    return x * _sigmoid(x)


def _log_sigmoid(x):
    return jnp.minimum(x, 0.0) - jnp.log(1.0 + jnp.exp(-jnp.abs(x)))


def _norm_mod(xv, g_pre, scale, shift):
    ms = jnp.mean(xv * xv, axis=-1, keepdims=True)
    gain = g_pre * (1.0 + scale)
    return (xv * lax.rsqrt(ms + RMS_EPS)) * gain + shift


def _split_hi_lo(a):
    hi = a.astype(BF16)
    lo = (a - hi.astype(F32)).astype(BF16)
    return hi, lo


def _run_round_robin(gens):
    live = list(gens)
    while live:
        for g in list(live):
            try:
                next(g)
            except StopIteration:
                live.remove(g)


def _ada_kernel(c_ref, w_ref, b_ref, o_ref):
    c_rows = jnp.broadcast_to(_silu(c_ref[...]), (SUBLANES, D_MODEL))
    s = c_rows.T[:, 0:1]
    for j in range(3):
        cols = slice(j * D_MODEL, (j + 1) * D_MODEL)
        o_ref[j] = jnp.sum(w_ref[:, cols] * s, axis=0, keepdims=True) + b_ref[:, cols]


def _ada_call(c_row, w_ada, b_ada):
    return pl.pallas_call(
        _ada_kernel,
        grid=(1,),
        in_specs=[
            pl.BlockSpec((1, D_MODEL), lambda j: (0, 0)),
            pl.BlockSpec((D_MODEL, 3 * D_MODEL), lambda j: (0, 0), pipeline_mode=pl.Buffered(1)),
            pl.BlockSpec((1, 3 * D_MODEL), lambda j: (0, 0)),
        ],
        out_specs=pl.BlockSpec((3, 1, D_MODEL), lambda j: (0, 0, 0)),
        out_shape=jax.ShapeDtypeStruct((3, 1, D_MODEL), F32),
        compiler_params=pltpu.CompilerParams(dimension_semantics=("arbitrary",)),
        name="ada_call",
    )(c_row, w_ada, b_ada)


def _prep_stages(sub, refs):
    (gkcat_ref, gbias_ref, sb_ref, k_ref, v_ref, lr_ref, st_scr) = refs
    t = SUB_TILE
    rows = slice(sub * t, (sub + 1) * t)
    k = k_ref[rows, :]
    v_bf = v_ref[rows, :]

    gk_bwd = gkcat_ref[:, GLA_KEY_WIDTH:]
    la = _log_sigmoid(_dot(lr_ref[rows, :], gk_bwd) + gbias_ref[...])
    la_hi, la_lo = _split_hi_lo(la)
    yield

    row = lax.broadcasted_iota(jnp.int32, (t, t), 0)
    col = lax.broadcasted_iota(jnp.int32, (t, t), 1)
    upper = jnp.where(col >= row, GATE_SCALE, 0.0).astype(BF16)
    rc = _dot(upper, la_hi) + _dot(upper, la_lo)
    yield

    tot = rc[0:1, :]
    k_end = (k * jnp.exp(tot - rc)).astype(BF16)
    decay = jnp.exp(tot)
    kvs = []
    for hd in range(GLA_HEADS):
        ks = slice(hd * GLA_HEAD_K, (hd + 1) * GLA_HEAD_K)
        vs = slice(hd * GLA_HEAD_V, (hd + 1) * GLA_HEAD_V)
        kvs.append(_dot_tn(v_bf[:, vs], k_end[:, ks]))
    yield

    per_main = MAIN_TILE // SUB_TILE
    for hd in range(GLA_HEADS):
        ks = slice(hd * GLA_HEAD_K, (hd + 1) * GLA_HEAD_K)
        st = st_scr[hd]
        if (sub + 1) % per_main == 0:
            sb_ref[sub // per_main, hd] = st
        st_scr[hd] = st * decay[:, ks] + kvs[hd]
    yield


def _prep_kernel(x_ref, mod_ref, gpre_ref, gkf_ref, gkb_ref, gbias_ref, wk_ref, wv_ref, wlr_ref,
                 win_ref, wpool_ref, wpp_ref, wpgla_ref, wout_ref,
                 sb_ref, k_ref, v_ref, h_ref, lr_ref, gkcat_ref,
                 winbf_ref, wpoolbf_ref, wppbf_ref, wpglabf_ref, woutbf_ref,
                 st_scr, wk_scr, wv_scr, wlr_scr):
    @pl.when(pl.program_id(0) == 0)
    def _():
        st_scr[...] = jnp.zeros_like(st_scr)
        wk_scr[...] = wk_ref[...].astype(BF16)
        wv_scr[...] = wv_ref[...].astype(BF16)
        wlr_scr[...] = wlr_ref[...].astype(BF16)
        r, w = GLA_GATE_RANK, GLA_KEY_WIDTH
        gkcat_ref[...] = jnp.zeros(gkcat_ref.shape, BF16)
        gkcat_ref[0:r, 0:w] = gkf_ref[...].astype(BF16)
        gkcat_ref[r:2 * r, w:2 * w] = gkb_ref[...].astype(BF16)

    winbf_ref[...] = win_ref[...].astype(BF16)
    wpoolbf_ref[...] = wpool_ref[...].astype(BF16)
    wppbf_ref[...] = wpp_ref[...].astype(BF16)
    wpglabf_ref[...] = wpgla_ref[...].astype(BF16)
    woutbf_ref[...] = wout_ref[...].astype(BF16)

    h = _norm_mod(x_ref[...], gpre_ref[...], mod_ref[1], mod_ref[0]).astype(BF16)
    h_ref[...] = h
    k_ref[...] = _dot_nt(h, wk_scr[...])
    lr_ref[...] = _dot_nt(h, wlr_scr[...]).astype(BF16)
    v_ref[...] = _dot_nt(h, wv_scr[...]).astype(BF16)

    refs = (gkcat_ref, gbias_ref, sb_ref, k_ref, v_ref, lr_ref, st_scr)
    gens = [_prep_stages(sub, refs) for sub in reversed(range(PREP_SUBTILES))]
    _run_round_robin(gens)


def _prep_call(x2, mod, g_pre, gk_f, gk_b, gbias_b, w_in_t, pool_w2, w_proj_pool, w_proj_gla, w_out):
    s = x2.shape[0]
    tb = PREP_SUBTILES * SUB_TILE
    nb = s // tb
    slab = D_MODEL // nb
    n_wt = IN_WIDTH // WT_SLAB
    const = lambda i: (0, 0)
    rev = lambda i: (nb - 1 - i, 0)
    fwd = lambda i: (i, 0)
    wt_slab = lambda i: (jnp.minimum(i, n_wt - 1), 0)
    one = pl.Buffered(1)

    def rows_of(rs):
        n = rs.stop - rs.start
        return pl.BlockSpec((n, D_MODEL), lambda i: (rs.start // n, 0), pipeline_mode=one)

    return pl.pallas_call(
        _prep_kernel,
        grid=(nb,),
        in_specs=[
            pl.BlockSpec((tb, D_MODEL), rev),
            pl.BlockSpec((3, 1, D_MODEL), lambda i: (0, 0, 0)),
            pl.BlockSpec((1, D_MODEL), const),
            pl.BlockSpec(gk_f.shape, const),
            pl.BlockSpec(gk_b.shape, const),
            pl.BlockSpec(gbias_b.shape, const),
            rows_of(K_ROWS),
            rows_of(V_ROWS),
            rows_of(LR_ROWS),
            pl.BlockSpec((WT_SLAB, D_MODEL), wt_slab),
            pl.BlockSpec((slab, POOL_GROUP_WIDTH), fwd),
            pl.BlockSpec((slab, D_MODEL), fwd),
            pl.BlockSpec((slab, D_MODEL), fwd),
            pl.BlockSpec((slab, D_MODEL), fwd),
        ],
        out_specs=[
            pl.BlockSpec((tb // MAIN_TILE, GLA_HEADS, GLA_HEAD_V, GLA_HEAD_K), lambda i: (nb - 1 - i, 0, 0, 0)),
            pl.BlockSpec((tb, GLA_KEY_WIDTH), rev),
            pl.BlockSpec((tb, D_MODEL), rev),
            pl.BlockSpec((tb, D_MODEL), rev),
            pl.BlockSpec((tb, LR_WIDTH), rev),
            pl.BlockSpec((LR_WIDTH, 2 * GLA_KEY_WIDTH), const),
            pl.BlockSpec((WT_SLAB, D_MODEL), wt_slab),
            pl.BlockSpec((slab, POOL_GROUP_WIDTH), fwd),
            pl.BlockSpec((slab, D_MODEL), fwd),
            pl.BlockSpec((slab, D_MODEL), fwd),
            pl.BlockSpec((slab, D_MODEL), fwd),
        ],
        out_shape=[
            jax.ShapeDtypeStruct((s // MAIN_TILE, GLA_HEADS, GLA_HEAD_V, GLA_HEAD_K), F32),
            jax.ShapeDtypeStruct((s, GLA_KEY_WIDTH), F32),
            jax.ShapeDtypeStruct((s, D_MODEL), BF16),
            jax.ShapeDtypeStruct((s, D_MODEL), BF16),
            jax.ShapeDtypeStruct((s, LR_WIDTH), BF16),
            jax.ShapeDtypeStruct((LR_WIDTH, 2 * GLA_KEY_WIDTH), BF16),
            jax.ShapeDtypeStruct((IN_WIDTH, D_MODEL), BF16),
            jax.ShapeDtypeStruct(pool_w2.shape, BF16),
            jax.ShapeDtypeStruct((D_MODEL, D_MODEL), BF16),
            jax.ShapeDtypeStruct((D_MODEL, D_MODEL), BF16),
            jax.ShapeDtypeStruct((D_MODEL, D_MODEL), BF16),
        ],
        scratch_shapes=[
            pltpu.VMEM((GLA_HEADS, GLA_HEAD_V, GLA_HEAD_K), F32),
            pltpu.VMEM((GLA_KEY_WIDTH, D_MODEL), BF16),
            pltpu.VMEM((D_MODEL, D_MODEL), BF16),
            pltpu.VMEM((LR_WIDTH, D_MODEL), BF16),
        ],
        compiler_params=pltpu.CompilerParams(dimension_semantics=("arbitrary",),
                                             vmem_limit_bytes=VMEM_LIMIT_BYTES),
        name="prep_call",
    )(x2, mod, g_pre, gk_f, gk_b, gbias_b, w_in_t, w_in_t, w_in_t, w_in_t, pool_w2, w_proj_pool, w_proj_gla, w_out)


def _subtile_stages(seq_len, sub, refs):
    (x_ref, mod_ref, gpost_ref, k_ref, v_ref, lr_ref,
     wt_ref,
     poolw_ref, pscale_ref, gkcat_ref, gbf_ref, gbb_ref, glag_ref,
     wpp_ref, wpgla_ref, wout_ref, sb_ref, o_ref,
     sf_scr, hext_scr, q_scr, cf_scr, rb_scr, qs_scr, qd_scr, ke_scr, kif_scr, kib_scr,
     kv_scr, scat_scr, sc_scr, og_scr) = refs
    t = MAIN_TILE
    r0 = sub * t
    rows = slice(r0, r0 + t)
    tile_row0 = pl.program_id(0) * (MAIN_SUBTILES * t) + r0
    sb_idx = sub
    n_pairs = t // GLA_PAIR
    hk, hv = GLA_HEAD_K, GLA_HEAD_V
    gate = mod_ref[2]

    h_ext = hext_scr[r0:r0 + t + 2 * HALO, :]
    h = hext_scr[r0 + HALO:r0 + HALO + t, :]

    gbias = jnp.concatenate([gbf_ref[...], gbb_ref[...]], axis=1)
    la_pre = _dot(lr_ref[rows, :], gkcat_ref[...]) + gbias
    q_scr[rows, :] = _dot_nt(h, wt_ref[Q_ROWS, :]) * (GLA_HEAD_K ** -0.5)
    te = t + 2 * HALO
    g_row = tile_row0 - HALO + lax.broadcasted_iota(jnp.int32, (te, 1), 0)
    valid = jnp.logical_and(g_row >= 0, g_row < seq_len)
    u_ext = jnp.where(valid, _dot_nt(h_ext, wt_ref[PIN_ROWS, :]), 0.0)
    yield

    la = _log_sigmoid(la_pre)
    p_gate = _dot_nt(h, wt_ref[PGATE_ROWS, :])
    gla_gate = _dot_nt(h, wt_ref[GGATE_ROWS, :])
    u_ext_bf = u_ext.astype(BF16)
    u = u_ext[HALO:HALO + t, :]
    laf_hi, laf_lo = _split_hi_lo(la[:, :GLA_KEY_WIDTH])
    lab_hi, lab_lo = _split_hi_lo(la[:, GLA_KEY_WIDTH:])
    yield

    cb = CUMSUM_BLOCK
    row = lax.broadcasted_iota(jnp.int32, (cb, cb), 0)
    col = lax.broadcasted_iota(jnp.int32, (cb, cb), 1)
    same_chunk = (row // GLA_CHUNK) == (col // GLA_CHUNK)
    lower = jnp.where(jnp.logical_and(same_chunk, col <= row), GATE_SCALE, 0.0).astype(BF16)
    upper = jnp.where(jnp.logical_and(same_chunk, col >= row), GATE_SCALE, 0.0).astype(BF16)
    for b in range(t // cb):
        rb_rows = slice(b * cb, (b + 1) * cb)
        dst = slice(r0 + b * cb, r0 + (b + 1) * cb)
        cf_scr[dst, :] = _dot(lower, laf_hi[rb_rows]) + _dot(lower, laf_lo[rb_rows])
        rb_scr[dst, :] = _dot(upper, lab_hi[rb_rows]) + _dot(upper, lab_lo[rb_rows])
    bg_pool = _sigmoid(_dot_nt(h, wt_ref[BGP_ROWS, :]))
    bg_gla = _sigmoid(_dot_nt(h, wt_ref[BGG_ROWS, :]))
    yield

    pos = tile_row0 + lax.broadcasted_iota(jnp.int32, (t, 1), 0)
    rb_ = POOL_ROW_BLOCK
    kb_ = POOL_ROW_BLOCK + 2 * HALO
    band_r = lax.broadcasted_iota(jnp.int32, (rb_, kb_), 0)
    band_c = lax.broadcasted_iota(jnp.int32, (rb_, kb_), 1)
    band_d = band_c - HALO - band_r
    a_parts = []
    for gi, w in enumerate(POOL_WINDOWS):
        hw = w // 2
        cs = slice(gi * POOL_GROUP_WIDTH, (gi + 1) * POOL_GROUP_WIDTH)
        ws = slice(gi * POOL_GROUP_WIDTH, (gi + 1) * POOL_GROUP_WIDTH)
        band = jnp.logical_and(band_d >= -hw, band_d < hw).astype(BF16)
        wsum = jnp.concatenate(
            [_dot(band, u_ext_bf[b * rb_:b * rb_ + kb_, cs]) for b in range(t // rb_)], axis=0)
        count = (jnp.minimum(pos + hw, seq_len) - jnp.maximum(pos - hw, 0)).astype(F32)
        pooled = wsum / count - u[:, cs]
        a_parts.append(_dot(pooled.astype(BF16), poolw_ref[ws, :]))
    a = jnp.concatenate(a_parts, axis=1) * pscale_ref[...]
    z_pool = (a * _silu(p_gate)).astype(BF16)

    def put_heads(dst, rs, fwd, bwd):
        for hd in range(GLA_HEADS):
            ks = slice(hd * hk, (hd + 1) * hk)
            dst[rs, 2 * hd * hk:(2 * hd + 1) * hk] = fwd[:, ks].astype(BF16)
            dst[rs, (2 * hd + 1) * hk:(2 * hd + 2) * hk] = bwd[:, ks].astype(BF16)

    bf = lambda v: v.astype(BF16)
    dec_f, dec_b = [], []
    for p in range(n_pairs):
        c0 = r0 + p * GLA_PAIR
        c1 = c0 + GLA_CHUNK
        ra = slice(c0, c1)
        rb = slice(c1, c1 + GLA_CHUNK)
        q0, q1 = bf(q_scr[ra, :]), bf(q_scr[rb, :])
        k0, k1 = bf(k_ref[ra, :]), bf(k_ref[rb, :])
        cum0, cum1 = cf_scr[ra, :], cf_scr[rb, :]
        tot0 = cf_scr[c1 - 1:c1, :]
        tot1 = cf_scr[c1 + GLA_CHUNK - 1:c1 + GLA_CHUNK, :]
        rc0, rc1 = rb_scr[ra, :], rb_scr[rb, :]
        tb0 = rb_scr[c0:c0 + 1, :]
        tb1 = rb_scr[c1:c1 + 1, :]
        e_tot0, e_tot1 = jnp.exp(tot0), jnp.exp(tot1)
        e_tb0, e_tb1 = jnp.exp(tb0), jnp.exp(tb1)

        a0 = cum0 - tot0
        ks_f0 = k0 * bf(jnp.exp(-a0))
        qs_f1 = q1 * bf(jnp.exp(cum1))
        qs_b0 = q0 * bf(jnp.exp(rc0))
        put_heads(qs_scr, ra, q0 * bf(jnp.exp(a0)), qs_b0)
        put_heads(qd_scr, ra, q0 * bf(jnp.exp(cum0)), qs_b0 * bf(e_tb1))
        kif_scr[ra, :] = ks_f0
        kif_scr[rb, :] = k1 * bf(jnp.exp(-cum1))
        a1 = rc1 - tb1
        ks_b1 = k1 * bf(jnp.exp(-a1))
        put_heads(qs_scr, rb, qs_f1, q1 * bf(jnp.exp(a1)))
        put_heads(qd_scr, rb, qs_f1 * bf(e_tot0), q1 * bf(jnp.exp(rc1)))
        kib_scr[ra, :] = k0 * bf(jnp.exp(-rc0))
        kib_scr[rb, :] = ks_b1
        put_heads(ke_scr, ra, ks_f0 * bf(e_tot1), k0 * bf(jnp.exp(tb0 - rc0)))
        put_heads(ke_scr, rb, k1 * bf(jnp.exp(tot1 - cum1)), ks_b1 * bf(e_tb0))
        dec_f.append(e_tot0 * e_tot1)
        dec_b.append(e_tb0 * e_tb1)
    yield

    merged = bg_pool * _dot(z_pool, wpp_ref[...])
    for hd in range(GLA_HEADS):
        vs = slice(hd * hv, (hd + 1) * hv)
        cat = slice(2 * hd * hk, (2 * hd + 2) * hk)
        for p in range(n_pairs):
            rs = slice(r0 + p * GLA_PAIR, r0 + (p + 1) * GLA_PAIR)
            kv_scr[sub, hd, p] = _dot_tn(v_ref[rs, vs], ke_scr[rs, cat])
    yield

    prow = lax.broadcasted_iota(jnp.int32, (GLA_PAIR, GLA_PAIR), 0)
    pcol = lax.broadcasted_iota(jnp.int32, (GLA_PAIR, GLA_PAIR), 1)
    mask_f = pcol <= prow
    for p in range(n_pairs):
        rs = slice(r0 + p * GLA_PAIR, r0 + (p + 1) * GLA_PAIR)
        for hd in range(GLA_HEADS):
            ks = slice(hd * hk, (hd + 1) * hk)
            sc_f = _dot_nt(qs_scr[rs, 2 * hd * hk:(2 * hd + 1) * hk], kif_scr[rs, ks])
            sc_b = _dot_nt(qs_scr[rs, (2 * hd + 1) * hk:(2 * hd + 2) * hk], kib_scr[rs, ks])
            sc_scr[sub, p, hd] = jnp.where(mask_f, sc_f, sc_b).astype(BF16)
    for hd in range(GLA_HEADS):
        ks = slice(hd * hk, (hd + 1) * hk)
        sf = sf_scr[hd]
        for p in range(n_pairs):
            scat_scr[sub, p, hd, 0:hk, :] = sf.T.astype(BF16)
            sf = sf * dec_f[p][:, ks] + kv_scr[sub, hd, p, :, 0:hk]
        sf_scr[hd] = sf
        sb = sb_ref[sb_idx, hd]
        for p in reversed(range(n_pairs)):
            scat_scr[sub, p, hd, hk:2 * hk, :] = sb.T.astype(BF16)
            if p > 0:
                sb = sb * dec_b[p][:, ks] + kv_scr[sub, hd, p, :, hk:2 * hk]
    yield

    for p in range(n_pairs):
        rs = slice(r0 + p * GLA_PAIR, r0 + (p + 1) * GLA_PAIR)
        for hd in range(GLA_HEADS):
            vs = slice(hd * hv, (hd + 1) * hv)
            cat = slice(2 * hd * hk, (2 * hd + 2) * hk)
            og_scr[rs, vs] = (_dot(sc_scr[sub, p, hd], v_ref[rs, vs])
                              + _dot(qd_scr[rs, cat], scat_scr[sub, p, hd]))
    yield

    glag = glag_ref[...]
    z_parts = []
    for hd in range(GLA_HEADS):
        vs = slice(hd * hv, (hd + 1) * hv)
        oh = og_scr[rows, vs]
        ms = jnp.mean(oh * oh, axis=-1, keepdims=True)
        z_parts.append((oh * lax.rsqrt(ms + RMS_EPS)) * glag)
    o_n = jnp.concatenate(z_parts, axis=1)
    y_gla = _dot((o_n * _silu(gla_gate)).astype(BF16), wpgla_ref[...])
    merged = merged + bg_gla * y_gla
    yield

    out = _dot(merged.astype(BF16), wout_ref[...])
    yield

    ms = jnp.mean(out * out, axis=-1, keepdims=True)
    gain = gate * gpost_ref[...]
    o_ref[rows, :] = x_ref[rows, :] + (out * lax.rsqrt(ms + RMS_EPS)) * gain
    yield


def _main_kernel(seq_len,
                 x_ref, h_ref, hp_ref, hn_ref, mod_ref, gpost_ref, k_ref, v_ref, lr_ref,
                 wt_ref,
                 poolw_ref, pscale_ref, gkcat_ref, gbf_ref, gbb_ref, glag_ref,
                 wpp_ref, wpgla_ref, wout_ref, sb_ref,
                 o_ref,
                 sf_scr, hext_scr, q_scr, cf_scr, rb_scr, qs_scr, qd_scr, ke_scr, kif_scr, kib_scr,
                 kv_scr, scat_scr, sc_scr, og_scr):
    tb = MAIN_SUBTILES * MAIN_TILE

    @pl.when(pl.program_id(0) == 0)
    def _():
        sf_scr[...] = jnp.zeros_like(sf_scr)

    hext_scr[0:HALO, :] = hp_ref[...]
    hext_scr[HALO:HALO + tb, :] = h_ref[...]
    hext_scr[HALO + tb:HALO + tb + HALO, :] = hn_ref[...]

    refs = (x_ref, mod_ref, gpost_ref, k_ref, v_ref, lr_ref,
            wt_ref,
            poolw_ref, pscale_ref, gkcat_ref, gbf_ref, gbb_ref, glag_ref,
            wpp_ref, wpgla_ref, wout_ref, sb_ref, o_ref,
            sf_scr, hext_scr, q_scr, cf_scr, rb_scr, qs_scr, qd_scr, ke_scr, kif_scr, kib_scr,
            kv_scr, scat_scr, sc_scr, og_scr)
    gens = [_subtile_stages(seq_len, sub, refs) for sub in range(MAIN_SUBTILES)]
    _run_round_robin(gens)


def _main_call(x2, h_all, mod, g_post, k_all, v_all, lr_all, w_bf, small, wpool_bf, wpp_bf, wpgla_bf, wout_bf, sb):
    s = x2.shape[0]
    tb = MAIN_SUBTILES * MAIN_TILE
    nt = s // tb
    hb = tb // HALO
    n_hb = s // HALO
    n_pairs = MAIN_TILE // GLA_PAIR
    const2 = lambda i: (0, 0)
    tile = lambda i: (i, 0)
    one = pl.Buffered(1)
    pool_scale, gk_cat, gbias_f, gbias_b, gla_norm_g = small

    def resident(arr):
        return pl.BlockSpec(arr.shape, const2, pipeline_mode=one)

    in_specs = [
        pl.BlockSpec((tb, D_MODEL), tile),
        pl.BlockSpec((tb, D_MODEL), tile),
        pl.BlockSpec((HALO, D_MODEL), lambda i: (jnp.maximum(i * hb - 1, 0), 0)),
        pl.BlockSpec((HALO, D_MODEL), lambda i: (jnp.minimum((i + 1) * hb, n_hb - 1), 0)),
        pl.BlockSpec((3, 1, D_MODEL), lambda i: (0, 0, 0)),
        pl.BlockSpec((1, D_MODEL), const2),
        pl.BlockSpec((tb, GLA_KEY_WIDTH), tile),
        pl.BlockSpec((tb, D_MODEL), tile),
        pl.BlockSpec((tb, LR_WIDTH), tile),
        resident(w_bf),
        resident(wpool_bf), resident(pool_scale), resident(gk_cat), resident(gbias_f), resident(gbias_b),
        resident(gla_norm_g),
        resident(wpp_bf), resident(wpgla_bf), resident(wout_bf),
        pl.BlockSpec((MAIN_SUBTILES, GLA_HEADS, GLA_HEAD_V, GLA_HEAD_K), lambda i: (i, 0, 0, 0)),
    ]
    scratch = [
        pltpu.VMEM((GLA_HEADS, GLA_HEAD_V, GLA_HEAD_K), F32),
        pltpu.VMEM((tb + 2 * HALO, D_MODEL), BF16),
        pltpu.VMEM((tb, GLA_KEY_WIDTH), F32),
        pltpu.VMEM((tb, GLA_KEY_WIDTH), F32),
        pltpu.VMEM((tb, GLA_KEY_WIDTH), F32),
        pltpu.VMEM((tb, 2 * GLA_KEY_WIDTH), BF16),
        pltpu.VMEM((tb, 2 * GLA_KEY_WIDTH), BF16),
        pltpu.VMEM((tb, 2 * GLA_KEY_WIDTH), BF16),
        pltpu.VMEM((tb, GLA_KEY_WIDTH), BF16),
        pltpu.VMEM((tb, GLA_KEY_WIDTH), BF16),
        pltpu.VMEM((MAIN_SUBTILES, GLA_HEADS, n_pairs, GLA_HEAD_V, 2 * GLA_HEAD_K), F32),
        pltpu.VMEM((MAIN_SUBTILES, n_pairs, GLA_HEADS, 2 * GLA_HEAD_K, GLA_HEAD_V), BF16),
        pltpu.VMEM((MAIN_SUBTILES, n_pairs, GLA_HEADS, GLA_PAIR, GLA_PAIR), BF16),
        pltpu.VMEM((tb, D_MODEL), F32),
    ]
    return pl.pallas_call(
        functools.partial(_main_kernel, s),
        grid=(nt,),
        in_specs=in_specs,
        out_specs=pl.BlockSpec((tb, D_MODEL), tile),
        out_shape=jax.ShapeDtypeStruct((s, D_MODEL), F32),
        scratch_shapes=scratch,
        compiler_params=pltpu.CompilerParams(dimension_semantics=("arbitrary",),
                                             vmem_limit_bytes=VMEM_LIMIT_BYTES),
        name="main_call",
    )(x2, h_all, h_all, h_all, mod, g_post, k_all, v_all, lr_all,
      w_bf,
      wpool_bf, pool_scale, gk_cat, gbias_f, gbias_b, gla_norm_g, wpp_bf, wpgla_bf, wout_bf, sb)


def _layer(x2, c, w_ada, b_ada, g_pre, g_post, w_in, pool_w, pool_scale, gk_up_fwd, gk_bias_fwd,
           gk_up_bwd, gk_bias_bwd, gla_norm_g, w_proj_pool, w_proj_gla, w_out):
    mod = _ada_call(c.reshape(1, D_MODEL), w_ada, b_ada.reshape(1, -1))
    gbias_f = gk_bias_fwd.reshape(1, -1)
    gbias_b = gk_bias_bwd.reshape(1, -1)

    pool_w2 = pool_w.reshape(POOL_GROUPS * POOL_GROUP_WIDTH, POOL_GROUP_WIDTH)
    sb, k_all, v_all, h_all, lr_all, gk_cat, w_bf, wpool_bf, wpp_bf, wpgla_bf, wout_bf = _prep_call(
        x2, mod, g_pre.reshape(1, -1), gk_up_fwd, gk_up_bwd, gbias_b, w_in.T, pool_w2, w_proj_pool, w_proj_gla,
        w_out)

    small = (pool_scale.reshape(1, -1), gk_cat, gbias_f, gbias_b, gla_norm_g.reshape(1, -1))
    return _main_call(x2, h_all, mod, g_post.reshape(1, -1), k_all, v_all, lr_all, w_bf, small,
                      wpool_bf, wpp_bf, wpgla_bf, wout_bf, sb)


def kernel(x, c, w_ada, b_ada, g_pre, g_post, w_in, pool_w, pool_scale, gk_up_fwd, gk_bias_fwd, gk_up_bwd, gk_bias_bwd, gla_norm_g, w_proj_pool, w_proj_gla, w_out):
    b, s, d = x.shape
    depth = w_in.shape[0]
    xf = x.reshape(b * s, d)
    outs = []
    for bi in range(b):
        xb = xf if b == 1 else lax.slice_in_dim(xf, bi * s, (bi + 1) * s, axis=0)
        for l in range(depth):
            xb = _layer(xb, c[bi:bi + 1], w_ada[l], b_ada[l], g_pre[l], g_post[l], w_in[l], pool_w[l],
                        pool_scale[l], gk_up_fwd[l], gk_bias_fwd[l], gk_up_bwd[l], gk_bias_bwd[l],
                        gla_norm_g[l], w_proj_pool[l], w_proj_gla[l], w_out[l])
        outs.append(xb)
    out = outs[0] if b == 1 else jnp.concatenate(outs, axis=0)
    return out.reshape(b, s, d)
```
